```python
import math
import jax, jax.numpy as jnp
from jax import lax
import numpy as np

D_MODEL = 2048
BATCH = 4
SEQ = 2048
DEPTH = 2
DEC_BATCH = 8
DEC_SEQ = 1
PAST_LEN = 16384
PAGE_SIZE = 128

F32 = jnp.float32
N_EVEN = (DEPTH + 1) // 2
N_ODD = DEPTH // 2
SSD_HEADS = 32
SSD_HEAD_DIM = 64
SSD_INNER = SSD_HEADS * SSD_HEAD_DIM
SSD_GROUPS = 8
SSD_STATE = 128
SSD_CONV = 4
SSD_CHUNK = 128
CONV_DIM = SSD_INNER + 2 * SSD_GROUPS * SSD_STATE
ATT_HEAD_DIM = 128
WINDOWS = (128, 512, 2048)
DILATIONS = (1, 4, 16)
ATT_HPP = 8
ATT_HEADS = ATT_HPP * len(WINDOWS)
ATT_QKV = ATT_HEADS * ATT_HEAD_DIM
ATT_OUT = ATT_HPP * ATT_HEAD_DIM
REL_BUCKETS = 32
REL_MAX_DIST = 2048
IN_COLS = SSD_INNER + CONV_DIM + SSD_HEADS + 3 * ATT_QKV
IN_SPLITS = [SSD_INNER, SSD_INNER + CONV_DIM, SSD_INNER + CONV_DIM + SSD_HEADS,
             SSD_INNER + CONV_DIM + SSD_HEADS + ATT_QKV, SSD_INNER + CONV_DIM + SSD_HEADS + 2 * ATT_QKV]
MIX_OUT = SSD_INNER + ATT_OUT
POOL_WINDOWS = (2, 4, 8, 16)
POOL_GROUPS = 4
POOL_CH = D_MODEL // POOL_GROUPS
POOL_MAX = 16
D_FF = 4 * D_MODEL
EPS = 1e-6
NEG = -1e30

kernel_name = "hybrid_ssd_dilated_pool_decoder_step"


def rmsnorm(x, g):
    xf = x.astype(F32)
    y = xf * lax.rsqrt(jnp.mean(xf * xf, axis=-1, keepdims=True) + EPS)
    return (y * g.astype(F32)).astype(x.dtype)


def rel_bucket(dist):
    max_exact = REL_BUCKETS // 2
    n = jnp.maximum(dist, 1).astype(F32)
    large = max_exact + (jnp.log(n / max_exact) / math.log(REL_MAX_DIST / max_exact)
                         * (REL_BUCKETS - max_exact)).astype(jnp.int32)
    large = jnp.minimum(large, REL_BUCKETS - 1)
    return jnp.where(dist < max_exact, dist, large)


def causal_conv(xbc, prev, w, b):
    t = xbc.shape[1]
    ext = jnp.concatenate([prev.astype(xbc.dtype), xbc], axis=1)
    out = b + sum(w[k] * ext[:, k:k + t] for k in range(SSD_CONV))
    return jax.nn.silu(out), ext[:, -(SSD_CONV - 1):]


def ssd_chunked(xs, dt, A, Bm, Cm):
    b, L = xs.shape[:2]
    Q, G = SSD_CHUNK, SSD_GROUPS
    E = SSD_HEADS // G
    nc = L // Q
    x = xs.reshape(b, nc, Q, G, E, SSD_HEAD_DIM).astype(F32)
    dtc = dt.reshape(b, nc, Q, G, E).astype(F32)
    Bc = Bm.reshape(b, nc, Q, G, SSD_STATE).astype(F32)
    Cc = Cm.reshape(b, nc, Q, G, SSD_STATE).astype(F32)
    acum = jnp.cumsum(dtc * A.reshape(G, E), axis=2)
    seg = acum[:, :, :, None] - acum[:, :, None, :]
    causal = (jnp.arange(Q)[:, None] >= jnp.arange(Q)[None, :])[:, :, None, None]
    Lw = jnp.where(causal, jnp.exp(jnp.where(causal, seg, 0.0)), 0.0) * dtc[:, :, None]
    CB = jnp.einsum('bcign,bcjgn->bcgij', Cc, Bc)
    y_diag = jnp.einsum('bcgij,bcijge,bcjgep->bcigep', CB, Lw, x)
    decay_end = jnp.exp(acum[:, :, -1:] - acum) * dtc
    states = jnp.einsum('bcjgn,bcjge,bcjgep->bcgepn', Bc, decay_end, x)
    chunk_decay = jnp.exp(acum[:, :, -1])

    def step(h, inp):
        dec, st = inp
        return dec[..., None, None] * h + st, h

    h0 = jnp.zeros((b, G, E, SSD_HEAD_DIM, SSD_STATE), F32)
    h_fin, h_prev = lax.scan(step, h0, (jnp.moveaxis(chunk_decay, 1, 0), jnp.moveaxis(states, 1, 0)))
    h_prev = jnp.moveaxis(h_prev, 0, 1)
    y_off = jnp.einsum('bcign,bcige,bcgepn->bcigep', Cc, jnp.exp(acum), h_prev)
    y = (y_diag + y_off).reshape(b, L, SSD_HEADS, SSD_HEAD_DIM)
    return y, h_fin.reshape(b, SSD_HEADS, SSD_HEAD_DIM, SSD_STATE)


def ssd_recurrent(xs, dt, A, Bm, Cm, h0):
    E = SSD_HEADS // SSD_GROUPS

    def step(h, inp):
        x_t, dt_t, B_t, C_t = inp
        Bh = jnp.repeat(B_t, E, axis=1)
        Ch = jnp.repeat(C_t, E, axis=1)
        h = jnp.exp(dt_t * A)[..., None, None] * h + (dt_t[..., None] * x_t)[..., None] * Bh[:, :, None, :]
        return h, jnp.einsum('bhpn,bhn->bhp', h, Ch)

    inp = (jnp.moveaxis(xs.astype(F32), 1, 0), jnp.moveaxis(dt.astype(F32), 1, 0),
           jnp.moveaxis(Bm.astype(F32), 1, 0), jnp.moveaxis(Cm.astype(F32), 1, 0))
    h, ys = lax.scan(step, h0.astype(F32), inp)
    return jnp.moveaxis(ys, 0, 1), h


def dilated_attn_prompt(q, k, v, dil, window, bias_g):
    b, S, H, hd = q.shape
    blk = window // dil
    Ls = S // dil
    nb = -(-Ls // blk)
    pad = nb * blk - Ls

    def to_sub(t):
        t = t.reshape(b, Ls, dil, H, hd).transpose(0, 2, 1, 3, 4).reshape(b * dil, Ls, H, hd)
        return jnp.pad(t, ((0, 0), (0, pad), (0, 0), (0, 0)))

    def band(t):
        tb = jnp.pad(t, ((0, 0), (blk, 0), (0, 0), (0, 0))).reshape(b * dil, nb + 1, blk, H, hd)
        return jnp.concatenate([tb[:, :-1], tb[:, 1:]], axis=2)

    qb = to_sub(q).reshape(b * dil, nb, blk, H, hd)
    kb, vb = band(to_sub(k)), band(to_sub(v))
    j = blk + jnp.arange(blk)[:, None] - jnp.arange(2 * blk)[None, :]
    valid = ((j >= 0) & (j <= blk))[None] & ((jnp.arange(nb)[:, None, None] > 0)
                                           | (jnp.arange(2 * blk)[None, None, :] >= blk))
    bias = jnp.transpose(bias_g[rel_bucket(jnp.clip(j, 0, blk) * dil)], (2, 0, 1))
    logits = jnp.einsum('nbqhd,nbkhd->nbhqk', qb, kb, preferred_element_type=F32) / math.sqrt(hd)
    logits = jnp.where(valid[None, :, None], logits + bias.astype(F32)[None, None], NEG)
    m = jnp.max(logits, axis=-1, keepdims=True)
    p = jnp.exp(logits - m)
    s = jnp.sum(p, axis=-1, keepdims=True)
    out = jnp.einsum('nbhqk,nbkhd->nbqhd', p / s, vb.astype(F32))
    lse = jnp.transpose((m + jnp.log(s))[..., 0], (0, 1, 3, 2))

    def from_sub(t):
        t = t.reshape(b, dil, nb * blk, *t.shape[3:])[:, :, :Ls]
        return jnp.swapaxes(t, 1, 2).reshape(b, S, *t.shape[3:])

    return from_sub(out), from_sub(lse)


def dilated_attn_sample(q, kv_buf, k_new, v_new, dil, window, bias_g):
    b, T, H, hd = q.shape
    Lw = kv_buf.shape[1]
    kv_ext = jnp.concatenate([kv_buf.astype(k_new.dtype), jnp.stack([k_new, v_new], axis=2)], axis=1)
    nk = window // dil + 1
    jj = jnp.arange(nk)
    idx = Lw + jnp.arange(T)[:, None] - dil * jj[None, :]
    valid = idx >= 0
    g = jnp.take(kv_ext, jnp.clip(idx, 0, None), axis=1)
    bias = bias_g[rel_bucket(dil * jj)].T.astype(F32)
    logits = jnp.einsum('bthd,btjhd->bthj', q, g[:, :, :, 0], preferred_element_type=F32) / math.sqrt(hd)
    logits = jnp.where(valid[None, :, None, :], logits + bias[None, None], NEG)
    m = jnp.max(logits, axis=-1, keepdims=True)
    p = jnp.exp(logits - m)
    s = jnp.sum(p, axis=-1, keepdims=True)
    out = jnp.einsum('bthj,btjhd->bthd', p / s, g[:, :, :, 1].astype(F32))
    lse = (m + jnp.log(s))[..., 0]
    return out, lse, kv_ext[:, -min(window, Lw + T):]


def even_mixer(u, prm, e, conv_prev, ssm_h0, kv_bufs):
    b, t, _ = u.shape
    proj = u @ prm['in_proj_w'][e]
    z, xbc, dt_raw, q, k, v = jnp.split(proj, IN_SPLITS, axis=-1)
    xbc, conv_new = causal_conv(xbc, conv_prev, prm['conv_w'][e], prm['conv_b'][e])
    xs, Bm, Cm = jnp.split(xbc, [SSD_INNER, SSD_INNER + SSD_GROUPS * SSD_STATE], axis=-1)
    xs = xs.reshape(b, t, SSD_HEADS, SSD_HEAD_DIM)
    Bm = Bm.reshape(b, t, SSD_GROUPS, SSD_STATE)
    Cm = Cm.reshape(b, t, SSD_GROUPS, SSD_STATE)
    dt = jax.nn.softplus(dt_raw.astype(F32) + prm['dt_bias'][e].astype(F32))
    A = -jnp.exp(prm['a_log'][e].astype(F32))
    if ssm_h0 is None:
        y, h_new = ssd_chunked(xs, dt, A, Bm, Cm)
    else:
        y, h_new = ssd_recurrent(xs, dt, A, Bm, Cm, ssm_h0)
    y = (y + prm['d_skip'][e].astype(F32)[:, None] * xs.astype(F32)).reshape(b, t, SSD_INNER)
    y = rmsnorm(y * jax.nn.silu(z.astype(F32)), prm['ssd_norm_g'][e])
    q = rmsnorm(q.reshape(b, t, ATT_HEADS, ATT_HEAD_DIM), prm['q_norm_g'][e])
    k = rmsnorm(k.reshape(b, t, ATT_HEADS, ATT_HEAD_DIM), prm['k_norm_g'][e])
    v = v.reshape(b, t, ATT_HEADS, ATT_HEAD_DIM)
    outs, lses, new_bufs = [], [], []
    for gi in range(len(WINDOWS)):
        hs = slice(gi * ATT_HPP, (gi + 1) * ATT_HPP)
        bias_g = prm['rel_bias'][:, hs]
        if kv_bufs is None:
            o, l = dilated_attn_prompt(q[:, :, hs], k[:, :, hs], v[:, :, hs], DILATIONS[gi], WINDOWS[gi], bias_g)
            buf = jnp.stack([k[:, :, hs], v[:, :, hs]], axis=2)[:, -min(WINDOWS[gi], t):]
        else:
            o, l, buf = dilated_attn_sample(q[:, :, hs], kv_bufs[gi], k[:, :, hs], v[:, :, hs],
                                            DILATIONS[gi], WINDOWS[gi], bias_g)
        outs.append(o)
        lses.append(l)
        new_bufs.append(buf)
    wts = jax.nn.softmax(jnp.stack(lses, axis=0), axis=0)
    att = jnp.sum(wts[..., None] * jnp.stack(outs, axis=0), axis=0).reshape(b, t, ATT_OUT)
    mix = jnp.concatenate([y.astype(F32), att], axis=-1).astype(u.dtype) @ prm['out_proj_w'][e]
    return mix.astype(u.dtype), conv_new, h_new, new_bufs


def pool_mixer(u, prev, pos0, w_pool, scale):
    b, t, _ = u.shape
    ext = jnp.concatenate([prev.astype(u.dtype), u], axis=1)
    cs = jnp.pad(jnp.cumsum(ext.astype(F32), axis=1), ((0, 0), (1, 0), (0, 0)))
    pos = pos0 + jnp.arange(t)
    pooled = []
    for gi, w in enumerate(POOL_WINDOWS):
        ch = slice(gi * POOL_CH, (gi + 1) * POOL_CH)
        s = cs[:, POOL_MAX:POOL_MAX + t, ch] - cs[:, POOL_MAX - w:POOL_MAX - w + t, ch]
        cnt = jnp.minimum(w, pos + 1).astype(F32)[None, :, None]
        pooled.append(s / cnt - u[:, :, ch].astype(F32))
    pg = jnp.stack(pooled, axis=2)
    out = jnp.einsum('btgc,gcd->btgd', pg, w_pool.astype(F32)).reshape(b, t, D_MODEL) * scale.astype(F32)
    return out.astype(u.dtype), ext[:, -(POOL_MAX - 1):]


def trunk(x, c, prm, past):
    b, t, _ = x.shape
    ssm_l, conv_l, pool_l = [], [], []
    kv_l = [[] for _ in WINDOWS]
    for layer in range(DEPTH):
        mod = jax.nn.silu(c.astype(F32)) @ prm['ada_w'][layer].astype(F32) + prm['ada_b'][layer].astype(F32)
        sh1, sc1, g1, sh2, sc2, g2 = jnp.split(mod[:, None, :].astype(x.dtype), 6, axis=-1)
        u = rmsnorm(x, prm['norm_mix_g'][layer]) * (1 + sc1) + sh1
        if layer % 2 == 0:
            e = layer // 2
            if past is None:
                conv_prev = jnp.zeros((b, SSD_CONV - 1, CONV_DIM), x.dtype)
                h0, bufs = None, None
            else:
                conv_prev, h0, bufs = past[1][e], past[0][e], [cache[e] for cache in past[2]]
            mix, conv_new, h_new, bufs_new = even_mixer(u, prm, e, conv_prev, h0, bufs)
            ssm_l.append(h_new)
            conv_l.append(conv_new)
            for gi in range(len(WINDOWS)):
                kv_l[gi].append(bufs_new[gi])
        else:
            o = layer // 2
            if past is None:
                prev, pos0 = jnp.zeros((b, POOL_MAX - 1, D_MODEL), x.dtype), 0
            else:
                prev, pos0 = past[3][o], PAST_LEN
            mix, pool_new = pool_mixer(u, prev, pos0, prm['pool_w'][o], prm['pool_scale'][o])
            pool_l.append(pool_new)
        x = x + g1 * mix
        u = rmsnorm(x, prm['norm_mlp_g'][layer]) * (1 + sc2) + sh2
        hdn = jnp.square(jax.nn.relu(u @ prm['mlp_w1'][layer]))
        x = x + g2 * (hdn @ prm['mlp_w2'][layer])
    return (x, jnp.stack(ssm_l), jnp.stack(conv_l), jnp.stack(kv_l[0]), jnp.stack(kv_l[1]),
            jnp.stack(kv_l[2]), jnp.stack(pool_l))


def setup_inputs(seed: int = 0) -> dict:
    key = jax.random.key(seed)
    ks = jax.random.split(key, 32)

    def nrm(k, shape, s):
        return s * jax.random.normal(k, shape, F32)

    def kv_shape(w):
        return (N_EVEN, DEC_BATCH, min(w, PAST_LEN), 2, ATT_HPP, ATT_HEAD_DIM)

    dt0 = jnp.exp(jax.random.uniform(ks[20], (N_EVEN, SSD_HEADS), F32, math.log(1e-3), math.log(1e-1)))
    return {
        "x_prompt": nrm(ks[0], (BATCH, SEQ, D_MODEL), 1.0),
        "x_sample": nrm(ks[1], (DEC_BATCH, DEC_SEQ, D_MODEL), 1.0),
        "state_ssm": nrm(ks[2], (N_EVEN, DEC_BATCH, SSD_HEADS, SSD_HEAD_DIM, SSD_STATE), 0.5),
        "state_conv": nrm(ks[3], (N_EVEN, DEC_BATCH, SSD_CONV - 1, CONV_DIM), 1.0),
        "cache_kv_w128": nrm(ks[4], kv_shape(WINDOWS[0]), 1.0),
        "cache_kv_w512": nrm(ks[5], kv_shape(WINDOWS[1]), 1.0),
        "cache_kv_w2048": nrm(ks[6], kv_shape(WINDOWS[2]), 1.0),
        "state_pool": nrm(ks[7], (N_ODD, DEC_BATCH, POOL_MAX - 1, D_MODEL), 1.0),
        "c_prompt": nrm(ks[8], (BATCH, D_MODEL), 1.0),
        "c_sample": nrm(ks[9], (DEC_BATCH, D_MODEL), 1.0),
        "rel_bias": nrm(ks[10], (REL_BUCKETS, ATT_HEADS), 0.5),
        "ada_w": nrm(ks[11], (DEPTH, D_MODEL, 6 * D_MODEL), 0.5 * D_MODEL ** -0.5),
        "ada_b": nrm(ks[12], (DEPTH, 6 * D_MODEL), 0.02),
        "norm_mix_g": 1.0 + nrm(ks[13], (DEPTH, D_MODEL), 0.05),
        "norm_mlp_g": 1.0 + nrm(ks[14], (DEPTH, D_MODEL), 0.05),
        "in_proj_w": nrm(ks[15], (N_EVEN, D_MODEL, IN_COLS), D_MODEL ** -0.5),
        "conv_w": nrm(ks[16], (N_EVEN, SSD_CONV, CONV_DIM), SSD_CONV ** -0.5),
        "conv_b": nrm(ks[17], (N_EVEN, CONV_DIM), 0.02),
        "dt_bias": dt0 + jnp.log(-jnp.expm1(-dt0)),
        "a_log": jnp.log(jax.random.uniform(ks[18], (N_EVEN, SSD_HEADS), F32, 1.0, 16.0)),
        "d_skip": 1.0 + nrm(ks[19], (N_EVEN, SSD_HEADS), 0.1),
        "ssd_norm_g": 1.0 + nrm(ks[21], (N_EVEN, SSD_INNER), 0.05),
        "q_norm_g": 1.0 + nrm(ks[22], (N_EVEN, ATT_HEAD_DIM), 0.05),
        "k_norm_g": 1.0 + nrm(ks[23], (N_EVEN, ATT_HEAD_DIM), 0.05),
        "out_proj_w": nrm(ks[24], (N_EVEN, MIX_OUT, D_MODEL), MIX_OUT ** -0.5),
        "pool_w": nrm(ks[25], (N_ODD, POOL_GROUPS, POOL_CH, POOL_CH), POOL_CH ** -0.5),
        "pool_scale": 1.0 + nrm(ks[26], (N_ODD, D_MODEL), 0.1),
        "mlp_w1": nrm(ks[27], (DEPTH, D_MODEL, D_FF), D_MODEL ** -0.5),
        "mlp_w2": nrm(ks[28], (DEPTH, D_FF, D_MODEL), D_FF ** -0.5),
    }


def reference(x_prompt, x_sample, state_ssm, state_conv, cache_kv_w128, cache_kv_w512, cache_kv_w2048,
              state_pool, c_prompt, c_sample, rel_bias, ada_w, ada_b, norm_mix_g, norm_mlp_g, in_proj_w,
              conv_w, conv_b, dt_bias, a_log, d_skip, ssd_norm_g, q_norm_g, k_norm_g, out_proj_w,
              pool_w, pool_scale, mlp_w1, mlp_w2):
    prm = {
        "rel_bias": rel_bias, "ada_w": ada_w, "ada_b": ada_b, "norm_mix_g": norm_mix_g,
        "norm_mlp_g": norm_mlp_g, "in_proj_w": in_proj_w, "conv_w": conv_w, "conv_b": conv_b,
        "dt_bias": dt_bias, "a_log": a_log, "d_skip": d_skip, "ssd_norm_g": ssd_norm_g,
        "q_norm_g": q_norm_g, "k_norm_g": k_norm_g, "out_proj_w": out_proj_w, "pool_w": pool_w,
        "pool_scale": pool_scale, "mlp_w1": mlp_w1, "mlp_w2": mlp_w2,
    }
    y_prompt, p_ssm, p_conv, p_kv128, p_kv512, p_kv2048, p_pool = trunk(x_prompt, c_prompt, prm, None)
    past = (state_ssm, state_conv, [cache_kv_w128, cache_kv_w512, cache_kv_w2048], state_pool)
    y_sample, s_ssm, s_conv, s_kv128, s_kv512, s_kv2048, s_pool = trunk(x_sample, c_sample, prm, past)
    return (y_prompt, y_sample, p_ssm, p_conv, p_kv128, p_kv512, p_kv2048, p_pool,
            s_ssm, s_conv, s_kv128, s_kv512, s_kv2048, s_pool)
```

```python
import functools
import math

import jax
import jax.numpy as jnp
from jax import lax
from jax.experimental import pallas as pl
from jax.experimental.pallas import tpu as pltpu

F32 = jnp.float32
BF16 = jnp.bfloat16

D_MODEL = 2048
SEQ = 2048
N_PROMPT = 4
N_SAMPLE = 8
SAMPLE_ROWS = 16
PAST_LEN = 16384
SSD_HEADS = 32
SSD_HEAD_DIM = 64
SSD_INNER = SSD_HEADS * SSD_HEAD_DIM
SSD_GROUPS = 8
SSD_HEADS_PER_GROUP = SSD_HEADS // SSD_GROUPS
SSD_STATE = 128
SSD_CONV = 4
SSD_CHUNK = 128
CONV_DIM = SSD_INNER + 2 * SSD_GROUPS * SSD_STATE
ATT_HEAD_DIM = 128
WINDOWS = (128, 512, 2048)
DILATIONS = (1, 4, 16)
ATT_BLOCK = 128
ATT_HPP = 8
ATT_HEADS = ATT_HPP * len(WINDOWS)
ATT_QKV = ATT_HEADS * ATT_HEAD_DIM
ATT_OUT = ATT_HPP * ATT_HEAD_DIM
REL_BUCKETS = 32
REL_MAX_DIST = 2048
ZX_COLS = SSD_INNER + CONV_DIM
DT_COL0 = ZX_COLS
QKV_COL0 = ZX_COLS + SSD_HEADS
DT_PAD = 128
MIX_OUT = SSD_INNER + ATT_OUT
POOL_WINDOWS = (2, 4, 8, 16)
POOL_CH = D_MODEL // len(POOL_WINDOWS)
POOL_MAX = 16
D_FF = 4 * D_MODEL
EPS = 1e-6
NEG = -1e30
MIB = 1024 * 1024


def _params(semantics, vmem_mib):
    return pltpu.CompilerParams(dimension_semantics=semantics, vmem_limit_bytes=vmem_mib * MIB)


def _sigmoid(x):
    return 1.0 / (1.0 + jnp.exp(-x))


def _silu(x):
    return x * _sigmoid(x)


def _softplus(x):
    return jnp.maximum(x, 0.0) + jnp.log1p(jnp.exp(-jnp.abs(x)))


def _rms(x):
    return x * lax.rsqrt(jnp.mean(x * x, axis=-1, keepdims=True) + EPS)


def _normmod(x, g, sc, sh):
    return (_rms(x) * g) * (1.0 + sc) + sh


def _normmod_rows(u_ref, x_ref, g_ref, sc_ref, sh_ref):
    tm = u_ref.shape[0]
    rc = min(tm, 256)
    per_row = sc_ref.shape[1] != 1

    def body(i, carry):
        rows = pl.ds(pl.multiple_of(i * rc, rc), rc)
        sc = sc_ref[0, rows, :] if per_row else sc_ref[0]
        sh = sh_ref[0, rows, :] if per_row else sh_ref[0]
        u_ref[rows, :] = _normmod(x_ref[0, rows, :], g_ref[...], sc, sh).astype(BF16)
        return carry

    lax.fori_loop(0, tm // rc, body, 0)


def _bdot(a, b):
    return jnp.dot(a.astype(BF16), b.astype(BF16), preferred_element_type=F32)


def _split3(v):
    hi = v.astype(BF16)
    r = v - hi.astype(F32)
    mid = r.astype(BF16)
    lo = (r - mid.astype(F32)).astype(BF16)
    return hi, mid, lo


def _select_right(v, sel):
    hi, mid, lo = _split3(v)
    d = lambda a: jnp.dot(a, sel, preferred_element_type=F32)
    return d(hi) + d(mid) + d(lo)


def _select_left(sel, v):
    hi, mid, lo = _split3(v)
    d = lambda a: jnp.dot(sel, a, preferred_element_type=F32)
    return d(hi) + d(mid) + d(lo)


def _ada_kernel(c_ref, w_ref, b_ref, o_ref):
    o_ref[0] = _bdot(_silu(c_ref[...]), w_ref[0]) + b_ref[0]


def _ada(c_all, ada_w, ada_b):
    depth, _, n = ada_w.shape
    tn = 1024
    return pl.pallas_call(
        _ada_kernel,
        grid=(depth, n // tn),
        in_specs=[
            pl.BlockSpec((SAMPLE_ROWS, D_MODEL), lambda l, j: (0, 0)),
            pl.BlockSpec((1, D_MODEL, tn), lambda l, j: (l, 0, j)),
            pl.BlockSpec((1, 1, tn), lambda l, j: (l, 0, j)),
        ],
        out_specs=pl.BlockSpec((1, SAMPLE_ROWS, tn), lambda l, j: (l, 0, j)),
        out_shape=jax.ShapeDtypeStruct((depth, SAMPLE_ROWS, n), F32),
        compiler_params=_params(("arbitrary", "arbitrary"), 40),
    )(c_all, ada_w, ada_b.reshape(depth, 1, n))


def _inproj_kernel(*refs, has_dt):
    if has_dt:
        x_ref, g_ref, sc_ref, sh_ref, w_ref, wdt_ref, dtb_ref, o_ref, dt_ref, u_ref = refs
    else:
        x_ref, g_ref, sc_ref, sh_ref, w_ref, o_ref, u_ref = refs

    @pl.when(pl.program_id(2) == 0)
    def _():
        _normmod_rows(u_ref, x_ref, g_ref, sc_ref, sh_ref)
        if has_dt:
            dt_ref[0] = _softplus(_bdot(u_ref[...], wdt_ref[...]) + dtb_ref[...])

    o_ref[0] = _bdot(u_ref[...], w_ref[...])


def _inproj(x, g, sc, sh, w, n_cols, tm, tn, dt_w=None, dt_b=None):
    nb, t, d = x.shape
    r = sc.shape[1]
    has_dt = dt_w is not None
    in_specs = [
        pl.BlockSpec((1, tm, d), lambda b, i, j: (b, i, 0)),
        pl.BlockSpec((1, d), lambda b, i, j: (0, 0)),
        pl.BlockSpec((1, r if r == 1 else tm, d), lambda b, i, j: (b, 0 if r == 1 else i, 0)),
        pl.BlockSpec((1, r if r == 1 else tm, d), lambda b, i, j: (b, 0 if r == 1 else i, 0)),
        pl.BlockSpec((d, tn), lambda b, i, j: (0, j)),
    ]
    args = [x, g, sc, sh, w]
    out_specs = [pl.BlockSpec((1, tm, tn), lambda b, i, j: (b, i, j))]
    out_shape = [jax.ShapeDtypeStruct((nb, t, n_cols), F32)]
    if has_dt:
        in_specs += [pl.BlockSpec((d, DT_PAD), lambda b, i, j: (0, 0)),
                     pl.BlockSpec((1, DT_PAD), lambda b, i, j: (0, 0))]
        args += [dt_w, dt_b]
        out_specs.append(pl.BlockSpec((1, tm, DT_PAD), lambda b, i, j: (b, i, 0)))
        out_shape.append(jax.ShapeDtypeStruct((nb, t, DT_PAD), F32))
    return pl.pallas_call(
        functools.partial(_inproj_kernel, has_dt=has_dt),
        grid=(nb, t // tm, n_cols // tn),
        in_specs=in_specs,
        out_specs=out_specs,
        out_shape=out_shape,
        scratch_shapes=[pltpu.VMEM((tm, d), BF16)],
        compiler_params=_params(("arbitrary", "arbitrary", "arbitrary"), 48),
    )(*args)


def _conv_kernel(x_ref, w_ref, b_ref, o_ref, ext_ref, *, tt):
    @pl.when(pl.program_id(2) == 0)
    def _():
        ext_ref[0:8, :] = jnp.zeros((8, ext_ref.shape[1]), F32)

    x = x_ref[0]
    ext_ref[8:8 + tt, :] = x
    w = w_ref[...]
    acc = b_ref[...] + w[SSD_CONV - 1:SSD_CONV] * x
    for k in range(1, SSD_CONV):
        acc = acc + w[SSD_CONV - 1 - k:SSD_CONV - k] * ext_ref[pl.ds(8 - k, tt), :]
    o_ref[0] = _silu(acc)
    ext_ref[0:8, :] = x[tt - 8:tt]


def _conv_prompt(zx, conv_w, conv_b):
    nb, t, _ = zx.shape
    tt, tc = 512, 512
    col0 = SSD_INNER // tc
    return pl.pallas_call(
        functools.partial(_conv_kernel, tt=tt),
        grid=(nb, CONV_DIM // tc, t // tt),
        in_specs=[
            pl.BlockSpec((1, tt, tc), lambda b, c, i: (b, i, col0 + c)),
            pl.BlockSpec((SSD_CONV, tc), lambda b, c, i: (0, c)),
            pl.BlockSpec((1, tc), lambda b, c, i: (0, c)),
        ],
        out_specs=pl.BlockSpec((1, tt, tc), lambda b, c, i: (b, i, c)),
        out_shape=jax.ShapeDtypeStruct((nb, t, CONV_DIM), F32),
        scratch_shapes=[pltpu.VMEM((tt + 8, tc), F32)],
        compiler_params=_params(("arbitrary", "arbitrary", "arbitrary"), 32),
    )(zx, conv_w, conv_b)


def _ssd_kernel(xs_ref, b_ref, c_ref, dt_ref, z_ref, alog_ref, dskip_ref, g_ref,
                y_ref, hout_ref, h_ref, yacc_ref):
    q = SSD_CHUNK
    gw = SSD_HEADS_PER_GROUP * SSD_HEAD_DIM
    chunk = pl.program_id(1)

    @pl.when(chunk == 0)
    def _():
        h_ref[...] = jnp.zeros(h_ref.shape, F32)

    row = lax.broadcasted_iota(jnp.int32, (q, q), 0)
    col = lax.broadcasted_iota(jnp.int32, (q, q), 1)
    causal = row >= col
    tri = jnp.where(causal, 1.0, 0.0).astype(BF16)
    head_of_lane = lax.broadcasted_iota(jnp.int32, (DT_PAD, SSD_INNER), 1) // SSD_HEAD_DIM
    expand = jnp.where(head_of_lane == lax.broadcasted_iota(jnp.int32, (DT_PAD, SSD_INNER), 0), 1.0, 0.0).astype(BF16)

    dt = dt_ref[0]
    a = dt * (-jnp.exp(alog_ref[...]))
    acum = _select_left(tri, a)
    acum_t = acum.T
    dt_t = dt.T
    last = acum[q - 1:q, :]
    decay_end_x = _select_right(jnp.exp(last - acum) * dt, expand)
    eacum_x = _select_right(jnp.exp(acum), expand)
    chunk_decay_x = _select_right(jnp.broadcast_to(jnp.exp(last), (8, DT_PAD)), expand)[0:1]

    xs = xs_ref[0]
    xw = (xs * decay_end_x).astype(BF16)
    xsb = xs.astype(BF16)
    head_in_group = lax.broadcasted_iota(jnp.int32, (q, gw), 1) // SSD_HEAD_DIM

    for g in range(SSD_GROUPS):
        ns = slice(g * SSD_STATE, (g + 1) * SSD_STATE)
        cs = slice(g * gw, (g + 1) * gw)
        cg = c_ref[0, :, ns].astype(BF16)
        bg_t = b_ref[0, :, ns].T.astype(BF16)
        cb = jnp.dot(cg, bg_t, preferred_element_type=F32)
        h_prev = h_ref[:, cs]
        y_g = jnp.dot(cg, h_prev.astype(BF16), preferred_element_type=F32) * eacum_x[:, cs]
        h_ref[:, cs] = h_prev * chunk_decay_x[:, cs] + jnp.dot(bg_t, xw[:, cs], preferred_element_type=F32)
        xg = xsb[:, cs]
        for e in range(SSD_HEADS_PER_GROUP):
            hd = g * SSD_HEADS_PER_GROUP + e
            seg = acum[:, hd:hd + 1] - acum_t[hd:hd + 1, :]
            lw = jnp.where(causal, jnp.exp(jnp.where(causal, seg, 0.0)), 0.0) * dt_t[hd:hd + 1, :]
            yd = jnp.dot((cb * lw).astype(BF16), xg, preferred_element_type=F32)
            y_g = y_g + jnp.where(head_in_group == e, yd, 0.0)
        yacc_ref[:, cs] = y_g

    y = yacc_ref[...] + dskip_ref[...] * xs
    y = y * _silu(z_ref[0])
    y_ref[0] = (_rms(y) * g_ref[...]).astype(BF16)

    @pl.when(chunk == pl.num_programs(1) - 1)
    def _():
        hout_ref[0] = h_ref[...].T


def _ssd_prompt(xbc, dt, zx, a_log_pad, dskip_x, norm_g):
    nb, t, _ = xbc.shape
    q = SSD_CHUNK
    bc_w = SSD_GROUPS * SSD_STATE
    return pl.pallas_call(
        _ssd_kernel,
        grid=(nb, t // q),
        in_specs=[
            pl.BlockSpec((1, q, SSD_INNER), lambda b, c: (b, c, 0)),
            pl.BlockSpec((1, q, bc_w), lambda b, c: (b, c, SSD_INNER // bc_w)),
            pl.BlockSpec((1, q, bc_w), lambda b, c: (b, c, SSD_INNER // bc_w + 1)),
            pl.BlockSpec((1, q, DT_PAD), lambda b, c: (b, c, 0)),
            pl.BlockSpec((1, q, SSD_INNER), lambda b, c: (b, c, 0)),
            pl.BlockSpec((1, DT_PAD), lambda b, c: (0, 0)),
            pl.BlockSpec((1, SSD_INNER), lambda b, c: (0, 0)),
            pl.BlockSpec((1, SSD_INNER), lambda b, c: (0, 0)),
        ],
        out_specs=[
            pl.BlockSpec((1, q, SSD_INNER), lambda b, c: (b, c, 0)),
            pl.BlockSpec((1, SSD_INNER, SSD_STATE), lambda b, c: (b, 0, 0)),
        ],
        out_shape=[
            jax.ShapeDtypeStruct((nb, t, SSD_INNER), BF16),
            jax.ShapeDtypeStruct((nb, SSD_INNER, SSD_STATE), F32),
        ],
        scratch_shapes=[pltpu.VMEM((SSD_STATE, SSD_INNER), F32), pltpu.VMEM((q, SSD_INNER), F32)],
        compiler_params=_params(("arbitrary", "arbitrary"), 40),
    )(xbc, xbc, xbc, dt, zx, a_log_pad, dskip_x, norm_g)


def _attn_kernel(q0, q1, q2, k0, k1, k2, v0, v1, v2, bias0, bias1, bias2, qg_ref, kg_ref,
                 att_ref, kv0_ref, kv1_ref, kv2_ref, qs, ks, os_, ls):
    blk = ATT_BLOCK
    scale = 1.0 / math.sqrt(ATT_HEAD_DIM)
    sel = pl.program_id(2)
    qrefs, krefs, vrefs, brefs = (q0, q1, q2), (k0, k1, k2), (v0, v1, v2), (bias0, bias1, bias2)
    kvrefs = (kv0_ref, kv1_ref, kv2_ref)
    n_chunks = SEQ // blk

    @pl.when(sel == 0)
    def _():
        row = lax.broadcasted_iota(jnp.int32, (blk, blk), 0)
        col = lax.broadcasted_iota(jnp.int32, (blk, blk), 1)
        prev_ok = row <= col
        cur_ok = row >= col

        for g in range(len(WINDOWS)):
            def norm_body(i, carry, g=g):
                rows = pl.ds(pl.multiple_of(i * blk, blk), blk)
                qs[g, rows, :] = _rms(qrefs[g][0, rows, :]) * qg_ref[...]
                ks[g, rows, :] = _rms(krefs[g][0, rows, :]) * kg_ref[...]
                return carry
            lax.fori_loop(0, n_chunks, norm_body, 0)

        for g, dil in enumerate(DILATIONS):
            nblk = SEQ // (blk * dil)
            bias_prev = brefs[g][0, :, 0:blk]
            bias_cur = brefs[g][0, :, blk:2 * blk]
            for r in range(dil):
                for n in range(nblk):
                    def rows_of(start):
                        return pl.ds(start, blk) if dil == 1 else pl.ds(start, blk, stride=dil)
                    rows = rows_of(r + n * blk * dil)
                    qb = qs[g, rows, :].astype(BF16)
                    kc = ks[g, rows, :].astype(BF16)
                    vc = vrefs[g][0, rows, :].astype(BF16)
                    lc = lax.dot_general(qb, kc, (((1,), (1,)), ((), ())), preferred_element_type=F32)
                    lc = jnp.where(cur_ok, lc * scale + bias_cur, NEG)
                    m = jnp.max(lc, axis=-1, keepdims=True)
                    if n > 0:
                        prows = rows_of(r + (n - 1) * blk * dil)
                        kp = ks[g, prows, :].astype(BF16)
                        vp = vrefs[g][0, prows, :].astype(BF16)
                        lp = lax.dot_general(qb, kp, (((1,), (1,)), ((), ())), preferred_element_type=F32)
                        lp = jnp.where(prev_ok, lp * scale + bias_prev, NEG)
                        m = jnp.maximum(m, jnp.max(lp, axis=-1, keepdims=True))
                    pc = jnp.exp(lc - m)
                    s = jnp.sum(pc, axis=-1, keepdims=True)
                    o = jnp.dot(pc.astype(BF16), vc, preferred_element_type=F32)
                    if n > 0:
                        pp = jnp.exp(lp - m)
                        s = s + jnp.sum(pp, axis=-1, keepdims=True)
                        o = o + jnp.dot(pp.astype(BF16), vp, preferred_element_type=F32)
                    os_[g, rows, :] = o / s
                    ls[g, rows, :] = jnp.broadcast_to(m + jnp.log(s), (blk, ATT_HEAD_DIM))

        def mix_body(i, carry):
            rows = pl.ds(pl.multiple_of(i * blk, blk), blk)
            l0, l1, l2 = ls[0, rows, :], ls[1, rows, :], ls[2, rows, :]
            mx = jnp.maximum(jnp.maximum(l0, l1), l2)
            e0, e1, e2 = jnp.exp(l0 - mx), jnp.exp(l1 - mx), jnp.exp(l2 - mx)
            num = e0 * os_[0, rows, :] + e1 * os_[1, rows, :] + e2 * os_[2, rows, :]
            att_ref[0, rows, :] = (num / (e0 + e1 + e2)).astype(BF16)
            return carry
        lax.fori_loop(0, n_chunks, mix_body, 0)

        for g, w in enumerate(WINDOWS):
            kvrefs[g][0] = ks[g, SEQ - w:SEQ, :]

    @pl.when(sel == 1)
    def _():
        for g, w in enumerate(WINDOWS):
            kvrefs[g][0] = vrefs[g][0, SEQ - w:SEQ, :]


def _attn_prompt(qkv, bias, q_norm_g, k_norm_g):
    nb = qkv.shape[0]
    hd = ATT_HEAD_DIM

    def head_spec(part, g):
        return pl.BlockSpec((1, SEQ, hd), lambda b, h, s: (b, 0, part * ATT_HEADS + g * ATT_HPP + h))

    in_specs = [head_spec(part, g) for part in range(3) for g in range(3)]
    in_specs += [pl.BlockSpec((1, ATT_BLOCK, 2 * ATT_BLOCK), functools.partial(lambda b, h, s, g: (g * ATT_HPP + h, 0, 0), g=g))
                 for g in range(3)]
    in_specs += [pl.BlockSpec((1, hd), lambda b, h, s: (0, 0))] * 2
    out_specs = [pl.BlockSpec((1, SEQ, hd), lambda b, h, s: (b, 0, h))]
    out_shape = [jax.ShapeDtypeStruct((nb, SEQ, ATT_OUT), BF16)]
    for w in WINDOWS:
        out_specs.append(pl.BlockSpec((1, w, hd), lambda b, h, s: (b, 0, s * ATT_HPP + h)))
        out_shape.append(jax.ShapeDtypeStruct((nb, w, 2 * ATT_OUT), F32))
    return pl.pallas_call(
        _attn_kernel,
        grid=(nb, ATT_HPP, 2),
        in_specs=in_specs,
        out_specs=out_specs,
        out_shape=out_shape,
        scratch_shapes=[pltpu.VMEM((3, SEQ, hd), F32)] * 4,
        compiler_params=_params(("arbitrary", "arbitrary", "arbitrary"), 48),
    )(*([qkv] * 9), bias, bias, bias, q_norm_g, k_norm_g)


def _outproj_kernel(y_ref, a_ref, wy_ref, wa_ref, x_ref, g1_ref, o_ref):
    mix = _bdot(y_ref[0], wy_ref[0]) + _bdot(a_ref[0], wa_ref[0])
    o_ref[0] = x_ref[0] + g1_ref[0] * mix


def _outproj(y, att, w, x, g1, tm, tn):
    nb, t, d = x.shape
    r = g1.shape[1]
    return pl.pallas_call(
        _outproj_kernel,
        grid=(nb, t // tm, d // tn),
        in_specs=[
            pl.BlockSpec((1, tm, SSD_INNER), lambda b, i, j: (b, i, 0)),
            pl.BlockSpec((1, tm, ATT_OUT), lambda b, i, j: (b, i, 0)),
            pl.BlockSpec((1, SSD_INNER, tn), lambda b, i, j: (0, 0, j)),
            pl.BlockSpec((1, ATT_OUT, tn), lambda b, i, j: (0, SSD_INNER // ATT_OUT, j)),
            pl.BlockSpec((1, tm, tn), lambda b, i, j: (b, i, j)),
            pl.BlockSpec((1, r if r == 1 else tm, tn), lambda b, i, j: (b, 0 if r == 1 else i, j)),
        ],
        out_specs=pl.BlockSpec((1, tm, tn), lambda b, i, j: (b, i, j)),
        out_shape=jax.ShapeDtypeStruct((nb, t, d), F32),
        compiler_params=_params(("arbitrary", "arbitrary", "arbitrary"), 48),
    )(y, att, w, w, x, g1)


def _mlp_kernel(x_ref, g_ref, sc_ref, sh_ref, g2_ref, w1_ref, w2_ref, o_ref, u_ref):
    f = pl.program_id(2)

    @pl.when(f == 0)
    def _():
        _normmod_rows(u_ref, x_ref, g_ref, sc_ref, sh_ref)
        o_ref[0] = jnp.zeros(o_ref.shape[1:], F32)

    h = jnp.maximum(_bdot(u_ref[...], w1_ref[0]), 0.0)
    o_ref[0] += _bdot(h * h, w2_ref[0])

    @pl.when(f == pl.num_programs(2) - 1)
    def _():
        o_ref[0] = x_ref[0] + g2_ref[0] * o_ref[0]


def _mlp(x, g, sc, sh, g2, w1, w2, layer, tm, tf):
    nb, t, d = x.shape
    r = sc.shape[1]
    mod_spec = pl.BlockSpec((1, r if r == 1 else tm, d), lambda b, i, f: (b, 0 if r == 1 else i, 0))
    return pl.pallas_call(
        _mlp_kernel,
        grid=(nb, t // tm, D_FF // tf),
        in_specs=[
            pl.BlockSpec((1, tm, d), lambda b, i, f: (b, i, 0), pipeline_mode=pl.Buffered(1)),
            pl.BlockSpec((1, d), lambda b, i, f: (0, 0)),
            mod_spec, mod_spec, mod_spec,
            pl.BlockSpec((1, d, tf), lambda b, i, f: (layer, 0, f)),
            pl.BlockSpec((1, tf, d), lambda b, i, f: (layer, f, 0)),
        ],
        out_specs=pl.BlockSpec((1, tm, d), lambda b, i, f: (b, i, 0)),
        out_shape=jax.ShapeDtypeStruct((nb, t, d), F32),
        scratch_shapes=[pltpu.VMEM((tm, d), BF16)],
        compiler_params=_params(("arbitrary", "arbitrary", "arbitrary"), 56),
    )(x, g, sc, sh, g2, w1, w2)


def _pool_kernel(x_ref, g_ref, sc_ref, sh_ref, g1_ref, w_ref, scale_ref, o_ref, ulast_ref, ext_ref, *, tt):
    halo = POOL_MAX
    ti = pl.program_id(1)

    @pl.when(ti == 0)
    def _():
        ext_ref[0:halo, :] = jnp.zeros((halo, D_MODEL), F32)

    x = x_ref[0]
    u = _normmod(x, g_ref[...], sc_ref[0], sh_ref[0])
    ext_ref[halo:halo + tt, :] = u
    pos = ti * tt + lax.broadcasted_iota(jnp.int32, (tt, 1), 0)
    for gi, w in enumerate(POOL_WINDOWS):
        ch = slice(gi * POOL_CH, (gi + 1) * POOL_CH)
        s = u[:, ch]
        for k in range(1, w):
            s = s + ext_ref[pl.ds(halo - k, tt), ch]
        cnt = jnp.minimum(w, pos + 1).astype(F32)
        pooled = s / cnt - u[:, ch]
        mix = _bdot(pooled, w_ref[gi]) * scale_ref[:, ch]
        o_ref[0, :, ch] = x[:, ch] + g1_ref[0, :, ch] * mix
    ulast_ref[0] = u[tt - halo:tt]
    ext_ref[0:halo, :] = u[tt - halo:tt]


def _pool_prompt(x, g, sc, sh, g1, pool_w, pool_scale):
    nb, t, d = x.shape
    tt = 512
    mod_spec = pl.BlockSpec((1, 1, d), lambda b, i: (b, 0, 0))
    return pl.pallas_call(
        functools.partial(_pool_kernel, tt=tt),
        grid=(nb, t // tt),
        in_specs=[
            pl.BlockSpec((1, tt, d), lambda b, i: (b, i, 0)),
            pl.BlockSpec((1, d), lambda b, i: (0, 0)),
            mod_spec, mod_spec, mod_spec,
            pl.BlockSpec((len(POOL_WINDOWS), POOL_CH, POOL_CH), lambda b, i: (0, 0, 0)),
            pl.BlockSpec((1, d), lambda b, i: (0, 0)),
        ],
        out_specs=[
            pl.BlockSpec((1, tt, d), lambda b, i: (b, i, 0)),
            pl.BlockSpec((1, POOL_MAX, d), lambda b, i: (b, 0, 0)),
        ],
        out_shape=[jax.ShapeDtypeStruct((nb, t, d), F32), jax.ShapeDtypeStruct((nb, POOL_MAX, d), F32)],
        scratch_shapes=[pltpu.VMEM((tt + POOL_MAX, d), F32)],
        compiler_params=_params(("arbitrary", "arbitrary"), 48),
    )(x, g, sc, sh, g1, pool_w, pool_scale)


def _sconv_kernel(zx_ref, prev_ref, w_ref, b_ref, o_ref):
    w = w_ref[...]
    acc = b_ref[...] + w[SSD_CONV - 1:SSD_CONV] * zx_ref[:, SSD_INNER:ZX_COLS]
    for k in range(SSD_CONV - 1):
        acc = acc + w[k:k + 1] * prev_ref[k]
    o_ref[...] = _silu(acc)


def _conv_sample(zx, prev_t, conv_w, conv_b):
    return pl.pallas_call(
        _sconv_kernel,
        out_shape=jax.ShapeDtypeStruct((SAMPLE_ROWS, CONV_DIM), F32),
        compiler_params=pltpu.CompilerParams(vmem_limit_bytes=32 * MIB),
    )(zx, prev_t, conv_w, conv_b)


def _sstep_kernel(h0_ref, x_ref, dt_ref, b_ref, c_ref, alog_ref, h_ref, y_ref):
    dt = dt_ref[0]
    h = jnp.exp(dt * (-jnp.exp(alog_ref[...]))) * h0_ref[0] + (dt * x_ref[0]) * b_ref[0]
    h_ref[0] = h
    y_ref[0] = jnp.sum(h * c_ref[0], axis=-1, keepdims=True)


def _ssd_step(h0, x, dt, bh, ch, a_log):
    nb = h0.shape[0]
    hp = (SSD_HEADS, SSD_HEAD_DIM)
    return pl.pallas_call(
        _sstep_kernel,
        grid=(nb,),
        in_specs=[
            pl.BlockSpec((1, *hp, SSD_STATE), lambda b: (b, 0, 0, 0)),
            pl.BlockSpec((1, *hp, 1), lambda b: (b, 0, 0, 0)),
            pl.BlockSpec((1, SSD_HEADS, 1, 1), lambda b: (b, 0, 0, 0)),
            pl.BlockSpec((1, SSD_HEADS, 1, SSD_STATE), lambda b: (b, 0, 0, 0)),
            pl.BlockSpec((1, SSD_HEADS, 1, SSD_STATE), lambda b: (b, 0, 0, 0)),
            pl.BlockSpec((SSD_HEADS, 1, 1), lambda b: (0, 0, 0)),
        ],
        out_specs=[
            pl.BlockSpec((1, *hp, SSD_STATE), lambda b: (b, 0, 0, 0)),
            pl.BlockSpec((1, *hp, 1), lambda b: (b, 0, 0, 0)),
        ],
        out_shape=[jax.ShapeDtypeStruct((nb, *hp, SSD_STATE), F32), jax.ShapeDtypeStruct((nb, *hp, 1), F32)],
        compiler_params=_params(("arbitrary",), 32),
    )(h0, x, dt, bh, ch, a_log)


def _sgate_kernel(y_ref, xs_ref, z_ref, dskip_ref, g_ref, o_ref):
    y = (y_ref[...] + dskip_ref[...] * xs_ref[...]) * _silu(z_ref[...])
    o_ref[...] = _rms(y) * g_ref[...]


def _gate_sample(y, xs, z, dskip_x, norm_g):
    return pl.pallas_call(
        _sgate_kernel,
        out_shape=jax.ShapeDtypeStruct(y.shape, F32),
    )(y, xs, z, dskip_x, norm_g)


def _sattn_kernel(qkv_ref, c0_ref, c1_ref, c2_ref, bbuf_ref, bnew_ref, qg_ref, kg_ref,
                  att_ref, knew_ref, vnew_ref):
    scale = 1.0 / math.sqrt(ATT_HEAD_DIM)
    caches = (c0_ref, c1_ref, c2_ref)
    outs, lses = [], []
    for g in range(len(WINDOWS)):
        hs = g * ATT_HPP
        q = _rms(qkv_ref[0, hs:hs + ATT_HPP, :]) * qg_ref[...]
        k = _rms(qkv_ref[0, ATT_HEADS + hs:ATT_HEADS + hs + ATT_HPP, :]) * kg_ref[...]
        v = qkv_ref[0, 2 * ATT_HEADS + hs:2 * ATT_HEADS + hs + ATT_HPP, :]
        knew_ref[0, g] = k
        vnew_ref[0, g] = v
        kbuf = caches[g][:, 0]
        vbuf = caches[g][:, 1]
        lb = jnp.sum(kbuf * q[None], axis=-1, keepdims=True) * scale + bbuf_ref[g]
        ln = jnp.sum(k * q, axis=-1, keepdims=True) * scale + bnew_ref[g]
        m = jnp.maximum(jnp.max(lb, axis=0), ln)
        pb = jnp.exp(lb - m[None])
        pn = jnp.exp(ln - m)
        s = jnp.sum(pb, axis=0) + pn
        outs.append((jnp.sum(pb * vbuf, axis=0) + pn * v) / s)
        lses.append(m + jnp.log(s))
    mx = jnp.maximum(jnp.maximum(lses[0], lses[1]), lses[2])
    es = [jnp.exp(l - mx) for l in lses]
    att_ref[0] = (es[0] * outs[0] + es[1] * outs[1] + es[2] * outs[2]) / (es[0] + es[1] + es[2])


def _attn_sample(qkv, caches, bias_buf, bias_new, q_norm_g, k_norm_g):
    nb = qkv.shape[0]
    hd = ATT_HEAD_DIM
    in_specs = [pl.BlockSpec((1, 3 * ATT_HEADS, hd), lambda b: (b, 0, 0))]
    for _ in caches:
        in_specs.append(pl.BlockSpec((None, ATT_BLOCK, None, 2, ATT_HPP, hd), lambda b: (b, 0, 0, 0, 0, 0)))
    in_specs += [
        pl.BlockSpec((3, ATT_BLOCK, ATT_HPP, 1), lambda b: (0, 0, 0, 0)),
        pl.BlockSpec((3, ATT_HPP, 1), lambda b: (0, 0, 0)),
        pl.BlockSpec((1, hd), lambda b: (0, 0)),
        pl.BlockSpec((1, hd), lambda b: (0, 0)),
    ]
    return pl.pallas_call(
        _sattn_kernel,
        grid=(nb,),
        in_specs=in_specs,
        out_specs=[
            pl.BlockSpec((1, ATT_HPP, hd), lambda b: (b, 0, 0)),
            pl.BlockSpec((1, 3, ATT_HPP, hd), lambda b: (b, 0, 0, 0)),
            pl.BlockSpec((1, 3, ATT_HPP, hd), lambda b: (b, 0, 0, 0)),
        ],
        out_shape=[
            jax.ShapeDtypeStruct((nb, ATT_HPP, hd), F32),
            jax.ShapeDtypeStruct((nb, 3, ATT_HPP, hd), F32),
            jax.ShapeDtypeStruct((nb, 3, ATT_HPP, hd), F32),
        ],
        compiler_params=_params(("arbitrary",), 32),
    )(qkv, *caches, bias_buf, bias_new, q_norm_g, k_norm_g)


def _kvshift_kernel(c0_ref, c1_ref, c2_ref, new_ref, o0_ref, o1_ref, o2_ref, sem):
    copies = []
    for ci, (c_ref, o_ref) in enumerate(((c0_ref, o0_ref), (c1_ref, o1_ref), (c2_ref, o2_ref))):
        w = c_ref.shape[2]
        for b in range(N_SAMPLE):
            copies.append(pltpu.make_async_copy(
                c_ref.at[0, b, pl.ds(1, w - 1)], o_ref.at[0, b, pl.ds(0, w - 1)], sem.at[len(copies)]))
        copies.append(pltpu.make_async_copy(
            new_ref.at[ci], o_ref.at[0, :, pl.ds(w - 1, 1)], sem.at[len(copies)]))
    for cp in copies:
        cp.start()
    for cp in copies:
        cp.wait()


def _kv_shift(caches, new_rows):
    any_spec = pl.BlockSpec(memory_space=pl.ANY)
    return pl.pallas_call(
        _kvshift_kernel,
        in_specs=[any_spec] * 3 + [pl.BlockSpec(memory_space=pltpu.VMEM)],
        out_specs=[any_spec] * 3,
        out_shape=[jax.ShapeDtypeStruct(c.shape, c.dtype) for c in caches],
        scratch_shapes=[pltpu.SemaphoreType.DMA((3 * (N_SAMPLE + 1),))],
    )(*caches, new_rows)


def _spool_kernel(x_ref, g_ref, sc_ref, sh_ref, g1_ref, prev_ref, w_ref, scale_ref, o_ref, u_ref):
    x = x_ref[...]
    u = _normmod(x, g_ref[...], sc_ref[...], sh_ref[...])
    u_ref[...] = u
    for gi, w in enumerate(POOL_WINDOWS):
        ch = slice(gi * POOL_CH, (gi + 1) * POOL_CH)
        s = u[:, ch]
        for k in range(1, w):
            s = s + prev_ref[POOL_MAX - 1 - k, :, ch]
        pooled = s / float(min(w, PAST_LEN + 1)) - u[:, ch]
        mix = _bdot(pooled, w_ref[gi]) * scale_ref[:, ch]
        o_ref[:, ch] = x[:, ch] + g1_ref[:, ch] * mix


def _pool_sample(x, g, sc, sh, g1, prev_t, pool_w, pool_scale):
    return pl.pallas_call(
        _spool_kernel,
        out_shape=[jax.ShapeDtypeStruct(x.shape, F32), jax.ShapeDtypeStruct(x.shape, F32)],
        compiler_params=pltpu.CompilerParams(vmem_limit_bytes=32 * MIB),
    )(x, g, sc, sh, g1, prev_t, pool_w, pool_scale)


def _rel_bucket(dist):
    max_exact = REL_BUCKETS // 2
    n = jnp.maximum(dist, 1).astype(F32)
    large = max_exact + (jnp.log(n / max_exact) / math.log(REL_MAX_DIST / max_exact)
                         * (REL_BUCKETS - max_exact)).astype(jnp.int32)
    large = jnp.minimum(large, REL_BUCKETS - 1)
    return jnp.where(dist < max_exact, dist, large)


def _prompt_bias_table(rel_bias):
    blk = ATT_BLOCK
    j = blk + jnp.arange(blk)[:, None] - jnp.arange(2 * blk)[None, :]
    tables = []
    for gi, dil in enumerate(DILATIONS):
        tbl = rel_bias[:, gi * ATT_HPP:(gi + 1) * ATT_HPP][_rel_bucket(jnp.clip(j, 0, blk) * dil)]
        tables.append(jnp.transpose(tbl, (2, 0, 1)))
    return jnp.concatenate(tables, axis=0).astype(F32)


def _sample_bias_table(rel_bias):
    blk = ATT_BLOCK
    bufs, news = [], []
    for gi, dil in enumerate(DILATIONS):
        tbl = rel_bias[:, gi * ATT_HPP:(gi + 1) * ATT_HPP][_rel_bucket(dil * jnp.arange(blk + 1))]
        bufs.append(tbl[blk:0:-1])
        news.append(tbl[0])
    return jnp.stack(bufs)[..., None].astype(F32), jnp.stack(news)[..., None].astype(F32)


def _pad_rows(a, rows, axis=0):
    pad = [(0, 0)] * a.ndim
    pad[axis] = (0, rows - a.shape[axis])
    return jnp.pad(a, pad)


def _mlp_tiles(t):
    return (1024, 256) if t >= 1024 else (t, 512)


def kernel(x_prompt, x_sample, state_ssm, state_conv, cache_kv_w128, cache_kv_w512, cache_kv_w2048, state_pool, c_prompt, c_sample, rel_bias, ada_w, ada_b, norm_mix_g, norm_mlp_g, in_proj_w, conv_w, conv_b, dt_bias, a_log, d_skip, ssd_norm_g, q_norm_g, k_norm_g, out_proj_w, pool_w, pool_scale, mlp_w1, mlp_w2):
    d = D_MODEL
    c_all = _pad_rows(jnp.concatenate([c_prompt, c_sample], axis=0), SAMPLE_ROWS)
    mod = _ada(c_all, ada_w, ada_b)

    def mods(layer):
        parts = [mod[layer, :, i * d:(i + 1) * d] for i in range(6)]
        prompt = [p[:N_PROMPT, None, :] for p in parts]
        sample = [_pad_rows(p[N_PROMPT:N_PROMPT + N_SAMPLE], SAMPLE_ROWS)[None] for p in parts]
        return prompt, sample

    w_in = in_proj_w.reshape(d, -1)
    w_dt = jnp.pad(w_in[:, DT_COL0:QKV_COL0], ((0, 0), (0, DT_PAD - SSD_HEADS)))
    b_dt = jnp.pad(dt_bias[0], (0, DT_PAD - SSD_HEADS)).reshape(1, DT_PAD)
    w_qkv = w_in[:, QKV_COL0:].astype(BF16)
    a_log_pad = jnp.pad(a_log[0], (0, DT_PAD - SSD_HEADS)).reshape(1, DT_PAD)
    dskip_x = jnp.repeat(d_skip[0], SSD_HEAD_DIM).reshape(1, SSD_INNER)
    ssd_g = ssd_norm_g.reshape(1, SSD_INNER)
    qg, kg = q_norm_g.reshape(1, ATT_HEAD_DIM), k_norm_g.reshape(1, ATT_HEAD_DIM)
    cw, cb = conv_w[0], conv_b.reshape(1, CONV_DIM)
    g_mix = [norm_mix_g[l].reshape(1, d) for l in range(2)]
    g_mlp = [norm_mlp_g[l].reshape(1, d) for l in range(2)]
    pscale = pool_scale.reshape(1, d)

    (sh1, sc1, g1, sh2, sc2, g2), (sh1s, sc1s, g1s, sh2s, sc2s, g2s) = mods(0)
    (sh1b, sc1b, g1b, sh2b, sc2b, g2b), (sh1sb, sc1sb, g1sb, sh2sb, sc2sb, g2sb) = mods(1)

    zx, dt = _inproj(x_prompt, g_mix[0], sc1, sh1, w_in, ZX_COLS, 1024, 512, w_dt, b_dt)
    (qkv,) = _inproj(x_prompt, g_mix[0], sc1, sh1, w_qkv, 3 * ATT_QKV, 1024, 512)
    p_conv = zx[:, SEQ - (SSD_CONV - 1):, SSD_INNER:][None]
    xbc = _conv_prompt(zx, cw, cb)
    y_ssd, h_fin = _ssd_prompt(xbc, dt, zx, a_log_pad, dskip_x, ssd_g)
    p_ssm = h_fin.reshape(1, N_PROMPT, SSD_HEADS, SSD_HEAD_DIM, SSD_STATE)
    att, kv128, kv512, kv2048 = _attn_prompt(qkv, _prompt_bias_table(rel_bias), qg, kg)
    p_kv = [kv.reshape(1, N_PROMPT, w, 2, ATT_HPP, ATT_HEAD_DIM) for kv, w in zip((kv128, kv512, kv2048), WINDOWS)]
    x1 = _outproj(y_ssd, att, out_proj_w, x_prompt, g1, 1024, 512)
    x2 = _mlp(x1, g_mlp[0], sc2, sh2, g2, mlp_w1, mlp_w2, 0, *_mlp_tiles(SEQ))
    x3, u_last = _pool_prompt(x2, g_mix[1], sc1b, sh1b, g1b, pool_w[0], pscale)
    p_pool = u_last[:, 1:][None]
    y_prompt = _mlp(x3, g_mlp[1], sc2b, sh2b, g2b, mlp_w1, mlp_w2, 1, *_mlp_tiles(SEQ))

    rows = SAMPLE_ROWS
    xs0 = _pad_rows(x_sample.reshape(N_SAMPLE, d), rows)[None]
    zx_s, dt_s = _inproj(xs0, g_mix[0], sc1s, sh1s, w_in, ZX_COLS, rows, 512, w_dt, b_dt)
    (qkv_s,) = _inproj(xs0, g_mix[0], sc1s, sh1s, w_qkv, 3 * ATT_QKV, rows, 512)
    zx_s, dt_s, qkv_s = zx_s[0], dt_s[0], qkv_s[0]
    xbc_new = zx_s[:N_SAMPLE, SSD_INNER:]
    s_conv = jnp.concatenate([state_conv[0][:, 1:], xbc_new[:, None, :]], axis=1)[None]
    conv_prev_t = _pad_rows(jnp.transpose(state_conv[0], (1, 0, 2)), rows, axis=1)
    xbc_s = _conv_sample(zx_s, conv_prev_t, cw, cb)
    xs_s = xbc_s[:N_SAMPLE, :SSD_INNER]
    e = SSD_HEADS_PER_GROUP
    b_s = jnp.repeat(xbc_s[:N_SAMPLE, SSD_INNER:SSD_INNER + SSD_GROUPS * SSD_STATE]
                     .reshape(N_SAMPLE, SSD_GROUPS, 1, SSD_STATE), e, axis=1)
    c_s = jnp.repeat(xbc_s[:N_SAMPLE, SSD_INNER + SSD_GROUPS * SSD_STATE:]
                     .reshape(N_SAMPLE, SSD_GROUPS, 1, SSD_STATE), e, axis=1)
    h_new, y_s = _ssd_step(state_ssm[0], xs_s.reshape(N_SAMPLE, SSD_HEADS, SSD_HEAD_DIM, 1),
                           dt_s[:N_SAMPLE, :SSD_HEADS].reshape(N_SAMPLE, SSD_HEADS, 1, 1), b_s, c_s,
                           a_log[0].reshape(SSD_HEADS, 1, 1))
    s_ssm = h_new[None]
    y_s = _gate_sample(_pad_rows(y_s.reshape(N_SAMPLE, SSD_INNER), rows), xbc_s[:, :SSD_INNER],
                       zx_s[:, :SSD_INNER], dskip_x, ssd_g)
    caches = (cache_kv_w128, cache_kv_w512, cache_kv_w2048)
    strided = [c.reshape(N_SAMPLE, ATT_BLOCK, dil, 2, ATT_HPP, ATT_HEAD_DIM) for c, dil in zip(caches, DILATIONS)]
    bias_buf, bias_new = _sample_bias_table(rel_bias)
    att_s, k_new, v_new = _attn_sample(qkv_s[:N_SAMPLE].reshape(N_SAMPLE, 3 * ATT_HEADS, ATT_HEAD_DIM),
                                       strided, bias_buf, bias_new, qg, kg)
    new_rows = jnp.transpose(jnp.stack([k_new, v_new], axis=2), (1, 0, 2, 3, 4))[:, :, None]
    s_kv = _kv_shift(caches, new_rows)
    att_s = _pad_rows(att_s.reshape(N_SAMPLE, ATT_OUT), rows)
    x1s = _outproj(y_s[None], att_s[None], out_proj_w, xs0, g1s, rows, 512)
    x2s = _mlp(x1s, g_mlp[0], sc2s, sh2s, g2s, mlp_w1, mlp_w2, 0, *_mlp_tiles(rows))
    pool_prev_t = _pad_rows(jnp.transpose(state_pool[0], (1, 0, 2)), rows, axis=1)
    x3s, u_s = _pool_sample(x2s[0], g_mix[1], sc1sb[0], sh1sb[0], g1sb[0], pool_prev_t, pool_w[0], pscale)
    s_pool = jnp.concatenate([state_pool[0][:, 1:], u_s[:N_SAMPLE, None, :]], axis=1)[None]
    y_s2 = _mlp(x3s[None], g_mlp[1], sc2sb, sh2sb, g2sb, mlp_w1, mlp_w2, 1, *_mlp_tiles(rows))
    y_sample = y_s2[0, :N_SAMPLE].reshape(N_SAMPLE, 1, d)

    return (y_prompt, y_sample, p_ssm, p_conv, p_kv[0], p_kv[1], p_kv[2], p_pool,
            s_ssm, s_conv, s_kv[0], s_kv[1], s_kv[2], s_pool)
```

```python
import functools
import math

import jax
import jax.numpy as jnp
from jax import lax
from jax.experimental import pallas as pl
from jax.experimental.pallas import tpu as pltpu

F32 = jnp.float32
BF16 = jnp.bfloat16

D_MODEL = 2048
SEQ = 2048
N_PROMPT = 4
N_SAMPLE = 8
SAMPLE_ROWS = 16
PAST_LEN = 16384
SSD_HEADS = 32
SSD_HEAD_DIM = 64
SSD_INNER = SSD_HEADS * SSD_HEAD_DIM
SSD_GROUPS = 8
SSD_HEADS_PER_GROUP = SSD_HEADS // SSD_GROUPS
SSD_STATE = 128
SSD_CONV = 4
SSD_CHUNK = 128
CONV_DIM = SSD_INNER + 2 * SSD_GROUPS * SSD_STATE
ATT_HEAD_DIM = 128
WINDOWS = (128, 512, 2048)
DILATIONS = (1, 4, 16)
ATT_BLOCK = 128
ATT_HPP = 8
ATT_HEADS = ATT_HPP * len(WINDOWS)
ATT_QKV = ATT_HEADS * ATT_HEAD_DIM
ATT_OUT = ATT_HPP * ATT_HEAD_DIM
REL_BUCKETS = 32
REL_MAX_DIST = 2048
ZX_COLS = SSD_INNER + CONV_DIM
DT_COL0 = ZX_COLS
QKV_COL0 = ZX_COLS + SSD_HEADS
DT_PAD = 128
MIX_OUT = SSD_INNER + ATT_OUT
POOL_WINDOWS = (2, 4, 8, 16)
POOL_CH = D_MODEL // len(POOL_WINDOWS)
POOL_MAX = 16
D_FF = 4 * D_MODEL
EPS = 1e-6
NEG = -1e30
MIB = 1024 * 1024


def _params(semantics, vmem_mib):
    return pltpu.CompilerParams(dimension_semantics=semantics, vmem_limit_bytes=vmem_mib * MIB)


def _sigmoid(x):
    return 1.0 / (1.0 + jnp.exp(-x))


def _silu(x):
    return x * _sigmoid(x)


def _softplus(x):
    return jnp.maximum(x, 0.0) + jnp.log1p(jnp.exp(-jnp.abs(x)))


def _rms(x):
    return x * lax.rsqrt(jnp.mean(x * x, axis=-1, keepdims=True) + EPS)


def _normmod(x, g, sc, sh):
    return (_rms(x) * g) * (1.0 + sc) + sh


def _normmod_rows(u_ref, x_ref, g_ref, sc_ref, sh_ref):
    tm = u_ref.shape[0]
    rc = min(tm, 256)
    per_row = sc_ref.shape[1] != 1

    def body(i, carry):
        rows = pl.ds(pl.multiple_of(i * rc, rc), rc)
        sc = sc_ref[0, rows, :] if per_row else sc_ref[0]
        sh = sh_ref[0, rows, :] if per_row else sh_ref[0]
        u_ref[rows, :] = _normmod(x_ref[0, rows, :], g_ref[...], sc, sh).astype(BF16)
        return carry

    lax.fori_loop(0, tm // rc, body, 0)


def _bdot(a, b):
    return jnp.dot(a.astype(BF16), b.astype(BF16), preferred_element_type=F32)


def _split3(v):
    hi = v.astype(BF16)
    r = v - hi.astype(F32)
    mid = r.astype(BF16)
    lo = (r - mid.astype(F32)).astype(BF16)
    return hi, mid, lo


def _select_right(v, sel):
    hi, mid, lo = _split3(v)
    d = lambda a: jnp.dot(a, sel, preferred_element_type=F32)
    return d(hi) + d(mid) + d(lo)


def _select_left(sel, v):
    hi, mid, lo = _split3(v)
    d = lambda a: jnp.dot(sel, a, preferred_element_type=F32)
    return d(hi) + d(mid) + d(lo)


def _ada_kernel(c_ref, w_ref, b_ref, o_ref):
    o_ref[0] = _bdot(_silu(c_ref[...]), w_ref[0]) + b_ref[0]


def _ada(c_all, ada_w, ada_b):
    depth, _, n = ada_w.shape
    tn = 1024
    return pl.pallas_call(
        _ada_kernel,
        name="ada_mod",
        grid=(depth, n // tn),
        in_specs=[
            pl.BlockSpec((SAMPLE_ROWS, D_MODEL), lambda l, j: (0, 0)),
            pl.BlockSpec((1, D_MODEL, tn), lambda l, j: (l, 0, j)),
            pl.BlockSpec((1, 1, tn), lambda l, j: (l, 0, j)),
        ],
        out_specs=pl.BlockSpec((1, SAMPLE_ROWS, tn), lambda l, j: (l, 0, j)),
        out_shape=jax.ShapeDtypeStruct((depth, SAMPLE_ROWS, n), F32),
        compiler_params=_params(("arbitrary", "arbitrary"), 40),
    )(c_all, ada_w, ada_b.reshape(depth, 1, n))


def _inproj_kernel(*refs, has_dt):
    if has_dt:
        x_ref, g_ref, sc_ref, sh_ref, w_ref, wdt_ref, dtb_ref, o_ref, dt_ref, u_ref = refs
    else:
        x_ref, g_ref, sc_ref, sh_ref, w_ref, o_ref, u_ref = refs

    @pl.when(pl.program_id(2) == 0)
    def _():
        _normmod_rows(u_ref, x_ref, g_ref, sc_ref, sh_ref)
        if has_dt:
            dt_ref[0] = _softplus(_bdot(u_ref[...], wdt_ref[...]) + dtb_ref[...])

    o_ref[0] = _bdot(u_ref[...], w_ref[...])


def _inproj(x, g, sc, sh, w, n_cols, tm, tn, dt_w=None, dt_b=None):
    nb, t, d = x.shape
    r = sc.shape[1]
    has_dt = dt_w is not None
    in_specs = [
        pl.BlockSpec((1, tm, d), lambda b, i, j: (b, i, 0)),
        pl.BlockSpec((1, d), lambda b, i, j: (0, 0)),
        pl.BlockSpec((1, r if r == 1 else tm, d), lambda b, i, j: (b, 0 if r == 1 else i, 0)),
        pl.BlockSpec((1, r if r == 1 else tm, d), lambda b, i, j: (b, 0 if r == 1 else i, 0)),
        pl.BlockSpec((d, tn), lambda b, i, j: (0, j)),
    ]
    args = [x, g, sc, sh, w]
    out_specs = [pl.BlockSpec((1, tm, tn), lambda b, i, j: (b, i, j))]
    out_shape = [jax.ShapeDtypeStruct((nb, t, n_cols), F32)]
    if has_dt:
        in_specs += [pl.BlockSpec((d, DT_PAD), lambda b, i, j: (0, 0)),
                     pl.BlockSpec((1, DT_PAD), lambda b, i, j: (0, 0))]
        args += [dt_w, dt_b]
        out_specs.append(pl.BlockSpec((1, tm, DT_PAD), lambda b, i, j: (b, i, 0)))
        out_shape.append(jax.ShapeDtypeStruct((nb, t, DT_PAD), F32))
    return pl.pallas_call(
        functools.partial(_inproj_kernel, has_dt=has_dt),
        name="in_proj",
        grid=(nb, t // tm, n_cols // tn),
        in_specs=in_specs,
        out_specs=out_specs,
        out_shape=out_shape,
        scratch_shapes=[pltpu.VMEM((tm, d), BF16)],
        compiler_params=_params(("arbitrary", "arbitrary", "arbitrary"), 48),
    )(*args)


def _conv_kernel(x_ref, w_ref, b_ref, o_ref, ext_ref, *, tt):
    @pl.when(pl.program_id(2) == 0)
    def _():
        ext_ref[0:8, :] = jnp.zeros((8, ext_ref.shape[1]), F32)

    x = x_ref[0]
    ext_ref[8:8 + tt, :] = x
    w = w_ref[...]
    acc = b_ref[...] + w[SSD_CONV - 1:SSD_CONV] * x
    for k in range(1, SSD_CONV):
        acc = acc + w[SSD_CONV - 1 - k:SSD_CONV - k] * ext_ref[pl.ds(8 - k, tt), :]
    o_ref[0] = _silu(acc)
    ext_ref[0:8, :] = x[tt - 8:tt]


def _conv_prompt(zx, conv_w, conv_b):
    nb, t, _ = zx.shape
    tt, tc = 512, 512
    col0 = SSD_INNER // tc
    return pl.pallas_call(
        functools.partial(_conv_kernel, tt=tt),
        name="conv_silu",
        grid=(nb, CONV_DIM // tc, t // tt),
        in_specs=[
            pl.BlockSpec((1, tt, tc), lambda b, c, i: (b, i, col0 + c)),
            pl.BlockSpec((SSD_CONV, tc), lambda b, c, i: (0, c)),
            pl.BlockSpec((1, tc), lambda b, c, i: (0, c)),
        ],
        out_specs=pl.BlockSpec((1, tt, tc), lambda b, c, i: (b, i, c)),
        out_shape=jax.ShapeDtypeStruct((nb, t, CONV_DIM), F32),
        scratch_shapes=[pltpu.VMEM((tt + 8, tc), F32)],
        compiler_params=_params(("arbitrary", "arbitrary", "arbitrary"), 32),
    )(zx, conv_w, conv_b)


def _ssd_kernel(xs_ref, b_ref, c_ref, dt_ref, z_ref, alog_ref, dskip_ref, g_ref,
                y_ref, hout_ref, h_ref, yacc_ref):
    q = SSD_CHUNK
    gw = SSD_HEADS_PER_GROUP * SSD_HEAD_DIM
    chunk = pl.program_id(1)

    @pl.when(chunk == 0)
    def _():
        h_ref[...] = jnp.zeros(h_ref.shape, F32)

    row = lax.broadcasted_iota(jnp.int32, (q, q), 0)
    col = lax.broadcasted_iota(jnp.int32, (q, q), 1)
    causal = row >= col
    tri = jnp.where(causal, 1.0, 0.0).astype(BF16)
    head_of_lane = lax.broadcasted_iota(jnp.int32, (DT_PAD, SSD_INNER), 1) // SSD_HEAD_DIM
    expand = jnp.where(head_of_lane == lax.broadcasted_iota(jnp.int32, (DT_PAD, SSD_INNER), 0), 1.0, 0.0).astype(BF16)

    dt = dt_ref[0]
    a = dt * (-jnp.exp(alog_ref[...]))
    acum = _select_left(tri, a)
    acum_t = acum.T
    dt_t = dt.T
    last = acum[q - 1:q, :]
    decay_end_x = _select_right(jnp.exp(last - acum) * dt, expand)
    eacum_x = _select_right(jnp.exp(acum), expand)
    chunk_decay_x = _select_right(jnp.broadcast_to(jnp.exp(last), (8, DT_PAD)), expand)[0:1]

    xs = xs_ref[0]
    xw = (xs * decay_end_x).astype(BF16)
    xsb = xs.astype(BF16)
    head_in_group = lax.broadcasted_iota(jnp.int32, (q, gw), 1) // SSD_HEAD_DIM

    for g in range(SSD_GROUPS):
        ns = slice(g * SSD_STATE, (g + 1) * SSD_STATE)
        cs = slice(g * gw, (g + 1) * gw)
        cg = c_ref[0, :, ns].astype(BF16)
        bg_t = b_ref[0, :, ns].T.astype(BF16)
        cb = jnp.dot(cg, bg_t, preferred_element_type=F32)
        h_prev = h_ref[:, cs]
        y_g = jnp.dot(cg, h_prev.astype(BF16), preferred_element_type=F32) * eacum_x[:, cs]
        h_ref[:, cs] = h_prev * chunk_decay_x[:, cs] + jnp.dot(bg_t, xw[:, cs], preferred_element_type=F32)
        xg = xsb[:, cs]
        for e in range(SSD_HEADS_PER_GROUP):
            hd = g * SSD_HEADS_PER_GROUP + e
            seg = acum[:, hd:hd + 1] - acum_t[hd:hd + 1, :]
            lw = jnp.where(causal, jnp.exp(jnp.where(causal, seg, 0.0)), 0.0) * dt_t[hd:hd + 1, :]
            yd = jnp.dot((cb * lw).astype(BF16), xg, preferred_element_type=F32)
            y_g = y_g + jnp.where(head_in_group == e, yd, 0.0)
        yacc_ref[:, cs] = y_g

    y = yacc_ref[...] + dskip_ref[...] * xs
    y = y * _silu(z_ref[0])
    y_ref[0] = (_rms(y) * g_ref[...]).astype(BF16)

    @pl.when(chunk == pl.num_programs(1) - 1)
    def _():
        hout_ref[0] = h_ref[...].T


def _ssd_prompt(xbc, dt, zx, a_log_pad, dskip_x, norm_g):
    nb, t, _ = xbc.shape
    q = SSD_CHUNK
    bc_w = SSD_GROUPS * SSD_STATE
    return pl.pallas_call(
        _ssd_kernel,
        name="ssd_chunk",
        grid=(nb, t // q),
        in_specs=[
            pl.BlockSpec((1, q, SSD_INNER), lambda b, c: (b, c, 0)),
            pl.BlockSpec((1, q, bc_w), lambda b, c: (b, c, SSD_INNER // bc_w)),
            pl.BlockSpec((1, q, bc_w), lambda b, c: (b, c, SSD_INNER // bc_w + 1)),
            pl.BlockSpec((1, q, DT_PAD), lambda b, c: (b, c, 0)),
            pl.BlockSpec((1, q, SSD_INNER), lambda b, c: (b, c, 0)),
            pl.BlockSpec((1, DT_PAD), lambda b, c: (0, 0)),
            pl.BlockSpec((1, SSD_INNER), lambda b, c: (0, 0)),
            pl.BlockSpec((1, SSD_INNER), lambda b, c: (0, 0)),
        ],
        out_specs=[
            pl.BlockSpec((1, q, SSD_INNER), lambda b, c: (b, c, 0)),
            pl.BlockSpec((1, SSD_INNER, SSD_STATE), lambda b, c: (b, 0, 0)),
        ],
        out_shape=[
            jax.ShapeDtypeStruct((nb, t, SSD_INNER), BF16),
            jax.ShapeDtypeStruct((nb, SSD_INNER, SSD_STATE), F32),
        ],
        scratch_shapes=[pltpu.VMEM((SSD_STATE, SSD_INNER), F32), pltpu.VMEM((q, SSD_INNER), F32)],
        compiler_params=_params(("arbitrary", "arbitrary"), 40),
    )(xbc, xbc, xbc, dt, zx, a_log_pad, dskip_x, norm_g)


def _attn_kernel(q0, q1, q2, k0, k1, k2, v0, v1, v2, bias0, bias1, bias2, qg_ref, kg_ref,
                 att_ref, kv0_ref, kv1_ref, kv2_ref, qs, ks, os_, ls):
    blk = ATT_BLOCK
    scale = 1.0 / math.sqrt(ATT_HEAD_DIM)
    sel = pl.program_id(2)
    qrefs, krefs, vrefs, brefs = (q0, q1, q2), (k0, k1, k2), (v0, v1, v2), (bias0, bias1, bias2)
    kvrefs = (kv0_ref, kv1_ref, kv2_ref)
    n_chunks = SEQ // blk

    @pl.when(sel == 0)
    def _():
        row = lax.broadcasted_iota(jnp.int32, (blk, blk), 0)
        col = lax.broadcasted_iota(jnp.int32, (blk, blk), 1)
        prev_ok = row <= col
        cur_ok = row >= col

        for g in range(len(WINDOWS)):
            def norm_body(i, carry, g=g):
                rows = pl.ds(pl.multiple_of(i * blk, blk), blk)
                qs[g, rows, :] = _rms(qrefs[g][0, rows, :]) * qg_ref[...]
                ks[g, rows, :] = _rms(krefs[g][0, rows, :]) * kg_ref[...]
                return carry
            lax.fori_loop(0, n_chunks, norm_body, 0)

        for g, dil in enumerate(DILATIONS):
            nblk = SEQ // (blk * dil)
            bias_prev = brefs[g][0, :, 0:blk]
            bias_cur = brefs[g][0, :, blk:2 * blk]
            for r in range(dil):
                for n in range(nblk):
                    def rows_of(start):
                        return pl.ds(start, blk) if dil == 1 else pl.ds(start, blk, stride=dil)
                    rows = rows_of(r + n * blk * dil)
                    qb = qs[g, rows, :].astype(BF16)
                    kc = ks[g, rows, :].astype(BF16)
                    vc = vrefs[g][0, rows, :].astype(BF16)
                    lc = lax.dot_general(qb, kc, (((1,), (1,)), ((), ())), preferred_element_type=F32)
                    lc = jnp.where(cur_ok, lc * scale + bias_cur, NEG)
                    m = jnp.max(lc, axis=-1, keepdims=True)
                    if n > 0:
                        prows = rows_of(r + (n - 1) * blk * dil)
                        kp = ks[g, prows, :].astype(BF16)
                        vp = vrefs[g][0, prows, :].astype(BF16)
                        lp = lax.dot_general(qb, kp, (((1,), (1,)), ((), ())), preferred_element_type=F32)
                        lp = jnp.where(prev_ok, lp * scale + bias_prev, NEG)
                        m = jnp.maximum(m, jnp.max(lp, axis=-1, keepdims=True))
                    pc = jnp.exp(lc - m)
                    s = jnp.sum(pc, axis=-1, keepdims=True)
                    o = jnp.dot(pc.astype(BF16), vc, preferred_element_type=F32)
                    if n > 0:
                        pp = jnp.exp(lp - m)
                        s = s + jnp.sum(pp, axis=-1, keepdims=True)
                        o = o + jnp.dot(pp.astype(BF16), vp, preferred_element_type=F32)
                    os_[g, rows, :] = o / s
                    ls[g, rows, :] = jnp.broadcast_to(m + jnp.log(s), (blk, ATT_HEAD_DIM))

        def mix_body(i, carry):
            rows = pl.ds(pl.multiple_of(i * blk, blk), blk)
            l0, l1, l2 = ls[0, rows, :], ls[1, rows, :], ls[2, rows, :]
            mx = jnp.maximum(jnp.maximum(l0, l1), l2)
            e0, e1, e2 = jnp.exp(l0 - mx), jnp.exp(l1 - mx), jnp.exp(l2 - mx)
            num = e0 * os_[0, rows, :] + e1 * os_[1, rows, :] + e2 * os_[2, rows, :]
            att_ref[0, rows, :] = (num / (e0 + e1 + e2)).astype(BF16)
            return carry
        lax.fori_loop(0, n_chunks, mix_body, 0)

        for g, w in enumerate(WINDOWS):
            kvrefs[g][0] = ks[g, SEQ - w:SEQ, :]

    @pl.when(sel == 1)
    def _():
        for g, w in enumerate(WINDOWS):
            kvrefs[g][0] = vrefs[g][0, SEQ - w:SEQ, :]


def _attn_prompt(qkv, bias, q_norm_g, k_norm_g):
    nb = qkv.shape[0]
    hd = ATT_HEAD_DIM

    def head_spec(part, g):
        return pl.BlockSpec((1, SEQ, hd), lambda b, h, s: (b, 0, part * ATT_HEADS + g * ATT_HPP + h))

    in_specs = [head_spec(part, g) for part in range(3) for g in range(3)]
    in_specs += [pl.BlockSpec((1, ATT_BLOCK, 2 * ATT_BLOCK), functools.partial(lambda b, h, s, g: (g * ATT_HPP + h, 0, 0), g=g))
                 for g in range(3)]
    in_specs += [pl.BlockSpec((1, hd), lambda b, h, s: (0, 0))] * 2
    out_specs = [pl.BlockSpec((1, SEQ, hd), lambda b, h, s: (b, 0, h))]
    out_shape = [jax.ShapeDtypeStruct((nb, SEQ, ATT_OUT), BF16)]
    for w in WINDOWS:
        out_specs.append(pl.BlockSpec((1, w, hd), lambda b, h, s: (b, 0, s * ATT_HPP + h)))
        out_shape.append(jax.ShapeDtypeStruct((nb, w, 2 * ATT_OUT), F32))
    return pl.pallas_call(
        _attn_kernel,
        name="attn_prompt",
        grid=(nb, ATT_HPP, 2),
        in_specs=in_specs,
        out_specs=out_specs,
        out_shape=out_shape,
        scratch_shapes=[pltpu.VMEM((3, SEQ, hd), F32)] * 4,
        compiler_params=_params(("arbitrary", "arbitrary", "arbitrary"), 48),
    )(*([qkv] * 9), bias, bias, bias, q_norm_g, k_norm_g)


def _outproj_kernel(y_ref, a_ref, wy_ref, wa_ref, x_ref, g1_ref, o_ref):
    mix = _bdot(y_ref[0], wy_ref[0]) + _bdot(a_ref[0], wa_ref[0])
    o_ref[0] = x_ref[0] + g1_ref[0] * mix


def _outproj(y, att, w, x, g1, tm, tn):
    nb, t, d = x.shape
    r = g1.shape[1]
    return pl.pallas_call(
        _outproj_kernel,
        name="out_proj",
        grid=(nb, t // tm, d // tn),
        in_specs=[
            pl.BlockSpec((1, tm, SSD_INNER), lambda b, i, j: (b, i, 0)),
            pl.BlockSpec((1, tm, ATT_OUT), lambda b, i, j: (b, i, 0)),
            pl.BlockSpec((1, SSD_INNER, tn), lambda b, i, j: (0, 0, j)),
            pl.BlockSpec((1, ATT_OUT, tn), lambda b, i, j: (0, SSD_INNER // ATT_OUT, j)),
            pl.BlockSpec((1, tm, tn), lambda b, i, j: (b, i, j)),
            pl.BlockSpec((1, r if r == 1 else tm, tn), lambda b, i, j: (b, 0 if r == 1 else i, j)),
        ],
        out_specs=pl.BlockSpec((1, tm, tn), lambda b, i, j: (b, i, j)),
        out_shape=jax.ShapeDtypeStruct((nb, t, d), F32),
        compiler_params=_params(("arbitrary", "arbitrary", "arbitrary"), 48),
    )(y, att, w, w, x, g1)


def _mlp_kernel(x_ref, g_ref, sc_ref, sh_ref, g2_ref, w1_ref, w2_ref, o_ref, u_ref):
    f = pl.program_id(2)

    @pl.when(f == 0)
    def _():
        _normmod_rows(u_ref, x_ref, g_ref, sc_ref, sh_ref)
        o_ref[0] = jnp.zeros(o_ref.shape[1:], F32)

    h = jnp.maximum(_bdot(u_ref[...], w1_ref[0]), 0.0)
    o_ref[0] += _bdot(h * h, w2_ref[0])

    @pl.when(f == pl.num_programs(2) - 1)
    def _():
        o_ref[0] = x_ref[0] + g2_ref[0] * o_ref[0]


def _mlp(x, g, sc, sh, g2, w1, w2, layer, tm, tf):
    nb, t, d = x.shape
    r = sc.shape[1]
    mod_spec = pl.BlockSpec((1, r if r == 1 else tm, d), lambda b, i, f: (b, 0 if r == 1 else i, 0))
    return pl.pallas_call(
        _mlp_kernel,
        name="mlp",
        grid=(nb, t // tm, D_FF // tf),
        in_specs=[
            pl.BlockSpec((1, tm, d), lambda b, i, f: (b, i, 0), pipeline_mode=pl.Buffered(1)),
            pl.BlockSpec((1, d), lambda b, i, f: (0, 0)),
            mod_spec, mod_spec, mod_spec,
            pl.BlockSpec((1, d, tf), lambda b, i, f: (layer, 0, f)),
            pl.BlockSpec((1, tf, d), lambda b, i, f: (layer, f, 0)),
        ],
        out_specs=pl.BlockSpec((1, tm, d), lambda b, i, f: (b, i, 0), pipeline_mode=pl.Buffered(1)),
        out_shape=jax.ShapeDtypeStruct((nb, t, d), F32),
        scratch_shapes=[pltpu.VMEM((tm, d), BF16)],
        compiler_params=_params(("arbitrary", "arbitrary", "arbitrary"), 56),
    )(x, g, sc, sh, g2, w1, w2)


def _pool_kernel(x_ref, g_ref, sc_ref, sh_ref, g1_ref, w_ref, scale_ref, o_ref, ulast_ref, ext_ref, *, tt):
    halo = POOL_MAX
    ti = pl.program_id(1)

    @pl.when(ti == 0)
    def _():
        ext_ref[0:halo, :] = jnp.zeros((halo, D_MODEL), F32)

    x = x_ref[0]
    u = _normmod(x, g_ref[...], sc_ref[0], sh_ref[0])
    ext_ref[halo:halo + tt, :] = u
    pos = ti * tt + lax.broadcasted_iota(jnp.int32, (tt, 1), 0)
    for gi, w in enumerate(POOL_WINDOWS):
        ch = slice(gi * POOL_CH, (gi + 1) * POOL_CH)
        s = u[:, ch]
        for k in range(1, w):
            s = s + ext_ref[pl.ds(halo - k, tt), ch]
        cnt = jnp.minimum(w, pos + 1).astype(F32)
        pooled = s / cnt - u[:, ch]
        mix = _bdot(pooled, w_ref[gi]) * scale_ref[:, ch]
        o_ref[0, :, ch] = x[:, ch] + g1_ref[0, :, ch] * mix
    ulast_ref[0] = u[tt - halo:tt]
    ext_ref[0:halo, :] = u[tt - halo:tt]


def _pool_prompt(x, g, sc, sh, g1, pool_w, pool_scale):
    nb, t, d = x.shape
    tt = 512
    mod_spec = pl.BlockSpec((1, 1, d), lambda b, i: (b, 0, 0))
    return pl.pallas_call(
        functools.partial(_pool_kernel, tt=tt),
        name="pool_mix",
        grid=(nb, t // tt),
        in_specs=[
            pl.BlockSpec((1, tt, d), lambda b, i: (b, i, 0)),
            pl.BlockSpec((1, d), lambda b, i: (0, 0)),
            mod_spec, mod_spec, mod_spec,
            pl.BlockSpec((len(POOL_WINDOWS), POOL_CH, POOL_CH), lambda b, i: (0, 0, 0)),
            pl.BlockSpec((1, d), lambda b, i: (0, 0)),
        ],
        out_specs=[
            pl.BlockSpec((1, tt, d), lambda b, i: (b, i, 0)),
            pl.BlockSpec((1, POOL_MAX, d), lambda b, i: (b, 0, 0)),
        ],
        out_shape=[jax.ShapeDtypeStruct((nb, t, d), F32), jax.ShapeDtypeStruct((nb, POOL_MAX, d), F32)],
        scratch_shapes=[pltpu.VMEM((tt + POOL_MAX, d), F32)],
        compiler_params=_params(("arbitrary", "arbitrary"), 48),
    )(x, g, sc, sh, g1, pool_w, pool_scale)


def _sconv_kernel(zx_ref, prev_ref, w_ref, b_ref, o_ref):
    w = w_ref[...]
    acc = b_ref[...] + w[SSD_CONV - 1:SSD_CONV] * zx_ref[:, SSD_INNER:ZX_COLS]
    for k in range(SSD_CONV - 1):
        acc = acc + w[k:k + 1] * prev_ref[k]
    o_ref[...] = _silu(acc)


def _conv_sample(zx, prev_t, conv_w, conv_b):
    return pl.pallas_call(
        _sconv_kernel,
        out_shape=jax.ShapeDtypeStruct((SAMPLE_ROWS, CONV_DIM), F32),
        compiler_params=pltpu.CompilerParams(vmem_limit_bytes=32 * MIB),
    )(zx, prev_t, conv_w, conv_b)


def _sstep_kernel(h0_ref, x_ref, dt_ref, b_ref, c_ref, alog_ref, h_ref, y_ref):
    dt = dt_ref[0]
    h = jnp.exp(dt * (-jnp.exp(alog_ref[...]))) * h0_ref[0] + (dt * x_ref[0]) * b_ref[0]
    h_ref[0] = h
    y_ref[0] = jnp.sum(h * c_ref[0], axis=-1, keepdims=True)


def _ssd_step(h0, x, dt, bh, ch, a_log):
    nb = h0.shape[0]
    hp = (SSD_HEADS, SSD_HEAD_DIM)
    return pl.pallas_call(
        _sstep_kernel,
        grid=(nb,),
        in_specs=[
            pl.BlockSpec((1, *hp, SSD_STATE), lambda b: (b, 0, 0, 0)),
            pl.BlockSpec((1, *hp, 1), lambda b: (b, 0, 0, 0)),
            pl.BlockSpec((1, SSD_HEADS, 1, 1), lambda b: (b, 0, 0, 0)),
            pl.BlockSpec((1, SSD_HEADS, 1, SSD_STATE), lambda b: (b, 0, 0, 0)),
            pl.BlockSpec((1, SSD_HEADS, 1, SSD_STATE), lambda b: (b, 0, 0, 0)),
            pl.BlockSpec((SSD_HEADS, 1, 1), lambda b: (0, 0, 0)),
        ],
        out_specs=[
            pl.BlockSpec((1, *hp, SSD_STATE), lambda b: (b, 0, 0, 0)),
            pl.BlockSpec((1, *hp, 1), lambda b: (b, 0, 0, 0)),
        ],
        out_shape=[jax.ShapeDtypeStruct((nb, *hp, SSD_STATE), F32), jax.ShapeDtypeStruct((nb, *hp, 1), F32)],
        compiler_params=_params(("arbitrary",), 32),
    )(h0, x, dt, bh, ch, a_log)


def _sgate_kernel(y_ref, xs_ref, z_ref, dskip_ref, g_ref, o_ref):
    y = (y_ref[...] + dskip_ref[...] * xs_ref[...]) * _silu(z_ref[...])
    o_ref[...] = _rms(y) * g_ref[...]


def _gate_sample(y, xs, z, dskip_x, norm_g):
    return pl.pallas_call(
        _sgate_kernel,
        out_shape=jax.ShapeDtypeStruct(y.shape, F32),
    )(y, xs, z, dskip_x, norm_g)


def _sattn_kernel(qkv_ref, c0_ref, c1_ref, c2_ref, bbuf_ref, bnew_ref, qg_ref, kg_ref,
                  att_ref, knew_ref, vnew_ref):
    scale = 1.0 / math.sqrt(ATT_HEAD_DIM)
    caches = (c0_ref, c1_ref, c2_ref)
    outs, lses = [], []
    for g in range(len(WINDOWS)):
        hs = g * ATT_HPP
        q = _rms(qkv_ref[0, hs:hs + ATT_HPP, :]) * qg_ref[...]
        k = _rms(qkv_ref[0, ATT_HEADS + hs:ATT_HEADS + hs + ATT_HPP, :]) * kg_ref[...]
        v = qkv_ref[0, 2 * ATT_HEADS + hs:2 * ATT_HEADS + hs + ATT_HPP, :]
        knew_ref[0, g] = k
        vnew_ref[0, g] = v
        kbuf = caches[g][:, 0]
        vbuf = caches[g][:, 1]
        lb = jnp.sum(kbuf * q[None], axis=-1, keepdims=True) * scale + bbuf_ref[g]
        ln = jnp.sum(k * q, axis=-1, keepdims=True) * scale + bnew_ref[g]
        m = jnp.maximum(jnp.max(lb, axis=0), ln)
        pb = jnp.exp(lb - m[None])
        pn = jnp.exp(ln - m)
        s = jnp.sum(pb, axis=0) + pn
        outs.append((jnp.sum(pb * vbuf, axis=0) + pn * v) / s)
        lses.append(m + jnp.log(s))
    mx = jnp.maximum(jnp.maximum(lses[0], lses[1]), lses[2])
    es = [jnp.exp(l - mx) for l in lses]
    att_ref[0] = (es[0] * outs[0] + es[1] * outs[1] + es[2] * outs[2]) / (es[0] + es[1] + es[2])


def _attn_sample(qkv, caches, bias_buf, bias_new, q_norm_g, k_norm_g):
    nb = qkv.shape[0]
    hd = ATT_HEAD_DIM
    in_specs = [pl.BlockSpec((1, 3 * ATT_HEADS, hd), lambda b: (b, 0, 0))]
    for _ in caches:
        in_specs.append(pl.BlockSpec((None, ATT_BLOCK, None, 2, ATT_HPP, hd), lambda b: (b, 0, 0, 0, 0, 0)))
    in_specs += [
        pl.BlockSpec((3, ATT_BLOCK, ATT_HPP, 1), lambda b: (0, 0, 0, 0)),
        pl.BlockSpec((3, ATT_HPP, 1), lambda b: (0, 0, 0)),
        pl.BlockSpec((1, hd), lambda b: (0, 0)),
        pl.BlockSpec((1, hd), lambda b: (0, 0)),
    ]
    return pl.pallas_call(
        _sattn_kernel,
        grid=(nb,),
        in_specs=in_specs,
        out_specs=[
            pl.BlockSpec((1, ATT_HPP, hd), lambda b: (b, 0, 0)),
            pl.BlockSpec((1, 3, ATT_HPP, hd), lambda b: (b, 0, 0, 0)),
            pl.BlockSpec((1, 3, ATT_HPP, hd), lambda b: (b, 0, 0, 0)),
        ],
        out_shape=[
            jax.ShapeDtypeStruct((nb, ATT_HPP, hd), F32),
            jax.ShapeDtypeStruct((nb, 3, ATT_HPP, hd), F32),
            jax.ShapeDtypeStruct((nb, 3, ATT_HPP, hd), F32),
        ],
        compiler_params=_params(("arbitrary",), 32),
    )(qkv, *caches, bias_buf, bias_new, q_norm_g, k_norm_g)


def _kvshift_kernel(x_ref, new_ref, o_ref, carry_ref):
    tw = x_ref.shape[0]

    @pl.when(pl.program_id(1) == 0)
    def _():
        carry_ref[...] = new_ref[0]

    o_ref[0:tw - 1] = x_ref[1:tw]
    o_ref[tw - 1] = carry_ref[...]
    carry_ref[...] = x_ref[0]


def _kv_shift(cache, new_row):
    _, nb, w, _, hpp, hd = cache.shape
    tw = min(w, 512)
    nblk = w // tw
    row = (2, hpp, hd)
    return pl.pallas_call(
        _kvshift_kernel,
        name="kv_shift",
        grid=(nb, nblk),
        in_specs=[
            pl.BlockSpec((None, None, tw, *row), lambda b, s: (0, b, nblk - 1 - s, 0, 0, 0)),
            pl.BlockSpec((None, 1, *row), lambda b, s: (b, 0, 0, 0, 0)),
        ],
        out_specs=pl.BlockSpec((None, None, tw, *row), lambda b, s: (0, b, nblk - 1 - s, 0, 0, 0)),
        out_shape=jax.ShapeDtypeStruct(cache.shape, cache.dtype),
        scratch_shapes=[pltpu.VMEM(row, F32)],
        compiler_params=_params(("arbitrary", "arbitrary"), 40),
    )(cache, new_row)


def _spool_kernel(x_ref, g_ref, sc_ref, sh_ref, g1_ref, prev_ref, w_ref, scale_ref, o_ref, u_ref):
    x = x_ref[...]
    u = _normmod(x, g_ref[...], sc_ref[...], sh_ref[...])
    u_ref[...] = u
    for gi, w in enumerate(POOL_WINDOWS):
        ch = slice(gi * POOL_CH, (gi + 1) * POOL_CH)
        s = u[:, ch]
        for k in range(1, w):
            s = s + prev_ref[POOL_MAX - 1 - k, :, ch]
        pooled = s / float(min(w, PAST_LEN + 1)) - u[:, ch]
        mix = _bdot(pooled, w_ref[gi]) * scale_ref[:, ch]
        o_ref[:, ch] = x[:, ch] + g1_ref[:, ch] * mix


def _pool_sample(x, g, sc, sh, g1, prev_t, pool_w, pool_scale):
    return pl.pallas_call(
        _spool_kernel,
        out_shape=[jax.ShapeDtypeStruct(x.shape, F32), jax.ShapeDtypeStruct(x.shape, F32)],
        compiler_params=pltpu.CompilerParams(vmem_limit_bytes=32 * MIB),
    )(x, g, sc, sh, g1, prev_t, pool_w, pool_scale)


def _rel_bucket(dist):
    max_exact = REL_BUCKETS // 2
    n = jnp.maximum(dist, 1).astype(F32)
    large = max_exact + (jnp.log(n / max_exact) / math.log(REL_MAX_DIST / max_exact)
                         * (REL_BUCKETS - max_exact)).astype(jnp.int32)
    large = jnp.minimum(large, REL_BUCKETS - 1)
    return jnp.where(dist < max_exact, dist, large)


def _prompt_bias_table(rel_bias):
    blk = ATT_BLOCK
    j = blk + jnp.arange(blk)[:, None] - jnp.arange(2 * blk)[None, :]
    tables = []
    for gi, dil in enumerate(DILATIONS):
        bucket = _rel_bucket(jnp.clip(j, 0, blk) * dil)[None]
        rb = rel_bias[:, gi * ATT_HPP:(gi + 1) * ATT_HPP].astype(F32)
        tbl = jnp.zeros((ATT_HPP, blk, 2 * blk), F32)
        for k in range(REL_BUCKETS):
            tbl = jnp.where(bucket == k, rb[k][:, None, None], tbl)
        tables.append(tbl)
    return jnp.concatenate(tables, axis=0)


def _sample_bias_table(rel_bias):
    blk = ATT_BLOCK
    bufs, news = [], []
    for gi, dil in enumerate(DILATIONS):
        tbl = rel_bias[:, gi * ATT_HPP:(gi + 1) * ATT_HPP][_rel_bucket(dil * jnp.arange(blk + 1))]
        bufs.append(tbl[blk:0:-1])
        news.append(tbl[0])
    return jnp.stack(bufs)[..., None].astype(F32), jnp.stack(news)[..., None].astype(F32)


def _pad_rows(a, rows, axis=0):
    pad = [(0, 0)] * a.ndim
    pad[axis] = (0, rows - a.shape[axis])
    return jnp.pad(a, pad)


def _mlp_tiles(t):
    return (1024, 512) if t >= 1024 else (t, 512)


def kernel(x_prompt, x_sample, state_ssm, state_conv, cache_kv_w128, cache_kv_w512, cache_kv_w2048, state_pool, c_prompt, c_sample, rel_bias, ada_w, ada_b, norm_mix_g, norm_mlp_g, in_proj_w, conv_w, conv_b, dt_bias, a_log, d_skip, ssd_norm_g, q_norm_g, k_norm_g, out_proj_w, pool_w, pool_scale, mlp_w1, mlp_w2):
    d = D_MODEL
    c_all = _pad_rows(jnp.concatenate([c_prompt, c_sample], axis=0), SAMPLE_ROWS)
    mod = _ada(c_all, ada_w, ada_b)

    def mods(layer):
        parts = [mod[layer, :, i * d:(i + 1) * d] for i in range(6)]
        prompt = [p[:N_PROMPT, None, :] for p in parts]
        sample = [_pad_rows(p[N_PROMPT:N_PROMPT + N_SAMPLE], SAMPLE_ROWS)[None] for p in parts]
        return prompt, sample

    w_in = in_proj_w.reshape(d, -1)
    w_dt = jnp.pad(w_in[:, DT_COL0:QKV_COL0], ((0, 0), (0, DT_PAD - SSD_HEADS)))
    b_dt = jnp.pad(dt_bias[0], (0, DT_PAD - SSD_HEADS)).reshape(1, DT_PAD)
    w_qkv = w_in[:, QKV_COL0:].astype(BF16)
    a_log_pad = jnp.pad(a_log[0], (0, DT_PAD - SSD_HEADS)).reshape(1, DT_PAD)
    dskip_x = jnp.repeat(d_skip[0], SSD_HEAD_DIM).reshape(1, SSD_INNER)
    ssd_g = ssd_norm_g.reshape(1, SSD_INNER)
    qg, kg = q_norm_g.reshape(1, ATT_HEAD_DIM), k_norm_g.reshape(1, ATT_HEAD_DIM)
    cw, cb = conv_w[0], conv_b.reshape(1, CONV_DIM)
    g_mix = [norm_mix_g[l].reshape(1, d) for l in range(2)]
    g_mlp = [norm_mlp_g[l].reshape(1, d) for l in range(2)]
    pscale = pool_scale.reshape(1, d)

    (sh1, sc1, g1, sh2, sc2, g2), (sh1s, sc1s, g1s, sh2s, sc2s, g2s) = mods(0)
    (sh1b, sc1b, g1b, sh2b, sc2b, g2b), (sh1sb, sc1sb, g1sb, sh2sb, sc2sb, g2sb) = mods(1)

    zx, dt = _inproj(x_prompt, g_mix[0], sc1, sh1, w_in, ZX_COLS, 1024, 512, w_dt, b_dt)
    (qkv,) = _inproj(x_prompt, g_mix[0], sc1, sh1, w_qkv, 3 * ATT_QKV, 1024, 512)
    p_conv = zx[:, SEQ - (SSD_CONV - 1):, SSD_INNER:][None]
    xbc = _conv_prompt(zx, cw, cb)
    y_ssd, h_fin = _ssd_prompt(xbc, dt, zx, a_log_pad, dskip_x, ssd_g)
    p_ssm = h_fin.reshape(1, N_PROMPT, SSD_HEADS, SSD_HEAD_DIM, SSD_STATE)
    att, kv128, kv512, kv2048 = _attn_prompt(qkv, _prompt_bias_table(rel_bias), qg, kg)
    p_kv = [kv.reshape(1, N_PROMPT, w, 2, ATT_HPP, ATT_HEAD_DIM) for kv, w in zip((kv128, kv512, kv2048), WINDOWS)]
    x1 = _outproj(y_ssd, att, out_proj_w, x_prompt, g1, 1024, 512)
    x2 = _mlp(x1, g_mlp[0], sc2, sh2, g2, mlp_w1, mlp_w2, 0, *_mlp_tiles(SEQ))
    x3, u_last = _pool_prompt(x2, g_mix[1], sc1b, sh1b, g1b, pool_w[0], pscale)
    p_pool = u_last[:, 1:][None]
    y_prompt = _mlp(x3, g_mlp[1], sc2b, sh2b, g2b, mlp_w1, mlp_w2, 1, *_mlp_tiles(SEQ))

    rows = SAMPLE_ROWS
    xs0 = _pad_rows(x_sample.reshape(N_SAMPLE, d), rows)[None]
    zx_s, dt_s = _inproj(xs0, g_mix[0], sc1s, sh1s, w_in, ZX_COLS, rows, 512, w_dt, b_dt)
    (qkv_s,) = _inproj(xs0, g_mix[0], sc1s, sh1s, w_qkv, 3 * ATT_QKV, rows, 512)
    zx_s, dt_s, qkv_s = zx_s[0], dt_s[0], qkv_s[0]
    xbc_new = zx_s[:N_SAMPLE, SSD_INNER:]
    s_conv = jnp.concatenate([state_conv[0][:, 1:], xbc_new[:, None, :]], axis=1)[None]
    conv_prev_t = _pad_rows(jnp.transpose(state_conv[0], (1, 0, 2)), rows, axis=1)
    xbc_s = _conv_sample(zx_s, conv_prev_t, cw, cb)
    xs_s = xbc_s[:N_SAMPLE, :SSD_INNER]
    e = SSD_HEADS_PER_GROUP
    b_s = jnp.repeat(xbc_s[:N_SAMPLE, SSD_INNER:SSD_INNER + SSD_GROUPS * SSD_STATE]
                     .reshape(N_SAMPLE, SSD_GROUPS, 1, SSD_STATE), e, axis=1)
    c_s = jnp.repeat(xbc_s[:N_SAMPLE, SSD_INNER + SSD_GROUPS * SSD_STATE:]
                     .reshape(N_SAMPLE, SSD_GROUPS, 1, SSD_STATE), e, axis=1)
    h_new, y_s = _ssd_step(state_ssm[0], xs_s.reshape(N_SAMPLE, SSD_HEADS, SSD_HEAD_DIM, 1),
                           dt_s[:N_SAMPLE, :SSD_HEADS].reshape(N_SAMPLE, SSD_HEADS, 1, 1), b_s, c_s,
                           a_log[0].reshape(SSD_HEADS, 1, 1))
    s_ssm = h_new[None]
    y_s = _gate_sample(_pad_rows(y_s.reshape(N_SAMPLE, SSD_INNER), rows), xbc_s[:, :SSD_INNER],
                       zx_s[:, :SSD_INNER], dskip_x, ssd_g)
    caches = (cache_kv_w128, cache_kv_w512, cache_kv_w2048)
    strided = [c.reshape(N_SAMPLE, ATT_BLOCK, dil, 2, ATT_HPP, ATT_HEAD_DIM) for c, dil in zip(caches, DILATIONS)]
    bias_buf, bias_new = _sample_bias_table(rel_bias)
    att_s, k_new, v_new = _attn_sample(qkv_s[:N_SAMPLE].reshape(N_SAMPLE, 3 * ATT_HEADS, ATT_HEAD_DIM),
                                       strided, bias_buf, bias_new, qg, kg)
    new_rows = jnp.stack([k_new, v_new], axis=2)
    s_kv = [_kv_shift(c, new_rows[:, gi:gi + 1]) for gi, c in enumerate(caches)]
    att_s = _pad_rows(att_s.reshape(N_SAMPLE, ATT_OUT), rows)
    x1s = _outproj(y_s[None], att_s[None], out_proj_w, xs0, g1s, rows, 512)
    x2s = _mlp(x1s, g_mlp[0], sc2s, sh2s, g2s, mlp_w1, mlp_w2, 0, *_mlp_tiles(rows))
    pool_prev_t = _pad_rows(jnp.transpose(state_pool[0], (1, 0, 2)), rows, axis=1)
    x3s, u_s = _pool_sample(x2s[0], g_mix[1], sc1sb[0], sh1sb[0], g1sb[0], pool_prev_t, pool_w[0], pscale)
    s_pool = jnp.concatenate([state_pool[0][:, 1:], u_s[:N_SAMPLE, None, :]], axis=1)[None]
    y_s2 = _mlp(x3s[None], g_mlp[1], sc2sb, sh2sb, g2sb, mlp_w1, mlp_w2, 1, *_mlp_tiles(rows))
    y_sample = y_s2[0, :N_SAMPLE].reshape(N_SAMPLE, 1, d)

    return (y_prompt, y_sample, p_ssm, p_conv, p_kv[0], p_kv[1], p_kv[2], p_pool,
            s_ssm, s_conv, s_kv[0], s_kv[1], s_kv[2], s_pool)
```

```python
import functools
import math

import jax
import jax.numpy as jnp
from jax import lax
from jax.experimental import pallas as pl
from jax.experimental.pallas import tpu as pltpu

F32 = jnp.float32
BF16 = jnp.bfloat16

D_MODEL = 2048
SEQ = 2048
N_PROMPT = 4
N_SAMPLE = 8
SAMPLE_ROWS = 16
PAST_LEN = 16384
SSD_HEADS = 32
SSD_HEAD_DIM = 64
SSD_INNER = SSD_HEADS * SSD_HEAD_DIM
SSD_GROUPS = 8
SSD_HEADS_PER_GROUP = SSD_HEADS // SSD_GROUPS
SSD_STATE = 128
SSD_CONV = 4
SSD_CHUNK = 128
CONV_DIM = SSD_INNER + 2 * SSD_GROUPS * SSD_STATE
ATT_HEAD_DIM = 128
WINDOWS = (128, 512, 2048)
DILATIONS = (1, 4, 16)
ATT_BLOCK = 128
ATT_UNIT_ROWS = 128
ATT_HPP = 8
ATT_HEADS = ATT_HPP * len(WINDOWS)
ATT_QKV = ATT_HEADS * ATT_HEAD_DIM
ATT_OUT = ATT_HPP * ATT_HEAD_DIM
REL_BUCKETS = 32
REL_MAX_DIST = 2048
ZX_COLS = SSD_INNER + CONV_DIM
DT_COL0 = ZX_COLS
QKV_COL0 = ZX_COLS + SSD_HEADS
QKV_LANE_SHIFT = QKV_COL0 % 128
DT_PAD = 128
MIX_OUT = SSD_INNER + ATT_OUT
POOL_WINDOWS = (2, 4, 8, 16)
POOL_CH = D_MODEL // len(POOL_WINDOWS)
POOL_MAX = 16
D_FF = 4 * D_MODEL
EPS = 1e-6
NEG = -1e30
MIB = 1024 * 1024


def _params(semantics, vmem_mib):
    return pltpu.CompilerParams(dimension_semantics=semantics, vmem_limit_bytes=vmem_mib * MIB)


def _sigmoid(x):
    return 0.5 * jnp.tanh(0.5 * x) + 0.5


def _silu(x):
    return x * _sigmoid(x)


def _softplus(x):
    return jnp.maximum(x, 0.0) + jnp.log1p(jnp.exp(-jnp.abs(x)))


def _rms(x):
    return x * lax.rsqrt(jnp.mean(x * x, axis=-1, keepdims=True) + EPS)


def _normmod(x, g, sc, sh):
    return (_rms(x) * g) * (1.0 + sc) + sh


def _normmod_rows(u_ref, x_ref, g_ref, sc_ref, sh_ref):
    tm = u_ref.shape[0]
    rc = min(tm, 256)
    per_row = sc_ref.shape[1] != 1

    def body(i, carry):
        rows = pl.ds(pl.multiple_of(i * rc, rc), rc)
        sc = sc_ref[0, rows, :] if per_row else sc_ref[0]
        sh = sh_ref[0, rows, :] if per_row else sh_ref[0]
        u_ref[rows, :] = _normmod(x_ref[0, rows, :], g_ref[...], sc, sh).astype(BF16)
        return carry

    lax.fori_loop(0, tm // rc, body, 0)


def _bdot(a, b):
    return jnp.dot(a.astype(BF16), b.astype(BF16), preferred_element_type=F32)


def _split3(v):
    hi = v.astype(BF16)
    r = v - hi.astype(F32)
    mid = r.astype(BF16)
    lo = (r - mid.astype(F32)).astype(BF16)
    return hi, mid, lo


def _select_right(v, sel):
    hi, mid, lo = _split3(v)
    d = lambda a: jnp.dot(a, sel, preferred_element_type=F32)
    return d(hi) + d(mid) + d(lo)


def _select_left(sel, v):
    hi, mid, lo = _split3(v)
    d = lambda a: jnp.dot(sel, a, preferred_element_type=F32)
    return d(hi) + d(mid) + d(lo)


def _ada_kernel(c_ref, w_ref, b_ref, o_ref):
    o_ref[0] = _bdot(_silu(c_ref[...]), w_ref[0]) + b_ref[0]


def _ada(c_all, ada_w, ada_b):
    depth, _, n = ada_w.shape
    tn = 1024
    return pl.pallas_call(
        _ada_kernel,
        name="ada_mod",
        grid=(depth, n // tn),
        in_specs=[
            pl.BlockSpec((SAMPLE_ROWS, D_MODEL), lambda l, j: (0, 0)),
            pl.BlockSpec((1, D_MODEL, tn), lambda l, j: (l, 0, j)),
            pl.BlockSpec((1, 1, tn), lambda l, j: (l, 0, j)),
        ],
        out_specs=pl.BlockSpec((1, SAMPLE_ROWS, tn), lambda l, j: (l, 0, j)),
        out_shape=jax.ShapeDtypeStruct((depth, SAMPLE_ROWS, n), F32),
        compiler_params=_params(("arbitrary", "arbitrary"), 40),
    )(c_all, ada_w, ada_b.reshape(depth, 1, n))


def _inproj_kernel(*refs, mode):
    if mode == "dt":
        x_ref, g_ref, sc_ref, sh_ref, w_ref, wdt_ref, dtb_ref, o_ref, dt_ref, u_ref = refs
    else:
        x_ref, g_ref, sc_ref, sh_ref, w_ref, wnext_ref, qg_ref, kg_ref, o_ref, u_ref = refs
    j = pl.program_id(2)
    tn = o_ref.shape[2]

    @pl.when(j == 0)
    def _():
        _normmod_rows(u_ref, x_ref, g_ref, sc_ref, sh_ref)
        if mode == "dt":
            dt_ref[0] = _softplus(_bdot(u_ref[...], wdt_ref[...]) + dtb_ref[...])

    if mode == "dt":
        o_ref[0] = _bdot(u_ref[...], w_ref[...])
    else:
        wcat = jnp.concatenate([w_ref[...], wnext_ref[...]], axis=1).astype(BF16)
        w = pltpu.roll(wcat, tn + 128 - QKV_LANE_SHIFT, axis=1)[:, 0:tn]
        acc = jnp.dot(u_ref[...], w, preferred_element_type=F32)
        q_tiles = ATT_QKV // tn

        @pl.when(j < 2 * q_tiles)
        def _():
            gain = jnp.where(j < q_tiles, qg_ref[...], kg_ref[...])
            for hc in range(tn // ATT_HEAD_DIM):
                cs = slice(hc * ATT_HEAD_DIM, (hc + 1) * ATT_HEAD_DIM)
                o_ref[0, :, cs] = _rms(acc[:, cs]) * gain

        @pl.when(j >= 2 * q_tiles)
        def _():
            o_ref[0] = acc


def _inproj(x, g, sc, sh, w, n_cols, tm, tn, dt_w=None, dt_b=None, qk_gains=None):
    nb, t, d = x.shape
    r = sc.shape[1]
    has_dt = dt_w is not None
    tile0 = 0 if has_dt else (QKV_COL0 - QKV_LANE_SHIFT) // tn
    in_specs = [
        pl.BlockSpec((1, tm, d), lambda b, i, j: (b, i, 0)),
        pl.BlockSpec((1, d), lambda b, i, j: (0, 0)),
        pl.BlockSpec((1, r if r == 1 else tm, d), lambda b, i, j: (b, 0 if r == 1 else i, 0)),
        pl.BlockSpec((1, r if r == 1 else tm, d), lambda b, i, j: (b, 0 if r == 1 else i, 0)),
        pl.BlockSpec((d, tn), lambda b, i, j: (0, tile0 + j)),
    ]
    args = [x, g, sc, sh, w]
    out_specs = [pl.BlockSpec((1, tm, tn), lambda b, i, j: (b, i, j))]
    out_shape = [jax.ShapeDtypeStruct((nb, t, n_cols), F32)]
    if has_dt:
        in_specs += [pl.BlockSpec((d, DT_PAD), lambda b, i, j: (0, 0)),
                     pl.BlockSpec((1, DT_PAD), lambda b, i, j: (0, 0))]
        args += [dt_w, dt_b]
        out_specs.append(pl.BlockSpec((1, tm, DT_PAD), lambda b, i, j: (b, i, 0)))
        out_shape.append(jax.ShapeDtypeStruct((nb, t, DT_PAD), F32))
    else:
        lane_tiles = tn // 128
        in_specs += [pl.BlockSpec((d, 128), lambda b, i, j: (0, (tile0 + j + 1) * lane_tiles))]
        in_specs += [pl.BlockSpec((1, ATT_HEAD_DIM), lambda b, i, j: (0, 0))] * 2
        args += [w, *qk_gains]
    return pl.pallas_call(
        functools.partial(_inproj_kernel, mode="dt" if has_dt else "qkv"),
        name="in_proj",
        grid=(nb, t // tm, n_cols // tn),
        in_specs=in_specs,
        out_specs=out_specs,
        out_shape=out_shape,
        scratch_shapes=[pltpu.VMEM((tm, d), BF16)],
        compiler_params=_params(("arbitrary", "arbitrary", "arbitrary"), 48),
    )(*args)


def _conv_kernel(x_ref, w_ref, b_ref, o_ref, carry_ref, *, tt):
    @pl.when(pl.program_id(2) == 0)
    def _():
        carry_ref[...] = jnp.zeros(carry_ref.shape, F32)

    w = w_ref[...]
    taps = [w[SSD_CONV - 1 - k:SSD_CONV - k] for k in range(SSD_CONV)]

    x = x_ref[0]
    acc = b_ref[...] + taps[0] * x
    for k in range(1, SSD_CONV):
        acc = acc + taps[k] * pltpu.roll(x, k, axis=0)
    o_ref[0] = _silu(acc)

    x8 = x[0:8]
    carry = carry_ref[...]
    row = lax.broadcasted_iota(jnp.int32, x8.shape, 0)
    acc8 = b_ref[...] + taps[0] * x8
    for k in range(1, SSD_CONV):
        acc8 = acc8 + taps[k] * jnp.where(row < k, pltpu.roll(carry, k, axis=0), pltpu.roll(x8, k, axis=0))
    o_ref[0, 0:8, :] = _silu(acc8)
    carry_ref[...] = x[tt - 8:tt]


def _conv_prompt(zx, conv_w, conv_b):
    nb, t, _ = zx.shape
    tt, tc = 512, 512
    col0 = SSD_INNER // tc
    return pl.pallas_call(
        functools.partial(_conv_kernel, tt=tt),
        name="conv_silu",
        grid=(nb, CONV_DIM // tc, t // tt),
        in_specs=[
            pl.BlockSpec((1, tt, tc), lambda b, c, i: (b, i, col0 + c)),
            pl.BlockSpec((SSD_CONV, tc), lambda b, c, i: (0, c)),
            pl.BlockSpec((1, tc), lambda b, c, i: (0, c)),
        ],
        out_specs=pl.BlockSpec((1, tt, tc), lambda b, c, i: (b, i, c)),
        out_shape=jax.ShapeDtypeStruct((nb, t, CONV_DIM), F32),
        scratch_shapes=[pltpu.VMEM((8, tc), F32)],
        compiler_params=_params(("arbitrary", "arbitrary", "arbitrary"), 32),
    )(zx, conv_w, conv_b)


def _ssd_kernel(xs_ref, b_ref, c_ref, dt_ref, z_ref, alog_ref, dskip_ref, g_ref,
                y_ref, hout_ref, h_ref, yacc_ref):
    q = SSD_CHUNK
    gw = SSD_HEADS_PER_GROUP * SSD_HEAD_DIM
    chunk = pl.program_id(1)

    @pl.when(chunk == 0)
    def _():
        h_ref[...] = jnp.zeros(h_ref.shape, F32)

    row = lax.broadcasted_iota(jnp.int32, (q, q), 0)
    col = lax.broadcasted_iota(jnp.int32, (q, q), 1)
    causal = row >= col
    tri = jnp.where(causal, 1.0, 0.0).astype(BF16)
    head_of_lane = lax.broadcasted_iota(jnp.int32, (DT_PAD, SSD_INNER), 1) // SSD_HEAD_DIM
    expand = jnp.where(head_of_lane == lax.broadcasted_iota(jnp.int32, (DT_PAD, SSD_INNER), 0), 1.0, 0.0).astype(BF16)

    dt = dt_ref[0]
    a = dt * (-jnp.exp(alog_ref[...]))
    acum = _select_left(tri, a)
    acum_t = acum.T
    dt_t = dt.T
    last = acum[q - 1:q, :]
    decay_end_x = _select_right(jnp.exp(last - acum) * dt, expand)
    eacum_x = _select_right(jnp.exp(acum), expand)
    chunk_decay_x = _select_right(jnp.broadcast_to(jnp.exp(last), (8, DT_PAD)), expand)[0:1]

    xs = xs_ref[0]
    xw = (xs * decay_end_x).astype(BF16)
    xsb = xs.astype(BF16)
    head_in_group = lax.broadcasted_iota(jnp.int32, (q, gw), 1) // SSD_HEAD_DIM

    for g in range(SSD_GROUPS):
        ns = slice(g * SSD_STATE, (g + 1) * SSD_STATE)
        cs = slice(g * gw, (g + 1) * gw)
        cg = c_ref[0, :, ns].astype(BF16)
        bg_t = b_ref[0, :, ns].T.astype(BF16)
        cb = jnp.dot(cg, bg_t, preferred_element_type=F32)
        h_prev = h_ref[:, cs]
        y_g = jnp.dot(cg, h_prev.astype(BF16), preferred_element_type=F32) * eacum_x[:, cs]
        h_ref[:, cs] = h_prev * chunk_decay_x[:, cs] + jnp.dot(bg_t, xw[:, cs], preferred_element_type=F32)
        xg = xsb[:, cs]
        for e in range(SSD_HEADS_PER_GROUP):
            hd = g * SSD_HEADS_PER_GROUP + e
            seg = acum[:, hd:hd + 1] - acum_t[hd:hd + 1, :]
            lw = jnp.where(causal, jnp.exp(jnp.where(causal, seg, 0.0)), 0.0) * dt_t[hd:hd + 1, :]
            yd = jnp.dot((cb * lw).astype(BF16), xg, preferred_element_type=F32)
            y_g = y_g + jnp.where(head_in_group == e, yd, 0.0)
        yacc_ref[:, cs] = y_g

    y = yacc_ref[...] + dskip_ref[...] * xs
    y = y * _silu(z_ref[0])
    y_ref[0] = (_rms(y) * g_ref[...]).astype(BF16)

    @pl.when(chunk == pl.num_programs(1) - 1)
    def _():
        hout_ref[0] = h_ref[...].T


def _ssd_prompt(xbc, dt, zx, a_log_pad, dskip_x, norm_g):
    nb, t, _ = xbc.shape
    q = SSD_CHUNK
    bc_w = SSD_GROUPS * SSD_STATE
    return pl.pallas_call(
        _ssd_kernel,
        name="ssd_chunk",
        grid=(nb, t // q),
        in_specs=[
            pl.BlockSpec((1, q, SSD_INNER), lambda b, c: (b, c, 0)),
            pl.BlockSpec((1, q, bc_w), lambda b, c: (b, c, SSD_INNER // bc_w)),
            pl.BlockSpec((1, q, bc_w), lambda b, c: (b, c, SSD_INNER // bc_w + 1)),
            pl.BlockSpec((1, q, DT_PAD), lambda b, c: (b, c, 0)),
            pl.BlockSpec((1, q, SSD_INNER), lambda b, c: (b, c, 0)),
            pl.BlockSpec((1, DT_PAD), lambda b, c: (0, 0)),
            pl.BlockSpec((1, SSD_INNER), lambda b, c: (0, 0)),
            pl.BlockSpec((1, SSD_INNER), lambda b, c: (0, 0)),
        ],
        out_specs=[
            pl.BlockSpec((1, q, SSD_INNER), lambda b, c: (b, c, 0)),
            pl.BlockSpec((1, SSD_INNER, SSD_STATE), lambda b, c: (b, 0, 0)),
        ],
        out_shape=[
            jax.ShapeDtypeStruct((nb, t, SSD_INNER), BF16),
            jax.ShapeDtypeStruct((nb, SSD_INNER, SSD_STATE), F32),
        ],
        scratch_shapes=[pltpu.VMEM((SSD_STATE, SSD_INNER), F32), pltpu.VMEM((q, SSD_INNER), F32)],
        compiler_params=_params(("arbitrary", "arbitrary"), 40),
    )(xbc, xbc, xbc, dt, zx, a_log_pad, dskip_x, norm_g)


def _attn_kernel(q0, q1, q2, k0, k1, k2, v0, v1, v2, bias0, bias1, bias2,
                 att_ref, kv0_ref, kv1_ref, kv2_ref, qb_s, kb_s, vb_s, ok_s, os_, ls):
    blk = ATT_BLOCK
    ur = ATT_UNIT_ROWS
    hd = ATT_HEAD_DIM
    scale = 1.0 / math.sqrt(hd)
    sel = pl.program_id(2)
    qrefs, krefs, vrefs, brefs = (q0, q1, q2), (k0, k1, k2), (v0, v1, v2), (bias0, bias1, bias2)
    kvrefs = (kv0_ref, kv1_ref, kv2_ref)
    nt = (((1,), (1,)), ((), ()))

    @pl.when(sel == 0)
    def _():
        row = lax.broadcasted_iota(jnp.int32, (blk, 2 * blk), 0)
        col = lax.broadcasted_iota(jnp.int32, (blk, 2 * blk), 1)
        ok_s[...] = jnp.where(jnp.where(col < blk, col - row, row - (col - blk)) >= 0, 1.0, 0.0)

        for g, dil in enumerate(DILATIONS):
            sub_len = SEQ // dil
            ch = min(sub_len, 256)
            vb_s[g, :, hd:2 * hd] = jnp.ones((SEQ, hd), BF16)
            for r in range(dil):
                for c in range(sub_len // ch):
                    src = pl.ds(r + c * ch * dil, ch) if dil == 1 else pl.ds(r + c * ch * dil, ch, stride=dil)
                    dst = pl.ds(r * sub_len + c * ch, ch)
                    qb_s[g, dst, :] = qrefs[g][0, src, :].astype(BF16)
                    kb_s[g, dst, :] = krefs[g][0, src, :].astype(BF16)
                    vb_s[g, dst, 0:hd] = vrefs[g][0, src, :].astype(BF16)

        for g, dil in enumerate(DILATIONS):
            nblk = SEQ // (blk * dil)
            for r in range(dil):
                for n in range(nblk):
                    base = (r * nblk + n) * blk
                    key0, col0 = (base - blk, 0) if n > 0 else (base, blk)
                    kcat = kb_s[g, key0:base + blk, :]
                    vcat = vb_s[g, key0:base + blk, :]
                    for half in range(blk // ur):
                        rs = slice(half * ur, (half + 1) * ur)
                        qb = qb_s[g, base + half * ur:base + (half + 1) * ur, :]
                        l = lax.dot_general(qb, kcat, nt, preferred_element_type=F32)
                        l = jnp.where(ok_s[rs, col0:2 * blk] > 0.5, l * scale + brefs[g][0, rs, col0:2 * blk], NEG)
                        m = jnp.max(l, axis=-1, keepdims=True)
                        p = jnp.exp(l - m).astype(BF16)
                        oe = jnp.dot(p, vcat, preferred_element_type=F32)
                        den = oe[:, hd:2 * hd]
                        t0 = r + (n * blk + half * ur) * dil
                        dst = pl.ds(t0, ur) if dil == 1 else pl.ds(t0, ur, stride=dil)
                        os_[g, dst, :] = oe[:, 0:hd] / den
                        ls[g, dst, :] = m + jnp.log(den)

        def mix_body(i, carry):
            rows = pl.ds(pl.multiple_of(i * blk, blk), blk)
            l0, l1, l2 = ls[0, rows, :], ls[1, rows, :], ls[2, rows, :]
            mx = jnp.maximum(jnp.maximum(l0, l1), l2)
            e0, e1, e2 = jnp.exp(l0 - mx), jnp.exp(l1 - mx), jnp.exp(l2 - mx)
            num = e0 * os_[0, rows, :] + e1 * os_[1, rows, :] + e2 * os_[2, rows, :]
            att_ref[0, rows, :] = (num / (e0 + e1 + e2)).astype(BF16)
            return carry
        lax.fori_loop(0, SEQ // blk, mix_body, 0)

        for g, w in enumerate(WINDOWS):
            kvrefs[g][0] = krefs[g][0, SEQ - w:SEQ, :]

    @pl.when(sel == 1)
    def _():
        for g, w in enumerate(WINDOWS):
            kvrefs[g][0] = vrefs[g][0, SEQ - w:SEQ, :]


def _attn_prompt(qkv, bias):
    nb = qkv.shape[0]
    hd = ATT_HEAD_DIM

    def head_spec(part, g):
        return pl.BlockSpec((1, SEQ, hd), lambda b, h, s: (b, 0, part * ATT_HEADS + g * ATT_HPP + h))

    in_specs = [head_spec(part, g) for part in range(3) for g in range(3)]
    in_specs += [pl.BlockSpec((1, ATT_BLOCK, 2 * ATT_BLOCK), functools.partial(lambda b, h, s, g: (g * ATT_HPP + h, 0, 0), g=g))
                 for g in range(3)]
    out_specs = [pl.BlockSpec((1, SEQ, hd), lambda b, h, s: (b, 0, h))]
    out_shape = [jax.ShapeDtypeStruct((nb, SEQ, ATT_OUT), BF16)]
    for w in WINDOWS:
        out_specs.append(pl.BlockSpec((1, w, hd), lambda b, h, s: (b, 0, s * ATT_HPP + h)))
        out_shape.append(jax.ShapeDtypeStruct((nb, w, 2 * ATT_OUT), F32))
    return pl.pallas_call(
        _attn_kernel,
        name="attn_prompt",
        grid=(nb, ATT_HPP, 2),
        in_specs=in_specs,
        out_specs=out_specs,
        out_shape=out_shape,
        scratch_shapes=[
            pltpu.VMEM((3, SEQ, hd), BF16), pltpu.VMEM((3, SEQ, hd), BF16), pltpu.VMEM((3, SEQ, 2 * hd), BF16),
            pltpu.VMEM((ATT_BLOCK, 2 * ATT_BLOCK), F32),
            pltpu.VMEM((3, SEQ, hd), F32), pltpu.VMEM((3, SEQ, hd), F32),
        ],
        compiler_params=_params(("arbitrary", "arbitrary", "arbitrary"), 48),
    )(*([qkv] * 9), bias, bias, bias)


def _outproj_kernel(y_ref, a_ref, wy_ref, wa_ref, x_ref, g1_ref, o_ref):
    mix = _bdot(y_ref[0], wy_ref[0]) + _bdot(a_ref[0], wa_ref[0])
    o_ref[0] = x_ref[0] + g1_ref[0] * mix


def _outproj(y, att, w, x, g1, tm, tn):
    nb, t, d = x.shape
    r = g1.shape[1]
    return pl.pallas_call(
        _outproj_kernel,
        name="out_proj",
        grid=(nb, t // tm, d // tn),
        in_specs=[
            pl.BlockSpec((1, tm, SSD_INNER), lambda b, i, j: (b, i, 0)),
            pl.BlockSpec((1, tm, ATT_OUT), lambda b, i, j: (b, i, 0)),
            pl.BlockSpec((1, SSD_INNER, tn), lambda b, i, j: (0, 0, j)),
            pl.BlockSpec((1, ATT_OUT, tn), lambda b, i, j: (0, SSD_INNER // ATT_OUT, j)),
            pl.BlockSpec((1, tm, tn), lambda b, i, j: (b, i, j)),
            pl.BlockSpec((1, r if r == 1 else tm, tn), lambda b, i, j: (b, 0 if r == 1 else i, j)),
        ],
        out_specs=pl.BlockSpec((1, tm, tn), lambda b, i, j: (b, i, j)),
        out_shape=jax.ShapeDtypeStruct((nb, t, d), F32),
        compiler_params=_params(("arbitrary", "arbitrary", "arbitrary"), 48),
    )(y, att, w, w, x, g1)


def _mlp_kernel(x_ref, g_ref, sc_ref, sh_ref, g2_ref, w1_ref, w2_ref, o_ref, u_ref):
    f = pl.program_id(2)

    @pl.when(f == 0)
    def _():
        _normmod_rows(u_ref, x_ref, g_ref, sc_ref, sh_ref)
        o_ref[0] = jnp.zeros(o_ref.shape[1:], F32)

    h = jnp.maximum(_bdot(u_ref[...], w1_ref[0]), 0.0)
    o_ref[0] += _bdot(h * h, w2_ref[0])

    @pl.when(f == pl.num_programs(2) - 1)
    def _():
        o_ref[0] = x_ref[0] + g2_ref[0] * o_ref[0]


def _mlp(x, g, sc, sh, g2, w1, w2, layer, tm, tf):
    nb, t, d = x.shape
    r = sc.shape[1]
    mod_spec = pl.BlockSpec((1, r if r == 1 else tm, d), lambda b, i, f: (b, 0 if r == 1 else i, 0))
    return pl.pallas_call(
        _mlp_kernel,
        name="mlp",
        grid=(nb, t // tm, D_FF // tf),
        in_specs=[
            pl.BlockSpec((1, tm, d), lambda b, i, f: (b, i, 0), pipeline_mode=pl.Buffered(1)),
            pl.BlockSpec((1, d), lambda b, i, f: (0, 0)),
            mod_spec, mod_spec, mod_spec,
            pl.BlockSpec((1, d, tf), lambda b, i, f: (layer, 0, f)),
            pl.BlockSpec((1, tf, d), lambda b, i, f: (layer, f, 0)),
        ],
        out_specs=pl.BlockSpec((1, tm, d), lambda b, i, f: (b, i, 0), pipeline_mode=pl.Buffered(1)),
        out_shape=jax.ShapeDtypeStruct((nb, t, d), F32),
        scratch_shapes=[pltpu.VMEM((tm, d), BF16)],
        compiler_params=_params(("arbitrary", "arbitrary", "arbitrary"), 56),
    )(x, g, sc, sh, g2, w1, w2)


def _pool_kernel(x_ref, g_ref, sc_ref, sh_ref, g1_ref, w_ref, scale_ref, o_ref, ulast_ref, ext_ref, *, tt):
    halo = POOL_MAX
    ti = pl.program_id(1)

    @pl.when(ti == 0)
    def _():
        ext_ref[0:halo, :] = jnp.zeros((halo, D_MODEL), F32)

    x = x_ref[0]
    u = _normmod(x, g_ref[...], sc_ref[0], sh_ref[0])
    ext_ref[halo:halo + tt, :] = u
    pos = ti * tt + lax.broadcasted_iota(jnp.int32, (tt, 1), 0)
    for gi, w in enumerate(POOL_WINDOWS):
        ch = slice(gi * POOL_CH, (gi + 1) * POOL_CH)
        s = u[:, ch]
        for k in range(1, w):
            s = s + ext_ref[pl.ds(halo - k, tt), ch]
        cnt = jnp.minimum(w, pos + 1).astype(F32)
        pooled = s / cnt - u[:, ch]
        mix = _bdot(pooled, w_ref[gi]) * scale_ref[:, ch]
        o_ref[0, :, ch] = x[:, ch] + g1_ref[0, :, ch] * mix
    ulast_ref[0] = u[tt - halo:tt]
    ext_ref[0:halo, :] = u[tt - halo:tt]


def _pool_prompt(x, g, sc, sh, g1, pool_w, pool_scale):
    nb, t, d = x.shape
    tt = 512
    mod_spec = pl.BlockSpec((1, 1, d), lambda b, i: (b, 0, 0))
    return pl.pallas_call(
        functools.partial(_pool_kernel, tt=tt),
        name="pool_mix",
        grid=(nb, t // tt),
        in_specs=[
            pl.BlockSpec((1, tt, d), lambda b, i: (b, i, 0)),
            pl.BlockSpec((1, d), lambda b, i: (0, 0)),
            mod_spec, mod_spec, mod_spec,
            pl.BlockSpec((len(POOL_WINDOWS), POOL_CH, POOL_CH), lambda b, i: (0, 0, 0)),
            pl.BlockSpec((1, d), lambda b, i: (0, 0)),
        ],
        out_specs=[
            pl.BlockSpec((1, tt, d), lambda b, i: (b, i, 0)),
            pl.BlockSpec((1, POOL_MAX, d), lambda b, i: (b, 0, 0)),
        ],
        out_shape=[jax.ShapeDtypeStruct((nb, t, d), F32), jax.ShapeDtypeStruct((nb, POOL_MAX, d), F32)],
        scratch_shapes=[pltpu.VMEM((tt + POOL_MAX, d), F32)],
        compiler_params=_params(("arbitrary", "arbitrary"), 48),
    )(x, g, sc, sh, g1, pool_w, pool_scale)


def _sconv_kernel(zx_ref, prev_ref, w_ref, b_ref, o_ref):
    w = w_ref[...]
    acc = b_ref[...] + w[SSD_CONV - 1:SSD_CONV] * zx_ref[:, SSD_INNER:ZX_COLS]
    for k in range(SSD_CONV - 1):
        acc = acc + w[k:k + 1] * prev_ref[k]
    o_ref[...] = _silu(acc)


def _conv_sample(zx, prev_t, conv_w, conv_b):
    return pl.pallas_call(
        _sconv_kernel,
        out_shape=jax.ShapeDtypeStruct((SAMPLE_ROWS, CONV_DIM), F32),
        compiler_params=pltpu.CompilerParams(vmem_limit_bytes=32 * MIB),
    )(zx, prev_t, conv_w, conv_b)


def _sstep_kernel(h0_ref, x_ref, dt_ref, b_ref, c_ref, alog_ref, h_ref, y_ref):
    dt = dt_ref[0]
    h = jnp.exp(dt * (-jnp.exp(alog_ref[...]))) * h0_ref[0] + (dt * x_ref[0]) * b_ref[0]
    h_ref[0] = h
    y_ref[0] = jnp.sum(h * c_ref[0], axis=-1, keepdims=True)


def _ssd_step(h0, x, dt, bh, ch, a_log):
    nb = h0.shape[0]
    hp = (SSD_HEADS, SSD_HEAD_DIM)
    return pl.pallas_call(
        _sstep_kernel,
        grid=(nb,),
        in_specs=[
            pl.BlockSpec((1, *hp, SSD_STATE), lambda b: (b, 0, 0, 0)),
            pl.BlockSpec((1, *hp, 1), lambda b: (b, 0, 0, 0)),
            pl.BlockSpec((1, SSD_HEADS, 1, 1), lambda b: (b, 0, 0, 0)),
            pl.BlockSpec((1, SSD_HEADS, 1, SSD_STATE), lambda b: (b, 0, 0, 0)),
            pl.BlockSpec((1, SSD_HEADS, 1, SSD_STATE), lambda b: (b, 0, 0, 0)),
            pl.BlockSpec((SSD_HEADS, 1, 1), lambda b: (0, 0, 0)),
        ],
        out_specs=[
            pl.BlockSpec((1, *hp, SSD_STATE), lambda b: (b, 0, 0, 0)),
            pl.BlockSpec((1, *hp, 1), lambda b: (b, 0, 0, 0)),
        ],
        out_shape=[jax.ShapeDtypeStruct((nb, *hp, SSD_STATE), F32), jax.ShapeDtypeStruct((nb, *hp, 1), F32)],
        compiler_params=_params(("arbitrary",), 32),
    )(h0, x, dt, bh, ch, a_log)


def _sgate_kernel(y_ref, xs_ref, z_ref, dskip_ref, g_ref, o_ref):
    y = (y_ref[...] + dskip_ref[...] * xs_ref[...]) * _silu(z_ref[...])
    o_ref[...] = _rms(y) * g_ref[...]


def _gate_sample(y, xs, z, dskip_x, norm_g):
    return pl.pallas_call(
        _sgate_kernel,
        out_shape=jax.ShapeDtypeStruct(y.shape, F32),
    )(y, xs, z, dskip_x, norm_g)


def _sattn_kernel(qkv_ref, c0_ref, c1_ref, c2_ref, bbuf_ref, bnew_ref, att_ref, knew_ref, vnew_ref):
    scale = 1.0 / math.sqrt(ATT_HEAD_DIM)
    caches = (c0_ref, c1_ref, c2_ref)
    outs, lses = [], []
    for g in range(len(WINDOWS)):
        hs = g * ATT_HPP
        q = qkv_ref[0, hs:hs + ATT_HPP, :]
        k = qkv_ref[0, ATT_HEADS + hs:ATT_HEADS + hs + ATT_HPP, :]
        v = qkv_ref[0, 2 * ATT_HEADS + hs:2 * ATT_HEADS + hs + ATT_HPP, :]
        knew_ref[0, g] = k
        vnew_ref[0, g] = v
        kbuf = caches[g][:, 0]
        vbuf = caches[g][:, 1]
        lb = jnp.sum(kbuf * q[None], axis=-1, keepdims=True) * scale + bbuf_ref[g]
        ln = jnp.sum(k * q, axis=-1, keepdims=True) * scale + bnew_ref[g]
        m = jnp.maximum(jnp.max(lb, axis=0), ln)
        pb = jnp.exp(lb - m[None])
        pn = jnp.exp(ln - m)
        s = jnp.sum(pb, axis=0) + pn
        outs.append((jnp.sum(pb * vbuf, axis=0) + pn * v) / s)
        lses.append(m + jnp.log(s))
    mx = jnp.maximum(jnp.maximum(lses[0], lses[1]), lses[2])
    es = [jnp.exp(l - mx) for l in lses]
    att_ref[0] = (es[0] * outs[0] + es[1] * outs[1] + es[2] * outs[2]) / (es[0] + es[1] + es[2])


def _attn_sample(qkv, caches, bias_buf, bias_new):
    nb = qkv.shape[0]
    hd = ATT_HEAD_DIM
    in_specs = [pl.BlockSpec((1, 3 * ATT_HEADS, hd), lambda b: (b, 0, 0))]
    for _ in caches:
        in_specs.append(pl.BlockSpec((None, ATT_BLOCK, None, 2, ATT_HPP, hd), lambda b: (b, 0, 0, 0, 0, 0)))
    in_specs += [
        pl.BlockSpec((3, ATT_BLOCK, ATT_HPP, 1), lambda b: (0, 0, 0, 0)),
        pl.BlockSpec((3, ATT_HPP, 1), lambda b: (0, 0, 0)),
    ]
    return pl.pallas_call(
        _sattn_kernel,
        grid=(nb,),
        in_specs=in_specs,
        out_specs=[
            pl.BlockSpec((1, ATT_HPP, hd), lambda b: (b, 0, 0)),
            pl.BlockSpec((1, 3, ATT_HPP, hd), lambda b: (b, 0, 0, 0)),
            pl.BlockSpec((1, 3, ATT_HPP, hd), lambda b: (b, 0, 0, 0)),
        ],
        out_shape=[
            jax.ShapeDtypeStruct((nb, ATT_HPP, hd), F32),
            jax.ShapeDtypeStruct((nb, 3, ATT_HPP, hd), F32),
            jax.ShapeDtypeStruct((nb, 3, ATT_HPP, hd), F32),
        ],
        compiler_params=_params(("arbitrary",), 32),
    )(qkv, *caches, bias_buf, bias_new)


def _kvshift_kernel(x_ref, new_ref, o_ref, carry_ref):
    tw = x_ref.shape[0]

    @pl.when(pl.program_id(1) == 0)
    def _():
        carry_ref[...] = new_ref[0]

    o_ref[0:tw - 1] = x_ref[1:tw]
    o_ref[tw - 1] = carry_ref[...]
    carry_ref[...] = x_ref[0]


def _kv_shift(cache, new_row):
    _, nb, w, _, hpp, hd = cache.shape
    tw = min(w, 512)
    nblk = w // tw
    row = (2, hpp, hd)
    return pl.pallas_call(
        _kvshift_kernel,
        name="kv_shift",
        grid=(nb, nblk),
        in_specs=[
            pl.BlockSpec((None, None, tw, *row), lambda b, s: (0, b, nblk - 1 - s, 0, 0, 0)),
            pl.BlockSpec((None, 1, *row), lambda b, s: (b, 0, 0, 0, 0)),
        ],
        out_specs=pl.BlockSpec((None, None, tw, *row), lambda b, s: (0, b, nblk - 1 - s, 0, 0, 0)),
        out_shape=jax.ShapeDtypeStruct(cache.shape, cache.dtype),
        scratch_shapes=[pltpu.VMEM(row, F32)],
        compiler_params=_params(("arbitrary", "arbitrary"), 40),
    )(cache, new_row)


def _spool_kernel(x_ref, g_ref, sc_ref, sh_ref, g1_ref, prev_ref, w_ref, scale_ref, o_ref, u_ref):
    x = x_ref[...]
    u = _normmod(x, g_ref[...], sc_ref[...], sh_ref[...])
    u_ref[...] = u
    for gi, w in enumerate(POOL_WINDOWS):
        ch = slice(gi * POOL_CH, (gi + 1) * POOL_CH)
        s = u[:, ch]
        for k in range(1, w):
            s = s + prev_ref[POOL_MAX - 1 - k, :, ch]
        pooled = s / float(min(w, PAST_LEN + 1)) - u[:, ch]
        mix = _bdot(pooled, w_ref[gi]) * scale_ref[:, ch]
        o_ref[:, ch] = x[:, ch] + g1_ref[:, ch] * mix


def _pool_sample(x, g, sc, sh, g1, prev_t, pool_w, pool_scale):
    return pl.pallas_call(
        _spool_kernel,
        out_shape=[jax.ShapeDtypeStruct(x.shape, F32), jax.ShapeDtypeStruct(x.shape, F32)],
        compiler_params=pltpu.CompilerParams(vmem_limit_bytes=32 * MIB),
    )(x, g, sc, sh, g1, prev_t, pool_w, pool_scale)


def _rel_bucket(dist):
    max_exact = REL_BUCKETS // 2
    n = jnp.maximum(dist, 1).astype(F32)
    large = max_exact + (jnp.log(n / max_exact) / math.log(REL_MAX_DIST / max_exact)
                         * (REL_BUCKETS - max_exact)).astype(jnp.int32)
    large = jnp.minimum(large, REL_BUCKETS - 1)
    return jnp.where(dist < max_exact, dist, large)


def _prompt_bias_table(rel_bias):
    blk = ATT_BLOCK
    j = blk + jnp.arange(blk)[:, None] - jnp.arange(2 * blk)[None, :]
    tables = []
    for gi, dil in enumerate(DILATIONS):
        bucket = _rel_bucket(jnp.clip(j, 0, blk) * dil)[None]
        rb = rel_bias[:, gi * ATT_HPP:(gi + 1) * ATT_HPP].astype(F32)
        tbl = jnp.zeros((ATT_HPP, blk, 2 * blk), F32)
        for k in range(REL_BUCKETS):
            tbl = jnp.where(bucket == k, rb[k][:, None, None], tbl)
        tables.append(tbl)
    return jnp.concatenate(tables, axis=0)


def _sample_bias_table(rel_bias):
    blk = ATT_BLOCK
    bufs, news = [], []
    for gi, dil in enumerate(DILATIONS):
        tbl = rel_bias[:, gi * ATT_HPP:(gi + 1) * ATT_HPP][_rel_bucket(dil * jnp.arange(blk + 1))]
        bufs.append(tbl[blk:0:-1])
        news.append(tbl[0])
    return jnp.stack(bufs)[..., None].astype(F32), jnp.stack(news)[..., None].astype(F32)


def _pad_rows(a, rows, axis=0):
    pad = [(0, 0)] * a.ndim
    pad[axis] = (0, rows - a.shape[axis])
    return jnp.pad(a, pad)


def _mlp_tiles(t):
    return (1024, 512) if t >= 1024 else (t, 512)


def kernel(x_prompt, x_sample, state_ssm, state_conv, cache_kv_w128, cache_kv_w512, cache_kv_w2048, state_pool, c_prompt, c_sample, rel_bias, ada_w, ada_b, norm_mix_g, norm_mlp_g, in_proj_w, conv_w, conv_b, dt_bias, a_log, d_skip, ssd_norm_g, q_norm_g, k_norm_g, out_proj_w, pool_w, pool_scale, mlp_w1, mlp_w2):
    d = D_MODEL
    c_all = _pad_rows(jnp.concatenate([c_prompt, c_sample], axis=0), SAMPLE_ROWS)
    mod = _ada(c_all, ada_w, ada_b)

    def mods(layer):
        parts = [mod[layer, :, i * d:(i + 1) * d] for i in range(6)]
        prompt = [p[:N_PROMPT, None, :] for p in parts]
        sample = [_pad_rows(p[N_PROMPT:N_PROMPT + N_SAMPLE], SAMPLE_ROWS)[None] for p in parts]
        return prompt, sample

    w_in = in_proj_w.reshape(d, -1).astype(BF16)
    w_dt = jnp.pad(w_in[:, DT_COL0:QKV_COL0], ((0, 0), (0, DT_PAD - SSD_HEADS)))
    b_dt = jnp.pad(dt_bias[0], (0, DT_PAD - SSD_HEADS)).reshape(1, DT_PAD)
    a_log_pad = jnp.pad(a_log[0], (0, DT_PAD - SSD_HEADS)).reshape(1, DT_PAD)
    dskip_x = jnp.repeat(d_skip[0], SSD_HEAD_DIM).reshape(1, SSD_INNER)
    ssd_g = ssd_norm_g.reshape(1, SSD_INNER)
    qg, kg = q_norm_g.reshape(1, ATT_HEAD_DIM), k_norm_g.reshape(1, ATT_HEAD_DIM)
    cw, cb = conv_w[0], conv_b.reshape(1, CONV_DIM)
    g_mix = [norm_mix_g[l].reshape(1, d) for l in range(2)]
    g_mlp = [norm_mlp_g[l].reshape(1, d) for l in range(2)]
    pscale = pool_scale.reshape(1, d)

    (sh1, sc1, g1, sh2, sc2, g2), (sh1s, sc1s, g1s, sh2s, sc2s, g2s) = mods(0)
    (sh1b, sc1b, g1b, sh2b, sc2b, g2b), (sh1sb, sc1sb, g1sb, sh2sb, sc2sb, g2sb) = mods(1)

    zx, dt = _inproj(x_prompt, g_mix[0], sc1, sh1, w_in, ZX_COLS, 1024, 512, w_dt, b_dt)
    (qkv,) = _inproj(x_prompt, g_mix[0], sc1, sh1, w_in, 3 * ATT_QKV, 1024, 512, qk_gains=(qg, kg))
    p_conv = zx[:, SEQ - (SSD_CONV - 1):, SSD_INNER:][None]
    xbc = _conv_prompt(zx, cw, cb)
    y_ssd, h_fin = _ssd_prompt(xbc, dt, zx, a_log_pad, dskip_x, ssd_g)
    p_ssm = h_fin.reshape(1, N_PROMPT, SSD_HEADS, SSD_HEAD_DIM, SSD_STATE)
    att, kv128, kv512, kv2048 = _attn_prompt(qkv, _prompt_bias_table(rel_bias))
    p_kv = [kv.reshape(1, N_PROMPT, w, 2, ATT_HPP, ATT_HEAD_DIM) for kv, w in zip((kv128, kv512, kv2048), WINDOWS)]
    x1 = _outproj(y_ssd, att, out_proj_w, x_prompt, g1, 1024, 512)
    x2 = _mlp(x1, g_mlp[0], sc2, sh2, g2, mlp_w1, mlp_w2, 0, *_mlp_tiles(SEQ))
    x3, u_last = _pool_prompt(x2, g_mix[1], sc1b, sh1b, g1b, pool_w[0], pscale)
    p_pool = u_last[:, 1:][None]
    y_prompt = _mlp(x3, g_mlp[1], sc2b, sh2b, g2b, mlp_w1, mlp_w2, 1, *_mlp_tiles(SEQ))

    rows = SAMPLE_ROWS
    xs0 = _pad_rows(x_sample.reshape(N_SAMPLE, d), rows)[None]
    zx_s, dt_s = _inproj(xs0, g_mix[0], sc1s, sh1s, w_in, ZX_COLS, rows, 512, w_dt, b_dt)
    (qkv_s,) = _inproj(xs0, g_mix[0], sc1s, sh1s, w_in, 3 * ATT_QKV, rows, 512, qk_gains=(qg, kg))
    zx_s, dt_s, qkv_s = zx_s[0], dt_s[0], qkv_s[0]
    xbc_new = zx_s[:N_SAMPLE, SSD_INNER:]
    s_conv = jnp.concatenate([state_conv[0][:, 1:], xbc_new[:, None, :]], axis=1)[None]
    conv_prev_t = _pad_rows(jnp.transpose(state_conv[0], (1, 0, 2)), rows, axis=1)
    xbc_s = _conv_sample(zx_s, conv_prev_t, cw, cb)
    xs_s = xbc_s[:N_SAMPLE, :SSD_INNER]
    e = SSD_HEADS_PER_GROUP
    b_s = jnp.repeat(xbc_s[:N_SAMPLE, SSD_INNER:SSD_INNER + SSD_GROUPS * SSD_STATE]
                     .reshape(N_SAMPLE, SSD_GROUPS, 1, SSD_STATE), e, axis=1)
    c_s = jnp.repeat(xbc_s[:N_SAMPLE, SSD_INNER + SSD_GROUPS * SSD_STATE:]
                     .reshape(N_SAMPLE, SSD_GROUPS, 1, SSD_STATE), e, axis=1)
    h_new, y_s = _ssd_step(state_ssm[0], xs_s.reshape(N_SAMPLE, SSD_HEADS, SSD_HEAD_DIM, 1),
                           dt_s[:N_SAMPLE, :SSD_HEADS].reshape(N_SAMPLE, SSD_HEADS, 1, 1), b_s, c_s,
                           a_log[0].reshape(SSD_HEADS, 1, 1))
    s_ssm = h_new[None]
    y_s = _gate_sample(_pad_rows(y_s.reshape(N_SAMPLE, SSD_INNER), rows), xbc_s[:, :SSD_INNER],
                       zx_s[:, :SSD_INNER], dskip_x, ssd_g)
    caches = (cache_kv_w128, cache_kv_w512, cache_kv_w2048)
    strided = [c.reshape(N_SAMPLE, ATT_BLOCK, dil, 2, ATT_HPP, ATT_HEAD_DIM) for c, dil in zip(caches, DILATIONS)]
    bias_buf, bias_new = _sample_bias_table(rel_bias)
    att_s, k_new, v_new = _attn_sample(qkv_s[:N_SAMPLE].reshape(N_SAMPLE, 3 * ATT_HEADS, ATT_HEAD_DIM),
                                       strided, bias_buf, bias_new)
    new_rows = jnp.stack([k_new, v_new], axis=2)
    s_kv = [_kv_shift(c, new_rows[:, gi:gi + 1]) for gi, c in enumerate(caches)]
    att_s = _pad_rows(att_s.reshape(N_SAMPLE, ATT_OUT), rows)
    x1s = _outproj(y_s[None], att_s[None], out_proj_w, xs0, g1s, rows, 512)
    x2s = _mlp(x1s, g_mlp[0], sc2s, sh2s, g2s, mlp_w1, mlp_w2, 0, *_mlp_tiles(rows))
    pool_prev_t = _pad_rows(jnp.transpose(state_pool[0], (1, 0, 2)), rows, axis=1)
    x3s, u_s = _pool_sample(x2s[0], g_mix[1], sc1sb[0], sh1sb[0], g1sb[0], pool_prev_t, pool_w[0], pscale)
    s_pool = jnp.concatenate([state_pool[0][:, 1:], u_s[:N_SAMPLE, None, :]], axis=1)[None]
    y_s2 = _mlp(x3s[None], g_mlp[1], sc2sb, sh2sb, g2sb, mlp_w1, mlp_w2, 1, *_mlp_tiles(rows))
    y_sample = y_s2[0, :N_SAMPLE].reshape(N_SAMPLE, 1, d)

    return (y_prompt, y_sample, p_ssm, p_conv, p_kv[0], p_kv[1], p_kv[2], p_pool,
            s_ssm, s_conv, s_kv[0], s_kv[1], s_kv[2], s_pool)
```

```python
import functools
import math

import jax
import jax.numpy as jnp
from jax import lax
from jax.experimental import pallas as pl
from jax.experimental.pallas import tpu as pltpu

F32 = jnp.float32
BF16 = jnp.bfloat16

D_MODEL = 2048
SEQ = 2048
N_PROMPT = 4
N_SAMPLE = 8
SAMPLE_ROWS = 16
PAST_LEN = 16384
SSD_HEADS = 32
SSD_HEAD_DIM = 64
SSD_INNER = SSD_HEADS * SSD_HEAD_DIM
SSD_GROUPS = 8
SSD_HEADS_PER_GROUP = SSD_HEADS // SSD_GROUPS
SSD_STATE = 128
SSD_CONV = 4
SSD_CHUNK = 128
CONV_DIM = SSD_INNER + 2 * SSD_GROUPS * SSD_STATE
ATT_HEAD_DIM = 128
WINDOWS = (128, 512, 2048)
DILATIONS = (1, 4, 16)
ATT_BLOCK = 128
ATT_UNIT_ROWS = 128
ATT_HPP = 8
ATT_HEADS = ATT_HPP * len(WINDOWS)
ATT_QKV = ATT_HEADS * ATT_HEAD_DIM
ATT_OUT = ATT_HPP * ATT_HEAD_DIM
REL_BUCKETS = 32
REL_MAX_DIST = 2048
ZX_COLS = SSD_INNER + CONV_DIM
DT_COL0 = ZX_COLS
QKV_COL0 = ZX_COLS + SSD_HEADS
QKV_LANE_SHIFT = QKV_COL0 % 128
DT_PAD = 128
MIX_OUT = SSD_INNER + ATT_OUT
POOL_WINDOWS = (2, 4, 8, 16)
POOL_CH = D_MODEL // len(POOL_WINDOWS)
POOL_MAX = 16
D_FF = 4 * D_MODEL
EPS = 1e-6
NEG = -1e30
MIB = 1024 * 1024


def _params(semantics, vmem_mib):
    return pltpu.CompilerParams(dimension_semantics=semantics, vmem_limit_bytes=vmem_mib * MIB)


def _sigmoid(x):
    return 0.5 * jnp.tanh(0.5 * x) + 0.5


def _silu(x):
    return x * _sigmoid(x)


def _softplus(x):
    return jnp.maximum(x, 0.0) + jnp.log1p(jnp.exp(-jnp.abs(x)))


def _rms(x):
    return x * lax.rsqrt(jnp.mean(x * x, axis=-1, keepdims=True) + EPS)


def _normmod(x, g, sc, sh):
    return (_rms(x) * g) * (1.0 + sc) + sh


def _normmod_rows(u_ref, x_ref, g_ref, sc_ref, sh_ref):
    tm = u_ref.shape[0]
    rc = min(tm, 256)
    per_row = sc_ref.shape[1] != 1

    def body(i, carry):
        rows = pl.ds(pl.multiple_of(i * rc, rc), rc)
        sc = sc_ref[0, rows, :] if per_row else sc_ref[0]
        sh = sh_ref[0, rows, :] if per_row else sh_ref[0]
        u_ref[rows, :] = _normmod(x_ref[0, rows, :], g_ref[...], sc, sh).astype(BF16)
        return carry

    lax.fori_loop(0, tm // rc, body, 0)


def _bdot(a, b):
    return jnp.dot(a.astype(BF16), b.astype(BF16), preferred_element_type=F32)


def _split3(v):
    hi = v.astype(BF16)
    r = v - hi.astype(F32)
    mid = r.astype(BF16)
    lo = (r - mid.astype(F32)).astype(BF16)
    return hi, mid, lo


def _select_right(v, sel, pieces=3):
    parts = _split3(v)[:pieces]
    out = jnp.dot(parts[0], sel, preferred_element_type=F32)
    for p in parts[1:]:
        out = out + jnp.dot(p, sel, preferred_element_type=F32)
    return out


def _select_left(sel, v):
    hi, mid, lo = _split3(v)
    d = lambda a: jnp.dot(sel, a, preferred_element_type=F32)
    return d(hi) + d(mid) + d(lo)


def _ada_kernel(c_ref, w_ref, b_ref, o_ref):
    o_ref[0] = _bdot(_silu(c_ref[...]), w_ref[0]) + b_ref[0]


def _ada(c_all, ada_w, ada_b):
    depth, _, n = ada_w.shape
    tn = 1024
    return pl.pallas_call(
        _ada_kernel,
        name="ada_mod",
        grid=(depth, n // tn),
        in_specs=[
            pl.BlockSpec((SAMPLE_ROWS, D_MODEL), lambda l, j: (0, 0)),
            pl.BlockSpec((1, D_MODEL, tn), lambda l, j: (l, 0, j)),
            pl.BlockSpec((1, 1, tn), lambda l, j: (l, 0, j)),
        ],
        out_specs=pl.BlockSpec((1, SAMPLE_ROWS, tn), lambda l, j: (l, 0, j)),
        out_shape=jax.ShapeDtypeStruct((depth, SAMPLE_ROWS, n), F32),
        compiler_params=_params(("arbitrary", "arbitrary"), 40),
    )(c_all, ada_w, ada_b.reshape(depth, 1, n))


def _qkv_weight_kernel(w_ref, wnext_ref, o_ref):
    tn = o_ref.shape[1]
    wcat = jnp.concatenate([w_ref[...], wnext_ref[...]], axis=1).astype(BF16)
    o_ref[...] = pltpu.roll(wcat, tn + 128 - QKV_LANE_SHIFT, axis=1)[:, 0:tn]


def _qkv_weights(w):
    d = w.shape[0]
    tn = 512
    tile0 = (QKV_COL0 - QKV_LANE_SHIFT) // tn
    return pl.pallas_call(
        _qkv_weight_kernel,
        name="qkv_weights",
        grid=(3 * ATT_QKV // tn,),
        in_specs=[
            pl.BlockSpec((d, tn), lambda j: (0, tile0 + j)),
            pl.BlockSpec((d, 128), lambda j: (0, (tile0 + j + 1) * (tn // 128))),
        ],
        out_specs=pl.BlockSpec((d, tn), lambda j: (0, j)),
        out_shape=jax.ShapeDtypeStruct((d, 3 * ATT_QKV), BF16),
        compiler_params=_params(("arbitrary",), 40),
    )(w, w)


def _inproj_kernel(*refs, mode):
    if mode == "dt":
        x_ref, g_ref, sc_ref, sh_ref, w_ref, wdt_ref, dtb_ref, o_ref, dt_ref, u_ref = refs
    else:
        x_ref, g_ref, sc_ref, sh_ref, w_ref, qg_ref, kg_ref, o_ref, u_ref = refs
    j = pl.program_id(2)

    @pl.when(j == 0)
    def _():
        _normmod_rows(u_ref, x_ref, g_ref, sc_ref, sh_ref)
        if mode == "dt":
            dt_ref[0] = _softplus(_bdot(u_ref[...], wdt_ref[...]) + dtb_ref[...])

    acc = _bdot(u_ref[...], w_ref[...])
    if mode == "dt":
        o_ref[0] = acc
    else:
        heads = o_ref.shape[1]
        q_tiles = ATT_HEADS // heads

        @pl.when(j < 2 * q_tiles)
        def _():
            gain = jnp.where(j < q_tiles, qg_ref[...], kg_ref[...])
            for hc in range(heads):
                o_ref[0, hc] = _rms(acc[:, hc * ATT_HEAD_DIM:(hc + 1) * ATT_HEAD_DIM]) * gain

        @pl.when(j >= 2 * q_tiles)
        def _():
            for hc in range(heads):
                o_ref[0, hc] = acc[:, hc * ATT_HEAD_DIM:(hc + 1) * ATT_HEAD_DIM]


def _inproj(x, g, sc, sh, w, n_cols, tm, tn, dt_w=None, dt_b=None, qk_gains=None):
    nb, t, d = x.shape
    r = sc.shape[1]
    has_dt = dt_w is not None
    in_specs = [
        pl.BlockSpec((1, tm, d), lambda b, i, j: (b, i, 0)),
        pl.BlockSpec((1, d), lambda b, i, j: (0, 0)),
        pl.BlockSpec((1, r if r == 1 else tm, d), lambda b, i, j: (b, 0 if r == 1 else i, 0)),
        pl.BlockSpec((1, r if r == 1 else tm, d), lambda b, i, j: (b, 0 if r == 1 else i, 0)),
        pl.BlockSpec((d, tn), lambda b, i, j: (0, j)),
    ]
    args = [x, g, sc, sh, w]
    if has_dt:
        in_specs += [pl.BlockSpec((d, DT_PAD), lambda b, i, j: (0, 0)),
                     pl.BlockSpec((1, DT_PAD), lambda b, i, j: (0, 0))]
        args += [dt_w, dt_b]
        out_specs = [pl.BlockSpec((1, tm, tn), lambda b, i, j: (b, i, j)),
                     pl.BlockSpec((1, tm, DT_PAD), lambda b, i, j: (b, i, 0))]
        out_shape = [jax.ShapeDtypeStruct((nb, t, n_cols), F32), jax.ShapeDtypeStruct((nb, t, DT_PAD), F32)]
    else:
        hd = ATT_HEAD_DIM
        in_specs += [pl.BlockSpec((1, hd), lambda b, i, j: (0, 0))] * 2
        args += list(qk_gains)
        out_specs = [pl.BlockSpec((1, tn // hd, tm, hd), lambda b, i, j: (b, j, i, 0))]
        out_shape = [jax.ShapeDtypeStruct((nb, n_cols // hd, t, hd), F32)]
    return pl.pallas_call(
        functools.partial(_inproj_kernel, mode="dt" if has_dt else "qkv"),
        name="in_proj",
        grid=(nb, t // tm, n_cols // tn),
        in_specs=in_specs,
        out_specs=out_specs,
        out_shape=out_shape,
        scratch_shapes=[pltpu.VMEM((tm, d), BF16)],
        compiler_params=_params(("arbitrary", "arbitrary", "arbitrary"), 48),
    )(*args)


def _conv_kernel(x_ref, w_ref, b_ref, o_ref, carry_ref, *, tt):
    @pl.when(pl.program_id(2) == 0)
    def _():
        carry_ref[...] = jnp.zeros(carry_ref.shape, F32)

    w = w_ref[...]
    taps = [w[SSD_CONV - 1 - k:SSD_CONV - k] for k in range(SSD_CONV)]

    x = x_ref[0]
    acc = b_ref[...] + taps[0] * x
    for k in range(1, SSD_CONV):
        acc = acc + taps[k] * pltpu.roll(x, k, axis=0)
    o_ref[0] = _silu(acc)

    x8 = x[0:8]
    carry = carry_ref[...]
    row = lax.broadcasted_iota(jnp.int32, x8.shape, 0)
    acc8 = b_ref[...] + taps[0] * x8
    for k in range(1, SSD_CONV):
        acc8 = acc8 + taps[k] * jnp.where(row < k, pltpu.roll(carry, k, axis=0), pltpu.roll(x8, k, axis=0))
    o_ref[0, 0:8, :] = _silu(acc8)
    carry_ref[...] = x[tt - 8:tt]


def _conv_prompt(zx, conv_w, conv_b):
    nb, t, _ = zx.shape
    tt, tc = 512, 512
    col0 = SSD_INNER // tc
    return pl.pallas_call(
        functools.partial(_conv_kernel, tt=tt),
        name="conv_silu",
        grid=(nb, CONV_DIM // tc, t // tt),
        in_specs=[
            pl.BlockSpec((1, tt, tc), lambda b, c, i: (b, i, col0 + c)),
            pl.BlockSpec((SSD_CONV, tc), lambda b, c, i: (0, c)),
            pl.BlockSpec((1, tc), lambda b, c, i: (0, c)),
        ],
        out_specs=pl.BlockSpec((1, tt, tc), lambda b, c, i: (b, i, c)),
        out_shape=jax.ShapeDtypeStruct((nb, t, CONV_DIM), F32),
        scratch_shapes=[pltpu.VMEM((8, tc), F32)],
        compiler_params=_params(("arbitrary", "arbitrary", "arbitrary"), 32),
    )(zx, conv_w, conv_b)


def _ssd_kernel(xs_ref, b_ref, c_ref, dt_ref, z_ref, alog_ref, dskip_ref, g_ref,
                y_ref, hout_ref, h_ref, yacc_ref):
    q = SSD_CHUNK
    gw = SSD_HEADS_PER_GROUP * SSD_HEAD_DIM
    chunk = pl.program_id(1)

    @pl.when(chunk == 0)
    def _():
        h_ref[...] = jnp.zeros(h_ref.shape, F32)

    row = lax.broadcasted_iota(jnp.int32, (q, q), 0)
    col = lax.broadcasted_iota(jnp.int32, (q, q), 1)
    causal = row >= col
    tri = jnp.where(causal, 1.0, 0.0).astype(BF16)
    head_of_lane = lax.broadcasted_iota(jnp.int32, (DT_PAD, SSD_INNER), 1) // SSD_HEAD_DIM
    expand = jnp.where(head_of_lane == lax.broadcasted_iota(jnp.int32, (DT_PAD, SSD_INNER), 0), 1.0, 0.0).astype(BF16)

    dt = dt_ref[0]
    a = dt * (-jnp.exp(alog_ref[...]))
    acum = _select_left(tri, a)
    acum_t = acum.T
    dt_t = dt.T
    last = acum[q - 1:q, :]
    decay_end_x = _select_right(jnp.exp(last - acum) * dt, expand, pieces=2)
    eacum_x = _select_right(jnp.exp(acum), expand, pieces=2)
    chunk_decay_x = _select_right(jnp.broadcast_to(jnp.exp(last), (8, DT_PAD)), expand)[0:1]

    xs = xs_ref[0]
    xw = (xs * decay_end_x).astype(BF16)
    xsb = xs.astype(BF16)
    head_in_group = lax.broadcasted_iota(jnp.int32, (q, gw), 1) // SSD_HEAD_DIM

    for g in range(SSD_GROUPS):
        ns = slice(g * SSD_STATE, (g + 1) * SSD_STATE)
        cs = slice(g * gw, (g + 1) * gw)
        cg = c_ref[0, :, ns].astype(BF16)
        bg_t = b_ref[0, :, ns].T.astype(BF16)
        cb = jnp.dot(cg, bg_t, preferred_element_type=F32)
        h_prev = h_ref[:, cs]
        y_g = jnp.dot(cg, h_prev.astype(BF16), preferred_element_type=F32) * eacum_x[:, cs]
        h_ref[:, cs] = h_prev * chunk_decay_x[:, cs] + jnp.dot(bg_t, xw[:, cs], preferred_element_type=F32)
        xg = xsb[:, cs]
        for e in range(SSD_HEADS_PER_GROUP):
            hd = g * SSD_HEADS_PER_GROUP + e
            seg = acum[:, hd:hd + 1] - acum_t[hd:hd + 1, :]
            lw = jnp.where(causal, jnp.exp(jnp.where(causal, seg, 0.0)), 0.0) * dt_t[hd:hd + 1, :]
            yd = jnp.dot((cb * lw).astype(BF16), xg, preferred_element_type=F32)
            y_g = y_g + jnp.where(head_in_group == e, yd, 0.0)
        yacc_ref[:, cs] = y_g

    y = yacc_ref[...] + dskip_ref[...] * xs
    y = y * _silu(z_ref[0])
    y_ref[0] = (_rms(y) * g_ref[...]).astype(BF16)

    @pl.when(chunk == pl.num_programs(1) - 1)
    def _():
        hout_ref[0] = h_ref[...].T


def _ssd_prompt(xbc, dt, zx, a_log_pad, dskip_x, norm_g):
    nb, t, _ = xbc.shape
    q = SSD_CHUNK
    bc_w = SSD_GROUPS * SSD_STATE
    return pl.pallas_call(
        _ssd_kernel,
        name="ssd_chunk",
        grid=(nb, t // q),
        in_specs=[
            pl.BlockSpec((1, q, SSD_INNER), lambda b, c: (b, c, 0)),
            pl.BlockSpec((1, q, bc_w), lambda b, c: (b, c, SSD_INNER // bc_w)),
            pl.BlockSpec((1, q, bc_w), lambda b, c: (b, c, SSD_INNER // bc_w + 1)),
            pl.BlockSpec((1, q, DT_PAD), lambda b, c: (b, c, 0)),
            pl.BlockSpec((1, q, SSD_INNER), lambda b, c: (b, c, 0)),
            pl.BlockSpec((1, DT_PAD), lambda b, c: (0, 0)),
            pl.BlockSpec((1, SSD_INNER), lambda b, c: (0, 0)),
            pl.BlockSpec((1, SSD_INNER), lambda b, c: (0, 0)),
        ],
        out_specs=[
            pl.BlockSpec((1, q, SSD_INNER), lambda b, c: (b, c, 0)),
            pl.BlockSpec((1, SSD_INNER, SSD_STATE), lambda b, c: (b, 0, 0)),
        ],
        out_shape=[
            jax.ShapeDtypeStruct((nb, t, SSD_INNER), BF16),
            jax.ShapeDtypeStruct((nb, SSD_INNER, SSD_STATE), F32),
        ],
        scratch_shapes=[pltpu.VMEM((SSD_STATE, SSD_INNER), F32), pltpu.VMEM((q, SSD_INNER), F32)],
        compiler_params=_params(("arbitrary", "arbitrary"), 40),
    )(xbc, xbc, xbc, dt, zx, a_log_pad, dskip_x, norm_g)


def _attn_kernel(q0, q1, q2, k0, k1, k2, v0, v1, v2, bias0, bias1, bias2,
                 att_ref, kv0_ref, kv1_ref, kv2_ref, qb_s, kb_s, vb_s, ok_s, os_, ls):
    blk = ATT_BLOCK
    ur = ATT_UNIT_ROWS
    hd = ATT_HEAD_DIM
    scale = 1.0 / math.sqrt(hd)
    sel = pl.program_id(2)
    qrefs, krefs, vrefs, brefs = (q0, q1, q2), (k0, k1, k2), (v0, v1, v2), (bias0, bias1, bias2)
    kvrefs = (kv0_ref, kv1_ref, kv2_ref)
    nt = (((1,), (1,)), ((), ()))

    @pl.when(sel == 0)
    def _():
        row = lax.broadcasted_iota(jnp.int32, (blk, 2 * blk), 0)
        col = lax.broadcasted_iota(jnp.int32, (blk, 2 * blk), 1)
        ok_s[...] = jnp.where(jnp.where(col < blk, col - row, row - (col - blk)) >= 0, 1.0, 0.0)

        for g, dil in enumerate(DILATIONS):
            sub_len = SEQ // dil
            ch = min(sub_len, 256)
            vb_s[g, :, hd:2 * hd] = jnp.ones((SEQ, hd), BF16)
            for r in range(dil):
                for c in range(sub_len // ch):
                    src = pl.ds(r + c * ch * dil, ch) if dil == 1 else pl.ds(r + c * ch * dil, ch, stride=dil)
                    dst = pl.ds(r * sub_len + c * ch, ch)
                    qb_s[g, dst, :] = qrefs[g][src, :].astype(BF16)
                    kb_s[g, dst, :] = krefs[g][src, :].astype(BF16)
                    vb_s[g, dst, 0:hd] = vrefs[g][src, :].astype(BF16)

        for g, dil in enumerate(DILATIONS):
            nblk = SEQ // (blk * dil)
            for r in range(dil):
                for n in range(nblk):
                    base = (r * nblk + n) * blk
                    key0, col0 = (base - blk, 0) if n > 0 else (base, blk)
                    kcat = kb_s[g, key0:base + blk, :]
                    vcat = vb_s[g, key0:base + blk, :]
                    for half in range(blk // ur):
                        rs = slice(half * ur, (half + 1) * ur)
                        qb = qb_s[g, base + half * ur:base + (half + 1) * ur, :]
                        l = lax.dot_general(qb, kcat, nt, preferred_element_type=F32)
                        l = jnp.where(ok_s[rs, col0:2 * blk] > 0.5, l * scale + brefs[g][0, rs, col0:2 * blk], NEG)
                        m = jnp.max(l, axis=-1, keepdims=True)
                        p = jnp.exp(l - m).astype(BF16)
                        oe = jnp.dot(p, vcat, preferred_element_type=F32)
                        den = oe[:, hd:2 * hd]
                        t0 = r + (n * blk + half * ur) * dil
                        dst = pl.ds(t0, ur) if dil == 1 else pl.ds(t0, ur, stride=dil)
                        os_[g, dst, :] = oe[:, 0:hd] / den
                        ls[g, dst, :] = m + jnp.log(den)

        def mix_body(i, carry):
            rows = pl.ds(pl.multiple_of(i * blk, blk), blk)
            l0, l1, l2 = ls[0, rows, :], ls[1, rows, :], ls[2, rows, :]
            mx = jnp.maximum(jnp.maximum(l0, l1), l2)
            e0, e1, e2 = jnp.exp(l0 - mx), jnp.exp(l1 - mx), jnp.exp(l2 - mx)
            num = e0 * os_[0, rows, :] + e1 * os_[1, rows, :] + e2 * os_[2, rows, :]
            att_ref[0, rows, :] = (num / (e0 + e1 + e2)).astype(BF16)
            return carry
        lax.fori_loop(0, SEQ // blk, mix_body, 0)

        for g, w in enumerate(WINDOWS):
            kvrefs[g][0] = krefs[g][SEQ - w:SEQ, :]

    @pl.when(sel == 1)
    def _():
        for g, w in enumerate(WINDOWS):
            kvrefs[g][0] = vrefs[g][SEQ - w:SEQ, :]


def _attn_prompt(qkv, bias):
    nb = qkv.shape[0]
    hd = ATT_HEAD_DIM

    def head_spec(part, g):
        return pl.BlockSpec((None, None, SEQ, hd), lambda b, h, s: (b, part * ATT_HEADS + g * ATT_HPP + h, 0, 0))

    in_specs = [head_spec(part, g) for part in range(3) for g in range(3)]
    in_specs += [pl.BlockSpec((1, ATT_BLOCK, 2 * ATT_BLOCK), functools.partial(lambda b, h, s, g: (g * ATT_HPP + h, 0, 0), g=g))
                 for g in range(3)]
    out_specs = [pl.BlockSpec((1, SEQ, hd), lambda b, h, s: (b, 0, h))]
    out_shape = [jax.ShapeDtypeStruct((nb, SEQ, ATT_OUT), BF16)]
    for w in WINDOWS:
        out_specs.append(pl.BlockSpec((1, w, hd), lambda b, h, s: (b, 0, s * ATT_HPP + h)))
        out_shape.append(jax.ShapeDtypeStruct((nb, w, 2 * ATT_OUT), F32))
    return pl.pallas_call(
        _attn_kernel,
        name="attn_prompt",
        grid=(nb, ATT_HPP, 2),
        in_specs=in_specs,
        out_specs=out_specs,
        out_shape=out_shape,
        scratch_shapes=[
            pltpu.VMEM((3, SEQ, hd), BF16), pltpu.VMEM((3, SEQ, hd), BF16), pltpu.VMEM((3, SEQ, 2 * hd), BF16),
            pltpu.VMEM((ATT_BLOCK, 2 * ATT_BLOCK), F32),
            pltpu.VMEM((3, SEQ, hd), F32), pltpu.VMEM((3, SEQ, hd), F32),
        ],
        compiler_params=_params(("arbitrary", "arbitrary", "arbitrary"), 48),
    )(*([qkv] * 9), bias, bias, bias)


def _outproj_kernel(y_ref, a_ref, wy_ref, wa_ref, x_ref, g1_ref, o_ref):
    mix = _bdot(y_ref[0], wy_ref[0]) + _bdot(a_ref[0], wa_ref[0])
    o_ref[0] = x_ref[0] + g1_ref[0] * mix


def _outproj(y, att, w, x, g1, tm, tn):
    nb, t, d = x.shape
    r = g1.shape[1]
    return pl.pallas_call(
        _outproj_kernel,
        name="out_proj",
        grid=(nb, t // tm, d // tn),
        in_specs=[
            pl.BlockSpec((1, tm, SSD_INNER), lambda b, i, j: (b, i, 0)),
            pl.BlockSpec((1, tm, ATT_OUT), lambda b, i, j: (b, i, 0)),
            pl.BlockSpec((1, SSD_INNER, tn), lambda b, i, j: (0, 0, j)),
            pl.BlockSpec((1, ATT_OUT, tn), lambda b, i, j: (0, SSD_INNER // ATT_OUT, j)),
            pl.BlockSpec((1, tm, tn), lambda b, i, j: (b, i, j)),
            pl.BlockSpec((1, r if r == 1 else tm, tn), lambda b, i, j: (b, 0 if r == 1 else i, j)),
        ],
        out_specs=pl.BlockSpec((1, tm, tn), lambda b, i, j: (b, i, j)),
        out_shape=jax.ShapeDtypeStruct((nb, t, d), F32),
        compiler_params=_params(("arbitrary", "arbitrary", "arbitrary"), 48),
    )(y, att, w, w, x, g1)


def _mlp_kernel(x_ref, g_ref, sc_ref, sh_ref, g2_ref, w1_ref, w2_ref, o_ref, u_ref):
    f = pl.program_id(2)

    @pl.when(f == 0)
    def _():
        _normmod_rows(u_ref, x_ref, g_ref, sc_ref, sh_ref)
        o_ref[0] = jnp.zeros(o_ref.shape[1:], F32)

    h = jnp.maximum(_bdot(u_ref[...], w1_ref[0]), 0.0)
    o_ref[0] += _bdot(h * h, w2_ref[0])

    @pl.when(f == pl.num_programs(2) - 1)
    def _():
        o_ref[0] = x_ref[0] + g2_ref[0] * o_ref[0]


def _mlp(x, g, sc, sh, g2, w1, w2, layer, tm, tf):
    nb, t, d = x.shape
    r = sc.shape[1]
    mod_spec = pl.BlockSpec((1, r if r == 1 else tm, d), lambda b, i, f: (b, 0 if r == 1 else i, 0))
    return pl.pallas_call(
        _mlp_kernel,
        name="mlp",
        grid=(nb, t // tm, D_FF // tf),
        in_specs=[
            pl.BlockSpec((1, tm, d), lambda b, i, f: (b, i, 0), pipeline_mode=pl.Buffered(1)),
            pl.BlockSpec((1, d), lambda b, i, f: (0, 0)),
            mod_spec, mod_spec, mod_spec,
            pl.BlockSpec((1, d, tf), lambda b, i, f: (layer, 0, f)),
            pl.BlockSpec((1, tf, d), lambda b, i, f: (layer, f, 0)),
        ],
        out_specs=pl.BlockSpec((1, tm, d), lambda b, i, f: (b, i, 0), pipeline_mode=pl.Buffered(1)),
        out_shape=jax.ShapeDtypeStruct((nb, t, d), F32),
        scratch_shapes=[pltpu.VMEM((tm, d), BF16)],
        compiler_params=_params(("arbitrary", "arbitrary", "arbitrary"), 56),
    )(x, g, sc, sh, g2, w1, w2)


def _pool_kernel(x_ref, g_ref, sc_ref, sh_ref, g1_ref, w_ref, scale_ref, o_ref, ulast_ref, ext_ref, *, tt):
    halo = POOL_MAX
    ti = pl.program_id(1)

    @pl.when(ti == 0)
    def _():
        ext_ref[0:halo, :] = jnp.zeros((halo, D_MODEL), F32)

    x = x_ref[0]
    u = _normmod(x, g_ref[...], sc_ref[0], sh_ref[0])
    ext_ref[halo:halo + tt, :] = u
    pos = ti * tt + lax.broadcasted_iota(jnp.int32, (tt, 1), 0)
    for gi, w in enumerate(POOL_WINDOWS):
        ch = slice(gi * POOL_CH, (gi + 1) * POOL_CH)
        acc = ext_ref[:, ch]
        span = 1
        while span < w:
            acc = acc + pltpu.roll(acc, span, axis=0)
            span *= 2
        s = acc[halo:halo + tt]
        inv_cnt = 1.0 / jnp.minimum(w, pos + 1).astype(F32)
        pooled = s * inv_cnt - u[:, ch]
        mix = _bdot(pooled, w_ref[gi]) * scale_ref[:, ch]
        o_ref[0, :, ch] = x[:, ch] + g1_ref[0, :, ch] * mix
    ulast_ref[0] = u[tt - halo:tt]
    ext_ref[0:halo, :] = u[tt - halo:tt]


def _pool_prompt(x, g, sc, sh, g1, pool_w, pool_scale):
    nb, t, d = x.shape
    tt = 512
    mod_spec = pl.BlockSpec((1, 1, d), lambda b, i: (b, 0, 0))
    return pl.pallas_call(
        functools.partial(_pool_kernel, tt=tt),
        name="pool_mix",
        grid=(nb, t // tt),
        in_specs=[
            pl.BlockSpec((1, tt, d), lambda b, i: (b, i, 0)),
            pl.BlockSpec((1, d), lambda b, i: (0, 0)),
            mod_spec, mod_spec, mod_spec,
            pl.BlockSpec((len(POOL_WINDOWS), POOL_CH, POOL_CH), lambda b, i: (0, 0, 0)),
            pl.BlockSpec((1, d), lambda b, i: (0, 0)),
        ],
        out_specs=[
            pl.BlockSpec((1, tt, d), lambda b, i: (b, i, 0)),
            pl.BlockSpec((1, POOL_MAX, d), lambda b, i: (b, 0, 0)),
        ],
        out_shape=[jax.ShapeDtypeStruct((nb, t, d), F32), jax.ShapeDtypeStruct((nb, POOL_MAX, d), F32)],
        scratch_shapes=[pltpu.VMEM((tt + POOL_MAX, d), F32)],
        compiler_params=_params(("arbitrary", "arbitrary"), 48),
    )(x, g, sc, sh, g1, pool_w, pool_scale)


def _sconv_kernel(zx_ref, prev_ref, w_ref, b_ref, o_ref):
    w = w_ref[...]
    acc = b_ref[...] + w[SSD_CONV - 1:SSD_CONV] * zx_ref[:, SSD_INNER:ZX_COLS]
    for k in range(SSD_CONV - 1):
        acc = acc + w[k:k + 1] * prev_ref[k]
    o_ref[...] = _silu(acc)


def _conv_sample(zx, prev_t, conv_w, conv_b):
    return pl.pallas_call(
        _sconv_kernel,
        out_shape=jax.ShapeDtypeStruct((SAMPLE_ROWS, CONV_DIM), F32),
        compiler_params=pltpu.CompilerParams(vmem_limit_bytes=32 * MIB),
    )(zx, prev_t, conv_w, conv_b)


def _sstep_kernel(h0_ref, x_ref, dt_ref, b_ref, c_ref, alog_ref, h_ref, y_ref):
    dt = dt_ref[0]
    h = jnp.exp(dt * (-jnp.exp(alog_ref[...]))) * h0_ref[0] + (dt * x_ref[0]) * b_ref[0]
    h_ref[0] = h
    y_ref[0] = jnp.sum(h * c_ref[0], axis=-1, keepdims=True)


def _ssd_step(h0, x, dt, bh, ch, a_log):
    nb = h0.shape[0]
    hp = (SSD_HEADS, SSD_HEAD_DIM)
    return pl.pallas_call(
        _sstep_kernel,
        grid=(nb,),
        in_specs=[
            pl.BlockSpec((1, *hp, SSD_STATE), lambda b: (b, 0, 0, 0)),
            pl.BlockSpec((1, *hp, 1), lambda b: (b, 0, 0, 0)),
            pl.BlockSpec((1, SSD_HEADS, 1, 1), lambda b: (b, 0, 0, 0)),
            pl.BlockSpec((1, SSD_HEADS, 1, SSD_STATE), lambda b: (b, 0, 0, 0)),
            pl.BlockSpec((1, SSD_HEADS, 1, SSD_STATE), lambda b: (b, 0, 0, 0)),
            pl.BlockSpec((SSD_HEADS, 1, 1), lambda b: (0, 0, 0)),
        ],
        out_specs=[
            pl.BlockSpec((1, *hp, SSD_STATE), lambda b: (b, 0, 0, 0)),
            pl.BlockSpec((1, *hp, 1), lambda b: (b, 0, 0, 0)),
        ],
        out_shape=[jax.ShapeDtypeStruct((nb, *hp, SSD_STATE), F32), jax.ShapeDtypeStruct((nb, *hp, 1), F32)],
        compiler_params=_params(("arbitrary",), 32),
    )(h0, x, dt, bh, ch, a_log)


def _sgate_kernel(y_ref, xs_ref, z_ref, dskip_ref, g_ref, o_ref):
    y = (y_ref[...] + dskip_ref[...] * xs_ref[...]) * _silu(z_ref[...])
    o_ref[...] = _rms(y) * g_ref[...]


def _gate_sample(y, xs, z, dskip_x, norm_g):
    return pl.pallas_call(
        _sgate_kernel,
        out_shape=jax.ShapeDtypeStruct(y.shape, F32),
    )(y, xs, z, dskip_x, norm_g)


def _sattn_kernel(qkv_ref, c0_ref, c1_ref, c2_ref, bbuf_ref, bnew_ref, att_ref, knew_ref, vnew_ref):
    scale = 1.0 / math.sqrt(ATT_HEAD_DIM)
    caches = (c0_ref, c1_ref, c2_ref)
    outs, lses = [], []
    for g in range(len(WINDOWS)):
        hs = g * ATT_HPP
        q = qkv_ref[0, hs:hs + ATT_HPP, :]
        k = qkv_ref[0, ATT_HEADS + hs:ATT_HEADS + hs + ATT_HPP, :]
        v = qkv_ref[0, 2 * ATT_HEADS + hs:2 * ATT_HEADS + hs + ATT_HPP, :]
        knew_ref[0, g] = k
        vnew_ref[0, g] = v
        kbuf = caches[g][:, 0]
        vbuf = caches[g][:, 1]
        lb = jnp.sum(kbuf * q[None], axis=-1, keepdims=True) * scale + bbuf_ref[g]
        ln = jnp.sum(k * q, axis=-1, keepdims=True) * scale + bnew_ref[g]
        m = jnp.maximum(jnp.max(lb, axis=0), ln)
        pb = jnp.exp(lb - m[None])
        pn = jnp.exp(ln - m)
        s = jnp.sum(pb, axis=0) + pn
        outs.append((jnp.sum(pb * vbuf, axis=0) + pn * v) / s)
        lses.append(m + jnp.log(s))
    mx = jnp.maximum(jnp.maximum(lses[0], lses[1]), lses[2])
    es = [jnp.exp(l - mx) for l in lses]
    att_ref[0] = (es[0] * outs[0] + es[1] * outs[1] + es[2] * outs[2]) / (es[0] + es[1] + es[2])


def _attn_sample(qkv, caches, bias_buf, bias_new):
    nb = qkv.shape[0]
    hd = ATT_HEAD_DIM
    in_specs = [pl.BlockSpec((1, 3 * ATT_HEADS, hd), lambda b: (b, 0, 0))]
    for _ in caches:
        in_specs.append(pl.BlockSpec((None, ATT_BLOCK, None, 2, ATT_HPP, hd), lambda b: (b, 0, 0, 0, 0, 0)))
    in_specs += [
        pl.BlockSpec((3, ATT_BLOCK, ATT_HPP, 1), lambda b: (0, 0, 0, 0)),
        pl.BlockSpec((3, ATT_HPP, 1), lambda b: (0, 0, 0)),
    ]
    return pl.pallas_call(
        _sattn_kernel,
        grid=(nb,),
        in_specs=in_specs,
        out_specs=[
            pl.BlockSpec((1, ATT_HPP, hd), lambda b: (b, 0, 0)),
            pl.BlockSpec((1, 3, ATT_HPP, hd), lambda b: (b, 0, 0, 0)),
            pl.BlockSpec((1, 3, ATT_HPP, hd), lambda b: (b, 0, 0, 0)),
        ],
        out_shape=[
            jax.ShapeDtypeStruct((nb, ATT_HPP, hd), F32),
            jax.ShapeDtypeStruct((nb, 3, ATT_HPP, hd), F32),
            jax.ShapeDtypeStruct((nb, 3, ATT_HPP, hd), F32),
        ],
        compiler_params=_params(("arbitrary",), 32),
    )(qkv, *caches, bias_buf, bias_new)


def _kvshift_kernel(x_ref, new_ref, o_ref, carry_ref):
    tw = x_ref.shape[0]

    @pl.when(pl.program_id(1) == 0)
    def _():
        carry_ref[...] = new_ref[0]

    o_ref[0:tw - 1] = x_ref[1:tw]
    o_ref[tw - 1] = carry_ref[...]
    carry_ref[...] = x_ref[0]


def _kv_shift(cache, new_row):
    _, nb, w, _, hpp, hd = cache.shape
    tw = min(w, 512)
    nblk = w // tw
    row = (2, hpp, hd)
    return pl.pallas_call(
        _kvshift_kernel,
        name="kv_shift",
        grid=(nb, nblk),
        in_specs=[
            pl.BlockSpec((None, None, tw, *row), lambda b, s: (0, b, nblk - 1 - s, 0, 0, 0)),
            pl.BlockSpec((None, 1, *row), lambda b, s: (b, 0, 0, 0, 0)),
        ],
        out_specs=pl.BlockSpec((None, None, tw, *row), lambda b, s: (0, b, nblk - 1 - s, 0, 0, 0)),
        out_shape=jax.ShapeDtypeStruct(cache.shape, cache.dtype),
        scratch_shapes=[pltpu.VMEM(row, F32)],
        compiler_params=_params(("arbitrary", "arbitrary"), 40),
    )(cache, new_row)


def _spool_kernel(x_ref, g_ref, sc_ref, sh_ref, g1_ref, prev_ref, w_ref, scale_ref, o_ref, u_ref):
    x = x_ref[...]
    u = _normmod(x, g_ref[...], sc_ref[...], sh_ref[...])
    u_ref[...] = u
    for gi, w in enumerate(POOL_WINDOWS):
        ch = slice(gi * POOL_CH, (gi + 1) * POOL_CH)
        s = u[:, ch]
        for k in range(1, w):
            s = s + prev_ref[POOL_MAX - 1 - k, :, ch]
        pooled = s / float(min(w, PAST_LEN + 1)) - u[:, ch]
        mix = _bdot(pooled, w_ref[gi]) * scale_ref[:, ch]
        o_ref[:, ch] = x[:, ch] + g1_ref[:, ch] * mix


def _pool_sample(x, g, sc, sh, g1, prev_t, pool_w, pool_scale):
    return pl.pallas_call(
        _spool_kernel,
        out_shape=[jax.ShapeDtypeStruct(x.shape, F32), jax.ShapeDtypeStruct(x.shape, F32)],
        compiler_params=pltpu.CompilerParams(vmem_limit_bytes=32 * MIB),
    )(x, g, sc, sh, g1, prev_t, pool_w, pool_scale)


def _rel_bucket(dist):
    max_exact = REL_BUCKETS // 2
    n = jnp.maximum(dist, 1).astype(F32)
    large = max_exact + (jnp.log(n / max_exact) / math.log(REL_MAX_DIST / max_exact)
                         * (REL_BUCKETS - max_exact)).astype(jnp.int32)
    large = jnp.minimum(large, REL_BUCKETS - 1)
    return jnp.where(dist < max_exact, dist, large)


def _prompt_bias_table(rel_bias):
    blk = ATT_BLOCK
    j = blk + jnp.arange(blk)[:, None] - jnp.arange(2 * blk)[None, :]
    tables = []
    for gi, dil in enumerate(DILATIONS):
        bucket = _rel_bucket(jnp.clip(j, 0, blk) * dil)[None]
        rb = rel_bias[:, gi * ATT_HPP:(gi + 1) * ATT_HPP].astype(F32)
        tbl = jnp.zeros((ATT_HPP, blk, 2 * blk), F32)
        for k in range(REL_BUCKETS):
            tbl = jnp.where(bucket == k, rb[k][:, None, None], tbl)
        tables.append(tbl)
    return jnp.concatenate(tables, axis=0)


def _sample_bias_table(rel_bias):
    blk = ATT_BLOCK
    bufs, news = [], []
    for gi, dil in enumerate(DILATIONS):
        tbl = rel_bias[:, gi * ATT_HPP:(gi + 1) * ATT_HPP][_rel_bucket(dil * jnp.arange(blk + 1))]
        bufs.append(tbl[blk:0:-1])
        news.append(tbl[0])
    return jnp.stack(bufs)[..., None].astype(F32), jnp.stack(news)[..., None].astype(F32)


def _pad_rows(a, rows, axis=0):
    pad = [(0, 0)] * a.ndim
    pad[axis] = (0, rows - a.shape[axis])
    return jnp.pad(a, pad)


def _mlp_tiles(t):
    return (1024, 512) if t >= 1024 else (t, 512)


def kernel(x_prompt, x_sample, state_ssm, state_conv, cache_kv_w128, cache_kv_w512, cache_kv_w2048, state_pool, c_prompt, c_sample, rel_bias, ada_w, ada_b, norm_mix_g, norm_mlp_g, in_proj_w, conv_w, conv_b, dt_bias, a_log, d_skip, ssd_norm_g, q_norm_g, k_norm_g, out_proj_w, pool_w, pool_scale, mlp_w1, mlp_w2):
    d = D_MODEL
    c_all = _pad_rows(jnp.concatenate([c_prompt, c_sample], axis=0), SAMPLE_ROWS)
    mod = _ada(c_all, ada_w, ada_b)

    def mods(layer):
        parts = [mod[layer, :, i * d:(i + 1) * d] for i in range(6)]
        prompt = [p[:N_PROMPT, None, :] for p in parts]
        sample = [_pad_rows(p[N_PROMPT:N_PROMPT + N_SAMPLE], SAMPLE_ROWS)[None] for p in parts]
        return prompt, sample

    w_in = in_proj_w.reshape(d, -1)
    w_qkv = _qkv_weights(w_in)
    w_dt = jnp.pad(w_in[:, DT_COL0:QKV_COL0], ((0, 0), (0, DT_PAD - SSD_HEADS)))
    b_dt = jnp.pad(dt_bias[0], (0, DT_PAD - SSD_HEADS)).reshape(1, DT_PAD)
    a_log_pad = jnp.pad(a_log[0], (0, DT_PAD - SSD_HEADS)).reshape(1, DT_PAD)
    dskip_x = jnp.repeat(d_skip[0], SSD_HEAD_DIM).reshape(1, SSD_INNER)
    ssd_g = ssd_norm_g.reshape(1, SSD_INNER)
    qg, kg = q_norm_g.reshape(1, ATT_HEAD_DIM), k_norm_g.reshape(1, ATT_HEAD_DIM)
    cw, cb = conv_w[0], conv_b.reshape(1, CONV_DIM)
    g_mix = [norm_mix_g[l].reshape(1, d) for l in range(2)]
    g_mlp = [norm_mlp_g[l].reshape(1, d) for l in range(2)]
    pscale = pool_scale.reshape(1, d)
    pool_wb = pool_w[0].astype(BF16)

    (sh1, sc1, g1, sh2, sc2, g2), (sh1s, sc1s, g1s, sh2s, sc2s, g2s) = mods(0)
    (sh1b, sc1b, g1b, sh2b, sc2b, g2b), (sh1sb, sc1sb, g1sb, sh2sb, sc2sb, g2sb) = mods(1)

    zx, dt = _inproj(x_prompt, g_mix[0], sc1, sh1, w_in, ZX_COLS, 1024, 512, w_dt, b_dt)
    (qkv,) = _inproj(x_prompt, g_mix[0], sc1, sh1, w_qkv, 3 * ATT_QKV, 1024, 512, qk_gains=(qg, kg))
    p_conv = zx[:, SEQ - (SSD_CONV - 1):, SSD_INNER:][None]
    xbc = _conv_prompt(zx, cw, cb)
    y_ssd, h_fin = _ssd_prompt(xbc, dt, zx, a_log_pad, dskip_x, ssd_g)
    p_ssm = h_fin.reshape(1, N_PROMPT, SSD_HEADS, SSD_HEAD_DIM, SSD_STATE)
    att, kv128, kv512, kv2048 = _attn_prompt(qkv, _prompt_bias_table(rel_bias))
    p_kv = [kv.reshape(1, N_PROMPT, w, 2, ATT_HPP, ATT_HEAD_DIM) for kv, w in zip((kv128, kv512, kv2048), WINDOWS)]
    x1 = _outproj(y_ssd, att, out_proj_w, x_prompt, g1, 1024, 512)
    x2 = _mlp(x1, g_mlp[0], sc2, sh2, g2, mlp_w1, mlp_w2, 0, *_mlp_tiles(SEQ))
    x3, u_last = _pool_prompt(x2, g_mix[1], sc1b, sh1b, g1b, pool_wb, pscale)
    p_pool = u_last[:, 1:][None]
    y_prompt = _mlp(x3, g_mlp[1], sc2b, sh2b, g2b, mlp_w1, mlp_w2, 1, *_mlp_tiles(SEQ))

    rows = SAMPLE_ROWS
    xs0 = _pad_rows(x_sample.reshape(N_SAMPLE, d), rows)[None]
    zx_s, dt_s = _inproj(xs0, g_mix[0], sc1s, sh1s, w_in, ZX_COLS, rows, 512, w_dt, b_dt)
    (qkv_s,) = _inproj(xs0, g_mix[0], sc1s, sh1s, w_qkv, 3 * ATT_QKV, rows, 512, qk_gains=(qg, kg))
    zx_s, dt_s = zx_s[0], dt_s[0]
    qkv_s = jnp.transpose(qkv_s[0, :, :N_SAMPLE], (1, 0, 2))
    xbc_new = zx_s[:N_SAMPLE, SSD_INNER:]
    s_conv = jnp.concatenate([state_conv[0][:, 1:], xbc_new[:, None, :]], axis=1)[None]
    conv_prev_t = _pad_rows(jnp.transpose(state_conv[0], (1, 0, 2)), rows, axis=1)
    xbc_s = _conv_sample(zx_s, conv_prev_t, cw, cb)
    xs_s = xbc_s[:N_SAMPLE, :SSD_INNER]
    e = SSD_HEADS_PER_GROUP
    b_s = jnp.repeat(xbc_s[:N_SAMPLE, SSD_INNER:SSD_INNER + SSD_GROUPS * SSD_STATE]
                     .reshape(N_SAMPLE, SSD_GROUPS, 1, SSD_STATE), e, axis=1)
    c_s = jnp.repeat(xbc_s[:N_SAMPLE, SSD_INNER + SSD_GROUPS * SSD_STATE:]
                     .reshape(N_SAMPLE, SSD_GROUPS, 1, SSD_STATE), e, axis=1)
    h_new, y_s = _ssd_step(state_ssm[0], xs_s.reshape(N_SAMPLE, SSD_HEADS, SSD_HEAD_DIM, 1),
                           dt_s[:N_SAMPLE, :SSD_HEADS].reshape(N_SAMPLE, SSD_HEADS, 1, 1), b_s, c_s,
                           a_log[0].reshape(SSD_HEADS, 1, 1))
    s_ssm = h_new[None]
    y_s = _gate_sample(_pad_rows(y_s.reshape(N_SAMPLE, SSD_INNER), rows), xbc_s[:, :SSD_INNER],
                       zx_s[:, :SSD_INNER], dskip_x, ssd_g)
    caches = (cache_kv_w128, cache_kv_w512, cache_kv_w2048)
    strided = [c.reshape(N_SAMPLE, ATT_BLOCK, dil, 2, ATT_HPP, ATT_HEAD_DIM) for c, dil in zip(caches, DILATIONS)]
    bias_buf, bias_new = _sample_bias_table(rel_bias)
    att_s, k_new, v_new = _attn_sample(qkv_s, strided, bias_buf, bias_new)
    new_rows = jnp.stack([k_new, v_new], axis=2)
    s_kv = [_kv_shift(c, new_rows[:, gi:gi + 1]) for gi, c in enumerate(caches)]
    att_s = _pad_rows(att_s.reshape(N_SAMPLE, ATT_OUT), rows)
    x1s = _outproj(y_s[None], att_s[None], out_proj_w, xs0, g1s, rows, 512)
    x2s = _mlp(x1s, g_mlp[0], sc2s, sh2s, g2s, mlp_w1, mlp_w2, 0, *_mlp_tiles(rows))
    pool_prev_t = _pad_rows(jnp.transpose(state_pool[0], (1, 0, 2)), rows, axis=1)
    x3s, u_s = _pool_sample(x2s[0], g_mix[1], sc1sb[0], sh1sb[0], g1sb[0], pool_prev_t, pool_wb, pscale)
    s_pool = jnp.concatenate([state_pool[0][:, 1:], u_s[:N_SAMPLE, None, :]], axis=1)[None]
    y_s2 = _mlp(x3s[None], g_mlp[1], sc2sb, sh2sb, g2sb, mlp_w1, mlp_w2, 1, *_mlp_tiles(rows))
    y_sample = y_s2[0, :N_SAMPLE].reshape(N_SAMPLE, 1, d)

    return (y_prompt, y_sample, p_ssm, p_conv, p_kv[0], p_kv[1], p_kv[2], p_pool,
            s_ssm, s_conv, s_kv[0], s_kv[1], s_kv[2], s_pool)
```

```python
import functools
import math

import jax
import jax.numpy as jnp
from jax import lax
from jax.experimental import pallas as pl
from jax.experimental.pallas import tpu as pltpu

F32 = jnp.float32
BF16 = jnp.bfloat16

D_MODEL = 2048
SEQ = 2048
N_PROMPT = 4
N_SAMPLE = 8
SAMPLE_ROWS = 16
PAST_LEN = 16384
SSD_HEADS = 32
SSD_HEAD_DIM = 64
SSD_INNER = SSD_HEADS * SSD_HEAD_DIM
SSD_GROUPS = 8
SSD_HEADS_PER_GROUP = SSD_HEADS // SSD_GROUPS
SSD_STATE = 128
SSD_CONV = 4
SSD_CHUNK = 128
CONV_DIM = SSD_INNER + 2 * SSD_GROUPS * SSD_STATE
ATT_HEAD_DIM = 128
WINDOWS = (128, 512, 2048)
DILATIONS = (1, 4, 16)
ATT_BLOCK = 128
ATT_UNIT_ROWS = 128
ATT_MAX_ROW_STRIDE = 4
ATT_HPP = 8
ATT_HEADS = ATT_HPP * len(WINDOWS)
ATT_QKV = ATT_HEADS * ATT_HEAD_DIM
ATT_OUT = ATT_HPP * ATT_HEAD_DIM
REL_BUCKETS = 32
REL_MAX_DIST = 2048
ZX_COLS = SSD_INNER + CONV_DIM
DT_COL0 = ZX_COLS
QKV_COL0 = ZX_COLS + SSD_HEADS
QKV_LANE_SHIFT = QKV_COL0 % 128
DT_PAD = 128
MIX_OUT = SSD_INNER + ATT_OUT
POOL_WINDOWS = (2, 4, 8, 16)
POOL_CH = D_MODEL // len(POOL_WINDOWS)
POOL_MAX = 16
D_FF = 4 * D_MODEL
EPS = 1e-6
NEG = -1e30
MIB = 1024 * 1024


def _params(semantics, vmem_mib):
    return pltpu.CompilerParams(dimension_semantics=semantics, vmem_limit_bytes=vmem_mib * MIB)


def _sigmoid(x):
    return 0.5 * jnp.tanh(0.5 * x) + 0.5


def _silu(x):
    return x * _sigmoid(x)


def _softplus(x):
    return jnp.maximum(x, 0.0) + jnp.log1p(jnp.exp(-jnp.abs(x)))


def _rms(x):
    return x * lax.rsqrt(jnp.mean(x * x, axis=-1, keepdims=True) + EPS)


def _normmod(x, g, sc, sh):
    return (_rms(x) * g) * (1.0 + sc) + sh


def _normmod_rows(u_ref, x_ref, g_ref, sc_ref, sh_ref):
    tm = u_ref.shape[0]
    rc = min(tm, 256)
    per_row = sc_ref.shape[1] != 1

    def body(i, carry):
        rows = pl.ds(pl.multiple_of(i * rc, rc), rc)
        sc = sc_ref[0, rows, :] if per_row else sc_ref[0]
        sh = sh_ref[0, rows, :] if per_row else sh_ref[0]
        u_ref[rows, :] = _normmod(x_ref[0, rows, :], g_ref[...], sc, sh).astype(BF16)
        return carry

    lax.fori_loop(0, tm // rc, body, 0)


def _bdot(a, b):
    return jnp.dot(a.astype(BF16), b.astype(BF16), preferred_element_type=F32)


def _split3(v):
    hi = v.astype(BF16)
    r = v - hi.astype(F32)
    mid = r.astype(BF16)
    lo = (r - mid.astype(F32)).astype(BF16)
    return hi, mid, lo


def _select_right(v, sel, pieces=3):
    parts = _split3(v)[:pieces]
    out = jnp.dot(parts[0], sel, preferred_element_type=F32)
    for p in parts[1:]:
        out = out + jnp.dot(p, sel, preferred_element_type=F32)
    return out


def _select_left(sel, v):
    hi, mid, lo = _split3(v)
    d = lambda a: jnp.dot(sel, a, preferred_element_type=F32)
    return d(hi) + d(mid) + d(lo)


def _ada_kernel(c_ref, w_ref, b_ref, o_ref):
    o_ref[0] = _bdot(_silu(c_ref[...]), w_ref[0]) + b_ref[0]


def _ada(c_all, ada_w, ada_b):
    depth, _, n = ada_w.shape
    tn = 1024
    return pl.pallas_call(
        _ada_kernel,
        name="ada_mod",
        grid=(depth, n // tn),
        in_specs=[
            pl.BlockSpec((SAMPLE_ROWS, D_MODEL), lambda l, j: (0, 0)),
            pl.BlockSpec((1, D_MODEL, tn), lambda l, j: (l, 0, j)),
            pl.BlockSpec((1, 1, tn), lambda l, j: (l, 0, j)),
        ],
        out_specs=pl.BlockSpec((1, SAMPLE_ROWS, tn), lambda l, j: (l, 0, j)),
        out_shape=jax.ShapeDtypeStruct((depth, SAMPLE_ROWS, n), F32),
        compiler_params=_params(("arbitrary", "arbitrary"), 40),
    )(c_all, ada_w, ada_b.reshape(depth, 1, n))


def _bdot_t(a, bt):
    return lax.dot_general(a.astype(BF16), bt.astype(BF16), (((1,), (1,)), ((), ())), preferred_element_type=F32)


def _inproj_kernel(*refs, mode):
    if mode == "dt":
        x_ref, g_ref, sc_ref, sh_ref, w_ref, wdt_ref, dtb_ref, o_ref, dt_ref, u_ref = refs
    else:
        x_ref, g_ref, sc_ref, sh_ref, w_ref, qg_ref, kg_ref, o_ref, u_ref = refs
    j = pl.program_id(2)

    @pl.when(j == 0)
    def _():
        _normmod_rows(u_ref, x_ref, g_ref, sc_ref, sh_ref)
        if mode == "dt":
            dt_ref[0] = _softplus(_bdot_t(u_ref[...], wdt_ref[...]) + dtb_ref[...])

    acc = _bdot_t(u_ref[...], w_ref[...])
    if mode == "dt":
        o_ref[0] = acc
    else:
        heads = o_ref.shape[1]
        q_tiles = ATT_HEADS // heads

        @pl.when(j < 2 * q_tiles)
        def _():
            gain = jnp.where(j < q_tiles, qg_ref[...], kg_ref[...])
            for hc in range(heads):
                o_ref[0, hc] = _rms(acc[:, hc * ATT_HEAD_DIM:(hc + 1) * ATT_HEAD_DIM]) * gain

        @pl.when(j >= 2 * q_tiles)
        def _():
            for hc in range(heads):
                o_ref[0, hc] = acc[:, hc * ATT_HEAD_DIM:(hc + 1) * ATT_HEAD_DIM]


def _inproj(x, g, sc, sh, wt, row0, n_cols, tm, tn, dt_b=None, qk_gains=None):
    nb, t, d = x.shape
    r = sc.shape[1]
    has_dt = dt_b is not None
    if row0 % tn == 0:
        w_spec = pl.BlockSpec((tn, d), lambda b, i, j: (row0 // tn + j, 0))
    else:
        assert row0 % 8 == 0
        w_spec = pl.BlockSpec((pl.Element(tn), pl.Element(d)),
                              lambda b, i, j: (pl.multiple_of(row0 + j * tn, 8), 0))
    in_specs = [
        pl.BlockSpec((1, tm, d), lambda b, i, j: (b, i, 0)),
        pl.BlockSpec((1, d), lambda b, i, j: (0, 0)),
        pl.BlockSpec((1, r if r == 1 else tm, d), lambda b, i, j: (b, 0 if r == 1 else i, 0)),
        pl.BlockSpec((1, r if r == 1 else tm, d), lambda b, i, j: (b, 0 if r == 1 else i, 0)),
        w_spec,
    ]
    args = [x, g, sc, sh, wt]
    if has_dt:
        in_specs += [pl.BlockSpec((DT_PAD, d), lambda b, i, j: (DT_COL0 // DT_PAD, 0)),
                     pl.BlockSpec((1, DT_PAD), lambda b, i, j: (0, 0))]
        args += [wt, dt_b]
        out_specs = [pl.BlockSpec((1, tm, tn), lambda b, i, j: (b, i, j)),
                     pl.BlockSpec((1, tm, DT_PAD), lambda b, i, j: (b, i, 0))]
        out_shape = [jax.ShapeDtypeStruct((nb, t, n_cols), F32), jax.ShapeDtypeStruct((nb, t, DT_PAD), F32)]
    else:
        hd = ATT_HEAD_DIM
        in_specs += [pl.BlockSpec((1, hd), lambda b, i, j: (0, 0))] * 2
        args += list(qk_gains)
        out_specs = [pl.BlockSpec((1, tn // hd, tm, hd), lambda b, i, j: (b, j, i, 0))]
        out_shape = [jax.ShapeDtypeStruct((nb, n_cols // hd, t, hd), F32)]
    return pl.pallas_call(
        functools.partial(_inproj_kernel, mode="dt" if has_dt else "qkv"),
        name="in_proj",
        grid=(nb, t // tm, n_cols // tn),
        in_specs=in_specs,
        out_specs=out_specs,
        out_shape=out_shape,
        scratch_shapes=[pltpu.VMEM((tm, d), BF16)],
        compiler_params=_params(("arbitrary", "arbitrary", "arbitrary"), 48),
    )(*args)


def _conv_kernel(x_ref, w_ref, b_ref, o_ref, carry_ref, *, tt):
    @pl.when(pl.program_id(2) == 0)
    def _():
        carry_ref[...] = jnp.zeros(carry_ref.shape, F32)

    w = w_ref[...]
    taps = [w[SSD_CONV - 1 - k:SSD_CONV - k] for k in range(SSD_CONV)]

    x = x_ref[0]
    acc = b_ref[...] + taps[0] * x
    for k in range(1, SSD_CONV):
        acc = acc + taps[k] * pltpu.roll(x, k, axis=0)
    o_ref[0] = _silu(acc)

    x8 = x[0:8]
    carry = carry_ref[...]
    row = lax.broadcasted_iota(jnp.int32, x8.shape, 0)
    acc8 = b_ref[...] + taps[0] * x8
    for k in range(1, SSD_CONV):
        acc8 = acc8 + taps[k] * jnp.where(row < k, pltpu.roll(carry, k, axis=0), pltpu.roll(x8, k, axis=0))
    o_ref[0, 0:8, :] = _silu(acc8)
    carry_ref[...] = x[tt - 8:tt]


def _conv_prompt(zx, conv_w, conv_b):
    nb, t, _ = zx.shape
    tt, tc = 512, 512
    col0 = SSD_INNER // tc
    return pl.pallas_call(
        functools.partial(_conv_kernel, tt=tt),
        name="conv_silu",
        grid=(nb, CONV_DIM // tc, t // tt),
        in_specs=[
            pl.BlockSpec((1, tt, tc), lambda b, c, i: (b, i, col0 + c)),
            pl.BlockSpec((SSD_CONV, tc), lambda b, c, i: (0, c)),
            pl.BlockSpec((1, tc), lambda b, c, i: (0, c)),
        ],
        out_specs=pl.BlockSpec((1, tt, tc), lambda b, c, i: (b, i, c)),
        out_shape=jax.ShapeDtypeStruct((nb, t, CONV_DIM), F32),
        scratch_shapes=[pltpu.VMEM((8, tc), F32)],
        compiler_params=_params(("arbitrary", "arbitrary", "arbitrary"), 32),
    )(zx, conv_w, conv_b)


def _ssd_kernel(xs_ref, b_ref, c_ref, dt_ref, z_ref, alog_ref, dskip_ref, g_ref,
                y_ref, hout_ref, h_ref, yacc_ref):
    q = SSD_CHUNK
    gw = SSD_HEADS_PER_GROUP * SSD_HEAD_DIM
    chunk = pl.program_id(1)

    @pl.when(chunk == 0)
    def _():
        h_ref[...] = jnp.zeros(h_ref.shape, F32)

    row = lax.broadcasted_iota(jnp.int32, (q, q), 0)
    col = lax.broadcasted_iota(jnp.int32, (q, q), 1)
    causal = row >= col
    tri = jnp.where(causal, 1.0, 0.0).astype(BF16)
    head_of_lane = lax.broadcasted_iota(jnp.int32, (DT_PAD, SSD_INNER), 1) // SSD_HEAD_DIM
    expand = jnp.where(head_of_lane == lax.broadcasted_iota(jnp.int32, (DT_PAD, SSD_INNER), 0), 1.0, 0.0).astype(BF16)

    dt = dt_ref[0]
    a = dt * (-jnp.exp(alog_ref[...]))
    acum = _select_left(tri, a)
    acum_t = acum.T
    dt_t = dt.T
    last = acum[q - 1:q, :]
    decay_end_x = _select_right(jnp.exp(last - acum) * dt, expand, pieces=2)
    eacum_x = _select_right(jnp.exp(acum), expand, pieces=2)
    chunk_decay_x = _select_right(jnp.broadcast_to(jnp.exp(last), (8, DT_PAD)), expand)[0:1]

    xs = xs_ref[0]
    xw = (xs * decay_end_x).astype(BF16)
    xsb = xs.astype(BF16)
    head_in_group = lax.broadcasted_iota(jnp.int32, (q, gw), 1) // SSD_HEAD_DIM

    for g in range(SSD_GROUPS):
        ns = slice(g * SSD_STATE, (g + 1) * SSD_STATE)
        cs = slice(g * gw, (g + 1) * gw)
        cg = c_ref[0, :, ns].astype(BF16)
        bg_t = b_ref[0, :, ns].T.astype(BF16)
        cb = jnp.dot(cg, bg_t, preferred_element_type=F32)
        h_prev = h_ref[:, cs]
        y_g = jnp.dot(cg, h_prev.astype(BF16), preferred_element_type=F32) * eacum_x[:, cs]
        h_ref[:, cs] = h_prev * chunk_decay_x[:, cs] + jnp.dot(bg_t, xw[:, cs], preferred_element_type=F32)
        xg = xsb[:, cs]
        for e in range(SSD_HEADS_PER_GROUP):
            hd = g * SSD_HEADS_PER_GROUP + e
            seg = acum[:, hd:hd + 1] - acum_t[hd:hd + 1, :]
            lw = jnp.where(causal, jnp.exp(jnp.where(causal, seg, 0.0)), 0.0) * dt_t[hd:hd + 1, :]
            yd = jnp.dot((cb * lw).astype(BF16), xg, preferred_element_type=F32)
            y_g = y_g + jnp.where(head_in_group == e, yd, 0.0)
        yacc_ref[:, cs] = y_g

    y = yacc_ref[...] + dskip_ref[...] * xs
    y = y * _silu(z_ref[0])
    y_ref[0] = (_rms(y) * g_ref[...]).astype(BF16)

    @pl.when(chunk == pl.num_programs(1) - 1)
    def _():
        hout_ref[0] = h_ref[...].T


def _ssd_prompt(xbc, dt, zx, a_log_pad, dskip_x, norm_g):
    nb, t, _ = xbc.shape
    q = SSD_CHUNK
    bc_w = SSD_GROUPS * SSD_STATE
    return pl.pallas_call(
        _ssd_kernel,
        name="ssd_chunk",
        grid=(nb, t // q),
        in_specs=[
            pl.BlockSpec((1, q, SSD_INNER), lambda b, c: (b, c, 0)),
            pl.BlockSpec((1, q, bc_w), lambda b, c: (b, c, SSD_INNER // bc_w)),
            pl.BlockSpec((1, q, bc_w), lambda b, c: (b, c, SSD_INNER // bc_w + 1)),
            pl.BlockSpec((1, q, DT_PAD), lambda b, c: (b, c, 0)),
            pl.BlockSpec((1, q, SSD_INNER), lambda b, c: (b, c, 0)),
            pl.BlockSpec((1, DT_PAD), lambda b, c: (0, 0)),
            pl.BlockSpec((1, SSD_INNER), lambda b, c: (0, 0)),
            pl.BlockSpec((1, SSD_INNER), lambda b, c: (0, 0)),
        ],
        out_specs=[
            pl.BlockSpec((1, q, SSD_INNER), lambda b, c: (b, c, 0)),
            pl.BlockSpec((1, SSD_INNER, SSD_STATE), lambda b, c: (b, 0, 0)),
        ],
        out_shape=[
            jax.ShapeDtypeStruct((nb, t, SSD_INNER), BF16),
            jax.ShapeDtypeStruct((nb, SSD_INNER, SSD_STATE), F32),
        ],
        scratch_shapes=[pltpu.VMEM((SSD_STATE, SSD_INNER), F32), pltpu.VMEM((q, SSD_INNER), F32)],
        compiler_params=_params(("arbitrary", "arbitrary"), 40),
    )(xbc, xbc, xbc, dt, zx, a_log_pad, dskip_x, norm_g)


def _attn_kernel(q0, q1, q2, k0, k1, k2, v0, v1, v2, bias0, bias1, bias2,
                 att_ref, kv0_ref, kv1_ref, kv2_ref, qb_s, kb_s, vb_s, ok_s, os_, ls, tmp_s, ltmp_s):
    blk = ATT_BLOCK
    ur = ATT_UNIT_ROWS
    hd = ATT_HEAD_DIM
    scale = 1.0 / math.sqrt(hd)
    sel = pl.program_id(2)
    qrefs, krefs, vrefs, brefs = (q0, q1, q2), (k0, k1, k2), (v0, v1, v2), (bias0, bias1, bias2)
    kvrefs = (kv0_ref, kv1_ref, kv2_ref)
    nt = (((1,), (1,)), ((), ()))

    @pl.when(sel == 0)
    def _():
        row = lax.broadcasted_iota(jnp.int32, (blk, 2 * blk), 0)
        col = lax.broadcasted_iota(jnp.int32, (blk, 2 * blk), 1)
        ok_s[...] = jnp.where(jnp.where(col < blk, col - row, row - (col - blk)) >= 0, 1.0, 0.0)

        f = ATT_MAX_ROW_STRIDE
        seg = SEQ // f

        def strided(start, size, stride):
            return pl.ds(start, size) if stride == 1 else pl.ds(start, size, stride=stride)

        def regroup(src_ref, dil, emit):
            if dil <= f:
                sub = SEQ // dil
                ch = min(sub, 256)
                for r in range(dil):
                    for c in range(sub // ch):
                        emit(pl.ds(r * sub + c * ch, ch), src_ref[strided(r + c * ch * dil, ch, dil), :])
            else:
                assert dil == f * f
                for r1 in range(f):
                    for c in range(seg // 256):
                        tmp_s[pl.ds(r1 * seg + c * 256, 256), :] = src_ref[strided(r1 + c * 256 * f, 256, f), :]
                sub = seg // f
                for r1 in range(f):
                    for r2 in range(f):
                        emit(pl.ds((f * r2 + r1) * sub, sub), tmp_s[strided(r1 * seg + r2, sub, f), :])

        for g, dil in enumerate(DILATIONS):
            vb_s[g, :, hd:2 * hd] = jnp.ones((SEQ, hd), BF16)

            def put_q(rows, val, g=g):
                qb_s[g, rows, :] = val.astype(BF16)

            def put_k(rows, val, g=g):
                kb_s[g, rows, :] = val.astype(BF16)

            def put_v(rows, val, g=g):
                vb_s[g, rows, 0:hd] = val.astype(BF16)

            regroup(qrefs[g], dil, put_q)
            regroup(krefs[g], dil, put_k)
            regroup(vrefs[g], dil, put_v)

        for g, dil in enumerate(DILATIONS):
            nblk = SEQ // (blk * dil)
            for r in range(dil):
                for n in range(nblk):
                    base = (r * nblk + n) * blk
                    key0, col0 = (base - blk, 0) if n > 0 else (base, blk)
                    kcat = kb_s[g, key0:base + blk, :]
                    vcat = vb_s[g, key0:base + blk, :]
                    for half in range(blk // ur):
                        rs = slice(half * ur, (half + 1) * ur)
                        qb = qb_s[g, base + half * ur:base + (half + 1) * ur, :]
                        l = lax.dot_general(qb, kcat, nt, preferred_element_type=F32)
                        l = jnp.where(ok_s[rs, col0:2 * blk] > 0.5, l * scale + brefs[g][0, rs, col0:2 * blk], NEG)
                        m = jnp.max(l, axis=-1, keepdims=True)
                        p = jnp.exp(l - m).astype(BF16)
                        oe = jnp.dot(p, vcat, preferred_element_type=F32)
                        den = oe[:, hd:2 * hd]
                        s0 = n * blk + half * ur
                        if dil <= f:
                            dst = strided(r + s0 * dil, ur, dil)
                            os_[g, dst, :] = oe[:, 0:hd] / den
                            ls[g, dst, :] = m + jnp.log(den)
                        else:
                            dst = strided((r % f) * seg + f * s0 + r // f, ur, f)
                            tmp_s[dst, :] = oe[:, 0:hd] / den
                            ltmp_s[dst, :] = m + jnp.log(den)
            if dil > f:
                for r1 in range(f):
                    for c in range(seg // 256):
                        dst = strided(r1 + c * 256 * f, 256, f)
                        os_[g, dst, :] = tmp_s[pl.ds(r1 * seg + c * 256, 256), :]
                        ls[g, dst, :] = ltmp_s[pl.ds(r1 * seg + c * 256, 256), :]

        def mix_body(i, carry):
            rows = pl.ds(pl.multiple_of(i * blk, blk), blk)
            l0, l1, l2 = ls[0, rows, :], ls[1, rows, :], ls[2, rows, :]
            mx = jnp.maximum(jnp.maximum(l0, l1), l2)
            e0, e1, e2 = jnp.exp(l0 - mx), jnp.exp(l1 - mx), jnp.exp(l2 - mx)
            num = e0 * os_[0, rows, :] + e1 * os_[1, rows, :] + e2 * os_[2, rows, :]
            att_ref[0, rows, :] = (num / (e0 + e1 + e2)).astype(BF16)
            return carry
        lax.fori_loop(0, SEQ // blk, mix_body, 0)

        for g, w in enumerate(WINDOWS):
            kvrefs[g][0] = krefs[g][SEQ - w:SEQ, :]

    @pl.when(sel == 1)
    def _():
        for g, w in enumerate(WINDOWS):
            kvrefs[g][0] = vrefs[g][SEQ - w:SEQ, :]


def _attn_prompt(qkv, bias):
    nb = qkv.shape[0]
    hd = ATT_HEAD_DIM

    def head_spec(part, g):
        return pl.BlockSpec((None, None, SEQ, hd), lambda b, h, s: (b, part * ATT_HEADS + g * ATT_HPP + h, 0, 0))

    in_specs = [head_spec(part, g) for part in range(3) for g in range(3)]
    in_specs += [pl.BlockSpec((1, ATT_BLOCK, 2 * ATT_BLOCK), functools.partial(lambda b, h, s, g: (g * ATT_HPP + h, 0, 0), g=g))
                 for g in range(3)]
    out_specs = [pl.BlockSpec((1, SEQ, hd), lambda b, h, s: (b, 0, h))]
    out_shape = [jax.ShapeDtypeStruct((nb, SEQ, ATT_OUT), BF16)]
    for w in WINDOWS:
        out_specs.append(pl.BlockSpec((1, w, hd), lambda b, h, s: (b, 0, s * ATT_HPP + h)))
        out_shape.append(jax.ShapeDtypeStruct((nb, w, 2 * ATT_OUT), F32))
    return pl.pallas_call(
        _attn_kernel,
        name="attn_prompt",
        grid=(nb, ATT_HPP, 2),
        in_specs=in_specs,
        out_specs=out_specs,
        out_shape=out_shape,
        scratch_shapes=[
            pltpu.VMEM((3, SEQ, hd), BF16), pltpu.VMEM((3, SEQ, hd), BF16), pltpu.VMEM((3, SEQ, 2 * hd), BF16),
            pltpu.VMEM((ATT_BLOCK, 2 * ATT_BLOCK), F32),
            pltpu.VMEM((3, SEQ, hd), F32), pltpu.VMEM((3, SEQ, hd), F32),
            pltpu.VMEM((SEQ, hd), F32), pltpu.VMEM((SEQ, hd), F32),
        ],
        compiler_params=_params(("arbitrary", "arbitrary", "arbitrary"), 48),
    )(*([qkv] * 9), bias, bias, bias)


def _outproj_kernel(y_ref, a_ref, wy_ref, wa_ref, x_ref, g1_ref, o_ref):
    mix = _bdot(y_ref[0], wy_ref[0]) + _bdot(a_ref[0], wa_ref[0])
    o_ref[0] = x_ref[0] + g1_ref[0] * mix


def _outproj(y, att, w, x, g1, tm, tn):
    nb, t, d = x.shape
    r = g1.shape[1]
    return pl.pallas_call(
        _outproj_kernel,
        name="out_proj",
        grid=(nb, t // tm, d // tn),
        in_specs=[
            pl.BlockSpec((1, tm, SSD_INNER), lambda b, i, j: (b, i, 0)),
            pl.BlockSpec((1, tm, ATT_OUT), lambda b, i, j: (b, i, 0)),
            pl.BlockSpec((1, SSD_INNER, tn), lambda b, i, j: (0, 0, j)),
            pl.BlockSpec((1, ATT_OUT, tn), lambda b, i, j: (0, SSD_INNER // ATT_OUT, j)),
            pl.BlockSpec((1, tm, tn), lambda b, i, j: (b, i, j)),
            pl.BlockSpec((1, r if r == 1 else tm, tn), lambda b, i, j: (b, 0 if r == 1 else i, j)),
        ],
        out_specs=pl.BlockSpec((1, tm, tn), lambda b, i, j: (b, i, j)),
        out_shape=jax.ShapeDtypeStruct((nb, t, d), F32),
        compiler_params=_params(("arbitrary", "arbitrary", "arbitrary"), 48),
    )(y, att, w, w, x, g1)


def _mlp_kernel(x_ref, g_ref, sc_ref, sh_ref, g2_ref, w1_ref, w2_ref, o_ref, u_ref):
    f = pl.program_id(2)

    @pl.when(f == 0)
    def _():
        _normmod_rows(u_ref, x_ref, g_ref, sc_ref, sh_ref)
        o_ref[0] = jnp.zeros(o_ref.shape[1:], F32)

    h = jnp.maximum(_bdot(u_ref[...], w1_ref[0]), 0.0)
    o_ref[0] += _bdot(h * h, w2_ref[0])

    @pl.when(f == pl.num_programs(2) - 1)
    def _():
        o_ref[0] = x_ref[0] + g2_ref[0] * o_ref[0]


def _mlp(x, g, sc, sh, g2, w1, w2, layer, tm, tf):
    nb, t, d = x.shape
    r = sc.shape[1]
    mod_spec = pl.BlockSpec((1, r if r == 1 else tm, d), lambda b, i, f: (b, 0 if r == 1 else i, 0))
    return pl.pallas_call(
        _mlp_kernel,
        name="mlp",
        grid=(nb, t // tm, D_FF // tf),
        in_specs=[
            pl.BlockSpec((1, tm, d), lambda b, i, f: (b, i, 0), pipeline_mode=pl.Buffered(1)),
            pl.BlockSpec((1, d), lambda b, i, f: (0, 0)),
            mod_spec, mod_spec, mod_spec,
            pl.BlockSpec((1, d, tf), lambda b, i, f: (layer, 0, f)),
            pl.BlockSpec((1, tf, d), lambda b, i, f: (layer, f, 0)),
        ],
        out_specs=pl.BlockSpec((1, tm, d), lambda b, i, f: (b, i, 0), pipeline_mode=pl.Buffered(1)),
        out_shape=jax.ShapeDtypeStruct((nb, t, d), F32),
        scratch_shapes=[pltpu.VMEM((tm, d), BF16)],
        compiler_params=_params(("arbitrary", "arbitrary", "arbitrary"), 56),
    )(x, g, sc, sh, g2, w1, w2)


def _pool_kernel(x_ref, g_ref, sc_ref, sh_ref, g1_ref, w_ref, scale_ref, o_ref, ulast_ref, ext_ref, *, tt):
    halo = POOL_MAX
    ti = pl.program_id(1)

    @pl.when(ti == 0)
    def _():
        ext_ref[0:halo, :] = jnp.zeros((halo, D_MODEL), F32)

    x = x_ref[0]
    u = _normmod(x, g_ref[...], sc_ref[0], sh_ref[0])
    ext_ref[halo:halo + tt, :] = u
    pos = ti * tt + lax.broadcasted_iota(jnp.int32, (tt, 1), 0)
    for gi, w in enumerate(POOL_WINDOWS):
        ch = slice(gi * POOL_CH, (gi + 1) * POOL_CH)
        acc = ext_ref[:, ch]
        span = 1
        while span < w:
            acc = acc + pltpu.roll(acc, span, axis=0)
            span *= 2
        s = acc[halo:halo + tt]
        inv_cnt = 1.0 / jnp.minimum(w, pos + 1).astype(F32)
        pooled = s * inv_cnt - u[:, ch]
        mix = _bdot(pooled, w_ref[gi]) * scale_ref[:, ch]
        o_ref[0, :, ch] = x[:, ch] + g1_ref[0, :, ch] * mix
    ulast_ref[0] = u[tt - halo:tt]
    ext_ref[0:halo, :] = u[tt - halo:tt]


def _pool_prompt(x, g, sc, sh, g1, pool_w, pool_scale):
    nb, t, d = x.shape
    tt = 512
    mod_spec = pl.BlockSpec((1, 1, d), lambda b, i: (b, 0, 0))
    return pl.pallas_call(
        functools.partial(_pool_kernel, tt=tt),
        name="pool_mix",
        grid=(nb, t // tt),
        in_specs=[
            pl.BlockSpec((1, tt, d), lambda b, i: (b, i, 0)),
            pl.BlockSpec((1, d), lambda b, i: (0, 0)),
            mod_spec, mod_spec, mod_spec,
            pl.BlockSpec((len(POOL_WINDOWS), POOL_CH, POOL_CH), lambda b, i: (0, 0, 0)),
            pl.BlockSpec((1, d), lambda b, i: (0, 0)),
        ],
        out_specs=[
            pl.BlockSpec((1, tt, d), lambda b, i: (b, i, 0)),
            pl.BlockSpec((1, POOL_MAX, d), lambda b, i: (b, 0, 0)),
        ],
        out_shape=[jax.ShapeDtypeStruct((nb, t, d), F32), jax.ShapeDtypeStruct((nb, POOL_MAX, d), F32)],
        scratch_shapes=[pltpu.VMEM((tt + POOL_MAX, d), F32)],
        compiler_params=_params(("arbitrary", "arbitrary"), 48),
    )(x, g, sc, sh, g1, pool_w, pool_scale)


def _sconv_kernel(zx_ref, prev_ref, w_ref, b_ref, o_ref):
    w = w_ref[...]
    acc = b_ref[...] + w[SSD_CONV - 1:SSD_CONV] * zx_ref[:, SSD_INNER:ZX_COLS]
    for k in range(SSD_CONV - 1):
        acc = acc + w[k:k + 1] * prev_ref[k]
    o_ref[...] = _silu(acc)


def _conv_sample(zx, prev_t, conv_w, conv_b):
    return pl.pallas_call(
        _sconv_kernel,
        out_shape=jax.ShapeDtypeStruct((SAMPLE_ROWS, CONV_DIM), F32),
        compiler_params=pltpu.CompilerParams(vmem_limit_bytes=32 * MIB),
    )(zx, prev_t, conv_w, conv_b)


def _sstep_kernel(h0_ref, x_ref, dt_ref, b_ref, c_ref, alog_ref, h_ref, y_ref):
    dt = dt_ref[0]
    h = jnp.exp(dt * (-jnp.exp(alog_ref[...]))) * h0_ref[0] + (dt * x_ref[0]) * b_ref[0]
    h_ref[0] = h
    y_ref[0] = jnp.sum(h * c_ref[0], axis=-1, keepdims=True)


def _ssd_step(h0, x, dt, bh, ch, a_log):
    nb = h0.shape[0]
    hp = (SSD_HEADS, SSD_HEAD_DIM)
    return pl.pallas_call(
        _sstep_kernel,
        grid=(nb,),
        in_specs=[
            pl.BlockSpec((1, *hp, SSD_STATE), lambda b: (b, 0, 0, 0)),
            pl.BlockSpec((1, *hp, 1), lambda b: (b, 0, 0, 0)),
            pl.BlockSpec((1, SSD_HEADS, 1, 1), lambda b: (b, 0, 0, 0)),
            pl.BlockSpec((1, SSD_HEADS, 1, SSD_STATE), lambda b: (b, 0, 0, 0)),
            pl.BlockSpec((1, SSD_HEADS, 1, SSD_STATE), lambda b: (b, 0, 0, 0)),
            pl.BlockSpec((SSD_HEADS, 1, 1), lambda b: (0, 0, 0)),
        ],
        out_specs=[
            pl.BlockSpec((1, *hp, SSD_STATE), lambda b: (b, 0, 0, 0)),
            pl.BlockSpec((1, *hp, 1), lambda b: (b, 0, 0, 0)),
        ],
        out_shape=[jax.ShapeDtypeStruct((nb, *hp, SSD_STATE), F32), jax.ShapeDtypeStruct((nb, *hp, 1), F32)],
        compiler_params=_params(("arbitrary",), 32),
    )(h0, x, dt, bh, ch, a_log)


def _sgate_kernel(y_ref, xs_ref, z_ref, dskip_ref, g_ref, o_ref):
    y = (y_ref[...] + dskip_ref[...] * xs_ref[...]) * _silu(z_ref[...])
    o_ref[...] = _rms(y) * g_ref[...]


def _gate_sample(y, xs, z, dskip_x, norm_g):
    return pl.pallas_call(
        _sgate_kernel,
        out_shape=jax.ShapeDtypeStruct(y.shape, F32),
    )(y, xs, z, dskip_x, norm_g)


def _sattn_kernel(qkv_ref, c0_ref, c1_ref, c2_ref, bbuf_ref, bnew_ref, att_ref, knew_ref, vnew_ref):
    scale = 1.0 / math.sqrt(ATT_HEAD_DIM)
    caches = (c0_ref, c1_ref, c2_ref)
    outs, lses = [], []
    for g in range(len(WINDOWS)):
        hs = g * ATT_HPP
        q = qkv_ref[0, hs:hs + ATT_HPP, :]
        k = qkv_ref[0, ATT_HEADS + hs:ATT_HEADS + hs + ATT_HPP, :]
        v = qkv_ref[0, 2 * ATT_HEADS + hs:2 * ATT_HEADS + hs + ATT_HPP, :]
        knew_ref[0, g] = k
        vnew_ref[0, g] = v
        kbuf = caches[g][:, 0]
        vbuf = caches[g][:, 1]
        lb = jnp.sum(kbuf * q[None], axis=-1, keepdims=True) * scale + bbuf_ref[g]
        ln = jnp.sum(k * q, axis=-1, keepdims=True) * scale + bnew_ref[g]
        m = jnp.maximum(jnp.max(lb, axis=0), ln)
        pb = jnp.exp(lb - m[None])
        pn = jnp.exp(ln - m)
        s = jnp.sum(pb, axis=0) + pn
        outs.append((jnp.sum(pb * vbuf, axis=0) + pn * v) / s)
        lses.append(m + jnp.log(s))
    mx = jnp.maximum(jnp.maximum(lses[0], lses[1]), lses[2])
    es = [jnp.exp(l - mx) for l in lses]
    att_ref[0] = (es[0] * outs[0] + es[1] * outs[1] + es[2] * outs[2]) / (es[0] + es[1] + es[2])


def _attn_sample(qkv, caches, bias_buf, bias_new):
    nb = qkv.shape[0]
    hd = ATT_HEAD_DIM
    in_specs = [pl.BlockSpec((1, 3 * ATT_HEADS, hd), lambda b: (b, 0, 0))]
    for _ in caches:
        in_specs.append(pl.BlockSpec((None, ATT_BLOCK, None, 2, ATT_HPP, hd), lambda b: (b, 0, 0, 0, 0, 0)))
    in_specs += [
        pl.BlockSpec((3, ATT_BLOCK, ATT_HPP, 1), lambda b: (0, 0, 0, 0)),
        pl.BlockSpec((3, ATT_HPP, 1), lambda b: (0, 0, 0)),
    ]
    return pl.pallas_call(
        _sattn_kernel,
        grid=(nb,),
        in_specs=in_specs,
        out_specs=[
            pl.BlockSpec((1, ATT_HPP, hd), lambda b: (b, 0, 0)),
            pl.BlockSpec((1, 3, ATT_HPP, hd), lambda b: (b, 0, 0, 0)),
            pl.BlockSpec((1, 3, ATT_HPP, hd), lambda b: (b, 0, 0, 0)),
        ],
        out_shape=[
            jax.ShapeDtypeStruct((nb, ATT_HPP, hd), F32),
            jax.ShapeDtypeStruct((nb, 3, ATT_HPP, hd), F32),
            jax.ShapeDtypeStruct((nb, 3, ATT_HPP, hd), F32),
        ],
        compiler_params=_params(("arbitrary",), 32),
    )(qkv, *caches, bias_buf, bias_new)


def _kvshift_kernel(x_ref, new_ref, o_ref, carry_ref):
    tw = x_ref.shape[0]

    @pl.when(pl.program_id(1) == 0)
    def _():
        carry_ref[...] = new_ref[0]

    o_ref[0:tw - 1] = x_ref[1:tw]
    o_ref[tw - 1] = carry_ref[...]
    carry_ref[...] = x_ref[0]


def _kv_shift(cache, new_row):
    _, nb, w, _, hpp, hd = cache.shape
    tw = min(w, 512)
    nblk = w // tw
    row = (2, hpp, hd)
    return pl.pallas_call(
        _kvshift_kernel,
        name="kv_shift",
        grid=(nb, nblk),
        in_specs=[
            pl.BlockSpec((None, None, tw, *row), lambda b, s: (0, b, nblk - 1 - s, 0, 0, 0)),
            pl.BlockSpec((None, 1, *row), lambda b, s: (b, 0, 0, 0, 0)),
        ],
        out_specs=pl.BlockSpec((None, None, tw, *row), lambda b, s: (0, b, nblk - 1 - s, 0, 0, 0)),
        out_shape=jax.ShapeDtypeStruct(cache.shape, cache.dtype),
        scratch_shapes=[pltpu.VMEM(row, F32)],
        compiler_params=_params(("arbitrary", "arbitrary"), 40),
    )(cache, new_row)


def _spool_kernel(x_ref, g_ref, sc_ref, sh_ref, g1_ref, prev_ref, w_ref, scale_ref, o_ref, u_ref):
    x = x_ref[...]
    u = _normmod(x, g_ref[...], sc_ref[...], sh_ref[...])
    u_ref[...] = u
    for gi, w in enumerate(POOL_WINDOWS):
        ch = slice(gi * POOL_CH, (gi + 1) * POOL_CH)
        s = u[:, ch]
        for k in range(1, w):
            s = s + prev_ref[POOL_MAX - 1 - k, :, ch]
        pooled = s / float(min(w, PAST_LEN + 1)) - u[:, ch]
        mix = _bdot(pooled, w_ref[gi]) * scale_ref[:, ch]
        o_ref[:, ch] = x[:, ch] + g1_ref[:, ch] * mix


def _pool_sample(x, g, sc, sh, g1, prev_t, pool_w, pool_scale):
    return pl.pallas_call(
        _spool_kernel,
        out_shape=[jax.ShapeDtypeStruct(x.shape, F32), jax.ShapeDtypeStruct(x.shape, F32)],
        compiler_params=pltpu.CompilerParams(vmem_limit_bytes=32 * MIB),
    )(x, g, sc, sh, g1, prev_t, pool_w, pool_scale)


def _rel_bucket(dist):
    max_exact = REL_BUCKETS // 2
    n = jnp.maximum(dist, 1).astype(F32)
    large = max_exact + (jnp.log(n / max_exact) / math.log(REL_MAX_DIST / max_exact)
                         * (REL_BUCKETS - max_exact)).astype(jnp.int32)
    large = jnp.minimum(large, REL_BUCKETS - 1)
    return jnp.where(dist < max_exact, dist, large)


def _prompt_bias_table(rel_bias):
    blk = ATT_BLOCK
    j = blk + jnp.arange(blk)[:, None] - jnp.arange(2 * blk)[None, :]
    tables = []
    for gi, dil in enumerate(DILATIONS):
        bucket = _rel_bucket(jnp.clip(j, 0, blk) * dil)[None]
        rb = rel_bias[:, gi * ATT_HPP:(gi + 1) * ATT_HPP].astype(F32)
        tbl = jnp.zeros((ATT_HPP, blk, 2 * blk), F32)
        for k in range(REL_BUCKETS):
            tbl = jnp.where(bucket == k, rb[k][:, None, None], tbl)
        tables.append(tbl)
    return jnp.concatenate(tables, axis=0)


def _sample_bias_table(rel_bias):
    blk = ATT_BLOCK
    bufs, news = [], []
    for gi, dil in enumerate(DILATIONS):
        tbl = rel_bias[:, gi * ATT_HPP:(gi + 1) * ATT_HPP][_rel_bucket(dil * jnp.arange(blk + 1))]
        bufs.append(tbl[blk:0:-1])
        news.append(tbl[0])
    return jnp.stack(bufs)[..., None].astype(F32), jnp.stack(news)[..., None].astype(F32)


def _pad_rows(a, rows, axis=0):
    pad = [(0, 0)] * a.ndim
    pad[axis] = (0, rows - a.shape[axis])
    return jnp.pad(a, pad)


def _mlp_tiles(t):
    return (1024, 512) if t >= 1024 else (t, 512)


def kernel(x_prompt, x_sample, state_ssm, state_conv, cache_kv_w128, cache_kv_w512, cache_kv_w2048, state_pool, c_prompt, c_sample, rel_bias, ada_w, ada_b, norm_mix_g, norm_mlp_g, in_proj_w, conv_w, conv_b, dt_bias, a_log, d_skip, ssd_norm_g, q_norm_g, k_norm_g, out_proj_w, pool_w, pool_scale, mlp_w1, mlp_w2):
    d = D_MODEL
    c_all = _pad_rows(jnp.concatenate([c_prompt, c_sample], axis=0), SAMPLE_ROWS)
    mod = _ada(c_all, ada_w, ada_b)

    def mods(layer):
        parts = [mod[layer, :, i * d:(i + 1) * d] for i in range(6)]
        prompt = [p[:N_PROMPT, None, :] for p in parts]
        sample = [_pad_rows(p[N_PROMPT:N_PROMPT + N_SAMPLE], SAMPLE_ROWS)[None] for p in parts]
        return prompt, sample

    w_in_t = jnp.transpose(in_proj_w[0])
    b_dt = jnp.pad(dt_bias[0], (0, DT_PAD - SSD_HEADS)).reshape(1, DT_PAD)
    a_log_pad = jnp.pad(a_log[0], (0, DT_PAD - SSD_HEADS)).reshape(1, DT_PAD)
    dskip_x = jnp.repeat(d_skip[0], SSD_HEAD_DIM).reshape(1, SSD_INNER)
    ssd_g = ssd_norm_g.reshape(1, SSD_INNER)
    qg, kg = q_norm_g.reshape(1, ATT_HEAD_DIM), k_norm_g.reshape(1, ATT_HEAD_DIM)
    cw, cb = conv_w[0], conv_b.reshape(1, CONV_DIM)
    g_mix = [norm_mix_g[l].reshape(1, d) for l in range(2)]
    g_mlp = [norm_mlp_g[l].reshape(1, d) for l in range(2)]
    pscale = pool_scale.reshape(1, d)
    pool_wb = pool_w[0].astype(BF16)

    (sh1, sc1, g1, sh2, sc2, g2), (sh1s, sc1s, g1s, sh2s, sc2s, g2s) = mods(0)
    (sh1b, sc1b, g1b, sh2b, sc2b, g2b), (sh1sb, sc1sb, g1sb, sh2sb, sc2sb, g2sb) = mods(1)

    zx, dt = _inproj(x_prompt, g_mix[0], sc1, sh1, w_in_t, 0, ZX_COLS, 1024, 512, dt_b=b_dt)
    (qkv,) = _inproj(x_prompt, g_mix[0], sc1, sh1, w_in_t, QKV_COL0, 3 * ATT_QKV, 1024, 512, qk_gains=(qg, kg))
    p_conv = zx[:, SEQ - (SSD_CONV - 1):, SSD_INNER:][None]
    xbc = _conv_prompt(zx, cw, cb)
    y_ssd, h_fin = _ssd_prompt(xbc, dt, zx, a_log_pad, dskip_x, ssd_g)
    p_ssm = h_fin.reshape(1, N_PROMPT, SSD_HEADS, SSD_HEAD_DIM, SSD_STATE)
    att, kv128, kv512, kv2048 = _attn_prompt(qkv, _prompt_bias_table(rel_bias))
    p_kv = [kv.reshape(1, N_PROMPT, w, 2, ATT_HPP, ATT_HEAD_DIM) for kv, w in zip((kv128, kv512, kv2048), WINDOWS)]
    x1 = _outproj(y_ssd, att, out_proj_w, x_prompt, g1, 1024, 512)
    x2 = _mlp(x1, g_mlp[0], sc2, sh2, g2, mlp_w1, mlp_w2, 0, *_mlp_tiles(SEQ))
    x3, u_last = _pool_prompt(x2, g_mix[1], sc1b, sh1b, g1b, pool_wb, pscale)
    p_pool = u_last[:, 1:][None]
    y_prompt = _mlp(x3, g_mlp[1], sc2b, sh2b, g2b, mlp_w1, mlp_w2, 1, *_mlp_tiles(SEQ))

    rows = SAMPLE_ROWS
    xs0 = _pad_rows(x_sample.reshape(N_SAMPLE, d), rows)[None]
    zx_s, dt_s = _inproj(xs0, g_mix[0], sc1s, sh1s, w_in_t, 0, ZX_COLS, rows, 512, dt_b=b_dt)
    (qkv_s,) = _inproj(xs0, g_mix[0], sc1s, sh1s, w_in_t, QKV_COL0, 3 * ATT_QKV, rows, 512, qk_gains=(qg, kg))
    zx_s, dt_s = zx_s[0], dt_s[0]
    qkv_s = jnp.transpose(qkv_s[0, :, :N_SAMPLE], (1, 0, 2))
    xbc_new = zx_s[:N_SAMPLE, SSD_INNER:]
    s_conv = jnp.concatenate([state_conv[0][:, 1:], xbc_new[:, None, :]], axis=1)[None]
    conv_prev_t = _pad_rows(jnp.transpose(state_conv[0], (1, 0, 2)), rows, axis=1)
    xbc_s = _conv_sample(zx_s, conv_prev_t, cw, cb)
    xs_s = xbc_s[:N_SAMPLE, :SSD_INNER]
    e = SSD_HEADS_PER_GROUP
    b_s = jnp.repeat(xbc_s[:N_SAMPLE, SSD_INNER:SSD_INNER + SSD_GROUPS * SSD_STATE]
                     .reshape(N_SAMPLE, SSD_GROUPS, 1, SSD_STATE), e, axis=1)
    c_s = jnp.repeat(xbc_s[:N_SAMPLE, SSD_INNER + SSD_GROUPS * SSD_STATE:]
                     .reshape(N_SAMPLE, SSD_GROUPS, 1, SSD_STATE), e, axis=1)
    h_new, y_s = _ssd_step(state_ssm[0], xs_s.reshape(N_SAMPLE, SSD_HEADS, SSD_HEAD_DIM, 1),
                           dt_s[:N_SAMPLE, :SSD_HEADS].reshape(N_SAMPLE, SSD_HEADS, 1, 1), b_s, c_s,
                           a_log[0].reshape(SSD_HEADS, 1, 1))
    s_ssm = h_new[None]
    y_s = _gate_sample(_pad_rows(y_s.reshape(N_SAMPLE, SSD_INNER), rows), xbc_s[:, :SSD_INNER],
                       zx_s[:, :SSD_INNER], dskip_x, ssd_g)
    caches = (cache_kv_w128, cache_kv_w512, cache_kv_w2048)
    strided = [c.reshape(N_SAMPLE, ATT_BLOCK, dil, 2, ATT_HPP, ATT_HEAD_DIM) for c, dil in zip(caches, DILATIONS)]
    bias_buf, bias_new = _sample_bias_table(rel_bias)
    att_s, k_new, v_new = _attn_sample(qkv_s, strided, bias_buf, bias_new)
    new_rows = jnp.stack([k_new, v_new], axis=2)
    s_kv = [_kv_shift(c, new_rows[:, gi:gi + 1]) for gi, c in enumerate(caches)]
    att_s = _pad_rows(att_s.reshape(N_SAMPLE, ATT_OUT), rows)
    x1s = _outproj(y_s[None], att_s[None], out_proj_w, xs0, g1s, rows, 512)
    x2s = _mlp(x1s, g_mlp[0], sc2s, sh2s, g2s, mlp_w1, mlp_w2, 0, *_mlp_tiles(rows))
    pool_prev_t = _pad_rows(jnp.transpose(state_pool[0], (1, 0, 2)), rows, axis=1)
    x3s, u_s = _pool_sample(x2s[0], g_mix[1], sc1sb[0], sh1sb[0], g1sb[0], pool_prev_t, pool_wb, pscale)
    s_pool = jnp.concatenate([state_pool[0][:, 1:], u_s[:N_SAMPLE, None, :]], axis=1)[None]
    y_s2 = _mlp(x3s[None], g_mlp[1], sc2sb, sh2sb, g2sb, mlp_w1, mlp_w2, 1, *_mlp_tiles(rows))
    y_sample = y_s2[0, :N_SAMPLE].reshape(N_SAMPLE, 1, d)

    return (y_prompt, y_sample, p_ssm, p_conv, p_kv[0], p_kv[1], p_kv[2], p_pool,
            s_ssm, s_conv, s_kv[0], s_kv[1], s_kv[2], s_pool)
```

```python
import functools
import math

import jax
import jax.numpy as jnp
from jax import lax
from jax.experimental import pallas as pl
from jax.experimental.pallas import tpu as pltpu

F32 = jnp.float32
BF16 = jnp.bfloat16

D_MODEL = 2048
SEQ = 2048
N_PROMPT = 4
N_SAMPLE = 8
SAMPLE_ROWS = 16
PAST_LEN = 16384
SSD_HEADS = 32
SSD_HEAD_DIM = 64
SSD_INNER = SSD_HEADS * SSD_HEAD_DIM
SSD_GROUPS = 8
SSD_HEADS_PER_GROUP = SSD_HEADS // SSD_GROUPS
SSD_STATE = 128
SSD_CONV = 4
SSD_CHUNK = 128
CONV_DIM = SSD_INNER + 2 * SSD_GROUPS * SSD_STATE
ATT_HEAD_DIM = 128
WINDOWS = (128, 512, 2048)
DILATIONS = (1, 4, 16)
ATT_BLOCK = 128
ATT_UNIT_ROWS = 128
ATT_MAX_ROW_STRIDE = 4
ATT_HPP = 8
ATT_HEADS = ATT_HPP * len(WINDOWS)
ATT_QKV = ATT_HEADS * ATT_HEAD_DIM
ATT_OUT = ATT_HPP * ATT_HEAD_DIM
REL_BUCKETS = 32
REL_MAX_DIST = 2048
ZX_COLS = SSD_INNER + CONV_DIM
DT_COL0 = ZX_COLS
QKV_COL0 = ZX_COLS + SSD_HEADS
QKV_LANE_SHIFT = QKV_COL0 % 128
DT_PAD = 128
MIX_OUT = SSD_INNER + ATT_OUT
POOL_WINDOWS = (2, 4, 8, 16)
POOL_CH = D_MODEL // len(POOL_WINDOWS)
POOL_MAX = 16
D_FF = 4 * D_MODEL
EPS = 1e-6
NEG = -1e30
MIB = 1024 * 1024


def _params(semantics, vmem_mib):
    return pltpu.CompilerParams(dimension_semantics=semantics, vmem_limit_bytes=vmem_mib * MIB)


def _sigmoid(x):
    return 0.5 * jnp.tanh(0.5 * x) + 0.5


def _silu(x):
    return x * _sigmoid(x)


def _softplus(x):
    return jnp.maximum(x, 0.0) + jnp.log1p(jnp.exp(-jnp.abs(x)))


def _rms(x):
    return x * lax.rsqrt(jnp.mean(x * x, axis=-1, keepdims=True) + EPS)


def _normmod(x, g, sc, sh):
    return (_rms(x) * g) * (1.0 + sc) + sh


def _normmod_rows(u_ref, x_ref, g_ref, sc_ref, sh_ref):
    tm = u_ref.shape[0]
    rc = min(tm, 256)
    per_row = sc_ref.shape[1] != 1

    def body(i, carry):
        rows = pl.ds(pl.multiple_of(i * rc, rc), rc)
        sc = sc_ref[0, rows, :] if per_row else sc_ref[0]
        sh = sh_ref[0, rows, :] if per_row else sh_ref[0]
        u_ref[rows, :] = _normmod(x_ref[0, rows, :], g_ref[...], sc, sh).astype(BF16)
        return carry

    lax.fori_loop(0, tm // rc, body, 0)


def _bdot(a, b):
    return jnp.dot(a.astype(BF16), b.astype(BF16), preferred_element_type=F32)


def _split3(v):
    hi = v.astype(BF16)
    r = v - hi.astype(F32)
    mid = r.astype(BF16)
    lo = (r - mid.astype(F32)).astype(BF16)
    return hi, mid, lo


def _select_right(v, sel, pieces=3):
    parts = _split3(v)[:pieces]
    out = jnp.dot(parts[0], sel, preferred_element_type=F32)
    for p in parts[1:]:
        out = out + jnp.dot(p, sel, preferred_element_type=F32)
    return out


def _select_left(sel, v):
    hi, mid, lo = _split3(v)
    d = lambda a: jnp.dot(sel, a, preferred_element_type=F32)
    return d(hi) + d(mid) + d(lo)


def _ada_kernel(c_ref, w_ref, b_ref, o_ref):
    o_ref[0] = _bdot(_silu(c_ref[...]), w_ref[0]) + b_ref[0]


def _ada(c_all, ada_w, ada_b):
    depth, _, n = ada_w.shape
    tn = 1024
    return pl.pallas_call(
        _ada_kernel,
        name="ada_mod",
        grid=(depth, n // tn),
        in_specs=[
            pl.BlockSpec((SAMPLE_ROWS, D_MODEL), lambda l, j: (0, 0)),
            pl.BlockSpec((1, D_MODEL, tn), lambda l, j: (l, 0, j)),
            pl.BlockSpec((1, 1, tn), lambda l, j: (l, 0, j)),
        ],
        out_specs=pl.BlockSpec((1, SAMPLE_ROWS, tn), lambda l, j: (l, 0, j)),
        out_shape=jax.ShapeDtypeStruct((depth, SAMPLE_ROWS, n), F32),
        compiler_params=_params(("arbitrary", "arbitrary"), 40),
    )(c_all, ada_w, ada_b.reshape(depth, 1, n))


def _bdot_t(a, bt):
    return lax.dot_general(a.astype(BF16), bt.astype(BF16), (((1,), (1,)), ((), ())), preferred_element_type=F32)


def _inproj_kernel(*refs, mode):
    if mode == "dt":
        x_ref, g_ref, sc_ref, sh_ref, w_ref, wdt_ref, dtb_ref, o_ref, dt_ref, u_ref = refs
    else:
        x_ref, g_ref, sc_ref, sh_ref, w_ref, qg_ref, kg_ref, o_ref, u_ref = refs
    j = pl.program_id(2)

    @pl.when(j == 0)
    def _():
        _normmod_rows(u_ref, x_ref, g_ref, sc_ref, sh_ref)
        if mode == "dt":
            dt_ref[0] = _softplus(_bdot_t(u_ref[...], wdt_ref[...]) + dtb_ref[...])

    acc = _bdot_t(u_ref[...], w_ref[...])
    if mode == "dt":
        o_ref[0] = acc
    else:
        heads = o_ref.shape[1]
        q_tiles = ATT_HEADS // heads

        @pl.when(j < 2 * q_tiles)
        def _():
            gain = jnp.where(j < q_tiles, qg_ref[...], kg_ref[...])
            for hc in range(heads):
                o_ref[0, hc] = _rms(acc[:, hc * ATT_HEAD_DIM:(hc + 1) * ATT_HEAD_DIM]) * gain

        @pl.when(j >= 2 * q_tiles)
        def _():
            for hc in range(heads):
                o_ref[0, hc] = acc[:, hc * ATT_HEAD_DIM:(hc + 1) * ATT_HEAD_DIM]


def _inproj(x, g, sc, sh, wt, row0, n_cols, tm, tn, dt_b=None, qk_gains=None):
    nb, t, d = x.shape
    r = sc.shape[1]
    has_dt = dt_b is not None
    if row0 % tn == 0:
        w_spec = pl.BlockSpec((tn, d), lambda b, i, j: (row0 // tn + j, 0))
    else:
        assert row0 % 8 == 0
        w_spec = pl.BlockSpec((pl.Element(tn), pl.Element(d)),
                              lambda b, i, j: (pl.multiple_of(row0 + j * tn, 8), 0))
    in_specs = [
        pl.BlockSpec((1, tm, d), lambda b, i, j: (b, i, 0)),
        pl.BlockSpec((1, d), lambda b, i, j: (0, 0)),
        pl.BlockSpec((1, r if r == 1 else tm, d), lambda b, i, j: (b, 0 if r == 1 else i, 0)),
        pl.BlockSpec((1, r if r == 1 else tm, d), lambda b, i, j: (b, 0 if r == 1 else i, 0)),
        w_spec,
    ]
    args = [x, g, sc, sh, wt]
    if has_dt:
        in_specs += [pl.BlockSpec((DT_PAD, d), lambda b, i, j: (DT_COL0 // DT_PAD, 0)),
                     pl.BlockSpec((1, DT_PAD), lambda b, i, j: (0, 0))]
        args += [wt, dt_b]
        out_specs = [pl.BlockSpec((1, tm, tn), lambda b, i, j: (b, i, j)),
                     pl.BlockSpec((1, tm, DT_PAD), lambda b, i, j: (b, i, 0))]
        out_shape = [jax.ShapeDtypeStruct((nb, t, n_cols), F32), jax.ShapeDtypeStruct((nb, t, DT_PAD), F32)]
    else:
        hd = ATT_HEAD_DIM
        in_specs += [pl.BlockSpec((1, hd), lambda b, i, j: (0, 0))] * 2
        args += list(qk_gains)
        out_specs = [pl.BlockSpec((1, tn // hd, tm, hd), lambda b, i, j: (b, j, i, 0))]
        out_shape = [jax.ShapeDtypeStruct((nb, n_cols // hd, t, hd), F32)]
    return pl.pallas_call(
        functools.partial(_inproj_kernel, mode="dt" if has_dt else "qkv"),
        name="in_proj",
        grid=(nb, t // tm, n_cols // tn),
        in_specs=in_specs,
        out_specs=out_specs,
        out_shape=out_shape,
        scratch_shapes=[pltpu.VMEM((tm, d), BF16)],
        compiler_params=_params(("arbitrary", "arbitrary", "arbitrary"), 48),
    )(*args)


def _ssd_kernel(z_ref, xraw_ref, bcraw_ref, dt_ref, cw_ref, cb_ref, alog_ref, dskip_ref, g_ref,
                y_ref, hout_ref, h_ref, yacc_ref, act_ref, carry_ref):
    q = SSD_CHUNK
    gw = SSD_HEADS_PER_GROUP * SSD_HEAD_DIM
    slab = 512
    chunk = pl.program_id(1)

    @pl.when(chunk == 0)
    def _():
        h_ref[...] = jnp.zeros(h_ref.shape, F32)
        carry_ref[...] = jnp.zeros(carry_ref.shape, F32)

    row8 = lax.broadcasted_iota(jnp.int32, (8, slab), 0)
    for s in range(CONV_DIM // slab):
        cs = slice(s * slab, (s + 1) * slab)
        src, off = (xraw_ref, s * slab) if s * slab < SSD_INNER else (bcraw_ref, s * slab - SSD_INNER)
        x = src[0, :, off:off + slab]
        w = cw_ref[:, cs]
        bias = cb_ref[:, cs]
        taps = [w[SSD_CONV - 1 - k:SSD_CONV - k] for k in range(SSD_CONV)]
        acc = bias + taps[0] * x
        for k in range(1, SSD_CONV):
            acc = acc + taps[k] * pltpu.roll(x, k, axis=0)
        act_ref[:, cs] = _silu(acc)
        x8 = x[0:8]
        carry = carry_ref[:, cs]
        acc8 = bias + taps[0] * x8
        for k in range(1, SSD_CONV):
            acc8 = acc8 + taps[k] * jnp.where(row8 < k, pltpu.roll(carry, k, axis=0), pltpu.roll(x8, k, axis=0))
        act_ref[0:8, cs] = _silu(acc8)
        carry_ref[:, cs] = x[q - 8:q]

    row = lax.broadcasted_iota(jnp.int32, (q, q), 0)
    col = lax.broadcasted_iota(jnp.int32, (q, q), 1)
    causal = row >= col
    tri = jnp.where(causal, 1.0, 0.0).astype(BF16)

    dt = dt_ref[0]
    a = dt * (-jnp.exp(alog_ref[...]))
    acum = _select_left(tri, a)
    acum_t = acum.T
    dt_t = dt.T
    last = acum[q - 1:q, :]
    decay_end = jnp.exp(last - acum) * dt
    eacum = jnp.exp(acum)
    chunk_decay = jnp.broadcast_to(jnp.exp(last), (8, DT_PAD))

    head_in_group = lax.broadcasted_iota(jnp.int32, (q, gw), 1) // SSD_HEAD_DIM
    lane_head = lax.broadcasted_iota(jnp.int32, (DT_PAD, gw), 1) // SSD_HEAD_DIM
    row_head = lax.broadcasted_iota(jnp.int32, (DT_PAD, gw), 0)
    b0 = SSD_INNER
    c0 = SSD_INNER + SSD_GROUPS * SSD_STATE
    ssq = jnp.zeros((q, 1), F32)

    for g in range(SSD_GROUPS):
        cs = slice(g * gw, (g + 1) * gw)
        expand = jnp.where(lane_head + g * SSD_HEADS_PER_GROUP == row_head, 1.0, 0.0).astype(BF16)
        decay_end_x = _select_right(decay_end, expand, pieces=2)
        eacum_x = _select_right(eacum, expand, pieces=2)
        chunk_decay_x = _select_right(chunk_decay, expand)[0:1]

        xs = act_ref[:, cs]
        bg_t = act_ref[:, b0 + g * SSD_STATE:b0 + (g + 1) * SSD_STATE].T.astype(BF16)
        cg = act_ref[:, c0 + g * SSD_STATE:c0 + (g + 1) * SSD_STATE].astype(BF16)
        cb = jnp.dot(cg, bg_t, preferred_element_type=F32)
        h_prev = h_ref[:, cs]
        y_g = jnp.dot(cg, h_prev.astype(BF16), preferred_element_type=F32) * eacum_x
        h_ref[:, cs] = h_prev * chunk_decay_x + jnp.dot(bg_t, (xs * decay_end_x).astype(BF16),
                                                        preferred_element_type=F32)
        xg = xs.astype(BF16)
        for e in range(SSD_HEADS_PER_GROUP):
            hd = g * SSD_HEADS_PER_GROUP + e
            seg = acum[:, hd:hd + 1] - acum_t[hd:hd + 1, :]
            lw = jnp.where(causal, jnp.exp(jnp.where(causal, seg, 0.0)), 0.0) * dt_t[hd:hd + 1, :]
            yd = jnp.dot((cb * lw).astype(BF16), xg, preferred_element_type=F32)
            y_g = y_g + jnp.where(head_in_group == e, yd, 0.0)
        y_g = (y_g + dskip_ref[:, cs] * xs) * _silu(z_ref[0, :, cs])
        yacc_ref[:, cs] = y_g
        ssq = ssq + jnp.sum(y_g * y_g, axis=-1, keepdims=True)

    inv_rms = lax.rsqrt(ssq * (1.0 / SSD_INNER) + EPS)
    y_ref[0] = (yacc_ref[...] * inv_rms * g_ref[...]).astype(BF16)

    @pl.when(chunk == pl.num_programs(1) - 1)
    def _():
        hout_ref[0] = h_ref[...].T


def _ssd_prompt(zx, dt, conv_w, conv_b, a_log_pad, dskip_x, norm_g):
    nb, t, _ = zx.shape
    q = SSD_CHUNK
    wide = SSD_INNER
    assert ZX_COLS == 3 * wide
    return pl.pallas_call(
        _ssd_kernel,
        name="ssd_chunk",
        grid=(nb, t // q),
        in_specs=[
            pl.BlockSpec((1, q, wide), lambda b, c: (b, c, 0)),
            pl.BlockSpec((1, q, wide), lambda b, c: (b, c, 1)),
            pl.BlockSpec((1, q, wide), lambda b, c: (b, c, 2)),
            pl.BlockSpec((1, q, DT_PAD), lambda b, c: (b, c, 0)),
            pl.BlockSpec((SSD_CONV, CONV_DIM), lambda b, c: (0, 0)),
            pl.BlockSpec((1, CONV_DIM), lambda b, c: (0, 0)),
            pl.BlockSpec((1, DT_PAD), lambda b, c: (0, 0)),
            pl.BlockSpec((1, SSD_INNER), lambda b, c: (0, 0)),
            pl.BlockSpec((1, SSD_INNER), lambda b, c: (0, 0)),
        ],
        out_specs=[
            pl.BlockSpec((1, q, SSD_INNER), lambda b, c: (b, c, 0)),
            pl.BlockSpec((1, SSD_INNER, SSD_STATE), lambda b, c: (b, 0, 0)),
        ],
        out_shape=[
            jax.ShapeDtypeStruct((nb, t, SSD_INNER), BF16),
            jax.ShapeDtypeStruct((nb, SSD_INNER, SSD_STATE), F32),
        ],
        scratch_shapes=[pltpu.VMEM((SSD_STATE, SSD_INNER), F32), pltpu.VMEM((q, SSD_INNER), F32),
                        pltpu.VMEM((q, CONV_DIM), F32), pltpu.VMEM((8, CONV_DIM), F32)],
        compiler_params=_params(("arbitrary", "arbitrary"), 40),
    )(zx, zx, zx, dt, conv_w, conv_b, a_log_pad, dskip_x, norm_g)


def _attn_kernel(q0, q1, q2, k0, k1, k2, v0, v1, v2, bias0, bias1, bias2,
                 att_ref, kv0_ref, kv1_ref, kv2_ref, qb_s, kb_s, vb_s, ok_s, os_, ls, tmp_s, ltmp_s, kv_sem):
    blk = ATT_BLOCK
    ur = ATT_UNIT_ROWS
    hd = ATT_HEAD_DIM
    scale = 1.0 / math.sqrt(hd)
    batch, slot = pl.program_id(0), pl.program_id(1)
    qrefs, krefs, vrefs, brefs = (q0, q1, q2), (k0, k1, k2), (v0, v1, v2), (bias0, bias1, bias2)
    kvrefs = (kv0_ref, kv1_ref, kv2_ref)
    nt = (((1,), (1,)), ((), ()))

    def cache_copies():
        copies = []
        for g, w in enumerate(WINDOWS):
            for part, src in enumerate((krefs[g], vrefs[g])):
                cols = pl.ds(pl.multiple_of((part * ATT_HPP + slot) * hd, hd), hd)
                copies.append(pltpu.make_async_copy(
                    src.at[0, 0, pl.ds(SEQ - w, w), :], kvrefs[g].at[batch, :, cols], kv_sem.at[2 * g + part]))
        return copies

    for cp in cache_copies():
        cp.start()

    def compute():
        row = lax.broadcasted_iota(jnp.int32, (blk, 2 * blk), 0)
        col = lax.broadcasted_iota(jnp.int32, (blk, 2 * blk), 1)
        ok_s[...] = jnp.where(jnp.where(col < blk, col - row, row - (col - blk)) >= 0, 1.0, 0.0)

        f = ATT_MAX_ROW_STRIDE
        seg = SEQ // f

        def strided(start, size, stride):
            return pl.ds(start, size) if stride == 1 else pl.ds(start, size, stride=stride)

        def regroup(src_ref, dil, emit):
            if dil <= f:
                sub = SEQ // dil
                ch = min(sub, 256)
                for r in range(dil):
                    for c in range(sub // ch):
                        emit(pl.ds(r * sub + c * ch, ch), src_ref[0, 0, strided(r + c * ch * dil, ch, dil), :])
            else:
                assert dil == f * f
                for r1 in range(f):
                    for c in range(seg // 256):
                        tmp_s[pl.ds(r1 * seg + c * 256, 256), :] = src_ref[0, 0, strided(r1 + c * 256 * f, 256, f), :]
                sub = seg // f
                for r1 in range(f):
                    for r2 in range(f):
                        emit(pl.ds((f * r2 + r1) * sub, sub), tmp_s[strided(r1 * seg + r2, sub, f), :])

        for g, dil in enumerate(DILATIONS):
            vb_s[g, :, hd:2 * hd] = jnp.ones((SEQ, hd), BF16)

            def put_q(rows, val, g=g):
                qb_s[g, rows, :] = val.astype(BF16)

            def put_k(rows, val, g=g):
                kb_s[g, rows, :] = val.astype(BF16)

            def put_v(rows, val, g=g):
                vb_s[g, rows, 0:hd] = val.astype(BF16)

            regroup(qrefs[g], dil, put_q)
            regroup(krefs[g], dil, put_k)
            regroup(vrefs[g], dil, put_v)

        for g, dil in enumerate(DILATIONS):
            nblk = SEQ // (blk * dil)
            for r in range(dil):
                for n in range(nblk):
                    base = (r * nblk + n) * blk
                    key0, col0 = (base - blk, 0) if n > 0 else (base, blk)
                    kcat = kb_s[g, key0:base + blk, :]
                    vcat = vb_s[g, key0:base + blk, :]
                    for half in range(blk // ur):
                        rs = slice(half * ur, (half + 1) * ur)
                        qb = qb_s[g, base + half * ur:base + (half + 1) * ur, :]
                        l = lax.dot_general(qb, kcat, nt, preferred_element_type=F32)
                        l = jnp.where(ok_s[rs, col0:2 * blk] > 0.5, l * scale + brefs[g][0, rs, col0:2 * blk], NEG)
                        m = jnp.max(l, axis=-1, keepdims=True)
                        p = jnp.exp(l - m).astype(BF16)
                        oe = jnp.dot(p, vcat, preferred_element_type=F32)
                        den = oe[:, hd:2 * hd]
                        s0 = n * blk + half * ur
                        if dil <= f:
                            dst = strided(r + s0 * dil, ur, dil)
                            os_[g, dst, :] = oe[:, 0:hd] / den
                            ls[g, dst, :] = m + jnp.log(den)
                        else:
                            dst = strided((r % f) * seg + f * s0 + r // f, ur, f)
                            tmp_s[dst, :] = oe[:, 0:hd] / den
                            ltmp_s[dst, :] = m + jnp.log(den)
            if dil > f:
                for r1 in range(f):
                    for c in range(seg // 256):
                        dst = strided(r1 + c * 256 * f, 256, f)
                        os_[g, dst, :] = tmp_s[pl.ds(r1 * seg + c * 256, 256), :]
                        ls[g, dst, :] = ltmp_s[pl.ds(r1 * seg + c * 256, 256), :]

        def mix_body(i, carry):
            rows = pl.ds(pl.multiple_of(i * blk, blk), blk)
            l0, l1, l2 = ls[0, rows, :], ls[1, rows, :], ls[2, rows, :]
            mx = jnp.maximum(jnp.maximum(l0, l1), l2)
            e0, e1, e2 = jnp.exp(l0 - mx), jnp.exp(l1 - mx), jnp.exp(l2 - mx)
            num = e0 * os_[0, rows, :] + e1 * os_[1, rows, :] + e2 * os_[2, rows, :]
            att_ref[0, rows, :] = (num / (e0 + e1 + e2)).astype(BF16)
            return carry
        lax.fori_loop(0, SEQ // blk, mix_body, 0)

    compute()
    for cp in cache_copies():
        cp.wait()


def _attn_prompt(qkv, bias):
    nb = qkv.shape[0]
    hd = ATT_HEAD_DIM

    def head_spec(part, g):
        return pl.BlockSpec((1, 1, SEQ, hd), lambda b, h: (b, part * ATT_HEADS + g * ATT_HPP + h, 0, 0))

    in_specs = [head_spec(part, g) for part in range(3) for g in range(3)]
    in_specs += [pl.BlockSpec((1, ATT_BLOCK, 2 * ATT_BLOCK), functools.partial(lambda b, h, g: (g * ATT_HPP + h, 0, 0), g=g))
                 for g in range(3)]
    out_specs = [pl.BlockSpec((1, SEQ, hd), lambda b, h: (b, 0, h))]
    out_shape = [jax.ShapeDtypeStruct((nb, SEQ, ATT_OUT), BF16)]
    for w in WINDOWS:
        out_specs.append(pl.BlockSpec(memory_space=pl.ANY))
        out_shape.append(jax.ShapeDtypeStruct((nb, w, 2 * ATT_OUT), F32))
    return pl.pallas_call(
        _attn_kernel,
        name="attn_prompt",
        grid=(nb, ATT_HPP),
        in_specs=in_specs,
        out_specs=out_specs,
        out_shape=out_shape,
        scratch_shapes=[
            pltpu.VMEM((3, SEQ, hd), BF16), pltpu.VMEM((3, SEQ, hd), BF16), pltpu.VMEM((3, SEQ, 2 * hd), BF16),
            pltpu.VMEM((ATT_BLOCK, 2 * ATT_BLOCK), F32),
            pltpu.VMEM((3, SEQ, hd), F32), pltpu.VMEM((3, SEQ, hd), F32),
            pltpu.VMEM((SEQ, hd), F32), pltpu.VMEM((SEQ, hd), F32),
            pltpu.SemaphoreType.DMA((2 * len(WINDOWS),)),
        ],
        compiler_params=_params(("arbitrary", "arbitrary"), 48),
    )(*([qkv] * 9), bias, bias, bias)


def _outproj_kernel(y_ref, a_ref, wy_ref, wa_ref, x_ref, g1_ref, o_ref):
    mix = _bdot(y_ref[0], wy_ref[0]) + _bdot(a_ref[0], wa_ref[0])
    o_ref[0] = x_ref[0] + g1_ref[0] * mix


def _outproj(y, att, w, x, g1, tm, tn):
    nb, t, d = x.shape
    r = g1.shape[1]
    return pl.pallas_call(
        _outproj_kernel,
        name="out_proj",
        grid=(nb, t // tm, d // tn),
        in_specs=[
            pl.BlockSpec((1, tm, SSD_INNER), lambda b, i, j: (b, i, 0)),
            pl.BlockSpec((1, tm, ATT_OUT), lambda b, i, j: (b, i, 0)),
            pl.BlockSpec((1, SSD_INNER, tn), lambda b, i, j: (0, 0, j)),
            pl.BlockSpec((1, ATT_OUT, tn), lambda b, i, j: (0, SSD_INNER // ATT_OUT, j)),
            pl.BlockSpec((1, tm, tn), lambda b, i, j: (b, i, j)),
            pl.BlockSpec((1, r if r == 1 else tm, tn), lambda b, i, j: (b, 0 if r == 1 else i, j)),
        ],
        out_specs=pl.BlockSpec((1, tm, tn), lambda b, i, j: (b, i, j)),
        out_shape=jax.ShapeDtypeStruct((nb, t, d), F32),
        compiler_params=_params(("arbitrary", "arbitrary", "arbitrary"), 48),
    )(y, att, w, w, x, g1)


def _mlp_kernel(x_ref, g_ref, sc_ref, sh_ref, g2_ref, w1_ref, w2_ref, o_ref, u_ref):
    f = pl.program_id(2)

    @pl.when(f == 0)
    def _():
        _normmod_rows(u_ref, x_ref, g_ref, sc_ref, sh_ref)
        o_ref[0] = jnp.zeros(o_ref.shape[1:], F32)

    h = jnp.maximum(_bdot(u_ref[...], w1_ref[0]), 0.0)
    o_ref[0] += _bdot(h * h, w2_ref[0])

    @pl.when(f == pl.num_programs(2) - 1)
    def _():
        o_ref[0] = x_ref[0] + g2_ref[0] * o_ref[0]


def _mlp(x, g, sc, sh, g2, w1, w2, layer, tm, tf):
    nb, t, d = x.shape
    r = sc.shape[1]
    mod_spec = pl.BlockSpec((1, r if r == 1 else tm, d), lambda b, i, f: (b, 0 if r == 1 else i, 0))
    return pl.pallas_call(
        _mlp_kernel,
        name="mlp",
        grid=(nb, t // tm, D_FF // tf),
        in_specs=[
            pl.BlockSpec((1, tm, d), lambda b, i, f: (b, i, 0), pipeline_mode=pl.Buffered(1)),
            pl.BlockSpec((1, d), lambda b, i, f: (0, 0)),
            mod_spec, mod_spec, mod_spec,
            pl.BlockSpec((1, d, tf), lambda b, i, f: (layer, 0, f)),
            pl.BlockSpec((1, tf, d), lambda b, i, f: (layer, f, 0)),
        ],
        out_specs=pl.BlockSpec((1, tm, d), lambda b, i, f: (b, i, 0), pipeline_mode=pl.Buffered(1)),
        out_shape=jax.ShapeDtypeStruct((nb, t, d), F32),
        scratch_shapes=[pltpu.VMEM((tm, d), BF16)],
        compiler_params=_params(("arbitrary", "arbitrary", "arbitrary"), 56),
    )(x, g, sc, sh, g2, w1, w2)


def _pool_kernel(x_ref, g_ref, sc_ref, sh_ref, g1_ref, w_ref, scale_ref, o_ref, ulast_ref, ext_ref, *, tt):
    halo = POOL_MAX
    ti = pl.program_id(1)

    @pl.when(ti == 0)
    def _():
        ext_ref[0:halo, :] = jnp.zeros((halo, D_MODEL), F32)

    x = x_ref[0]
    u = _normmod(x, g_ref[...], sc_ref[0], sh_ref[0])
    ext_ref[halo:halo + tt, :] = u
    pos = ti * tt + lax.broadcasted_iota(jnp.int32, (tt, 1), 0)
    for gi, w in enumerate(POOL_WINDOWS):
        ch = slice(gi * POOL_CH, (gi + 1) * POOL_CH)
        acc = ext_ref[:, ch]
        span = 1
        while span < w:
            acc = acc + pltpu.roll(acc, span, axis=0)
            span *= 2
        s = acc[halo:halo + tt]
        inv_cnt = 1.0 / jnp.minimum(w, pos + 1).astype(F32)
        pooled = s * inv_cnt - u[:, ch]
        mix = _bdot(pooled, w_ref[gi]) * scale_ref[:, ch]
        o_ref[0, :, ch] = x[:, ch] + g1_ref[0, :, ch] * mix
    ulast_ref[0] = u[tt - halo:tt]
    ext_ref[0:halo, :] = u[tt - halo:tt]


def _pool_prompt(x, g, sc, sh, g1, pool_w, pool_scale):
    nb, t, d = x.shape
    tt = 512
    mod_spec = pl.BlockSpec((1, 1, d), lambda b, i: (b, 0, 0))
    return pl.pallas_call(
        functools.partial(_pool_kernel, tt=tt),
        name="pool_mix",
        grid=(nb, t // tt),
        in_specs=[
            pl.BlockSpec((1, tt, d), lambda b, i: (b, i, 0)),
            pl.BlockSpec((1, d), lambda b, i: (0, 0)),
            mod_spec, mod_spec, mod_spec,
            pl.BlockSpec((len(POOL_WINDOWS), POOL_CH, POOL_CH), lambda b, i: (0, 0, 0)),
            pl.BlockSpec((1, d), lambda b, i: (0, 0)),
        ],
        out_specs=[
            pl.BlockSpec((1, tt, d), lambda b, i: (b, i, 0)),
            pl.BlockSpec((1, POOL_MAX, d), lambda b, i: (b, 0, 0)),
        ],
        out_shape=[jax.ShapeDtypeStruct((nb, t, d), F32), jax.ShapeDtypeStruct((nb, POOL_MAX, d), F32)],
        scratch_shapes=[pltpu.VMEM((tt + POOL_MAX, d), F32)],
        compiler_params=_params(("arbitrary", "arbitrary"), 48),
    )(x, g, sc, sh, g1, pool_w, pool_scale)


def _sconv_kernel(zx_ref, prev_ref, w_ref, b_ref, o_ref):
    w = w_ref[...]
    acc = b_ref[...] + w[SSD_CONV - 1:SSD_CONV] * zx_ref[:, SSD_INNER:ZX_COLS]
    for k in range(SSD_CONV - 1):
        acc = acc + w[k:k + 1] * prev_ref[k]
    o_ref[...] = _silu(acc)


def _conv_sample(zx, prev_t, conv_w, conv_b):
    return pl.pallas_call(
        _sconv_kernel,
        out_shape=jax.ShapeDtypeStruct((SAMPLE_ROWS, CONV_DIM), F32),
        compiler_params=pltpu.CompilerParams(vmem_limit_bytes=32 * MIB),
    )(zx, prev_t, conv_w, conv_b)


def _sstep_kernel(h0_ref, x_ref, dt_ref, b_ref, c_ref, alog_ref, h_ref, y_ref):
    dt = dt_ref[0]
    h = jnp.exp(dt * (-jnp.exp(alog_ref[...]))) * h0_ref[0] + (dt * x_ref[0]) * b_ref[0]
    h_ref[0] = h
    y_ref[0] = jnp.sum(h * c_ref[0], axis=-1, keepdims=True)


def _ssd_step(h0, x, dt, bh, ch, a_log):
    nb = h0.shape[0]
    hp = (SSD_HEADS, SSD_HEAD_DIM)
    return pl.pallas_call(
        _sstep_kernel,
        grid=(nb,),
        in_specs=[
            pl.BlockSpec((1, *hp, SSD_STATE), lambda b: (b, 0, 0, 0)),
            pl.BlockSpec((1, *hp, 1), lambda b: (b, 0, 0, 0)),
            pl.BlockSpec((1, SSD_HEADS, 1, 1), lambda b: (b, 0, 0, 0)),
            pl.BlockSpec((1, SSD_HEADS, 1, SSD_STATE), lambda b: (b, 0, 0, 0)),
            pl.BlockSpec((1, SSD_HEADS, 1, SSD_STATE), lambda b: (b, 0, 0, 0)),
            pl.BlockSpec((SSD_HEADS, 1, 1), lambda b: (0, 0, 0)),
        ],
        out_specs=[
            pl.BlockSpec((1, *hp, SSD_STATE), lambda b: (b, 0, 0, 0)),
            pl.BlockSpec((1, *hp, 1), lambda b: (b, 0, 0, 0)),
        ],
        out_shape=[jax.ShapeDtypeStruct((nb, *hp, SSD_STATE), F32), jax.ShapeDtypeStruct((nb, *hp, 1), F32)],
        compiler_params=_params(("arbitrary",), 32),
    )(h0, x, dt, bh, ch, a_log)


def _sgate_kernel(y_ref, xs_ref, z_ref, dskip_ref, g_ref, o_ref):
    y = (y_ref[...] + dskip_ref[...] * xs_ref[...]) * _silu(z_ref[...])
    o_ref[...] = _rms(y) * g_ref[...]


def _gate_sample(y, xs, z, dskip_x, norm_g):
    return pl.pallas_call(
        _sgate_kernel,
        out_shape=jax.ShapeDtypeStruct(y.shape, F32),
    )(y, xs, z, dskip_x, norm_g)


def _sattn_kernel(qkv_ref, c0_ref, c1_ref, c2_ref, bbuf_ref, bnew_ref, att_ref, knew_ref, vnew_ref):
    scale = 1.0 / math.sqrt(ATT_HEAD_DIM)
    caches = (c0_ref, c1_ref, c2_ref)
    outs, lses = [], []
    for g in range(len(WINDOWS)):
        hs = g * ATT_HPP
        q = qkv_ref[0, hs:hs + ATT_HPP, :]
        k = qkv_ref[0, ATT_HEADS + hs:ATT_HEADS + hs + ATT_HPP, :]
        v = qkv_ref[0, 2 * ATT_HEADS + hs:2 * ATT_HEADS + hs + ATT_HPP, :]
        knew_ref[0, g] = k
        vnew_ref[0, g] = v
        kbuf = caches[g][:, 0]
        vbuf = caches[g][:, 1]
        lb = jnp.sum(kbuf * q[None], axis=-1, keepdims=True) * scale + bbuf_ref[g]
        ln = jnp.sum(k * q, axis=-1, keepdims=True) * scale + bnew_ref[g]
        m = jnp.maximum(jnp.max(lb, axis=0), ln)
        pb = jnp.exp(lb - m[None])
        pn = jnp.exp(ln - m)
        s = jnp.sum(pb, axis=0) + pn
        outs.append((jnp.sum(pb * vbuf, axis=0) + pn * v) / s)
        lses.append(m + jnp.log(s))
    mx = jnp.maximum(jnp.maximum(lses[0], lses[1]), lses[2])
    es = [jnp.exp(l - mx) for l in lses]
    att_ref[0] = (es[0] * outs[0] + es[1] * outs[1] + es[2] * outs[2]) / (es[0] + es[1] + es[2])


def _attn_sample(qkv, caches, bias_buf, bias_new):
    nb = qkv.shape[0]
    hd = ATT_HEAD_DIM
    in_specs = [pl.BlockSpec((1, 3 * ATT_HEADS, hd), lambda b: (b, 0, 0))]
    for _ in caches:
        in_specs.append(pl.BlockSpec((None, ATT_BLOCK, None, 2, ATT_HPP, hd), lambda b: (b, 0, 0, 0, 0, 0)))
    in_specs += [
        pl.BlockSpec((3, ATT_BLOCK, ATT_HPP, 1), lambda b: (0, 0, 0, 0)),
        pl.BlockSpec((3, ATT_HPP, 1), lambda b: (0, 0, 0)),
    ]
    return pl.pallas_call(
        _sattn_kernel,
        grid=(nb,),
        in_specs=in_specs,
        out_specs=[
            pl.BlockSpec((1, ATT_HPP, hd), lambda b: (b, 0, 0)),
            pl.BlockSpec((1, 3, ATT_HPP, hd), lambda b: (b, 0, 0, 0)),
            pl.BlockSpec((1, 3, ATT_HPP, hd), lambda b: (b, 0, 0, 0)),
        ],
        out_shape=[
            jax.ShapeDtypeStruct((nb, ATT_HPP, hd), F32),
            jax.ShapeDtypeStruct((nb, 3, ATT_HPP, hd), F32),
            jax.ShapeDtypeStruct((nb, 3, ATT_HPP, hd), F32),
        ],
        compiler_params=_params(("arbitrary",), 32),
    )(qkv, *caches, bias_buf, bias_new)


def _kvshift_kernel(x_ref, new_ref, o_ref, carry_ref):
    tw = x_ref.shape[0]

    @pl.when(pl.program_id(1) == 0)
    def _():
        carry_ref[...] = new_ref[0]

    o_ref[0:tw - 1] = x_ref[1:tw]
    o_ref[tw - 1] = carry_ref[...]
    carry_ref[...] = x_ref[0]


def _kv_shift(cache, new_row):
    _, nb, w, _, hpp, hd = cache.shape
    tw = min(w, 512)
    nblk = w // tw
    row = (2, hpp, hd)
    return pl.pallas_call(
        _kvshift_kernel,
        name="kv_shift",
        grid=(nb, nblk),
        in_specs=[
            pl.BlockSpec((None, None, tw, *row), lambda b, s: (0, b, nblk - 1 - s, 0, 0, 0)),
            pl.BlockSpec((None, 1, *row), lambda b, s: (b, 0, 0, 0, 0)),
        ],
        out_specs=pl.BlockSpec((None, None, tw, *row), lambda b, s: (0, b, nblk - 1 - s, 0, 0, 0)),
        out_shape=jax.ShapeDtypeStruct(cache.shape, cache.dtype),
        scratch_shapes=[pltpu.VMEM(row, F32)],
        compiler_params=_params(("arbitrary", "arbitrary"), 40),
    )(cache, new_row)


def _spool_kernel(x_ref, g_ref, sc_ref, sh_ref, g1_ref, prev_ref, w_ref, scale_ref, o_ref, u_ref):
    x = x_ref[...]
    u = _normmod(x, g_ref[...], sc_ref[...], sh_ref[...])
    u_ref[...] = u
    for gi, w in enumerate(POOL_WINDOWS):
        ch = slice(gi * POOL_CH, (gi + 1) * POOL_CH)
        s = u[:, ch]
        for k in range(1, w):
            s = s + prev_ref[POOL_MAX - 1 - k, :, ch]
        pooled = s / float(min(w, PAST_LEN + 1)) - u[:, ch]
        mix = _bdot(pooled, w_ref[gi]) * scale_ref[:, ch]
        o_ref[:, ch] = x[:, ch] + g1_ref[:, ch] * mix


def _pool_sample(x, g, sc, sh, g1, prev_t, pool_w, pool_scale):
    return pl.pallas_call(
        _spool_kernel,
        out_shape=[jax.ShapeDtypeStruct(x.shape, F32), jax.ShapeDtypeStruct(x.shape, F32)],
        compiler_params=pltpu.CompilerParams(vmem_limit_bytes=32 * MIB),
    )(x, g, sc, sh, g1, prev_t, pool_w, pool_scale)


def _rel_bucket(dist):
    max_exact = REL_BUCKETS // 2
    n = jnp.maximum(dist, 1).astype(F32)
    large = max_exact + (jnp.log(n / max_exact) / math.log(REL_MAX_DIST / max_exact)
                         * (REL_BUCKETS - max_exact)).astype(jnp.int32)
    large = jnp.minimum(large, REL_BUCKETS - 1)
    return jnp.where(dist < max_exact, dist, large)


def _prompt_bias_table(rel_bias):
    blk = ATT_BLOCK
    j = blk + jnp.arange(blk)[:, None] - jnp.arange(2 * blk)[None, :]
    tables = []
    for gi, dil in enumerate(DILATIONS):
        bucket = _rel_bucket(jnp.clip(j, 0, blk) * dil)[None]
        rb = rel_bias[:, gi * ATT_HPP:(gi + 1) * ATT_HPP].astype(F32)
        tbl = jnp.zeros((ATT_HPP, blk, 2 * blk), F32)
        for k in range(REL_BUCKETS):
            tbl = jnp.where(bucket == k, rb[k][:, None, None], tbl)
        tables.append(tbl)
    return jnp.concatenate(tables, axis=0)


def _sample_bias_table(rel_bias):
    blk = ATT_BLOCK
    bufs, news = [], []
    for gi, dil in enumerate(DILATIONS):
        tbl = rel_bias[:, gi * ATT_HPP:(gi + 1) * ATT_HPP][_rel_bucket(dil * jnp.arange(blk + 1))]
        bufs.append(tbl[blk:0:-1])
        news.append(tbl[0])
    return jnp.stack(bufs)[..., None].astype(F32), jnp.stack(news)[..., None].astype(F32)


def _pad_rows(a, rows, axis=0):
    pad = [(0, 0)] * a.ndim
    pad[axis] = (0, rows - a.shape[axis])
    return jnp.pad(a, pad)


def _mlp_tiles(t):
    return (1024, 512) if t >= 1024 else (t, 512)


def kernel(x_prompt, x_sample, state_ssm, state_conv, cache_kv_w128, cache_kv_w512, cache_kv_w2048, state_pool, c_prompt, c_sample, rel_bias, ada_w, ada_b, norm_mix_g, norm_mlp_g, in_proj_w, conv_w, conv_b, dt_bias, a_log, d_skip, ssd_norm_g, q_norm_g, k_norm_g, out_proj_w, pool_w, pool_scale, mlp_w1, mlp_w2):
    d = D_MODEL
    c_all = _pad_rows(jnp.concatenate([c_prompt, c_sample], axis=0), SAMPLE_ROWS)
    mod = _ada(c_all, ada_w, ada_b)

    def mods(layer):
        parts = [mod[layer, :, i * d:(i + 1) * d] for i in range(6)]
        prompt = [p[:N_PROMPT, None, :] for p in parts]
        sample = [_pad_rows(p[N_PROMPT:N_PROMPT + N_SAMPLE], SAMPLE_ROWS)[None] for p in parts]
        return prompt, sample

    w_in_t = jnp.transpose(in_proj_w[0])
    b_dt = jnp.pad(dt_bias[0], (0, DT_PAD - SSD_HEADS)).reshape(1, DT_PAD)
    a_log_pad = jnp.pad(a_log[0], (0, DT_PAD - SSD_HEADS)).reshape(1, DT_PAD)
    dskip_x = jnp.repeat(d_skip[0], SSD_HEAD_DIM).reshape(1, SSD_INNER)
    ssd_g = ssd_norm_g.reshape(1, SSD_INNER)
    qg, kg = q_norm_g.reshape(1, ATT_HEAD_DIM), k_norm_g.reshape(1, ATT_HEAD_DIM)
    cw, cb = conv_w[0], conv_b.reshape(1, CONV_DIM)
    g_mix = [norm_mix_g[l].reshape(1, d) for l in range(2)]
    g_mlp = [norm_mlp_g[l].reshape(1, d) for l in range(2)]
    pscale = pool_scale.reshape(1, d)
    pool_wb = pool_w[0].astype(BF16)

    (sh1, sc1, g1, sh2, sc2, g2), (sh1s, sc1s, g1s, sh2s, sc2s, g2s) = mods(0)
    (sh1b, sc1b, g1b, sh2b, sc2b, g2b), (sh1sb, sc1sb, g1sb, sh2sb, sc2sb, g2sb) = mods(1)

    zx, dt = _inproj(x_prompt, g_mix[0], sc1, sh1, w_in_t, 0, ZX_COLS, 1024, 512, dt_b=b_dt)
    (qkv,) = _inproj(x_prompt, g_mix[0], sc1, sh1, w_in_t, QKV_COL0, 3 * ATT_QKV, 1024, 512, qk_gains=(qg, kg))
    p_conv = zx[:, SEQ - (SSD_CONV - 1):, SSD_INNER:][None]
    y_ssd, h_fin = _ssd_prompt(zx, dt, cw, cb, a_log_pad, dskip_x, ssd_g)
    p_ssm = h_fin.reshape(1, N_PROMPT, SSD_HEADS, SSD_HEAD_DIM, SSD_STATE)
    att, kv128, kv512, kv2048 = _attn_prompt(qkv, _prompt_bias_table(rel_bias))
    p_kv = [kv.reshape(1, N_PROMPT, w, 2, ATT_HPP, ATT_HEAD_DIM) for kv, w in zip((kv128, kv512, kv2048), WINDOWS)]
    x1 = _outproj(y_ssd, att, out_proj_w, x_prompt, g1, 1024, 512)
    x2 = _mlp(x1, g_mlp[0], sc2, sh2, g2, mlp_w1, mlp_w2, 0, *_mlp_tiles(SEQ))
    x3, u_last = _pool_prompt(x2, g_mix[1], sc1b, sh1b, g1b, pool_wb, pscale)
    p_pool = u_last[:, 1:][None]
    y_prompt = _mlp(x3, g_mlp[1], sc2b, sh2b, g2b, mlp_w1, mlp_w2, 1, *_mlp_tiles(SEQ))

    rows = SAMPLE_ROWS
    xs0 = _pad_rows(x_sample.reshape(N_SAMPLE, d), rows)[None]
    zx_s, dt_s = _inproj(xs0, g_mix[0], sc1s, sh1s, w_in_t, 0, ZX_COLS, rows, 512, dt_b=b_dt)
    (qkv_s,) = _inproj(xs0, g_mix[0], sc1s, sh1s, w_in_t, QKV_COL0, 3 * ATT_QKV, rows, 512, qk_gains=(qg, kg))
    zx_s, dt_s = zx_s[0], dt_s[0]
    qkv_s = jnp.transpose(qkv_s[0, :, :N_SAMPLE], (1, 0, 2))
    xbc_new = zx_s[:N_SAMPLE, SSD_INNER:]
    s_conv = jnp.concatenate([state_conv[0][:, 1:], xbc_new[:, None, :]], axis=1)[None]
    conv_prev_t = _pad_rows(jnp.transpose(state_conv[0], (1, 0, 2)), rows, axis=1)
    xbc_s = _conv_sample(zx_s, conv_prev_t, cw, cb)
    xs_s = xbc_s[:N_SAMPLE, :SSD_INNER]
    e = SSD_HEADS_PER_GROUP
    b_s = jnp.repeat(xbc_s[:N_SAMPLE, SSD_INNER:SSD_INNER + SSD_GROUPS * SSD_STATE]
                     .reshape(N_SAMPLE, SSD_GROUPS, 1, SSD_STATE), e, axis=1)
    c_s = jnp.repeat(xbc_s[:N_SAMPLE, SSD_INNER + SSD_GROUPS * SSD_STATE:]
                     .reshape(N_SAMPLE, SSD_GROUPS, 1, SSD_STATE), e, axis=1)
    h_new, y_s = _ssd_step(state_ssm[0], xs_s.reshape(N_SAMPLE, SSD_HEADS, SSD_HEAD_DIM, 1),
                           dt_s[:N_SAMPLE, :SSD_HEADS].reshape(N_SAMPLE, SSD_HEADS, 1, 1), b_s, c_s,
                           a_log[0].reshape(SSD_HEADS, 1, 1))
    s_ssm = h_new[None]
    y_s = _gate_sample(_pad_rows(y_s.reshape(N_SAMPLE, SSD_INNER), rows), xbc_s[:, :SSD_INNER],
                       zx_s[:, :SSD_INNER], dskip_x, ssd_g)
    caches = (cache_kv_w128, cache_kv_w512, cache_kv_w2048)
    strided = [c.reshape(N_SAMPLE, ATT_BLOCK, dil, 2, ATT_HPP, ATT_HEAD_DIM) for c, dil in zip(caches, DILATIONS)]
    bias_buf, bias_new = _sample_bias_table(rel_bias)
    att_s, k_new, v_new = _attn_sample(qkv_s, strided, bias_buf, bias_new)
    new_rows = jnp.stack([k_new, v_new], axis=2)
    s_kv = [_kv_shift(c, new_rows[:, gi:gi + 1]) for gi, c in enumerate(caches)]
    att_s = _pad_rows(att_s.reshape(N_SAMPLE, ATT_OUT), rows)
    x1s = _outproj(y_s[None], att_s[None], out_proj_w, xs0, g1s, rows, 512)
    x2s = _mlp(x1s, g_mlp[0], sc2s, sh2s, g2s, mlp_w1, mlp_w2, 0, *_mlp_tiles(rows))
    pool_prev_t = _pad_rows(jnp.transpose(state_pool[0], (1, 0, 2)), rows, axis=1)
    x3s, u_s = _pool_sample(x2s[0], g_mix[1], sc1sb[0], sh1sb[0], g1sb[0], pool_prev_t, pool_wb, pscale)
    s_pool = jnp.concatenate([state_pool[0][:, 1:], u_s[:N_SAMPLE, None, :]], axis=1)[None]
    y_s2 = _mlp(x3s[None], g_mlp[1], sc2sb, sh2sb, g2sb, mlp_w1, mlp_w2, 1, *_mlp_tiles(rows))
    y_sample = y_s2[0, :N_SAMPLE].reshape(N_SAMPLE, 1, d)

    return (y_prompt, y_sample, p_ssm, p_conv, p_kv[0], p_kv[1], p_kv[2], p_pool,
            s_ssm, s_conv, s_kv[0], s_kv[1], s_kv[2], s_pool)
```

```python
import functools
import math

import jax
import jax.numpy as jnp
from jax import lax
from jax.experimental import pallas as pl
from jax.experimental.pallas import tpu as pltpu

F32 = jnp.float32
BF16 = jnp.bfloat16

D_MODEL = 2048
SEQ = 2048
N_PROMPT = 4
N_SAMPLE = 8
SAMPLE_ROWS = 16
PAST_LEN = 16384
SSD_HEADS = 32
SSD_HEAD_DIM = 64
SSD_INNER = SSD_HEADS * SSD_HEAD_DIM
SSD_GROUPS = 8
SSD_HEADS_PER_GROUP = SSD_HEADS // SSD_GROUPS
SSD_STATE = 128
SSD_CONV = 4
SSD_CHUNK = 128
CONV_DIM = SSD_INNER + 2 * SSD_GROUPS * SSD_STATE
ATT_HEAD_DIM = 128
WINDOWS = (128, 512, 2048)
DILATIONS = (1, 4, 16)
ATT_BLOCK = 128
ATT_UNIT_ROWS = 128
ATT_MAX_ROW_STRIDE = 4
ATT_HPP = 8
ATT_HEADS = ATT_HPP * len(WINDOWS)
ATT_QKV = ATT_HEADS * ATT_HEAD_DIM
ATT_OUT = ATT_HPP * ATT_HEAD_DIM
REL_BUCKETS = 32
REL_MAX_DIST = 2048
ZX_COLS = SSD_INNER + CONV_DIM
DT_COL0 = ZX_COLS
QKV_COL0 = ZX_COLS + SSD_HEADS
QKV_LANE_SHIFT = QKV_COL0 % 128
DT_PAD = 128
MIX_OUT = SSD_INNER + ATT_OUT
POOL_WINDOWS = (2, 4, 8, 16)
POOL_CH = D_MODEL // len(POOL_WINDOWS)
POOL_MAX = 16
D_FF = 4 * D_MODEL
EPS = 1e-6
NEG = -1e30
MIB = 1024 * 1024


def _params(semantics, vmem_mib):
    return pltpu.CompilerParams(dimension_semantics=semantics, vmem_limit_bytes=vmem_mib * MIB)


def _sigmoid(x):
    return 0.5 * jnp.tanh(0.5 * x) + 0.5


def _silu(x):
    return x * _sigmoid(x)


def _softplus(x):
    return jnp.maximum(x, 0.0) + jnp.log1p(jnp.exp(-jnp.abs(x)))


def _rms(x):
    return x * lax.rsqrt(jnp.mean(x * x, axis=-1, keepdims=True) + EPS)


def _normmod(x, g, sc, sh):
    return (_rms(x) * g) * (1.0 + sc) + sh


def _normmod_rows(u_ref, x_ref, g_ref, sc_ref, sh_ref):
    tm = u_ref.shape[0]
    rc = min(tm, 256)
    per_row = sc_ref.shape[1] != 1

    def body(i, carry):
        rows = pl.ds(pl.multiple_of(i * rc, rc), rc)
        sc = sc_ref[0, rows, :] if per_row else sc_ref[0]
        sh = sh_ref[0, rows, :] if per_row else sh_ref[0]
        u_ref[rows, :] = _normmod(x_ref[0, rows, :], g_ref[...], sc, sh).astype(BF16)
        return carry

    lax.fori_loop(0, tm // rc, body, 0)


def _bdot(a, b):
    return jnp.dot(a.astype(BF16), b.astype(BF16), preferred_element_type=F32)


def _split3(v):
    hi = v.astype(BF16)
    r = v - hi.astype(F32)
    mid = r.astype(BF16)
    lo = (r - mid.astype(F32)).astype(BF16)
    return hi, mid, lo


def _select_right(v, sel, pieces=3):
    parts = _split3(v)[:pieces]
    out = jnp.dot(parts[0], sel, preferred_element_type=F32)
    for p in parts[1:]:
        out = out + jnp.dot(p, sel, preferred_element_type=F32)
    return out


def _select_left(sel, v):
    hi, mid, lo = _split3(v)
    d = lambda a: jnp.dot(sel, a, preferred_element_type=F32)
    return d(hi) + d(mid) + d(lo)


def _ada_kernel(c_ref, w_ref, b_ref, o_ref):
    o_ref[0] = _bdot(_silu(c_ref[...]), w_ref[0]) + b_ref[0]


def _ada(c_all, ada_w, ada_b):
    depth, _, n = ada_w.shape
    tn = 1024
    return pl.pallas_call(
        _ada_kernel,
        name="ada_mod",
        grid=(depth, n // tn),
        in_specs=[
            pl.BlockSpec((SAMPLE_ROWS, D_MODEL), lambda l, j: (0, 0)),
            pl.BlockSpec((1, D_MODEL, tn), lambda l, j: (l, 0, j)),
            pl.BlockSpec((1, 1, tn), lambda l, j: (l, 0, j)),
        ],
        out_specs=pl.BlockSpec((1, SAMPLE_ROWS, tn), lambda l, j: (l, 0, j)),
        out_shape=jax.ShapeDtypeStruct((depth, SAMPLE_ROWS, n), F32),
        compiler_params=_params(("arbitrary", "arbitrary"), 40),
    )(c_all, ada_w, ada_b.reshape(depth, 1, n))


def _bdot_t(a, bt):
    return lax.dot_general(a.astype(BF16), bt.astype(BF16), (((1,), (1,)), ((), ())), preferred_element_type=F32)


def _norm_kernel(x_ref, g_ref, sc_ref, sh_ref, o_ref):
    o_ref[0] = _normmod(x_ref[0], g_ref[...], sc_ref[0], sh_ref[0]).astype(BF16)


def _norm_mod(x, g, sc, sh, tt):
    nb, t, d = x.shape
    r = sc.shape[1]
    mod_spec = pl.BlockSpec((1, r if r == 1 else tt, d), lambda b, i: (b, 0 if r == 1 else i, 0))
    return pl.pallas_call(
        _norm_kernel,
        name="norm_mod",
        grid=(nb, t // tt),
        in_specs=[pl.BlockSpec((1, tt, d), lambda b, i: (b, i, 0)), pl.BlockSpec((1, d), lambda b, i: (0, 0)),
                  mod_spec, mod_spec],
        out_specs=pl.BlockSpec((1, tt, d), lambda b, i: (b, i, 0)),
        out_shape=jax.ShapeDtypeStruct((nb, t, d), BF16),
        compiler_params=_params(("arbitrary", "arbitrary"), 40),
    )(x, g, sc, sh)


def _inproj_kernel(*refs, mode):
    if mode == "dt":
        u_ref, w_ref, wdt_ref, dtb_ref, o_ref, dt_ref = refs
    else:
        u_ref, w_ref, qg_ref, kg_ref, o_ref = refs
    j = pl.program_id(2)

    if mode == "dt":
        @pl.when(j == 0)
        def _():
            dt_ref[0] = _softplus(_bdot_t(u_ref[0], wdt_ref[...]) + dtb_ref[...])

    acc = _bdot_t(u_ref[0], w_ref[...])
    if mode == "dt":
        o_ref[0] = acc
    else:
        heads = o_ref.shape[1]
        q_tiles = ATT_HEADS // heads

        @pl.when(j < 2 * q_tiles)
        def _():
            gain = jnp.where(j < q_tiles, qg_ref[...], kg_ref[...])
            for hc in range(heads):
                o_ref[0, hc] = _rms(acc[:, hc * ATT_HEAD_DIM:(hc + 1) * ATT_HEAD_DIM]) * gain

        @pl.when(j >= 2 * q_tiles)
        def _():
            for hc in range(heads):
                o_ref[0, hc] = acc[:, hc * ATT_HEAD_DIM:(hc + 1) * ATT_HEAD_DIM]


def _inproj(u, wt, row0, n_cols, tm, tn, dt_b=None, qk_gains=None):
    nb, t, d = u.shape
    has_dt = dt_b is not None
    if row0 % tn == 0:
        w_spec = pl.BlockSpec((tn, d), lambda b, i, j: (row0 // tn + j, 0))
    else:
        assert row0 % 8 == 0
        w_spec = pl.BlockSpec((pl.Element(tn), pl.Element(d)),
                              lambda b, i, j: (pl.multiple_of(row0 + j * tn, 8), 0))
    in_specs = [pl.BlockSpec((1, tm, d), lambda b, i, j: (b, i, 0), pipeline_mode=pl.Buffered(1)), w_spec]
    args = [u, wt]
    if has_dt:
        in_specs += [pl.BlockSpec((DT_PAD, d), lambda b, i, j: (DT_COL0 // DT_PAD, 0)),
                     pl.BlockSpec((1, DT_PAD), lambda b, i, j: (0, 0))]
        args += [wt, dt_b]
        out_specs = [pl.BlockSpec((1, tm, tn), lambda b, i, j: (b, i, j)),
                     pl.BlockSpec((1, tm, DT_PAD), lambda b, i, j: (b, i, 0))]
        out_shape = [jax.ShapeDtypeStruct((nb, t, n_cols), F32), jax.ShapeDtypeStruct((nb, t, DT_PAD), F32)]
    else:
        hd = ATT_HEAD_DIM
        in_specs += [pl.BlockSpec((1, hd), lambda b, i, j: (0, 0))] * 2
        args += list(qk_gains)
        out_specs = [pl.BlockSpec((1, tn // hd, tm, hd), lambda b, i, j: (b, j, i, 0))]
        out_shape = [jax.ShapeDtypeStruct((nb, n_cols // hd, t, hd), F32)]
    return pl.pallas_call(
        functools.partial(_inproj_kernel, mode="dt" if has_dt else "qkv"),
        name="in_proj",
        grid=(nb, t // tm, n_cols // tn),
        in_specs=in_specs,
        out_specs=out_specs,
        out_shape=out_shape,
        compiler_params=_params(("arbitrary", "arbitrary", "arbitrary"), 48),
    )(*args)


def _ssd_kernel(z_ref, xraw_ref, bcraw_ref, dt_ref, cw_ref, cb_ref, alog_ref, dskip_ref, g_ref,
                y_ref, hout_ref, h_ref, yacc_ref, act_ref, carry_ref):
    q = SSD_CHUNK
    gw = SSD_HEADS_PER_GROUP * SSD_HEAD_DIM
    slab = 512
    chunk = pl.program_id(1)

    @pl.when(chunk == 0)
    def _():
        h_ref[...] = jnp.zeros(h_ref.shape, F32)
        carry_ref[...] = jnp.zeros(carry_ref.shape, F32)

    row8 = lax.broadcasted_iota(jnp.int32, (8, slab), 0)
    for s in range(CONV_DIM // slab):
        cs = slice(s * slab, (s + 1) * slab)
        src, off = (xraw_ref, s * slab) if s * slab < SSD_INNER else (bcraw_ref, s * slab - SSD_INNER)
        x = src[0, :, off:off + slab]
        w = cw_ref[:, cs]
        bias = cb_ref[:, cs]
        taps = [w[SSD_CONV - 1 - k:SSD_CONV - k] for k in range(SSD_CONV)]
        acc = bias + taps[0] * x
        for k in range(1, SSD_CONV):
            acc = acc + taps[k] * pltpu.roll(x, k, axis=0)
        act_ref[:, cs] = _silu(acc)
        x8 = x[0:8]
        carry = carry_ref[:, cs]
        acc8 = bias + taps[0] * x8
        for k in range(1, SSD_CONV):
            acc8 = acc8 + taps[k] * jnp.where(row8 < k, pltpu.roll(carry, k, axis=0), pltpu.roll(x8, k, axis=0))
        act_ref[0:8, cs] = _silu(acc8)
        carry_ref[:, cs] = x[q - 8:q]

    row = lax.broadcasted_iota(jnp.int32, (q, q), 0)
    col = lax.broadcasted_iota(jnp.int32, (q, q), 1)
    causal = row >= col
    tri = jnp.where(causal, 1.0, 0.0).astype(BF16)

    dt = dt_ref[0]
    a = dt * (-jnp.exp(alog_ref[...]))
    acum = _select_left(tri, a)
    acum_t = acum.T
    dt_t = dt.T
    last = acum[q - 1:q, :]
    decay_end = jnp.exp(last - acum) * dt
    eacum = jnp.exp(acum)
    chunk_decay = jnp.broadcast_to(jnp.exp(last), (8, DT_PAD))

    head_in_group = lax.broadcasted_iota(jnp.int32, (q, gw), 1) // SSD_HEAD_DIM
    lane_head = lax.broadcasted_iota(jnp.int32, (DT_PAD, gw), 1) // SSD_HEAD_DIM
    row_head = lax.broadcasted_iota(jnp.int32, (DT_PAD, gw), 0)
    b0 = SSD_INNER
    c0 = SSD_INNER + SSD_GROUPS * SSD_STATE
    ssq = jnp.zeros((q, 1), F32)

    for g in range(SSD_GROUPS):
        cs = slice(g * gw, (g + 1) * gw)
        expand = jnp.where(lane_head + g * SSD_HEADS_PER_GROUP == row_head, 1.0, 0.0).astype(BF16)
        decay_end_x = _select_right(decay_end, expand, pieces=2)
        eacum_x = _select_right(eacum, expand, pieces=2)
        chunk_decay_x = _select_right(chunk_decay, expand)[0:1]

        xs = act_ref[:, cs]
        bg_t = act_ref[:, b0 + g * SSD_STATE:b0 + (g + 1) * SSD_STATE].T.astype(BF16)
        cg = act_ref[:, c0 + g * SSD_STATE:c0 + (g + 1) * SSD_STATE].astype(BF16)
        cb = jnp.dot(cg, bg_t, preferred_element_type=F32)
        h_prev = h_ref[:, cs]
        y_g = jnp.dot(cg, h_prev.astype(BF16), preferred_element_type=F32) * eacum_x
        h_ref[:, cs] = h_prev * chunk_decay_x + jnp.dot(bg_t, (xs * decay_end_x).astype(BF16),
                                                        preferred_element_type=F32)
        xg = xs.astype(BF16)
        for e in range(SSD_HEADS_PER_GROUP):
            hd = g * SSD_HEADS_PER_GROUP + e
            seg = acum[:, hd:hd + 1] - acum_t[hd:hd + 1, :]
            lw = jnp.where(causal, jnp.exp(jnp.where(causal, seg, 0.0)), 0.0) * dt_t[hd:hd + 1, :]
            yd = jnp.dot((cb * lw).astype(BF16), xg, preferred_element_type=F32)
            y_g = y_g + jnp.where(head_in_group == e, yd, 0.0)
        y_g = (y_g + dskip_ref[:, cs] * xs) * _silu(z_ref[0, :, cs])
        yacc_ref[:, cs] = y_g
        ssq = ssq + jnp.sum(y_g * y_g, axis=-1, keepdims=True)

    inv_rms = lax.rsqrt(ssq * (1.0 / SSD_INNER) + EPS)
    y_ref[0] = (yacc_ref[...] * inv_rms * g_ref[...]).astype(BF16)

    @pl.when(chunk == pl.num_programs(1) - 1)
    def _():
        hout_ref[0] = h_ref[...].T


def _ssd_prompt(zx, dt, conv_w, conv_b, a_log_pad, dskip_x, norm_g):
    nb, t, _ = zx.shape
    q = SSD_CHUNK
    wide = SSD_INNER
    assert ZX_COLS == 3 * wide
    return pl.pallas_call(
        _ssd_kernel,
        name="ssd_chunk",
        grid=(nb, t // q),
        in_specs=[
            pl.BlockSpec((1, q, wide), lambda b, c: (b, c, 0)),
            pl.BlockSpec((1, q, wide), lambda b, c: (b, c, 1)),
            pl.BlockSpec((1, q, wide), lambda b, c: (b, c, 2)),
            pl.BlockSpec((1, q, DT_PAD), lambda b, c: (b, c, 0)),
            pl.BlockSpec((SSD_CONV, CONV_DIM), lambda b, c: (0, 0)),
            pl.BlockSpec((1, CONV_DIM), lambda b, c: (0, 0)),
            pl.BlockSpec((1, DT_PAD), lambda b, c: (0, 0)),
            pl.BlockSpec((1, SSD_INNER), lambda b, c: (0, 0)),
            pl.BlockSpec((1, SSD_INNER), lambda b, c: (0, 0)),
        ],
        out_specs=[
            pl.BlockSpec((1, q, SSD_INNER), lambda b, c: (b, c, 0)),
            pl.BlockSpec((1, SSD_INNER, SSD_STATE), lambda b, c: (b, 0, 0)),
        ],
        out_shape=[
            jax.ShapeDtypeStruct((nb, t, SSD_INNER), BF16),
            jax.ShapeDtypeStruct((nb, SSD_INNER, SSD_STATE), F32),
        ],
        scratch_shapes=[pltpu.VMEM((SSD_STATE, SSD_INNER), F32), pltpu.VMEM((q, SSD_INNER), F32),
                        pltpu.VMEM((q, CONV_DIM), F32), pltpu.VMEM((8, CONV_DIM), F32)],
        compiler_params=_params(("arbitrary", "arbitrary"), 40),
    )(zx, zx, zx, dt, conv_w, conv_b, a_log_pad, dskip_x, norm_g)


def _attn_kernel(q0, q1, q2, k0, k1, k2, v0, v1, v2, bias0, bias1, bias2,
                 att_ref, kv0_ref, kv1_ref, kv2_ref, qb_s, kb_s, vb_s, ok_s, os_, ls, tmp_s, ltmp_s, kv_sem):
    blk = ATT_BLOCK
    ur = ATT_UNIT_ROWS
    hd = ATT_HEAD_DIM
    scale = 1.0 / math.sqrt(hd)
    batch, slot = pl.program_id(0), pl.program_id(1)
    qrefs, krefs, vrefs, brefs = (q0, q1, q2), (k0, k1, k2), (v0, v1, v2), (bias0, bias1, bias2)
    kvrefs = (kv0_ref, kv1_ref, kv2_ref)
    nt = (((1,), (1,)), ((), ()))

    def cache_copies():
        copies = []
        for g, w in enumerate(WINDOWS):
            for part, src in enumerate((krefs[g], vrefs[g])):
                cols = pl.ds(pl.multiple_of((part * ATT_HPP + slot) * hd, hd), hd)
                copies.append(pltpu.make_async_copy(
                    src.at[0, 0, pl.ds(SEQ - w, w), :], kvrefs[g].at[batch, :, cols], kv_sem.at[2 * g + part]))
        return copies

    for cp in cache_copies():
        cp.start()

    def compute():
        row = lax.broadcasted_iota(jnp.int32, (blk, 2 * blk), 0)
        col = lax.broadcasted_iota(jnp.int32, (blk, 2 * blk), 1)
        ok_s[...] = jnp.where(jnp.where(col < blk, col - row, row - (col - blk)) >= 0, 1.0, 0.0)

        f = ATT_MAX_ROW_STRIDE
        seg = SEQ // f

        def strided(start, size, stride):
            return pl.ds(start, size) if stride == 1 else pl.ds(start, size, stride=stride)

        def regroup(src_ref, dil, emit):
            if dil <= f:
                sub = SEQ // dil
                ch = min(sub, 256)
                for r in range(dil):
                    for c in range(sub // ch):
                        emit(pl.ds(r * sub + c * ch, ch), src_ref[0, 0, strided(r + c * ch * dil, ch, dil), :])
            else:
                assert dil == f * f
                for r1 in range(f):
                    for c in range(seg // 256):
                        tmp_s[pl.ds(r1 * seg + c * 256, 256), :] = src_ref[0, 0, strided(r1 + c * 256 * f, 256, f), :]
                sub = seg // f
                for r1 in range(f):
                    for r2 in range(f):
                        emit(pl.ds((f * r2 + r1) * sub, sub), tmp_s[strided(r1 * seg + r2, sub, f), :])

        for g, dil in enumerate(DILATIONS):
            vb_s[g, :, hd:2 * hd] = jnp.ones((SEQ, hd), BF16)

            def put_q(rows, val, g=g):
                qb_s[g, rows, :] = val.astype(BF16)

            def put_k(rows, val, g=g):
                kb_s[g, rows, :] = val.astype(BF16)

            def put_v(rows, val, g=g):
                vb_s[g, rows, 0:hd] = val.astype(BF16)

            regroup(qrefs[g], dil, put_q)
            regroup(krefs[g], dil, put_k)
            regroup(vrefs[g], dil, put_v)

        for g, dil in enumerate(DILATIONS):
            nblk = SEQ // (blk * dil)
            for r in range(dil):
                for n in range(nblk):
                    base = (r * nblk + n) * blk
                    key0, col0 = (base - blk, 0) if n > 0 else (base, blk)
                    kcat = kb_s[g, key0:base + blk, :]
                    vcat = vb_s[g, key0:base + blk, :]
                    for half in range(blk // ur):
                        rs = slice(half * ur, (half + 1) * ur)
                        qb = qb_s[g, base + half * ur:base + (half + 1) * ur, :]
                        l = lax.dot_general(qb, kcat, nt, preferred_element_type=F32)
                        l = jnp.where(ok_s[rs, col0:2 * blk] > 0.5, l * scale + brefs[g][0, rs, col0:2 * blk], NEG)
                        m = jnp.max(l, axis=-1, keepdims=True)
                        p = jnp.exp(l - m).astype(BF16)
                        oe = jnp.dot(p, vcat, preferred_element_type=F32)
                        den = oe[:, hd:2 * hd]
                        s0 = n * blk + half * ur
                        if dil <= f:
                            dst = strided(r + s0 * dil, ur, dil)
                            os_[g, dst, :] = oe[:, 0:hd] / den
                            ls[g, dst, :] = m + jnp.log(den)
                        else:
                            dst = strided((r % f) * seg + f * s0 + r // f, ur, f)
                            tmp_s[dst, :] = oe[:, 0:hd] / den
                            ltmp_s[dst, :] = m + jnp.log(den)
            if dil > f:
                for r1 in range(f):
                    for c in range(seg // 256):
                        dst = strided(r1 + c * 256 * f, 256, f)
                        os_[g, dst, :] = tmp_s[pl.ds(r1 * seg + c * 256, 256), :]
                        ls[g, dst, :] = ltmp_s[pl.ds(r1 * seg + c * 256, 256), :]

        def mix_body(i, carry):
            rows = pl.ds(pl.multiple_of(i * blk, blk), blk)
            l0, l1, l2 = ls[0, rows, :], ls[1, rows, :], ls[2, rows, :]
            mx = jnp.maximum(jnp.maximum(l0, l1), l2)
            e0, e1, e2 = jnp.exp(l0 - mx), jnp.exp(l1 - mx), jnp.exp(l2 - mx)
            num = e0 * os_[0, rows, :] + e1 * os_[1, rows, :] + e2 * os_[2, rows, :]
            att_ref[0, rows, :] = (num / (e0 + e1 + e2)).astype(BF16)
            return carry
        lax.fori_loop(0, SEQ // blk, mix_body, 0)

    compute()
    for cp in cache_copies():
        cp.wait()


def _attn_prompt(qkv, bias):
    nb = qkv.shape[0]
    hd = ATT_HEAD_DIM

    def head_spec(part, g):
        return pl.BlockSpec((1, 1, SEQ, hd), lambda b, h: (b, part * ATT_HEADS + g * ATT_HPP + h, 0, 0))

    in_specs = [head_spec(part, g) for part in range(3) for g in range(3)]
    in_specs += [pl.BlockSpec((1, ATT_BLOCK, 2 * ATT_BLOCK), functools.partial(lambda b, h, g: (g * ATT_HPP + h, 0, 0), g=g))
                 for g in range(3)]
    out_specs = [pl.BlockSpec((1, SEQ, hd), lambda b, h: (b, 0, h))]
    out_shape = [jax.ShapeDtypeStruct((nb, SEQ, ATT_OUT), BF16)]
    for w in WINDOWS:
        out_specs.append(pl.BlockSpec(memory_space=pl.ANY))
        out_shape.append(jax.ShapeDtypeStruct((nb, w, 2 * ATT_OUT), F32))
    return pl.pallas_call(
        _attn_kernel,
        name="attn_prompt",
        grid=(nb, ATT_HPP),
        in_specs=in_specs,
        out_specs=out_specs,
        out_shape=out_shape,
        scratch_shapes=[
            pltpu.VMEM((3, SEQ, hd), BF16), pltpu.VMEM((3, SEQ, hd), BF16), pltpu.VMEM((3, SEQ, 2 * hd), BF16),
            pltpu.VMEM((ATT_BLOCK, 2 * ATT_BLOCK), F32),
            pltpu.VMEM((3, SEQ, hd), F32), pltpu.VMEM((3, SEQ, hd), F32),
            pltpu.VMEM((SEQ, hd), F32), pltpu.VMEM((SEQ, hd), F32),
            pltpu.SemaphoreType.DMA((2 * len(WINDOWS),)),
        ],
        compiler_params=_params(("arbitrary", "arbitrary"), 48),
    )(*([qkv] * 9), bias, bias, bias)


def _outproj_kernel(y_ref, a_ref, wy_ref, wa_ref, x_ref, g1_ref, o_ref):
    mix = _bdot(y_ref[0], wy_ref[0]) + _bdot(a_ref[0], wa_ref[0])
    o_ref[0] = x_ref[0] + g1_ref[0] * mix


def _outproj(y, att, w, x, g1, tm, tn):
    nb, t, d = x.shape
    r = g1.shape[1]
    return pl.pallas_call(
        _outproj_kernel,
        name="out_proj",
        grid=(nb, t // tm, d // tn),
        in_specs=[
            pl.BlockSpec((1, tm, SSD_INNER), lambda b, i, j: (b, i, 0), pipeline_mode=pl.Buffered(1)),
            pl.BlockSpec((1, tm, ATT_OUT), lambda b, i, j: (b, i, 0), pipeline_mode=pl.Buffered(1)),
            pl.BlockSpec((1, SSD_INNER, tn), lambda b, i, j: (0, 0, j)),
            pl.BlockSpec((1, ATT_OUT, tn), lambda b, i, j: (0, SSD_INNER // ATT_OUT, j)),
            pl.BlockSpec((1, tm, tn), lambda b, i, j: (b, i, j)),
            pl.BlockSpec((1, r if r == 1 else tm, tn), lambda b, i, j: (b, 0 if r == 1 else i, j)),
        ],
        out_specs=pl.BlockSpec((1, tm, tn), lambda b, i, j: (b, i, j)),
        out_shape=jax.ShapeDtypeStruct((nb, t, d), F32),
        compiler_params=_params(("arbitrary", "arbitrary", "arbitrary"), 48),
    )(y, att, w, w, x, g1)


def _mlp_kernel(x_ref, g_ref, sc_ref, sh_ref, g2_ref, w1_ref, w2_ref, o_ref, u_ref):
    f = pl.program_id(2)

    @pl.when(f == 0)
    def _():
        _normmod_rows(u_ref, x_ref, g_ref, sc_ref, sh_ref)
        o_ref[0] = jnp.zeros(o_ref.shape[1:], F32)

    h = jnp.maximum(_bdot(u_ref[...], w1_ref[0]), 0.0)
    o_ref[0] += _bdot(h * h, w2_ref[0])

    @pl.when(f == pl.num_programs(2) - 1)
    def _():
        o_ref[0] = x_ref[0] + g2_ref[0] * o_ref[0]


def _mlp(x, g, sc, sh, g2, w1, w2, layer, tm, tf):
    nb, t, d = x.shape
    r = sc.shape[1]
    mod_spec = pl.BlockSpec((1, r if r == 1 else tm, d), lambda b, i, f: (b, 0 if r == 1 else i, 0))
    return pl.pallas_call(
        _mlp_kernel,
        name="mlp",
        grid=(nb, t // tm, D_FF // tf),
        in_specs=[
            pl.BlockSpec((1, tm, d), lambda b, i, f: (b, i, 0), pipeline_mode=pl.Buffered(1)),
            pl.BlockSpec((1, d), lambda b, i, f: (0, 0)),
            mod_spec, mod_spec, mod_spec,
            pl.BlockSpec((1, d, tf), lambda b, i, f: (layer, 0, f)),
            pl.BlockSpec((1, tf, d), lambda b, i, f: (layer, f, 0)),
        ],
        out_specs=pl.BlockSpec((1, tm, d), lambda b, i, f: (b, i, 0), pipeline_mode=pl.Buffered(1)),
        out_shape=jax.ShapeDtypeStruct((nb, t, d), F32),
        scratch_shapes=[pltpu.VMEM((tm, d), BF16)],
        compiler_params=_params(("arbitrary", "arbitrary", "arbitrary"), 56),
    )(x, g, sc, sh, g2, w1, w2)


def _pool_kernel(x_ref, g_ref, sc_ref, sh_ref, g1_ref, w_ref, scale_ref, o_ref, ulast_ref, ext_ref, *, tt):
    halo = POOL_MAX
    ti = pl.program_id(1)

    @pl.when(ti == 0)
    def _():
        ext_ref[0:halo, :] = jnp.zeros((halo, D_MODEL), F32)

    x = x_ref[0]
    u = _normmod(x, g_ref[...], sc_ref[0], sh_ref[0])
    ext_ref[halo:halo + tt, :] = u
    pos = ti * tt + lax.broadcasted_iota(jnp.int32, (tt, 1), 0)
    for gi, w in enumerate(POOL_WINDOWS):
        ch = slice(gi * POOL_CH, (gi + 1) * POOL_CH)
        acc = ext_ref[:, ch]
        span = 1
        while span < w:
            acc = acc + pltpu.roll(acc, span, axis=0)
            span *= 2
        s = acc[halo:halo + tt]
        inv_cnt = 1.0 / jnp.minimum(w, pos + 1).astype(F32)
        pooled = s * inv_cnt - u[:, ch]
        mix = _bdot(pooled, w_ref[gi]) * scale_ref[:, ch]
        o_ref[0, :, ch] = x[:, ch] + g1_ref[0, :, ch] * mix
    ulast_ref[0] = u[tt - halo:tt]
    ext_ref[0:halo, :] = u[tt - halo:tt]


def _pool_prompt(x, g, sc, sh, g1, pool_w, pool_scale):
    nb, t, d = x.shape
    tt = 512
    mod_spec = pl.BlockSpec((1, 1, d), lambda b, i: (b, 0, 0))
    return pl.pallas_call(
        functools.partial(_pool_kernel, tt=tt),
        name="pool_mix",
        grid=(nb, t // tt),
        in_specs=[
            pl.BlockSpec((1, tt, d), lambda b, i: (b, i, 0)),
            pl.BlockSpec((1, d), lambda b, i: (0, 0)),
            mod_spec, mod_spec, mod_spec,
            pl.BlockSpec((len(POOL_WINDOWS), POOL_CH, POOL_CH), lambda b, i: (0, 0, 0)),
            pl.BlockSpec((1, d), lambda b, i: (0, 0)),
        ],
        out_specs=[
            pl.BlockSpec((1, tt, d), lambda b, i: (b, i, 0)),
            pl.BlockSpec((1, POOL_MAX, d), lambda b, i: (b, 0, 0)),
        ],
        out_shape=[jax.ShapeDtypeStruct((nb, t, d), F32), jax.ShapeDtypeStruct((nb, POOL_MAX, d), F32)],
        scratch_shapes=[pltpu.VMEM((tt + POOL_MAX, d), F32)],
        compiler_params=_params(("arbitrary", "arbitrary"), 48),
    )(x, g, sc, sh, g1, pool_w, pool_scale)


def _sconv_kernel(zx_ref, prev_ref, w_ref, b_ref, o_ref):
    w = w_ref[...]
    acc = b_ref[...] + w[SSD_CONV - 1:SSD_CONV] * zx_ref[:, SSD_INNER:ZX_COLS]
    for k in range(SSD_CONV - 1):
        acc = acc + w[k:k + 1] * prev_ref[k]
    o_ref[...] = _silu(acc)


def _conv_sample(zx, prev_t, conv_w, conv_b):
    return pl.pallas_call(
        _sconv_kernel,
        out_shape=jax.ShapeDtypeStruct((SAMPLE_ROWS, CONV_DIM), F32),
        compiler_params=pltpu.CompilerParams(vmem_limit_bytes=32 * MIB),
    )(zx, prev_t, conv_w, conv_b)


def _sstep_kernel(h0_ref, x_ref, dt_ref, b_ref, c_ref, alog_ref, h_ref, y_ref):
    dt = dt_ref[0]
    h = jnp.exp(dt * (-jnp.exp(alog_ref[...]))) * h0_ref[0] + (dt * x_ref[0]) * b_ref[0]
    h_ref[0] = h
    y_ref[0] = jnp.sum(h * c_ref[0], axis=-1, keepdims=True)


def _ssd_step(h0, x, dt, bh, ch, a_log):
    nb = h0.shape[0]
    hp = (SSD_HEADS, SSD_HEAD_DIM)
    return pl.pallas_call(
        _sstep_kernel,
        grid=(nb,),
        in_specs=[
            pl.BlockSpec((1, *hp, SSD_STATE), lambda b: (b, 0, 0, 0)),
            pl.BlockSpec((1, *hp, 1), lambda b: (b, 0, 0, 0)),
            pl.BlockSpec((1, SSD_HEADS, 1, 1), lambda b: (b, 0, 0, 0)),
            pl.BlockSpec((1, SSD_HEADS, 1, SSD_STATE), lambda b: (b, 0, 0, 0)),
            pl.BlockSpec((1, SSD_HEADS, 1, SSD_STATE), lambda b: (b, 0, 0, 0)),
            pl.BlockSpec((SSD_HEADS, 1, 1), lambda b: (0, 0, 0)),
        ],
        out_specs=[
            pl.BlockSpec((1, *hp, SSD_STATE), lambda b: (b, 0, 0, 0)),
            pl.BlockSpec((1, *hp, 1), lambda b: (b, 0, 0, 0)),
        ],
        out_shape=[jax.ShapeDtypeStruct((nb, *hp, SSD_STATE), F32), jax.ShapeDtypeStruct((nb, *hp, 1), F32)],
        compiler_params=_params(("arbitrary",), 32),
    )(h0, x, dt, bh, ch, a_log)


def _sgate_kernel(y_ref, xs_ref, z_ref, dskip_ref, g_ref, o_ref):
    y = (y_ref[...] + dskip_ref[...] * xs_ref[...]) * _silu(z_ref[...])
    o_ref[...] = _rms(y) * g_ref[...]


def _gate_sample(y, xs, z, dskip_x, norm_g):
    return pl.pallas_call(
        _sgate_kernel,
        out_shape=jax.ShapeDtypeStruct(y.shape, F32),
    )(y, xs, z, dskip_x, norm_g)


def _sattn_kernel(qkv_ref, c0_ref, c1_ref, c2_ref, bbuf_ref, bnew_ref, att_ref, knew_ref, vnew_ref):
    scale = 1.0 / math.sqrt(ATT_HEAD_DIM)
    caches = (c0_ref, c1_ref, c2_ref)
    outs, lses = [], []
    for g in range(len(WINDOWS)):
        hs = g * ATT_HPP
        q = qkv_ref[0, hs:hs + ATT_HPP, :]
        k = qkv_ref[0, ATT_HEADS + hs:ATT_HEADS + hs + ATT_HPP, :]
        v = qkv_ref[0, 2 * ATT_HEADS + hs:2 * ATT_HEADS + hs + ATT_HPP, :]
        knew_ref[0, g] = k
        vnew_ref[0, g] = v
        kbuf = caches[g][:, 0]
        vbuf = caches[g][:, 1]
        lb = jnp.sum(kbuf * q[None], axis=-1, keepdims=True) * scale + bbuf_ref[g]
        ln = jnp.sum(k * q, axis=-1, keepdims=True) * scale + bnew_ref[g]
        m = jnp.maximum(jnp.max(lb, axis=0), ln)
        pb = jnp.exp(lb - m[None])
        pn = jnp.exp(ln - m)
        s = jnp.sum(pb, axis=0) + pn
        outs.append((jnp.sum(pb * vbuf, axis=0) + pn * v) / s)
        lses.append(m + jnp.log(s))
    mx = jnp.maximum(jnp.maximum(lses[0], lses[1]), lses[2])
    es = [jnp.exp(l - mx) for l in lses]
    att_ref[0] = (es[0] * outs[0] + es[1] * outs[1] + es[2] * outs[2]) / (es[0] + es[1] + es[2])


def _attn_sample(qkv, caches, bias_buf, bias_new):
    nb = qkv.shape[0]
    hd = ATT_HEAD_DIM
    in_specs = [pl.BlockSpec((1, 3 * ATT_HEADS, hd), lambda b: (b, 0, 0))]
    for _ in caches:
        in_specs.append(pl.BlockSpec((None, ATT_BLOCK, None, 2, ATT_HPP, hd), lambda b: (b, 0, 0, 0, 0, 0)))
    in_specs += [
        pl.BlockSpec((3, ATT_BLOCK, ATT_HPP, 1), lambda b: (0, 0, 0, 0)),
        pl.BlockSpec((3, ATT_HPP, 1), lambda b: (0, 0, 0)),
    ]
    return pl.pallas_call(
        _sattn_kernel,
        grid=(nb,),
        in_specs=in_specs,
        out_specs=[
            pl.BlockSpec((1, ATT_HPP, hd), lambda b: (b, 0, 0)),
            pl.BlockSpec((1, 3, ATT_HPP, hd), lambda b: (b, 0, 0, 0)),
            pl.BlockSpec((1, 3, ATT_HPP, hd), lambda b: (b, 0, 0, 0)),
        ],
        out_shape=[
            jax.ShapeDtypeStruct((nb, ATT_HPP, hd), F32),
            jax.ShapeDtypeStruct((nb, 3, ATT_HPP, hd), F32),
            jax.ShapeDtypeStruct((nb, 3, ATT_HPP, hd), F32),
        ],
        compiler_params=_params(("arbitrary",), 32),
    )(qkv, *caches, bias_buf, bias_new)


def _kvshift_kernel(x_ref, new_ref, o_ref, carry_ref):
    tw = x_ref.shape[0]

    @pl.when(pl.program_id(1) == 0)
    def _():
        carry_ref[...] = new_ref[0]

    o_ref[0:tw - 1] = x_ref[1:tw]
    o_ref[tw - 1] = carry_ref[...]
    carry_ref[...] = x_ref[0]


def _kv_shift(cache, new_row):
    _, nb, w, _, hpp, hd = cache.shape
    tw = min(w, 512)
    nblk = w // tw
    row = (2, hpp, hd)
    return pl.pallas_call(
        _kvshift_kernel,
        name="kv_shift",
        grid=(nb, nblk),
        in_specs=[
            pl.BlockSpec((None, None, tw, *row), lambda b, s: (0, b, nblk - 1 - s, 0, 0, 0)),
            pl.BlockSpec((None, 1, *row), lambda b, s: (b, 0, 0, 0, 0)),
        ],
        out_specs=pl.BlockSpec((None, None, tw, *row), lambda b, s: (0, b, nblk - 1 - s, 0, 0, 0)),
        out_shape=jax.ShapeDtypeStruct(cache.shape, cache.dtype),
        scratch_shapes=[pltpu.VMEM(row, F32)],
        compiler_params=_params(("arbitrary", "arbitrary"), 40),
    )(cache, new_row)


def _spool_kernel(x_ref, g_ref, sc_ref, sh_ref, g1_ref, prev_ref, w_ref, scale_ref, o_ref, u_ref):
    x = x_ref[...]
    u = _normmod(x, g_ref[...], sc_ref[...], sh_ref[...])
    u_ref[...] = u
    for gi, w in enumerate(POOL_WINDOWS):
        ch = slice(gi * POOL_CH, (gi + 1) * POOL_CH)
        s = u[:, ch]
        for k in range(1, w):
            s = s + prev_ref[POOL_MAX - 1 - k, :, ch]
        pooled = s / float(min(w, PAST_LEN + 1)) - u[:, ch]
        mix = _bdot(pooled, w_ref[gi]) * scale_ref[:, ch]
        o_ref[:, ch] = x[:, ch] + g1_ref[:, ch] * mix


def _pool_sample(x, g, sc, sh, g1, prev_t, pool_w, pool_scale):
    return pl.pallas_call(
        _spool_kernel,
        out_shape=[jax.ShapeDtypeStruct(x.shape, F32), jax.ShapeDtypeStruct(x.shape, F32)],
        compiler_params=pltpu.CompilerParams(vmem_limit_bytes=32 * MIB),
    )(x, g, sc, sh, g1, prev_t, pool_w, pool_scale)


def _rel_bucket(dist):
    max_exact = REL_BUCKETS // 2
    n = jnp.maximum(dist, 1).astype(F32)
    large = max_exact + (jnp.log(n / max_exact) / math.log(REL_MAX_DIST / max_exact)
                         * (REL_BUCKETS - max_exact)).astype(jnp.int32)
    large = jnp.minimum(large, REL_BUCKETS - 1)
    return jnp.where(dist < max_exact, dist, large)


def _prompt_bias_table(rel_bias):
    blk = ATT_BLOCK
    j = blk + jnp.arange(blk)[:, None] - jnp.arange(2 * blk)[None, :]
    tables = []
    for gi, dil in enumerate(DILATIONS):
        bucket = _rel_bucket(jnp.clip(j, 0, blk) * dil)[None]
        rb = rel_bias[:, gi * ATT_HPP:(gi + 1) * ATT_HPP].astype(F32)
        tbl = jnp.zeros((ATT_HPP, blk, 2 * blk), F32)
        for k in range(REL_BUCKETS):
            tbl = jnp.where(bucket == k, rb[k][:, None, None], tbl)
        tables.append(tbl)
    return jnp.concatenate(tables, axis=0)


def _sample_bias_table(rel_bias):
    blk = ATT_BLOCK
    bufs, news = [], []
    for gi, dil in enumerate(DILATIONS):
        tbl = rel_bias[:, gi * ATT_HPP:(gi + 1) * ATT_HPP][_rel_bucket(dil * jnp.arange(blk + 1))]
        bufs.append(tbl[blk:0:-1])
        news.append(tbl[0])
    return jnp.stack(bufs)[..., None].astype(F32), jnp.stack(news)[..., None].astype(F32)


def _pad_rows(a, rows, axis=0):
    pad = [(0, 0)] * a.ndim
    pad[axis] = (0, rows - a.shape[axis])
    return jnp.pad(a, pad)


def _mlp_tiles(t):
    return (1024, 512) if t >= 1024 else (t, 512)


def kernel(x_prompt, x_sample, state_ssm, state_conv, cache_kv_w128, cache_kv_w512, cache_kv_w2048, state_pool, c_prompt, c_sample, rel_bias, ada_w, ada_b, norm_mix_g, norm_mlp_g, in_proj_w, conv_w, conv_b, dt_bias, a_log, d_skip, ssd_norm_g, q_norm_g, k_norm_g, out_proj_w, pool_w, pool_scale, mlp_w1, mlp_w2):
    d = D_MODEL
    c_all = _pad_rows(jnp.concatenate([c_prompt, c_sample], axis=0), SAMPLE_ROWS)
    mod = _ada(c_all, ada_w, ada_b)

    def mods(layer):
        parts = [mod[layer, :, i * d:(i + 1) * d] for i in range(6)]
        prompt = [p[:N_PROMPT, None, :] for p in parts]
        sample = [_pad_rows(p[N_PROMPT:N_PROMPT + N_SAMPLE], SAMPLE_ROWS)[None] for p in parts]
        return prompt, sample

    w_in_t = jnp.transpose(in_proj_w[0])
    b_dt = jnp.pad(dt_bias[0], (0, DT_PAD - SSD_HEADS)).reshape(1, DT_PAD)
    a_log_pad = jnp.pad(a_log[0], (0, DT_PAD - SSD_HEADS)).reshape(1, DT_PAD)
    dskip_x = jnp.repeat(d_skip[0], SSD_HEAD_DIM).reshape(1, SSD_INNER)
    ssd_g = ssd_norm_g.reshape(1, SSD_INNER)
    qg, kg = q_norm_g.reshape(1, ATT_HEAD_DIM), k_norm_g.reshape(1, ATT_HEAD_DIM)
    cw, cb = conv_w[0], conv_b.reshape(1, CONV_DIM)
    g_mix = [norm_mix_g[l].reshape(1, d) for l in range(2)]
    g_mlp = [norm_mlp_g[l].reshape(1, d) for l in range(2)]
    pscale = pool_scale.reshape(1, d)
    pool_wb = pool_w[0].astype(BF16)

    (sh1, sc1, g1, sh2, sc2, g2), (sh1s, sc1s, g1s, sh2s, sc2s, g2s) = mods(0)
    (sh1b, sc1b, g1b, sh2b, sc2b, g2b), (sh1sb, sc1sb, g1sb, sh2sb, sc2sb, g2sb) = mods(1)

    u0 = _norm_mod(x_prompt, g_mix[0], sc1, sh1, 512)
    zx, dt = _inproj(u0, w_in_t, 0, ZX_COLS, SEQ, 512, dt_b=b_dt)
    (qkv,) = _inproj(u0, w_in_t, QKV_COL0, 3 * ATT_QKV, SEQ, 512, qk_gains=(qg, kg))
    p_conv = zx[:, SEQ - (SSD_CONV - 1):, SSD_INNER:][None]
    y_ssd, h_fin = _ssd_prompt(zx, dt, cw, cb, a_log_pad, dskip_x, ssd_g)
    p_ssm = h_fin.reshape(1, N_PROMPT, SSD_HEADS, SSD_HEAD_DIM, SSD_STATE)
    att, kv128, kv512, kv2048 = _attn_prompt(qkv, _prompt_bias_table(rel_bias))
    p_kv = [kv.reshape(1, N_PROMPT, w, 2, ATT_HPP, ATT_HEAD_DIM) for kv, w in zip((kv128, kv512, kv2048), WINDOWS)]
    x1 = _outproj(y_ssd, att, out_proj_w, x_prompt, g1, SEQ, 512)
    x2 = _mlp(x1, g_mlp[0], sc2, sh2, g2, mlp_w1, mlp_w2, 0, *_mlp_tiles(SEQ))
    x3, u_last = _pool_prompt(x2, g_mix[1], sc1b, sh1b, g1b, pool_wb, pscale)
    p_pool = u_last[:, 1:][None]
    y_prompt = _mlp(x3, g_mlp[1], sc2b, sh2b, g2b, mlp_w1, mlp_w2, 1, *_mlp_tiles(SEQ))

    rows = SAMPLE_ROWS
    xs0 = _pad_rows(x_sample.reshape(N_SAMPLE, d), rows)[None]
    u0s = _norm_mod(xs0, g_mix[0], sc1s, sh1s, rows)
    zx_s, dt_s = _inproj(u0s, w_in_t, 0, ZX_COLS, rows, 512, dt_b=b_dt)
    (qkv_s,) = _inproj(u0s, w_in_t, QKV_COL0, 3 * ATT_QKV, rows, 512, qk_gains=(qg, kg))
    zx_s, dt_s = zx_s[0], dt_s[0]
    qkv_s = jnp.transpose(qkv_s[0, :, :N_SAMPLE], (1, 0, 2))
    xbc_new = zx_s[:N_SAMPLE, SSD_INNER:]
    s_conv = jnp.concatenate([state_conv[0][:, 1:], xbc_new[:, None, :]], axis=1)[None]
    conv_prev_t = _pad_rows(jnp.transpose(state_conv[0], (1, 0, 2)), rows, axis=1)
    xbc_s = _conv_sample(zx_s, conv_prev_t, cw, cb)
    xs_s = xbc_s[:N_SAMPLE, :SSD_INNER]
    e = SSD_HEADS_PER_GROUP
    b_s = jnp.repeat(xbc_s[:N_SAMPLE, SSD_INNER:SSD_INNER + SSD_GROUPS * SSD_STATE]
                     .reshape(N_SAMPLE, SSD_GROUPS, 1, SSD_STATE), e, axis=1)
    c_s = jnp.repeat(xbc_s[:N_SAMPLE, SSD_INNER + SSD_GROUPS * SSD_STATE:]
                     .reshape(N_SAMPLE, SSD_GROUPS, 1, SSD_STATE), e, axis=1)
    h_new, y_s = _ssd_step(state_ssm[0], xs_s.reshape(N_SAMPLE, SSD_HEADS, SSD_HEAD_DIM, 1),
                           dt_s[:N_SAMPLE, :SSD_HEADS].reshape(N_SAMPLE, SSD_HEADS, 1, 1), b_s, c_s,
                           a_log[0].reshape(SSD_HEADS, 1, 1))
    s_ssm = h_new[None]
    y_s = _gate_sample(_pad_rows(y_s.reshape(N_SAMPLE, SSD_INNER), rows), xbc_s[:, :SSD_INNER],
                       zx_s[:, :SSD_INNER], dskip_x, ssd_g)
    caches = (cache_kv_w128, cache_kv_w512, cache_kv_w2048)
    strided = [c.reshape(N_SAMPLE, ATT_BLOCK, dil, 2, ATT_HPP, ATT_HEAD_DIM) for c, dil in zip(caches, DILATIONS)]
    bias_buf, bias_new = _sample_bias_table(rel_bias)
    att_s, k_new, v_new = _attn_sample(qkv_s, strided, bias_buf, bias_new)
    new_rows = jnp.stack([k_new, v_new], axis=2)
    s_kv = [_kv_shift(c, new_rows[:, gi:gi + 1]) for gi, c in enumerate(caches)]
    att_s = _pad_rows(att_s.reshape(N_SAMPLE, ATT_OUT), rows)
    x1s = _outproj(y_s[None], att_s[None], out_proj_w, xs0, g1s, rows, 512)
    x2s = _mlp(x1s, g_mlp[0], sc2s, sh2s, g2s, mlp_w1, mlp_w2, 0, *_mlp_tiles(rows))
    pool_prev_t = _pad_rows(jnp.transpose(state_pool[0], (1, 0, 2)), rows, axis=1)
    x3s, u_s = _pool_sample(x2s[0], g_mix[1], sc1sb[0], sh1sb[0], g1sb[0], pool_prev_t, pool_wb, pscale)
    s_pool = jnp.concatenate([state_pool[0][:, 1:], u_s[:N_SAMPLE, None, :]], axis=1)[None]
    y_s2 = _mlp(x3s[None], g_mlp[1], sc2sb, sh2sb, g2sb, mlp_w1, mlp_w2, 1, *_mlp_tiles(rows))
    y_sample = y_s2[0, :N_SAMPLE].reshape(N_SAMPLE, 1, d)

    return (y_prompt, y_sample, p_ssm, p_conv, p_kv[0], p_kv[1], p_kv[2], p_pool,
            s_ssm, s_conv, s_kv[0], s_kv[1], s_kv[2], s_pool)
```

```python
import functools
import math

import jax
import jax.numpy as jnp
from jax import lax
from jax.experimental import pallas as pl
from jax.experimental.pallas import tpu as pltpu

F32 = jnp.float32
BF16 = jnp.bfloat16

D_MODEL = 2048
SEQ = 2048
N_PROMPT = 4
N_SAMPLE = 8
SAMPLE_ROWS = 16
PAST_LEN = 16384
SSD_HEADS = 32
SSD_HEAD_DIM = 64
SSD_INNER = SSD_HEADS * SSD_HEAD_DIM
SSD_GROUPS = 8
SSD_HEADS_PER_GROUP = SSD_HEADS // SSD_GROUPS
SSD_STATE = 128
SSD_CONV = 4
SSD_CHUNK = 128
CONV_DIM = SSD_INNER + 2 * SSD_GROUPS * SSD_STATE
ATT_HEAD_DIM = 128
WINDOWS = (128, 512, 2048)
DILATIONS = (1, 4, 16)
ATT_BLOCK = 128
ATT_UNIT_ROWS = 128
ATT_MAX_ROW_STRIDE = 4
ATT_HPP = 8
ATT_HEADS = ATT_HPP * len(WINDOWS)
ATT_QKV = ATT_HEADS * ATT_HEAD_DIM
ATT_OUT = ATT_HPP * ATT_HEAD_DIM
REL_BUCKETS = 32
REL_MAX_DIST = 2048
ZX_COLS = SSD_INNER + CONV_DIM
DT_COL0 = ZX_COLS
QKV_COL0 = ZX_COLS + SSD_HEADS
QKV_LANE_SHIFT = QKV_COL0 % 128
DT_PAD = 128
MIX_OUT = SSD_INNER + ATT_OUT
POOL_WINDOWS = (2, 4, 8, 16)
POOL_CH = D_MODEL // len(POOL_WINDOWS)
POOL_MAX = 16
D_FF = 4 * D_MODEL
EPS = 1e-6
NEG = -1e30
MIB = 1024 * 1024


def _params(semantics, vmem_mib):
    return pltpu.CompilerParams(dimension_semantics=semantics, vmem_limit_bytes=vmem_mib * MIB)


def _sigmoid(x):
    return 0.5 * jnp.tanh(0.5 * x) + 0.5


def _silu(x):
    return x * _sigmoid(x)


def _softplus(x):
    return jnp.maximum(x, 0.0) + jnp.log1p(jnp.exp(-jnp.abs(x)))


def _rms(x):
    return x * lax.rsqrt(jnp.mean(x * x, axis=-1, keepdims=True) + EPS)


def _normmod(x, g, sc, sh):
    return (_rms(x) * g) * (1.0 + sc) + sh


def _normmod_rows(u_ref, x_ref, g_ref, sc_ref, sh_ref):
    tm = u_ref.shape[0]
    rc = min(tm, 256)
    per_row = sc_ref.shape[1] != 1

    def body(i, carry):
        rows = pl.ds(pl.multiple_of(i * rc, rc), rc)
        sc = sc_ref[0, rows, :] if per_row else sc_ref[0]
        sh = sh_ref[0, rows, :] if per_row else sh_ref[0]
        u_ref[rows, :] = _normmod(x_ref[0, rows, :], g_ref[...], sc, sh).astype(BF16)
        return carry

    lax.fori_loop(0, tm // rc, body, 0)


def _bdot(a, b):
    return jnp.dot(a.astype(BF16), b.astype(BF16), preferred_element_type=F32)


def _split3(v):
    hi = v.astype(BF16)
    r = v - hi.astype(F32)
    mid = r.astype(BF16)
    lo = (r - mid.astype(F32)).astype(BF16)
    return hi, mid, lo


def _select_right(v, sel, pieces=3):
    parts = _split3(v)[:pieces]
    out = jnp.dot(parts[0], sel, preferred_element_type=F32)
    for p in parts[1:]:
        out = out + jnp.dot(p, sel, preferred_element_type=F32)
    return out


def _select_left(sel, v):
    hi, mid, lo = _split3(v)
    d = lambda a: jnp.dot(sel, a, preferred_element_type=F32)
    return d(hi) + d(mid) + d(lo)


def _ada_kernel(c_ref, w_ref, b_ref, o_ref):
    o_ref[0] = _bdot(_silu(c_ref[...]), w_ref[0]) + b_ref[0]


def _ada(c_all, ada_w, ada_b):
    depth, _, n = ada_w.shape
    tn = 1024
    return pl.pallas_call(
        _ada_kernel,
        name="ada_mod",
        grid=(depth, n // tn),
        in_specs=[
            pl.BlockSpec((SAMPLE_ROWS, D_MODEL), lambda l, j: (0, 0)),
            pl.BlockSpec((1, D_MODEL, tn), lambda l, j: (l, 0, j)),
            pl.BlockSpec((1, 1, tn), lambda l, j: (l, 0, j)),
        ],
        out_specs=pl.BlockSpec((1, SAMPLE_ROWS, tn), lambda l, j: (l, 0, j)),
        out_shape=jax.ShapeDtypeStruct((depth, SAMPLE_ROWS, n), F32),
        compiler_params=_params(("arbitrary", "arbitrary"), 40),
    )(c_all, ada_w, ada_b.reshape(depth, 1, n))


def _bdot_t(a, bt):
    return lax.dot_general(a.astype(BF16), bt.astype(BF16), (((1,), (1,)), ((), ())), preferred_element_type=F32)


def _norm_kernel(x_ref, g_ref, sc_ref, sh_ref, o_ref):
    o_ref[0] = _normmod(x_ref[0], g_ref[...], sc_ref[0], sh_ref[0]).astype(BF16)


def _norm_mod(x, g, sc, sh, tt):
    nb, t, d = x.shape
    r = sc.shape[1]
    mod_spec = pl.BlockSpec((1, r if r == 1 else tt, d), lambda b, i: (b, 0 if r == 1 else i, 0))
    return pl.pallas_call(
        _norm_kernel,
        name="norm_mod",
        grid=(nb, t // tt),
        in_specs=[pl.BlockSpec((1, tt, d), lambda b, i: (b, i, 0)), pl.BlockSpec((1, d), lambda b, i: (0, 0)),
                  mod_spec, mod_spec],
        out_specs=pl.BlockSpec((1, tt, d), lambda b, i: (b, i, 0)),
        out_shape=jax.ShapeDtypeStruct((nb, t, d), BF16),
        compiler_params=_params(("arbitrary", "arbitrary"), 40),
    )(x, g, sc, sh)


def _inproj_kernel(*refs, mode):
    if mode == "dt":
        u_ref, w_ref, wdt_ref, dtb_ref, o_ref, dt_ref = refs
    else:
        u_ref, w_ref, qg_ref, kg_ref, o_ref = refs
    j = pl.program_id(2)

    if mode == "dt":
        @pl.when(j == 0)
        def _():
            dt_ref[0] = _softplus(_bdot_t(u_ref[0], wdt_ref[...]) + dtb_ref[...])

    acc = _bdot_t(u_ref[0], w_ref[...])
    if mode == "dt":
        o_ref[0] = acc
    else:
        heads = o_ref.shape[1]
        q_tiles = ATT_HEADS // heads

        @pl.when(j < 2 * q_tiles)
        def _():
            gain = jnp.where(j < q_tiles, qg_ref[...], kg_ref[...])
            for hc in range(heads):
                o_ref[0, hc] = _rms(acc[:, hc * ATT_HEAD_DIM:(hc + 1) * ATT_HEAD_DIM]) * gain

        @pl.when(j >= 2 * q_tiles)
        def _():
            for hc in range(heads):
                o_ref[0, hc] = acc[:, hc * ATT_HEAD_DIM:(hc + 1) * ATT_HEAD_DIM]


def _inproj(u, wt, row0, n_cols, tm, tn, dt_b=None, qk_gains=None):
    nb, t, d = u.shape
    has_dt = dt_b is not None
    if row0 % tn == 0:
        w_spec = pl.BlockSpec((tn, d), lambda b, i, j: (row0 // tn + j, 0))
    else:
        assert row0 % 8 == 0
        w_spec = pl.BlockSpec((pl.Element(tn), pl.Element(d)),
                              lambda b, i, j: (pl.multiple_of(row0 + j * tn, 8), 0))
    in_specs = [pl.BlockSpec((1, tm, d), lambda b, i, j: (b, i, 0), pipeline_mode=pl.Buffered(1)), w_spec]
    args = [u, wt]
    if has_dt:
        in_specs += [pl.BlockSpec((DT_PAD, d), lambda b, i, j: (DT_COL0 // DT_PAD, 0)),
                     pl.BlockSpec((1, DT_PAD), lambda b, i, j: (0, 0))]
        args += [wt, dt_b]
        out_specs = [pl.BlockSpec((1, tm, tn), lambda b, i, j: (b, i, j)),
                     pl.BlockSpec((1, tm, DT_PAD), lambda b, i, j: (b, i, 0))]
        out_shape = [jax.ShapeDtypeStruct((nb, t, n_cols), F32), jax.ShapeDtypeStruct((nb, t, DT_PAD), F32)]
    else:
        hd = ATT_HEAD_DIM
        in_specs += [pl.BlockSpec((1, hd), lambda b, i, j: (0, 0))] * 2
        args += list(qk_gains)
        out_specs = [pl.BlockSpec((1, tn // hd, tm, hd), lambda b, i, j: (b, j, i, 0))]
        out_shape = [jax.ShapeDtypeStruct((nb, n_cols // hd, t, hd), F32)]
    return pl.pallas_call(
        functools.partial(_inproj_kernel, mode="dt" if has_dt else "qkv"),
        name="in_proj",
        grid=(nb, t // tm, n_cols // tn),
        in_specs=in_specs,
        out_specs=out_specs,
        out_shape=out_shape,
        compiler_params=_params(("arbitrary", "arbitrary", "arbitrary"), 48),
    )(*args)


def _ssd_kernel(z_ref, xraw_ref, bcraw_ref, dt_ref, cw_ref, cb_ref, alog_ref, dskip_ref, g_ref,
                y_ref, hout_ref, h_ref, yacc_ref, act_ref, carry_ref):
    q = SSD_CHUNK
    gw = SSD_HEADS_PER_GROUP * SSD_HEAD_DIM
    slab = 512
    chunk = pl.program_id(1)

    @pl.when(chunk == 0)
    def _():
        h_ref[...] = jnp.zeros(h_ref.shape, F32)
        carry_ref[...] = jnp.zeros(carry_ref.shape, F32)

    row8 = lax.broadcasted_iota(jnp.int32, (8, slab), 0)
    for s in range(CONV_DIM // slab):
        cs = slice(s * slab, (s + 1) * slab)
        src, off = (xraw_ref, s * slab) if s * slab < SSD_INNER else (bcraw_ref, s * slab - SSD_INNER)
        x = src[0, :, off:off + slab]
        w = cw_ref[:, cs]
        bias = cb_ref[:, cs]
        taps = [w[SSD_CONV - 1 - k:SSD_CONV - k] for k in range(SSD_CONV)]
        acc = bias + taps[0] * x
        for k in range(1, SSD_CONV):
            acc = acc + taps[k] * pltpu.roll(x, k, axis=0)
        act_ref[:, cs] = _silu(acc)
        x8 = x[0:8]
        carry = carry_ref[:, cs]
        acc8 = bias + taps[0] * x8
        for k in range(1, SSD_CONV):
            acc8 = acc8 + taps[k] * jnp.where(row8 < k, pltpu.roll(carry, k, axis=0), pltpu.roll(x8, k, axis=0))
        act_ref[0:8, cs] = _silu(acc8)
        carry_ref[:, cs] = x[q - 8:q]

    row = lax.broadcasted_iota(jnp.int32, (q, q), 0)
    col = lax.broadcasted_iota(jnp.int32, (q, q), 1)
    causal = row >= col
    tri = jnp.where(causal, 1.0, 0.0).astype(BF16)

    dt = dt_ref[0]
    a = dt * (-jnp.exp(alog_ref[...]))
    acum = _select_left(tri, a)
    acum_t = acum.T
    dt_t = dt.T
    last = acum[q - 1:q, :]
    decay_end = jnp.exp(last - acum) * dt
    eacum = jnp.exp(acum)
    chunk_decay = jnp.broadcast_to(jnp.exp(last), (8, DT_PAD))

    first_head_lanes = lax.broadcasted_iota(jnp.int32, (q, 2 * SSD_HEAD_DIM), 1) < SSD_HEAD_DIM
    lane_head = lax.broadcasted_iota(jnp.int32, (DT_PAD, gw), 1) // SSD_HEAD_DIM
    row_head = lax.broadcasted_iota(jnp.int32, (DT_PAD, gw), 0)
    b0 = SSD_INNER
    c0 = SSD_INNER + SSD_GROUPS * SSD_STATE
    ssq = jnp.zeros((q, 1), F32)

    for g in range(SSD_GROUPS):
        cs = slice(g * gw, (g + 1) * gw)
        expand = jnp.where(lane_head + g * SSD_HEADS_PER_GROUP == row_head, 1.0, 0.0).astype(BF16)
        decay_end_x = _select_right(decay_end, expand, pieces=2)
        eacum_x = _select_right(eacum, expand, pieces=2)
        chunk_decay_x = _select_right(chunk_decay, expand)[0:1]

        xs = act_ref[:, cs]
        bg_t = act_ref[:, b0 + g * SSD_STATE:b0 + (g + 1) * SSD_STATE].T.astype(BF16)
        cg = act_ref[:, c0 + g * SSD_STATE:c0 + (g + 1) * SSD_STATE].astype(BF16)
        cb = jnp.dot(cg, bg_t, preferred_element_type=F32)
        h_prev = h_ref[:, cs]
        y_g = jnp.dot(cg, h_prev.astype(BF16), preferred_element_type=F32) * eacum_x
        h_ref[:, cs] = h_prev * chunk_decay_x + jnp.dot(bg_t, (xs * decay_end_x).astype(BF16),
                                                        preferred_element_type=F32)
        xg = xs.astype(BF16)

        def y_diag(e):
            hd = g * SSD_HEADS_PER_GROUP + e
            seg = acum[:, hd:hd + 1] - acum_t[hd:hd + 1, :]
            lw = jnp.where(causal, jnp.exp(seg), 0.0) * dt_t[hd:hd + 1, :]
            pair_lanes = slice((e // 2) * 2 * SSD_HEAD_DIM, (e // 2 + 1) * 2 * SSD_HEAD_DIM)
            return jnp.dot((cb * lw).astype(BF16), xg[:, pair_lanes], preferred_element_type=F32)

        y_pairs = [jnp.where(first_head_lanes, y_diag(2 * p), y_diag(2 * p + 1))
                   for p in range(SSD_HEADS_PER_GROUP // 2)]
        y_g = y_g + jnp.concatenate(y_pairs, axis=1)
        y_g = (y_g + dskip_ref[:, cs] * xs) * _silu(z_ref[0, :, cs])
        yacc_ref[:, cs] = y_g
        ssq = ssq + jnp.sum(y_g * y_g, axis=-1, keepdims=True)

    inv_rms = lax.rsqrt(ssq * (1.0 / SSD_INNER) + EPS)
    y_ref[0] = (yacc_ref[...] * inv_rms * g_ref[...]).astype(BF16)

    @pl.when(chunk == pl.num_programs(1) - 1)
    def _():
        hout_ref[0] = h_ref[...].T


def _ssd_prompt(zx, dt, conv_w, conv_b, a_log_pad, dskip_x, norm_g):
    nb, t, _ = zx.shape
    q = SSD_CHUNK
    wide = SSD_INNER
    assert ZX_COLS == 3 * wide
    return pl.pallas_call(
        _ssd_kernel,
        name="ssd_chunk",
        grid=(nb, t // q),
        in_specs=[
            pl.BlockSpec((1, q, wide), lambda b, c: (b, c, 0)),
            pl.BlockSpec((1, q, wide), lambda b, c: (b, c, 1)),
            pl.BlockSpec((1, q, wide), lambda b, c: (b, c, 2)),
            pl.BlockSpec((1, q, DT_PAD), lambda b, c: (b, c, 0)),
            pl.BlockSpec((SSD_CONV, CONV_DIM), lambda b, c: (0, 0)),
            pl.BlockSpec((1, CONV_DIM), lambda b, c: (0, 0)),
            pl.BlockSpec((1, DT_PAD), lambda b, c: (0, 0)),
            pl.BlockSpec((1, SSD_INNER), lambda b, c: (0, 0)),
            pl.BlockSpec((1, SSD_INNER), lambda b, c: (0, 0)),
        ],
        out_specs=[
            pl.BlockSpec((1, q, SSD_INNER), lambda b, c: (b, c, 0)),
            pl.BlockSpec((1, SSD_INNER, SSD_STATE), lambda b, c: (b, 0, 0)),
        ],
        out_shape=[
            jax.ShapeDtypeStruct((nb, t, SSD_INNER), BF16),
            jax.ShapeDtypeStruct((nb, SSD_INNER, SSD_STATE), F32),
        ],
        scratch_shapes=[pltpu.VMEM((SSD_STATE, SSD_INNER), F32), pltpu.VMEM((q, SSD_INNER), F32),
                        pltpu.VMEM((q, CONV_DIM), F32), pltpu.VMEM((8, CONV_DIM), F32)],
        compiler_params=_params(("arbitrary", "arbitrary"), 40),
    )(zx, zx, zx, dt, conv_w, conv_b, a_log_pad, dskip_x, norm_g)


def _attn_kernel(q0, q1, q2, k0, k1, k2, v0, v1, v2, bias0, bias1, bias2,
                 att_ref, kv0_ref, kv1_ref, kv2_ref, qb_s, kb_s, vb_s, ok_s, os_, ls, tmp_s, ltmp_s, kv_sem):
    blk = ATT_BLOCK
    ur = ATT_UNIT_ROWS
    hd = ATT_HEAD_DIM
    scale = 1.0 / math.sqrt(hd)
    batch, slot = pl.program_id(0), pl.program_id(1)
    qrefs, krefs, vrefs, brefs = (q0, q1, q2), (k0, k1, k2), (v0, v1, v2), (bias0, bias1, bias2)
    kvrefs = (kv0_ref, kv1_ref, kv2_ref)
    nt = (((1,), (1,)), ((), ()))

    def cache_copies():
        copies = []
        for g, w in enumerate(WINDOWS):
            for part, src in enumerate((krefs[g], vrefs[g])):
                cols = pl.ds(pl.multiple_of((part * ATT_HPP + slot) * hd, hd), hd)
                copies.append(pltpu.make_async_copy(
                    src.at[0, 0, pl.ds(SEQ - w, w), :], kvrefs[g].at[batch, :, cols], kv_sem.at[2 * g + part]))
        return copies

    for cp in cache_copies():
        cp.start()

    def compute():
        row = lax.broadcasted_iota(jnp.int32, (blk, 2 * blk), 0)
        col = lax.broadcasted_iota(jnp.int32, (blk, 2 * blk), 1)
        ok_s[...] = jnp.where(jnp.where(col < blk, col - row, row - (col - blk)) >= 0, 1.0, 0.0)

        f = ATT_MAX_ROW_STRIDE
        seg = SEQ // f

        def strided(start, size, stride):
            return pl.ds(start, size) if stride == 1 else pl.ds(start, size, stride=stride)

        def regroup(src_ref, dil, emit):
            if dil <= f:
                sub = SEQ // dil
                ch = min(sub, 256)
                for r in range(dil):
                    for c in range(sub // ch):
                        emit(pl.ds(r * sub + c * ch, ch), src_ref[0, 0, strided(r + c * ch * dil, ch, dil), :])
            else:
                assert dil == f * f
                for r1 in range(f):
                    for c in range(seg // 256):
                        tmp_s[pl.ds(r1 * seg + c * 256, 256), :] = src_ref[0, 0, strided(r1 + c * 256 * f, 256, f), :]
                sub = seg // f
                for r1 in range(f):
                    for r2 in range(f):
                        emit(pl.ds((f * r2 + r1) * sub, sub), tmp_s[strided(r1 * seg + r2, sub, f), :])

        for g, dil in enumerate(DILATIONS):
            vb_s[g, :, hd:2 * hd] = jnp.ones((SEQ, hd), BF16)

            def put_q(rows, val, g=g):
                qb_s[g, rows, :] = val.astype(BF16)

            def put_k(rows, val, g=g):
                kb_s[g, rows, :] = val.astype(BF16)

            def put_v(rows, val, g=g):
                vb_s[g, rows, 0:hd] = val.astype(BF16)

            regroup(qrefs[g], dil, put_q)
            regroup(krefs[g], dil, put_k)
            regroup(vrefs[g], dil, put_v)

        for g, dil in enumerate(DILATIONS):
            nblk = SEQ // (blk * dil)
            for r in range(dil):
                for n in range(nblk):
                    base = (r * nblk + n) * blk
                    key0, col0 = (base - blk, 0) if n > 0 else (base, blk)
                    kcat = kb_s[g, key0:base + blk, :]
                    vcat = vb_s[g, key0:base + blk, :]
                    for half in range(blk // ur):
                        rs = slice(half * ur, (half + 1) * ur)
                        qb = qb_s[g, base + half * ur:base + (half + 1) * ur, :]
                        l = lax.dot_general(qb, kcat, nt, preferred_element_type=F32)
                        l = jnp.where(ok_s[rs, col0:2 * blk] > 0.5, l * scale + brefs[g][0, rs, col0:2 * blk], NEG)
                        m = jnp.max(l, axis=-1, keepdims=True)
                        p = jnp.exp(l - m).astype(BF16)
                        oe = jnp.dot(p, vcat, preferred_element_type=F32)
                        den = oe[:, hd:2 * hd]
                        s0 = n * blk + half * ur
                        if dil <= f:
                            dst = strided(r + s0 * dil, ur, dil)
                            os_[g, dst, :] = oe[:, 0:hd] / den
                            ls[g, dst, :] = m + jnp.log(den)
                        else:
                            dst = strided((r % f) * seg + f * s0 + r // f, ur, f)
                            tmp_s[dst, :] = oe[:, 0:hd] / den
                            ltmp_s[dst, :] = m + jnp.log(den)
            if dil > f:
                for r1 in range(f):
                    for c in range(seg // 256):
                        dst = strided(r1 + c * 256 * f, 256, f)
                        os_[g, dst, :] = tmp_s[pl.ds(r1 * seg + c * 256, 256), :]
                        ls[g, dst, :] = ltmp_s[pl.ds(r1 * seg + c * 256, 256), :]

        def mix_body(i, carry):
            rows = pl.ds(pl.multiple_of(i * blk, blk), blk)
            l0, l1, l2 = ls[0, rows, :], ls[1, rows, :], ls[2, rows, :]
            mx = jnp.maximum(jnp.maximum(l0, l1), l2)
            e0, e1, e2 = jnp.exp(l0 - mx), jnp.exp(l1 - mx), jnp.exp(l2 - mx)
            num = e0 * os_[0, rows, :] + e1 * os_[1, rows, :] + e2 * os_[2, rows, :]
            att_ref[0, rows, :] = (num / (e0 + e1 + e2)).astype(BF16)
            return carry
        lax.fori_loop(0, SEQ // blk, mix_body, 0)

    compute()
    for cp in cache_copies():
        cp.wait()


def _attn_prompt(qkv, bias):
    nb = qkv.shape[0]
    hd = ATT_HEAD_DIM

    def head_spec(part, g):
        return pl.BlockSpec((1, 1, SEQ, hd), lambda b, h: (b, part * ATT_HEADS + g * ATT_HPP + h, 0, 0))

    in_specs = [head_spec(part, g) for part in range(3) for g in range(3)]
    in_specs += [pl.BlockSpec((1, ATT_BLOCK, 2 * ATT_BLOCK), functools.partial(lambda b, h, g: (g * ATT_HPP + h, 0, 0), g=g))
                 for g in range(3)]
    out_specs = [pl.BlockSpec((1, SEQ, hd), lambda b, h: (b, 0, h))]
    out_shape = [jax.ShapeDtypeStruct((nb, SEQ, ATT_OUT), BF16)]
    for w in WINDOWS:
        out_specs.append(pl.BlockSpec(memory_space=pl.ANY))
        out_shape.append(jax.ShapeDtypeStruct((nb, w, 2 * ATT_OUT), F32))
    return pl.pallas_call(
        _attn_kernel,
        name="attn_prompt",
        grid=(nb, ATT_HPP),
        in_specs=in_specs,
        out_specs=out_specs,
        out_shape=out_shape,
        scratch_shapes=[
            pltpu.VMEM((3, SEQ, hd), BF16), pltpu.VMEM((3, SEQ, hd), BF16), pltpu.VMEM((3, SEQ, 2 * hd), BF16),
            pltpu.VMEM((ATT_BLOCK, 2 * ATT_BLOCK), F32),
            pltpu.VMEM((3, SEQ, hd), F32), pltpu.VMEM((3, SEQ, hd), F32),
            pltpu.VMEM((SEQ, hd), F32), pltpu.VMEM((SEQ, hd), F32),
            pltpu.SemaphoreType.DMA((2 * len(WINDOWS),)),
        ],
        compiler_params=_params(("arbitrary", "arbitrary"), 48),
    )(*([qkv] * 9), bias, bias, bias)


def _outproj_kernel(y_ref, a_ref, wy_ref, wa_ref, x_ref, g1_ref, o_ref):
    mix = _bdot(y_ref[0], wy_ref[0]) + _bdot(a_ref[0], wa_ref[0])
    o_ref[0] = x_ref[0] + g1_ref[0] * mix


def _outproj(y, att, w, x, g1, tm, tn):
    nb, t, d = x.shape
    r = g1.shape[1]
    return pl.pallas_call(
        _outproj_kernel,
        name="out_proj",
        grid=(nb, t // tm, d // tn),
        in_specs=[
            pl.BlockSpec((1, tm, SSD_INNER), lambda b, i, j: (b, i, 0), pipeline_mode=pl.Buffered(1)),
            pl.BlockSpec((1, tm, ATT_OUT), lambda b, i, j: (b, i, 0), pipeline_mode=pl.Buffered(1)),
            pl.BlockSpec((1, SSD_INNER, tn), lambda b, i, j: (0, 0, j)),
            pl.BlockSpec((1, ATT_OUT, tn), lambda b, i, j: (0, SSD_INNER // ATT_OUT, j)),
            pl.BlockSpec((1, tm, tn), lambda b, i, j: (b, i, j)),
            pl.BlockSpec((1, r if r == 1 else tm, tn), lambda b, i, j: (b, 0 if r == 1 else i, j)),
        ],
        out_specs=pl.BlockSpec((1, tm, tn), lambda b, i, j: (b, i, j)),
        out_shape=jax.ShapeDtypeStruct((nb, t, d), F32),
        compiler_params=_params(("arbitrary", "arbitrary", "arbitrary"), 48),
    )(y, att, w, w, x, g1)


def _mlp_kernel(x_ref, g_ref, sc_ref, sh_ref, g2_ref, w1_ref, w2_ref, o_ref, u_ref):
    f = pl.program_id(2)

    last = pl.num_programs(2) - 1

    def partial_out():
        h = jnp.maximum(_bdot(u_ref[...], w1_ref[0]), 0.0)
        return _bdot(h * h, w2_ref[0])

    @pl.when(f == 0)
    def _():
        _normmod_rows(u_ref, x_ref, g_ref, sc_ref, sh_ref)
        o_ref[0] = partial_out()

    @pl.when(jnp.logical_and(f > 0, f < last))
    def _():
        o_ref[0] += partial_out()

    @pl.when(f == last)
    def _():
        o_ref[0] = x_ref[0] + g2_ref[0] * (o_ref[0] + partial_out())


def _mlp(x, g, sc, sh, g2, w1, w2, layer, tm, tf):
    nb, t, d = x.shape
    r = sc.shape[1]
    mod_spec = pl.BlockSpec((1, r if r == 1 else tm, d), lambda b, i, f: (b, 0 if r == 1 else i, 0))
    return pl.pallas_call(
        _mlp_kernel,
        name="mlp",
        grid=(nb, t // tm, D_FF // tf),
        in_specs=[
            pl.BlockSpec((1, tm, d), lambda b, i, f: (b, i, 0), pipeline_mode=pl.Buffered(1)),
            pl.BlockSpec((1, d), lambda b, i, f: (0, 0)),
            mod_spec, mod_spec, mod_spec,
            pl.BlockSpec((1, d, tf), lambda b, i, f: (layer, 0, f)),
            pl.BlockSpec((1, tf, d), lambda b, i, f: (layer, f, 0)),
        ],
        out_specs=pl.BlockSpec((1, tm, d), lambda b, i, f: (b, i, 0), pipeline_mode=pl.Buffered(1)),
        out_shape=jax.ShapeDtypeStruct((nb, t, d), F32),
        scratch_shapes=[pltpu.VMEM((tm, d), BF16)],
        compiler_params=_params(("arbitrary", "arbitrary", "arbitrary"), 56),
    )(x, g, sc, sh, g2, w1, w2)


def _pool_kernel(x_ref, g_ref, sc_ref, sh_ref, g1_ref, w_ref, scale_ref, o_ref, ulast_ref, ext_ref, *, tt):
    halo = POOL_MAX
    ti = pl.program_id(1)

    @pl.when(ti == 0)
    def _():
        ext_ref[0:halo, :] = jnp.zeros((halo, D_MODEL), F32)

    x = x_ref[0]
    u = _normmod(x, g_ref[...], sc_ref[0], sh_ref[0])
    ext_ref[halo:halo + tt, :] = u
    pos = ti * tt + lax.broadcasted_iota(jnp.int32, (tt, 1), 0)
    for gi, w in enumerate(POOL_WINDOWS):
        ch = slice(gi * POOL_CH, (gi + 1) * POOL_CH)
        acc = ext_ref[:, ch]
        span = 1
        while span < w:
            acc = acc + pltpu.roll(acc, span, axis=0)
            span *= 2
        s = acc[halo:halo + tt]
        inv_cnt = 1.0 / jnp.minimum(w, pos + 1).astype(F32)
        pooled = s * inv_cnt - u[:, ch]
        mix = _bdot(pooled, w_ref[gi]) * scale_ref[:, ch]
        o_ref[0, :, ch] = x[:, ch] + g1_ref[0, :, ch] * mix
    ulast_ref[0] = u[tt - halo:tt]
    ext_ref[0:halo, :] = u[tt - halo:tt]


def _pool_prompt(x, g, sc, sh, g1, pool_w, pool_scale):
    nb, t, d = x.shape
    tt = 512
    mod_spec = pl.BlockSpec((1, 1, d), lambda b, i: (b, 0, 0))
    return pl.pallas_call(
        functools.partial(_pool_kernel, tt=tt),
        name="pool_mix",
        grid=(nb, t // tt),
        in_specs=[
            pl.BlockSpec((1, tt, d), lambda b, i: (b, i, 0)),
            pl.BlockSpec((1, d), lambda b, i: (0, 0)),
            mod_spec, mod_spec, mod_spec,
            pl.BlockSpec((len(POOL_WINDOWS), POOL_CH, POOL_CH), lambda b, i: (0, 0, 0)),
            pl.BlockSpec((1, d), lambda b, i: (0, 0)),
        ],
        out_specs=[
            pl.BlockSpec((1, tt, d), lambda b, i: (b, i, 0)),
            pl.BlockSpec((1, POOL_MAX, d), lambda b, i: (b, 0, 0)),
        ],
        out_shape=[jax.ShapeDtypeStruct((nb, t, d), F32), jax.ShapeDtypeStruct((nb, POOL_MAX, d), F32)],
        scratch_shapes=[pltpu.VMEM((tt + POOL_MAX, d), F32)],
        compiler_params=_params(("arbitrary", "arbitrary"), 48),
    )(x, g, sc, sh, g1, pool_w, pool_scale)


def _sconv_kernel(zx_ref, prev_ref, w_ref, b_ref, o_ref):
    w = w_ref[...]
    acc = b_ref[...] + w[SSD_CONV - 1:SSD_CONV] * zx_ref[:, SSD_INNER:ZX_COLS]
    for k in range(SSD_CONV - 1):
        acc = acc + w[k:k + 1] * prev_ref[k]
    o_ref[...] = _silu(acc)


def _conv_sample(zx, prev_t, conv_w, conv_b):
    return pl.pallas_call(
        _sconv_kernel,
        out_shape=jax.ShapeDtypeStruct((SAMPLE_ROWS, CONV_DIM), F32),
        compiler_params=pltpu.CompilerParams(vmem_limit_bytes=32 * MIB),
    )(zx, prev_t, conv_w, conv_b)


def _sstep_kernel(h0_ref, x_ref, dt_ref, b_ref, c_ref, alog_ref, h_ref, y_ref):
    dt = dt_ref[0]
    h = jnp.exp(dt * (-jnp.exp(alog_ref[...]))) * h0_ref[0] + (dt * x_ref[0]) * b_ref[0]
    h_ref[0] = h
    y_ref[0] = jnp.sum(h * c_ref[0], axis=-1, keepdims=True)


def _ssd_step(h0, x, dt, bh, ch, a_log):
    nb = h0.shape[0]
    hp = (SSD_HEADS, SSD_HEAD_DIM)
    return pl.pallas_call(
        _sstep_kernel,
        grid=(nb,),
        in_specs=[
            pl.BlockSpec((1, *hp, SSD_STATE), lambda b: (b, 0, 0, 0)),
            pl.BlockSpec((1, *hp, 1), lambda b: (b, 0, 0, 0)),
            pl.BlockSpec((1, SSD_HEADS, 1, 1), lambda b: (b, 0, 0, 0)),
            pl.BlockSpec((1, SSD_HEADS, 1, SSD_STATE), lambda b: (b, 0, 0, 0)),
            pl.BlockSpec((1, SSD_HEADS, 1, SSD_STATE), lambda b: (b, 0, 0, 0)),
            pl.BlockSpec((SSD_HEADS, 1, 1), lambda b: (0, 0, 0)),
        ],
        out_specs=[
            pl.BlockSpec((1, *hp, SSD_STATE), lambda b: (b, 0, 0, 0)),
            pl.BlockSpec((1, *hp, 1), lambda b: (b, 0, 0, 0)),
        ],
        out_shape=[jax.ShapeDtypeStruct((nb, *hp, SSD_STATE), F32), jax.ShapeDtypeStruct((nb, *hp, 1), F32)],
        compiler_params=_params(("arbitrary",), 32),
    )(h0, x, dt, bh, ch, a_log)


def _sgate_kernel(y_ref, xs_ref, z_ref, dskip_ref, g_ref, o_ref):
    y = (y_ref[...] + dskip_ref[...] * xs_ref[...]) * _silu(z_ref[...])
    o_ref[...] = _rms(y) * g_ref[...]


def _gate_sample(y, xs, z, dskip_x, norm_g):
    return pl.pallas_call(
        _sgate_kernel,
        out_shape=jax.ShapeDtypeStruct(y.shape, F32),
    )(y, xs, z, dskip_x, norm_g)


def _sattn_kernel(qkv_ref, c0_ref, c1_ref, c2_ref, bbuf_ref, bnew_ref, att_ref, knew_ref, vnew_ref):
    scale = 1.0 / math.sqrt(ATT_HEAD_DIM)
    caches = (c0_ref, c1_ref, c2_ref)
    outs, lses = [], []
    for g in range(len(WINDOWS)):
        hs = g * ATT_HPP
        q = qkv_ref[0, hs:hs + ATT_HPP, :]
        k = qkv_ref[0, ATT_HEADS + hs:ATT_HEADS + hs + ATT_HPP, :]
        v = qkv_ref[0, 2 * ATT_HEADS + hs:2 * ATT_HEADS + hs + ATT_HPP, :]
        knew_ref[0, g] = k
        vnew_ref[0, g] = v
        kbuf = caches[g][:, 0]
        vbuf = caches[g][:, 1]
        lb = jnp.sum(kbuf * q[None], axis=-1, keepdims=True) * scale + bbuf_ref[g]
        ln = jnp.sum(k * q, axis=-1, keepdims=True) * scale + bnew_ref[g]
        m = jnp.maximum(jnp.max(lb, axis=0), ln)
        pb = jnp.exp(lb - m[None])
        pn = jnp.exp(ln - m)
        s = jnp.sum(pb, axis=0) + pn
        outs.append((jnp.sum(pb * vbuf, axis=0) + pn * v) / s)
        lses.append(m + jnp.log(s))
    mx = jnp.maximum(jnp.maximum(lses[0], lses[1]), lses[2])
    es = [jnp.exp(l - mx) for l in lses]
    att_ref[0] = (es[0] * outs[0] + es[1] * outs[1] + es[2] * outs[2]) / (es[0] + es[1] + es[2])


def _attn_sample(qkv, caches, bias_buf, bias_new):
    nb = qkv.shape[0]
    hd = ATT_HEAD_DIM
    in_specs = [pl.BlockSpec((1, 3 * ATT_HEADS, hd), lambda b: (b, 0, 0))]
    for _ in caches:
        in_specs.append(pl.BlockSpec((None, ATT_BLOCK, None, 2, ATT_HPP, hd), lambda b: (b, 0, 0, 0, 0, 0)))
    in_specs += [
        pl.BlockSpec((3, ATT_BLOCK, ATT_HPP, 1), lambda b: (0, 0, 0, 0)),
        pl.BlockSpec((3, ATT_HPP, 1), lambda b: (0, 0, 0)),
    ]
    return pl.pallas_call(
        _sattn_kernel,
        grid=(nb,),
        in_specs=in_specs,
        out_specs=[
            pl.BlockSpec((1, ATT_HPP, hd), lambda b: (b, 0, 0)),
            pl.BlockSpec((1, 3, ATT_HPP, hd), lambda b: (b, 0, 0, 0)),
            pl.BlockSpec((1, 3, ATT_HPP, hd), lambda b: (b, 0, 0, 0)),
        ],
        out_shape=[
            jax.ShapeDtypeStruct((nb, ATT_HPP, hd), F32),
            jax.ShapeDtypeStruct((nb, 3, ATT_HPP, hd), F32),
            jax.ShapeDtypeStruct((nb, 3, ATT_HPP, hd), F32),
        ],
        compiler_params=_params(("arbitrary",), 32),
    )(qkv, *caches, bias_buf, bias_new)


def _kvshift_kernel(x_ref, new_ref, o_ref, carry_ref):
    tw = x_ref.shape[0]

    @pl.when(pl.program_id(1) == 0)
    def _():
        carry_ref[...] = new_ref[0]

    o_ref[0:tw - 1] = x_ref[1:tw]
    o_ref[tw - 1] = carry_ref[...]
    carry_ref[...] = x_ref[0]


def _kv_shift(cache, new_row):
    _, nb, w, _, hpp, hd = cache.shape
    tw = min(w, 512)
    nblk = w // tw
    row = (2, hpp, hd)
    return pl.pallas_call(
        _kvshift_kernel,
        name="kv_shift",
        grid=(nb, nblk),
        in_specs=[
            pl.BlockSpec((None, None, tw, *row), lambda b, s: (0, b, nblk - 1 - s, 0, 0, 0)),
            pl.BlockSpec((None, 1, *row), lambda b, s: (b, 0, 0, 0, 0)),
        ],
        out_specs=pl.BlockSpec((None, None, tw, *row), lambda b, s: (0, b, nblk - 1 - s, 0, 0, 0)),
        out_shape=jax.ShapeDtypeStruct(cache.shape, cache.dtype),
        scratch_shapes=[pltpu.VMEM(row, F32)],
        compiler_params=_params(("arbitrary", "arbitrary"), 40),
    )(cache, new_row)


def _spool_kernel(x_ref, g_ref, sc_ref, sh_ref, g1_ref, prev_ref, w_ref, scale_ref, o_ref, u_ref):
    x = x_ref[...]
    u = _normmod(x, g_ref[...], sc_ref[...], sh_ref[...])
    u_ref[...] = u
    for gi, w in enumerate(POOL_WINDOWS):
        ch = slice(gi * POOL_CH, (gi + 1) * POOL_CH)
        s = u[:, ch]
        for k in range(1, w):
            s = s + prev_ref[POOL_MAX - 1 - k, :, ch]
        pooled = s / float(min(w, PAST_LEN + 1)) - u[:, ch]
        mix = _bdot(pooled, w_ref[gi]) * scale_ref[:, ch]
        o_ref[:, ch] = x[:, ch] + g1_ref[:, ch] * mix


def _pool_sample(x, g, sc, sh, g1, prev_t, pool_w, pool_scale):
    return pl.pallas_call(
        _spool_kernel,
        out_shape=[jax.ShapeDtypeStruct(x.shape, F32), jax.ShapeDtypeStruct(x.shape, F32)],
        compiler_params=pltpu.CompilerParams(vmem_limit_bytes=32 * MIB),
    )(x, g, sc, sh, g1, prev_t, pool_w, pool_scale)


def _rel_bucket(dist):
    max_exact = REL_BUCKETS // 2
    n = jnp.maximum(dist, 1).astype(F32)
    large = max_exact + (jnp.log(n / max_exact) / math.log(REL_MAX_DIST / max_exact)
                         * (REL_BUCKETS - max_exact)).astype(jnp.int32)
    large = jnp.minimum(large, REL_BUCKETS - 1)
    return jnp.where(dist < max_exact, dist, large)


def _prompt_bias_table(rel_bias):
    blk = ATT_BLOCK
    j = blk + jnp.arange(blk)[:, None] - jnp.arange(2 * blk)[None, :]
    tables = []
    for gi, dil in enumerate(DILATIONS):
        bucket = _rel_bucket(jnp.clip(j, 0, blk) * dil)[None]
        rb = rel_bias[:, gi * ATT_HPP:(gi + 1) * ATT_HPP].astype(F32)
        tbl = jnp.zeros((ATT_HPP, blk, 2 * blk), F32)
        for k in range(REL_BUCKETS):
            tbl = jnp.where(bucket == k, rb[k][:, None, None], tbl)
        tables.append(tbl)
    return jnp.concatenate(tables, axis=0)


def _sample_bias_table(rel_bias):
    blk = ATT_BLOCK
    bufs, news = [], []
    for gi, dil in enumerate(DILATIONS):
        tbl = rel_bias[:, gi * ATT_HPP:(gi + 1) * ATT_HPP][_rel_bucket(dil * jnp.arange(blk + 1))]
        bufs.append(tbl[blk:0:-1])
        news.append(tbl[0])
    return jnp.stack(bufs)[..., None].astype(F32), jnp.stack(news)[..., None].astype(F32)


def _pad_rows(a, rows, axis=0):
    pad = [(0, 0)] * a.ndim
    pad[axis] = (0, rows - a.shape[axis])
    return jnp.pad(a, pad)


def _mlp_tiles(t):
    return (1024, 512) if t >= 1024 else (t, 512)


def kernel(x_prompt, x_sample, state_ssm, state_conv, cache_kv_w128, cache_kv_w512, cache_kv_w2048, state_pool, c_prompt, c_sample, rel_bias, ada_w, ada_b, norm_mix_g, norm_mlp_g, in_proj_w, conv_w, conv_b, dt_bias, a_log, d_skip, ssd_norm_g, q_norm_g, k_norm_g, out_proj_w, pool_w, pool_scale, mlp_w1, mlp_w2):
    d = D_MODEL
    c_all = _pad_rows(jnp.concatenate([c_prompt, c_sample], axis=0), SAMPLE_ROWS)
    mod = _ada(c_all, ada_w, ada_b)

    def mods(layer):
        parts = [mod[layer, :, i * d:(i + 1) * d] for i in range(6)]
        prompt = [p[:N_PROMPT, None, :] for p in parts]
        sample = [_pad_rows(p[N_PROMPT:N_PROMPT + N_SAMPLE], SAMPLE_ROWS)[None] for p in parts]
        return prompt, sample

    w_in_t = jnp.transpose(in_proj_w[0])
    b_dt = jnp.pad(dt_bias[0], (0, DT_PAD - SSD_HEADS)).reshape(1, DT_PAD)
    a_log_pad = jnp.pad(a_log[0], (0, DT_PAD - SSD_HEADS)).reshape(1, DT_PAD)
    dskip_x = jnp.repeat(d_skip[0], SSD_HEAD_DIM).reshape(1, SSD_INNER)
    ssd_g = ssd_norm_g.reshape(1, SSD_INNER)
    qg, kg = q_norm_g.reshape(1, ATT_HEAD_DIM), k_norm_g.reshape(1, ATT_HEAD_DIM)
    cw, cb = conv_w[0], conv_b.reshape(1, CONV_DIM)
    g_mix = [norm_mix_g[l].reshape(1, d) for l in range(2)]
    g_mlp = [norm_mlp_g[l].reshape(1, d) for l in range(2)]
    pscale = pool_scale.reshape(1, d)
    pool_wb = pool_w[0].astype(BF16)

    (sh1, sc1, g1, sh2, sc2, g2), (sh1s, sc1s, g1s, sh2s, sc2s, g2s) = mods(0)
    (sh1b, sc1b, g1b, sh2b, sc2b, g2b), (sh1sb, sc1sb, g1sb, sh2sb, sc2sb, g2sb) = mods(1)

    u0 = _norm_mod(x_prompt, g_mix[0], sc1, sh1, 512)
    zx, dt = _inproj(u0, w_in_t, 0, ZX_COLS, SEQ, 512, dt_b=b_dt)
    (qkv,) = _inproj(u0, w_in_t, QKV_COL0, 3 * ATT_QKV, SEQ, 512, qk_gains=(qg, kg))
    p_conv = zx[:, SEQ - (SSD_CONV - 1):, SSD_INNER:][None]
    y_ssd, h_fin = _ssd_prompt(zx, dt, cw, cb, a_log_pad, dskip_x, ssd_g)
    p_ssm = h_fin.reshape(1, N_PROMPT, SSD_HEADS, SSD_HEAD_DIM, SSD_STATE)
    att, kv128, kv512, kv2048 = _attn_prompt(qkv, _prompt_bias_table(rel_bias))
    p_kv = [kv.reshape(1, N_PROMPT, w, 2, ATT_HPP, ATT_HEAD_DIM) for kv, w in zip((kv128, kv512, kv2048), WINDOWS)]
    x1 = _outproj(y_ssd, att, out_proj_w, x_prompt, g1, SEQ, 512)
    x2 = _mlp(x1, g_mlp[0], sc2, sh2, g2, mlp_w1, mlp_w2, 0, *_mlp_tiles(SEQ))
    x3, u_last = _pool_prompt(x2, g_mix[1], sc1b, sh1b, g1b, pool_wb, pscale)
    p_pool = u_last[:, 1:][None]
    y_prompt = _mlp(x3, g_mlp[1], sc2b, sh2b, g2b, mlp_w1, mlp_w2, 1, *_mlp_tiles(SEQ))

    rows = SAMPLE_ROWS
    xs0 = _pad_rows(x_sample.reshape(N_SAMPLE, d), rows)[None]
    u0s = _norm_mod(xs0, g_mix[0], sc1s, sh1s, rows)
    zx_s, dt_s = _inproj(u0s, w_in_t, 0, ZX_COLS, rows, 512, dt_b=b_dt)
    (qkv_s,) = _inproj(u0s, w_in_t, QKV_COL0, 3 * ATT_QKV, rows, 512, qk_gains=(qg, kg))
    zx_s, dt_s = zx_s[0], dt_s[0]
    qkv_s = jnp.transpose(qkv_s[0, :, :N_SAMPLE], (1, 0, 2))
    xbc_new = zx_s[:N_SAMPLE, SSD_INNER:]
    s_conv = jnp.concatenate([state_conv[0][:, 1:], xbc_new[:, None, :]], axis=1)[None]
    conv_prev_t = _pad_rows(jnp.transpose(state_conv[0], (1, 0, 2)), rows, axis=1)
    xbc_s = _conv_sample(zx_s, conv_prev_t, cw, cb)
    xs_s = xbc_s[:N_SAMPLE, :SSD_INNER]
    e = SSD_HEADS_PER_GROUP
    b_s = jnp.repeat(xbc_s[:N_SAMPLE, SSD_INNER:SSD_INNER + SSD_GROUPS * SSD_STATE]
                     .reshape(N_SAMPLE, SSD_GROUPS, 1, SSD_STATE), e, axis=1)
    c_s = jnp.repeat(xbc_s[:N_SAMPLE, SSD_INNER + SSD_GROUPS * SSD_STATE:]
                     .reshape(N_SAMPLE, SSD_GROUPS, 1, SSD_STATE), e, axis=1)
    h_new, y_s = _ssd_step(state_ssm[0], xs_s.reshape(N_SAMPLE, SSD_HEADS, SSD_HEAD_DIM, 1),
                           dt_s[:N_SAMPLE, :SSD_HEADS].reshape(N_SAMPLE, SSD_HEADS, 1, 1), b_s, c_s,
                           a_log[0].reshape(SSD_HEADS, 1, 1))
    s_ssm = h_new[None]
    y_s = _gate_sample(_pad_rows(y_s.reshape(N_SAMPLE, SSD_INNER), rows), xbc_s[:, :SSD_INNER],
                       zx_s[:, :SSD_INNER], dskip_x, ssd_g)
    caches = (cache_kv_w128, cache_kv_w512, cache_kv_w2048)
    strided = [c.reshape(N_SAMPLE, ATT_BLOCK, dil, 2, ATT_HPP, ATT_HEAD_DIM) for c, dil in zip(caches, DILATIONS)]
    bias_buf, bias_new = _sample_bias_table(rel_bias)
    att_s, k_new, v_new = _attn_sample(qkv_s, strided, bias_buf, bias_new)
    new_rows = jnp.stack([k_new, v_new], axis=2)
    s_kv = [_kv_shift(c, new_rows[:, gi:gi + 1]) for gi, c in enumerate(caches)]
    att_s = _pad_rows(att_s.reshape(N_SAMPLE, ATT_OUT), rows)
    x1s = _outproj(y_s[None], att_s[None], out_proj_w, xs0, g1s, rows, 512)
    x2s = _mlp(x1s, g_mlp[0], sc2s, sh2s, g2s, mlp_w1, mlp_w2, 0, *_mlp_tiles(rows))
    pool_prev_t = _pad_rows(jnp.transpose(state_pool[0], (1, 0, 2)), rows, axis=1)
    x3s, u_s = _pool_sample(x2s[0], g_mix[1], sc1sb[0], sh1sb[0], g1sb[0], pool_prev_t, pool_wb, pscale)
    s_pool = jnp.concatenate([state_pool[0][:, 1:], u_s[:N_SAMPLE, None, :]], axis=1)[None]
    y_s2 = _mlp(x3s[None], g_mlp[1], sc2sb, sh2sb, g2sb, mlp_w1, mlp_w2, 1, *_mlp_tiles(rows))
    y_sample = y_s2[0, :N_SAMPLE].reshape(N_SAMPLE, 1, d)

    return (y_prompt, y_sample, p_ssm, p_conv, p_kv[0], p_kv[1], p_kv[2], p_pool,
            s_ssm, s_conv, s_kv[0], s_kv[1], s_kv[2], s_pool)
```

```python
import functools
import math

import jax
import jax.numpy as jnp
from jax import lax
from jax.experimental import pallas as pl
from jax.experimental.pallas import tpu as pltpu

F32 = jnp.float32
BF16 = jnp.bfloat16

D_MODEL = 2048
SEQ = 2048
N_PROMPT = 4
N_SAMPLE = 8
SAMPLE_ROWS = 16
PAST_LEN = 16384
SSD_HEADS = 32
SSD_HEAD_DIM = 64
SSD_INNER = SSD_HEADS * SSD_HEAD_DIM
SSD_GROUPS = 8
SSD_HEADS_PER_GROUP = SSD_HEADS // SSD_GROUPS
SSD_STATE = 128
SSD_CONV = 4
SSD_CHUNK = 128
CONV_DIM = SSD_INNER + 2 * SSD_GROUPS * SSD_STATE
ATT_HEAD_DIM = 128
WINDOWS = (128, 512, 2048)
DILATIONS = (1, 4, 16)
ATT_BLOCK = 128
ATT_UNIT_ROWS = 128
ATT_MAX_ROW_STRIDE = 4
ATT_HPP = 8
ATT_HEADS = ATT_HPP * len(WINDOWS)
ATT_QKV = ATT_HEADS * ATT_HEAD_DIM
ATT_OUT = ATT_HPP * ATT_HEAD_DIM
REL_BUCKETS = 32
REL_MAX_DIST = 2048
ZX_COLS = SSD_INNER + CONV_DIM
DT_COL0 = ZX_COLS
QKV_COL0 = ZX_COLS + SSD_HEADS
QKV_LANE_SHIFT = QKV_COL0 % 128
DT_PAD = 128
MIX_OUT = SSD_INNER + ATT_OUT
POOL_WINDOWS = (2, 4, 8, 16)
POOL_CH = D_MODEL // len(POOL_WINDOWS)
POOL_MAX = 16
D_FF = 4 * D_MODEL
EPS = 1e-6
NEG = -1e30
MIB = 1024 * 1024


def _params(semantics, vmem_mib):
    return pltpu.CompilerParams(dimension_semantics=semantics, vmem_limit_bytes=vmem_mib * MIB)


def _sigmoid(x):
    return 0.5 * jnp.tanh(0.5 * x) + 0.5


def _silu(x):
    return x * _sigmoid(x)


def _softplus(x):
    return jnp.maximum(x, 0.0) + jnp.log1p(jnp.exp(-jnp.abs(x)))


def _rms(x):
    return x * lax.rsqrt(jnp.mean(x * x, axis=-1, keepdims=True) + EPS)


def _normmod(x, g, sc, sh):
    return (_rms(x) * g) * (1.0 + sc) + sh


def _normmod_rows(u_ref, x_ref, g_ref, sc_ref, sh_ref):
    tm = u_ref.shape[0]
    rc = min(tm, 256)
    per_row = sc_ref.shape[1] != 1

    def body(i, carry):
        rows = pl.ds(pl.multiple_of(i * rc, rc), rc)
        sc = sc_ref[0, rows, :] if per_row else sc_ref[0]
        sh = sh_ref[0, rows, :] if per_row else sh_ref[0]
        u_ref[rows, :] = _normmod(x_ref[0, rows, :], g_ref[...], sc, sh).astype(BF16)
        return carry

    lax.fori_loop(0, tm // rc, body, 0)


def _bdot(a, b):
    return jnp.dot(a.astype(BF16), b.astype(BF16), preferred_element_type=F32)


def _split3(v):
    hi = v.astype(BF16)
    r = v - hi.astype(F32)
    mid = r.astype(BF16)
    lo = (r - mid.astype(F32)).astype(BF16)
    return hi, mid, lo


def _select_right(v, sel, pieces=3):
    parts = _split3(v)[:pieces]
    out = jnp.dot(parts[0], sel, preferred_element_type=F32)
    for p in parts[1:]:
        out = out + jnp.dot(p, sel, preferred_element_type=F32)
    return out


def _select_left(sel, v):
    hi, mid, lo = _split3(v)
    d = lambda a: jnp.dot(sel, a, preferred_element_type=F32)
    return d(hi) + d(mid) + d(lo)


def _ada_kernel(c_ref, w_ref, b_ref, o_ref):
    o_ref[0] = _bdot(_silu(c_ref[...]), w_ref[0]) + b_ref[0]


def _ada(c_all, ada_w, ada_b):
    depth, _, n = ada_w.shape
    tn = 1024
    return pl.pallas_call(
        _ada_kernel,
        name="ada_mod",
        grid=(depth, n // tn),
        in_specs=[
            pl.BlockSpec((SAMPLE_ROWS, D_MODEL), lambda l, j: (0, 0)),
            pl.BlockSpec((1, D_MODEL, tn), lambda l, j: (l, 0, j)),
            pl.BlockSpec((1, 1, tn), lambda l, j: (l, 0, j)),
        ],
        out_specs=pl.BlockSpec((1, SAMPLE_ROWS, tn), lambda l, j: (l, 0, j)),
        out_shape=jax.ShapeDtypeStruct((depth, SAMPLE_ROWS, n), F32),
        compiler_params=_params(("arbitrary", "arbitrary"), 40),
    )(c_all, ada_w, ada_b.reshape(depth, 1, n))


def _bdot_t(a, bt):
    return lax.dot_general(a.astype(BF16), bt.astype(BF16), (((1,), (1,)), ((), ())), preferred_element_type=F32)


def _norm_kernel(x_ref, g_ref, sc_ref, sh_ref, o_ref):
    o_ref[0] = _normmod(x_ref[0], g_ref[...], sc_ref[0], sh_ref[0]).astype(BF16)


def _norm_mod(x, g, sc, sh, tt):
    nb, t, d = x.shape
    r = sc.shape[1]
    mod_spec = pl.BlockSpec((1, r if r == 1 else tt, d), lambda b, i: (b, 0 if r == 1 else i, 0))
    return pl.pallas_call(
        _norm_kernel,
        name="norm_mod",
        grid=(nb, t // tt),
        in_specs=[pl.BlockSpec((1, tt, d), lambda b, i: (b, i, 0)), pl.BlockSpec((1, d), lambda b, i: (0, 0)),
                  mod_spec, mod_spec],
        out_specs=pl.BlockSpec((1, tt, d), lambda b, i: (b, i, 0)),
        out_shape=jax.ShapeDtypeStruct((nb, t, d), BF16),
        compiler_params=_params(("arbitrary", "arbitrary"), 40),
    )(x, g, sc, sh)


def _inproj_kernel(*refs, mode):
    if mode == "dt":
        u_ref, w_ref, wdt_ref, dtb_ref, o_ref, dt_ref = refs
    else:
        u_ref, w_ref, qg_ref, kg_ref, o_ref = refs
    j = pl.program_id(2)

    if mode == "dt":
        @pl.when(j == 0)
        def _():
            dt_ref[0] = _softplus(_bdot_t(u_ref[0], wdt_ref[...]) + dtb_ref[...])

    acc = _bdot_t(u_ref[0], w_ref[...])
    if mode == "dt":
        o_ref[0] = acc
    else:
        heads = o_ref.shape[1]
        q_tiles = ATT_HEADS // heads

        @pl.when(j < 2 * q_tiles)
        def _():
            gain = jnp.where(j < q_tiles, qg_ref[...], kg_ref[...])
            for hc in range(heads):
                o_ref[0, hc] = _rms(acc[:, hc * ATT_HEAD_DIM:(hc + 1) * ATT_HEAD_DIM]) * gain

        @pl.when(j >= 2 * q_tiles)
        def _():
            for hc in range(heads):
                o_ref[0, hc] = acc[:, hc * ATT_HEAD_DIM:(hc + 1) * ATT_HEAD_DIM]


def _inproj(u, wt, row0, n_cols, tm, tn, dt_b=None, qk_gains=None):
    nb, t, d = u.shape
    has_dt = dt_b is not None
    if row0 % tn == 0:
        w_spec = pl.BlockSpec((tn, d), lambda b, i, j: (row0 // tn + j, 0))
    else:
        assert row0 % 8 == 0
        w_spec = pl.BlockSpec((pl.Element(tn), pl.Element(d)),
                              lambda b, i, j: (pl.multiple_of(row0 + j * tn, 8), 0))
    in_specs = [pl.BlockSpec((1, tm, d), lambda b, i, j: (b, i, 0), pipeline_mode=pl.Buffered(1)), w_spec]
    args = [u, wt]
    if has_dt:
        in_specs += [pl.BlockSpec((DT_PAD, d), lambda b, i, j: (DT_COL0 // DT_PAD, 0)),
                     pl.BlockSpec((1, DT_PAD), lambda b, i, j: (0, 0))]
        args += [wt, dt_b]
        out_specs = [pl.BlockSpec((1, tm, tn), lambda b, i, j: (b, i, j)),
                     pl.BlockSpec((1, tm, DT_PAD), lambda b, i, j: (b, i, 0))]
        out_shape = [jax.ShapeDtypeStruct((nb, t, n_cols), F32), jax.ShapeDtypeStruct((nb, t, DT_PAD), F32)]
    else:
        hd = ATT_HEAD_DIM
        in_specs += [pl.BlockSpec((1, hd), lambda b, i, j: (0, 0))] * 2
        args += list(qk_gains)
        out_specs = [pl.BlockSpec((1, tn // hd, tm, hd), lambda b, i, j: (b, j, i, 0))]
        out_shape = [jax.ShapeDtypeStruct((nb, n_cols // hd, t, hd), F32)]
    return pl.pallas_call(
        functools.partial(_inproj_kernel, mode="dt" if has_dt else "qkv"),
        name="in_proj",
        grid=(nb, t // tm, n_cols // tn),
        in_specs=in_specs,
        out_specs=out_specs,
        out_shape=out_shape,
        compiler_params=_params(("arbitrary", "arbitrary", "arbitrary"), 48),
    )(*args)


def _ssd_kernel(z_ref, xraw_ref, bcraw_ref, dt_ref, cw_ref, cb_ref, alog_ref, dskip_ref, g_ref,
                y_ref, hout_ref, h_ref, yacc_ref, act_ref, carry_ref):
    q = SSD_CHUNK
    gw = SSD_HEADS_PER_GROUP * SSD_HEAD_DIM
    slab = 512
    chunk = pl.program_id(1)

    @pl.when(chunk == 0)
    def _():
        h_ref[...] = jnp.zeros(h_ref.shape, F32)
        carry_ref[...] = jnp.zeros(carry_ref.shape, F32)

    row8 = lax.broadcasted_iota(jnp.int32, (8, slab), 0)
    for s in range(CONV_DIM // slab):
        cs = slice(s * slab, (s + 1) * slab)
        src, off = (xraw_ref, s * slab) if s * slab < SSD_INNER else (bcraw_ref, s * slab - SSD_INNER)
        x = src[0, :, off:off + slab]
        w = cw_ref[:, cs]
        bias = cb_ref[:, cs]
        taps = [w[SSD_CONV - 1 - k:SSD_CONV - k] for k in range(SSD_CONV)]
        acc = bias + taps[0] * x
        for k in range(1, SSD_CONV):
            acc = acc + taps[k] * pltpu.roll(x, k, axis=0)
        act_ref[:, cs] = _silu(acc)
        x8 = x[0:8]
        carry = carry_ref[:, cs]
        acc8 = bias + taps[0] * x8
        for k in range(1, SSD_CONV):
            acc8 = acc8 + taps[k] * jnp.where(row8 < k, pltpu.roll(carry, k, axis=0), pltpu.roll(x8, k, axis=0))
        act_ref[0:8, cs] = _silu(acc8)
        carry_ref[:, cs] = x[q - 8:q]

    row = lax.broadcasted_iota(jnp.int32, (q, q), 0)
    col = lax.broadcasted_iota(jnp.int32, (q, q), 1)
    causal = row >= col
    tri = jnp.where(causal, 1.0, 0.0).astype(BF16)

    dt = dt_ref[0]
    a = dt * (-jnp.exp(alog_ref[...]))
    acum = _select_left(tri, a)
    acum_t = acum.T
    dt_t = dt.T
    last = acum[q - 1:q, :]
    decay_end = jnp.exp(last - acum) * dt
    eacum = jnp.exp(acum)
    chunk_decay = jnp.broadcast_to(jnp.exp(last), (8, DT_PAD))

    first_head_lanes = lax.broadcasted_iota(jnp.int32, (q, 2 * SSD_HEAD_DIM), 1) < SSD_HEAD_DIM
    lane_head = lax.broadcasted_iota(jnp.int32, (DT_PAD, gw), 1) // SSD_HEAD_DIM
    row_head = lax.broadcasted_iota(jnp.int32, (DT_PAD, gw), 0)
    b0 = SSD_INNER
    c0 = SSD_INNER + SSD_GROUPS * SSD_STATE
    ssq = jnp.zeros((q, 1), F32)

    for g in range(SSD_GROUPS):
        cs = slice(g * gw, (g + 1) * gw)
        expand = jnp.where(lane_head + g * SSD_HEADS_PER_GROUP == row_head, 1.0, 0.0).astype(BF16)
        decay_end_x = _select_right(decay_end, expand, pieces=2)
        eacum_x = _select_right(eacum, expand, pieces=2)
        chunk_decay_x = _select_right(chunk_decay, expand)[0:1]

        xs = act_ref[:, cs]
        bg_t = act_ref[:, b0 + g * SSD_STATE:b0 + (g + 1) * SSD_STATE].T.astype(BF16)
        cg = act_ref[:, c0 + g * SSD_STATE:c0 + (g + 1) * SSD_STATE].astype(BF16)
        cb = jnp.dot(cg, bg_t, preferred_element_type=F32)
        h_prev = h_ref[:, cs]
        y_g = jnp.dot(cg, h_prev.astype(BF16), preferred_element_type=F32) * eacum_x
        h_ref[:, cs] = h_prev * chunk_decay_x + jnp.dot(bg_t, (xs * decay_end_x).astype(BF16),
                                                        preferred_element_type=F32)
        xg = xs.astype(BF16)

        def y_diag(e):
            hd = g * SSD_HEADS_PER_GROUP + e
            seg = acum[:, hd:hd + 1] - acum_t[hd:hd + 1, :]
            lw = jnp.where(causal, jnp.exp(seg), 0.0) * dt_t[hd:hd + 1, :]
            pair_lanes = slice((e // 2) * 2 * SSD_HEAD_DIM, (e // 2 + 1) * 2 * SSD_HEAD_DIM)
            return jnp.dot((cb * lw).astype(BF16), xg[:, pair_lanes], preferred_element_type=F32)

        y_pairs = [jnp.where(first_head_lanes, y_diag(2 * p), y_diag(2 * p + 1))
                   for p in range(SSD_HEADS_PER_GROUP // 2)]
        y_g = y_g + jnp.concatenate(y_pairs, axis=1)
        y_g = (y_g + dskip_ref[:, cs] * xs) * _silu(z_ref[0, :, cs])
        yacc_ref[:, cs] = y_g
        ssq = ssq + jnp.sum(y_g * y_g, axis=-1, keepdims=True)

    inv_rms = lax.rsqrt(ssq * (1.0 / SSD_INNER) + EPS)
    y_ref[0] = (yacc_ref[...] * inv_rms * g_ref[...]).astype(BF16)

    @pl.when(chunk == pl.num_programs(1) - 1)
    def _():
        hout_ref[0] = h_ref[...].T


def _ssd_prompt(zx, dt, conv_w, conv_b, a_log_pad, dskip_x, norm_g):
    nb, t, _ = zx.shape
    q = SSD_CHUNK
    wide = SSD_INNER
    assert ZX_COLS == 3 * wide
    return pl.pallas_call(
        _ssd_kernel,
        name="ssd_chunk",
        grid=(nb, t // q),
        in_specs=[
            pl.BlockSpec((1, q, wide), lambda b, c: (b, c, 0)),
            pl.BlockSpec((1, q, wide), lambda b, c: (b, c, 1)),
            pl.BlockSpec((1, q, wide), lambda b, c: (b, c, 2)),
            pl.BlockSpec((1, q, DT_PAD), lambda b, c: (b, c, 0)),
            pl.BlockSpec((SSD_CONV, CONV_DIM), lambda b, c: (0, 0)),
            pl.BlockSpec((1, CONV_DIM), lambda b, c: (0, 0)),
            pl.BlockSpec((1, DT_PAD), lambda b, c: (0, 0)),
            pl.BlockSpec((1, SSD_INNER), lambda b, c: (0, 0)),
            pl.BlockSpec((1, SSD_INNER), lambda b, c: (0, 0)),
        ],
        out_specs=[
            pl.BlockSpec((1, q, SSD_INNER), lambda b, c: (b, c, 0)),
            pl.BlockSpec((1, SSD_INNER, SSD_STATE), lambda b, c: (b, 0, 0)),
        ],
        out_shape=[
            jax.ShapeDtypeStruct((nb, t, SSD_INNER), BF16),
            jax.ShapeDtypeStruct((nb, SSD_INNER, SSD_STATE), F32),
        ],
        scratch_shapes=[pltpu.VMEM((SSD_STATE, SSD_INNER), F32), pltpu.VMEM((q, SSD_INNER), F32),
                        pltpu.VMEM((q, CONV_DIM), F32), pltpu.VMEM((8, CONV_DIM), F32)],
        compiler_params=_params(("arbitrary", "arbitrary"), 40),
    )(zx, zx, zx, dt, conv_w, conv_b, a_log_pad, dskip_x, norm_g)


def _attn_kernel(q0, q1, q2, k0, k1, k2, v0, v1, v2, bias0, bias1, bias2,
                 att_ref, kv0_ref, kv1_ref, kv2_ref, qb_s, kb_s, vb_s, ok_s, os_, ls, tmp_s, ltmp_s, kv_sem):
    blk = ATT_BLOCK
    ur = ATT_UNIT_ROWS
    hd = ATT_HEAD_DIM
    scale = 1.0 / math.sqrt(hd)
    batch, slot = pl.program_id(0), pl.program_id(1)
    qrefs, krefs, vrefs, brefs = (q0, q1, q2), (k0, k1, k2), (v0, v1, v2), (bias0, bias1, bias2)
    kvrefs = (kv0_ref, kv1_ref, kv2_ref)
    nt = (((1,), (1,)), ((), ()))

    def cache_copies():
        copies = []
        for g, w in enumerate(WINDOWS):
            for part, src in enumerate((krefs[g], vrefs[g])):
                cols = pl.ds(pl.multiple_of((part * ATT_HPP + slot) * hd, hd), hd)
                copies.append(pltpu.make_async_copy(
                    src.at[0, 0, pl.ds(SEQ - w, w), :], kvrefs[g].at[batch, :, cols], kv_sem.at[2 * g + part]))
        return copies

    for cp in cache_copies():
        cp.start()

    def compute():
        row = lax.broadcasted_iota(jnp.int32, (blk, 2 * blk), 0)
        col = lax.broadcasted_iota(jnp.int32, (blk, 2 * blk), 1)
        ok_s[...] = jnp.where(jnp.where(col < blk, col - row, row - (col - blk)) >= 0, 1.0, 0.0)

        f = ATT_MAX_ROW_STRIDE
        seg = SEQ // f

        def strided(start, size, stride):
            return pl.ds(start, size) if stride == 1 else pl.ds(start, size, stride=stride)

        def regroup(src_ref, dil, emit):
            if dil <= f:
                sub = SEQ // dil
                ch = min(sub, 256)
                for r in range(dil):
                    for c in range(sub // ch):
                        emit(pl.ds(r * sub + c * ch, ch), src_ref[0, 0, strided(r + c * ch * dil, ch, dil), :])
            else:
                assert dil == f * f
                for r1 in range(f):
                    for c in range(seg // 256):
                        tmp_s[pl.ds(r1 * seg + c * 256, 256), :] = src_ref[0, 0, strided(r1 + c * 256 * f, 256, f), :]
                sub = seg // f
                for r1 in range(f):
                    for r2 in range(f):
                        emit(pl.ds((f * r2 + r1) * sub, sub), tmp_s[strided(r1 * seg + r2, sub, f), :])

        for g, dil in enumerate(DILATIONS):
            vb_s[g, :, hd:2 * hd] = jnp.ones((SEQ, hd), BF16)

            def put_q(rows, val, g=g):
                qb_s[g, rows, :] = val.astype(BF16)

            def put_k(rows, val, g=g):
                kb_s[g, rows, :] = val.astype(BF16)

            def put_v(rows, val, g=g):
                vb_s[g, rows, 0:hd] = val.astype(BF16)

            regroup(qrefs[g], dil, put_q)
            regroup(krefs[g], dil, put_k)
            regroup(vrefs[g], dil, put_v)

        for g, dil in enumerate(DILATIONS):
            nblk = SEQ // (blk * dil)
            for r in range(dil):
                for n in range(nblk):
                    base = (r * nblk + n) * blk
                    key0, col0 = (base - blk, 0) if n > 0 else (base, blk)
                    kcat = kb_s[g, key0:base + blk, :]
                    vcat = vb_s[g, key0:base + blk, :]
                    for half in range(blk // ur):
                        rs = slice(half * ur, (half + 1) * ur)
                        qb = qb_s[g, base + half * ur:base + (half + 1) * ur, :]
                        l = lax.dot_general(qb, kcat, nt, preferred_element_type=F32)
                        l = jnp.where(ok_s[rs, col0:2 * blk] > 0.5, l * scale + brefs[g][0, rs, col0:2 * blk], NEG)
                        m = jnp.max(l, axis=-1, keepdims=True)
                        p = jnp.exp(l - m).astype(BF16)
                        oe = jnp.dot(p, vcat, preferred_element_type=F32)
                        den = oe[:, hd:2 * hd]
                        s0 = n * blk + half * ur
                        if dil <= f:
                            dst = strided(r + s0 * dil, ur, dil)
                            os_[g, dst, :] = oe[:, 0:hd] / den
                            ls[g, dst, :] = m + jnp.log(den)
                        else:
                            dst = strided((r % f) * seg + f * s0 + r // f, ur, f)
                            tmp_s[dst, :] = oe[:, 0:hd] / den
                            ltmp_s[dst, :] = m + jnp.log(den)
            if dil > f:
                for r1 in range(f):
                    for c in range(seg // 256):
                        dst = strided(r1 + c * 256 * f, 256, f)
                        os_[g, dst, :] = tmp_s[pl.ds(r1 * seg + c * 256, 256), :]
                        ls[g, dst, :] = ltmp_s[pl.ds(r1 * seg + c * 256, 256), :]

        def mix_body(i, carry):
            rows = pl.ds(pl.multiple_of(i * blk, blk), blk)
            l0, l1, l2 = ls[0, rows, :], ls[1, rows, :], ls[2, rows, :]
            mx = jnp.maximum(jnp.maximum(l0, l1), l2)
            e0, e1, e2 = jnp.exp(l0 - mx), jnp.exp(l1 - mx), jnp.exp(l2 - mx)
            num = e0 * os_[0, rows, :] + e1 * os_[1, rows, :] + e2 * os_[2, rows, :]
            att_ref[0, rows, :] = (num / (e0 + e1 + e2)).astype(BF16)
            return carry
        lax.fori_loop(0, SEQ // blk, mix_body, 0)

    compute()
    for cp in cache_copies():
        cp.wait()


def _attn_prompt(qkv, bias):
    nb = qkv.shape[0]
    hd = ATT_HEAD_DIM

    def head_spec(part, g):
        return pl.BlockSpec((1, 1, SEQ, hd), lambda b, h: (b, part * ATT_HEADS + g * ATT_HPP + h, 0, 0))

    in_specs = [head_spec(part, g) for part in range(3) for g in range(3)]
    in_specs += [pl.BlockSpec((1, ATT_BLOCK, 2 * ATT_BLOCK), functools.partial(lambda b, h, g: (g * ATT_HPP + h, 0, 0), g=g))
                 for g in range(3)]
    out_specs = [pl.BlockSpec((1, SEQ, hd), lambda b, h: (b, 0, h))]
    out_shape = [jax.ShapeDtypeStruct((nb, SEQ, ATT_OUT), BF16)]
    for w in WINDOWS:
        out_specs.append(pl.BlockSpec(memory_space=pl.ANY))
        out_shape.append(jax.ShapeDtypeStruct((nb, w, 2 * ATT_OUT), F32))
    return pl.pallas_call(
        _attn_kernel,
        name="attn_prompt",
        grid=(nb, ATT_HPP),
        in_specs=in_specs,
        out_specs=out_specs,
        out_shape=out_shape,
        scratch_shapes=[
            pltpu.VMEM((3, SEQ, hd), BF16), pltpu.VMEM((3, SEQ, hd), BF16), pltpu.VMEM((3, SEQ, 2 * hd), BF16),
            pltpu.VMEM((ATT_BLOCK, 2 * ATT_BLOCK), F32),
            pltpu.VMEM((3, SEQ, hd), F32), pltpu.VMEM((3, SEQ, hd), F32),
            pltpu.VMEM((SEQ, hd), F32), pltpu.VMEM((SEQ, hd), F32),
            pltpu.SemaphoreType.DMA((2 * len(WINDOWS),)),
        ],
        compiler_params=_params(("arbitrary", "arbitrary"), 48),
    )(*([qkv] * 9), bias, bias, bias)


def _outproj_kernel(y_ref, a_ref, wy_ref, wa_ref, x_ref, g1_ref, o_ref):
    mix = _bdot(y_ref[0], wy_ref[0]) + _bdot(a_ref[0], wa_ref[0])
    o_ref[0] = x_ref[0] + g1_ref[0] * mix


def _outproj(y, att, w, x, g1, tm, tn):
    nb, t, d = x.shape
    r = g1.shape[1]
    return pl.pallas_call(
        _outproj_kernel,
        name="out_proj",
        grid=(nb, t // tm, d // tn),
        in_specs=[
            pl.BlockSpec((1, tm, SSD_INNER), lambda b, i, j: (b, i, 0), pipeline_mode=pl.Buffered(1)),
            pl.BlockSpec((1, tm, ATT_OUT), lambda b, i, j: (b, i, 0), pipeline_mode=pl.Buffered(1)),
            pl.BlockSpec((1, SSD_INNER, tn), lambda b, i, j: (0, 0, j)),
            pl.BlockSpec((1, ATT_OUT, tn), lambda b, i, j: (0, SSD_INNER // ATT_OUT, j)),
            pl.BlockSpec((1, tm, tn), lambda b, i, j: (b, i, j)),
            pl.BlockSpec((1, r if r == 1 else tm, tn), lambda b, i, j: (b, 0 if r == 1 else i, j)),
        ],
        out_specs=pl.BlockSpec((1, tm, tn), lambda b, i, j: (b, i, j)),
        out_shape=jax.ShapeDtypeStruct((nb, t, d), F32),
        compiler_params=_params(("arbitrary", "arbitrary", "arbitrary"), 48),
    )(y, att, w, w, x, g1)


def _mlp_kernel(x_ref, g_ref, sc_ref, sh_ref, g2_ref, w1_ref, w2_ref, o_ref, u_ref):
    f = pl.program_id(2)

    last = pl.num_programs(2) - 1

    tf = w2_ref.shape[1]

    def partial_out():
        w1 = jnp.where(f % 2 == 0, w1_ref[0, :, 0:tf], w1_ref[0, :, tf:2 * tf])
        h = jnp.maximum(_bdot(u_ref[...], w1), 0.0)
        return _bdot(h * h, w2_ref[0])

    @pl.when(f == 0)
    def _():
        _normmod_rows(u_ref, x_ref, g_ref, sc_ref, sh_ref)
        o_ref[0] = partial_out()

    @pl.when(jnp.logical_and(f > 0, f < last))
    def _():
        o_ref[0] += partial_out()

    @pl.when(f == last)
    def _():
        o_ref[0] = x_ref[0] + g2_ref[0] * (o_ref[0] + partial_out())


def _mlp(x, g, sc, sh, g2, w1, w2, layer, tm, tf):
    nb, t, d = x.shape
    r = sc.shape[1]
    mod_spec = pl.BlockSpec((1, r if r == 1 else tm, d), lambda b, i, f: (b, 0 if r == 1 else i, 0))
    return pl.pallas_call(
        _mlp_kernel,
        name="mlp",
        grid=(nb, t // tm, D_FF // tf),
        in_specs=[
            pl.BlockSpec((1, tm, d), lambda b, i, f: (b, i, 0), pipeline_mode=pl.Buffered(1)),
            pl.BlockSpec((1, d), lambda b, i, f: (0, 0)),
            mod_spec, mod_spec, mod_spec,
            pl.BlockSpec((1, d, 2 * tf), lambda b, i, f: (layer, 0, f // 2)),
            pl.BlockSpec((1, tf, d), lambda b, i, f: (layer, f, 0)),
        ],
        out_specs=pl.BlockSpec((1, tm, d), lambda b, i, f: (b, i, 0), pipeline_mode=pl.Buffered(1)),
        out_shape=jax.ShapeDtypeStruct((nb, t, d), F32),
        scratch_shapes=[pltpu.VMEM((tm, d), BF16)],
        compiler_params=_params(("arbitrary", "arbitrary", "arbitrary"), 56),
    )(x, g, sc, sh, g2, w1, w2)


def _pool_kernel(x_ref, g_ref, sc_ref, sh_ref, g1_ref, w_ref, scale_ref, o_ref, ulast_ref, ext_ref, *, tt):
    halo = POOL_MAX
    ti = pl.program_id(1)

    @pl.when(ti == 0)
    def _():
        ext_ref[0:halo, :] = jnp.zeros((halo, D_MODEL), F32)

    x = x_ref[0]
    u = _normmod(x, g_ref[...], sc_ref[0], sh_ref[0])
    ext_ref[halo:halo + tt, :] = u
    pos = ti * tt + lax.broadcasted_iota(jnp.int32, (tt, 1), 0)
    for gi, w in enumerate(POOL_WINDOWS):
        ch = slice(gi * POOL_CH, (gi + 1) * POOL_CH)
        acc = ext_ref[:, ch]
        span = 1
        while span < w:
            acc = acc + pltpu.roll(acc, span, axis=0)
            span *= 2
        s = acc[halo:halo + tt]
        inv_cnt = 1.0 / jnp.minimum(w, pos + 1).astype(F32)
        pooled = s * inv_cnt - u[:, ch]
        mix = _bdot(pooled, w_ref[gi]) * scale_ref[:, ch]
        o_ref[0, :, ch] = x[:, ch] + g1_ref[0, :, ch] * mix
    ulast_ref[0] = u[tt - halo:tt]
    ext_ref[0:halo, :] = u[tt - halo:tt]


def _pool_prompt(x, g, sc, sh, g1, pool_w, pool_scale):
    nb, t, d = x.shape
    tt = 512
    mod_spec = pl.BlockSpec((1, 1, d), lambda b, i: (b, 0, 0))
    return pl.pallas_call(
        functools.partial(_pool_kernel, tt=tt),
        name="pool_mix",
        grid=(nb, t // tt),
        in_specs=[
            pl.BlockSpec((1, tt, d), lambda b, i: (b, i, 0)),
            pl.BlockSpec((1, d), lambda b, i: (0, 0)),
            mod_spec, mod_spec, mod_spec,
            pl.BlockSpec((len(POOL_WINDOWS), POOL_CH, POOL_CH), lambda b, i: (0, 0, 0)),
            pl.BlockSpec((1, d), lambda b, i: (0, 0)),
        ],
        out_specs=[
            pl.BlockSpec((1, tt, d), lambda b, i: (b, i, 0)),
            pl.BlockSpec((1, POOL_MAX, d), lambda b, i: (b, 0, 0)),
        ],
        out_shape=[jax.ShapeDtypeStruct((nb, t, d), F32), jax.ShapeDtypeStruct((nb, POOL_MAX, d), F32)],
        scratch_shapes=[pltpu.VMEM((tt + POOL_MAX, d), F32)],
        compiler_params=_params(("arbitrary", "arbitrary"), 48),
    )(x, g, sc, sh, g1, pool_w, pool_scale)


def _sconv_kernel(zx_ref, prev_ref, w_ref, b_ref, o_ref):
    w = w_ref[...]
    acc = b_ref[...] + w[SSD_CONV - 1:SSD_CONV] * zx_ref[:, SSD_INNER:ZX_COLS]
    for k in range(SSD_CONV - 1):
        acc = acc + w[k:k + 1] * prev_ref[k]
    o_ref[...] = _silu(acc)


def _conv_sample(zx, prev_t, conv_w, conv_b):
    return pl.pallas_call(
        _sconv_kernel,
        out_shape=jax.ShapeDtypeStruct((SAMPLE_ROWS, CONV_DIM), F32),
        compiler_params=pltpu.CompilerParams(vmem_limit_bytes=32 * MIB),
    )(zx, prev_t, conv_w, conv_b)


def _sstep_kernel(h0_ref, x_ref, dt_ref, b_ref, c_ref, alog_ref, h_ref, y_ref):
    dt = dt_ref[0]
    h = jnp.exp(dt * (-jnp.exp(alog_ref[...]))) * h0_ref[0] + (dt * x_ref[0]) * b_ref[0]
    h_ref[0] = h
    y_ref[0] = jnp.sum(h * c_ref[0], axis=-1, keepdims=True)


def _ssd_step(h0, x, dt, bh, ch, a_log):
    nb = h0.shape[0]
    hp = (SSD_HEADS, SSD_HEAD_DIM)
    return pl.pallas_call(
        _sstep_kernel,
        grid=(nb,),
        in_specs=[
            pl.BlockSpec((1, *hp, SSD_STATE), lambda b: (b, 0, 0, 0)),
            pl.BlockSpec((1, *hp, 1), lambda b: (b, 0, 0, 0)),
            pl.BlockSpec((1, SSD_HEADS, 1, 1), lambda b: (b, 0, 0, 0)),
            pl.BlockSpec((1, SSD_HEADS, 1, SSD_STATE), lambda b: (b, 0, 0, 0)),
            pl.BlockSpec((1, SSD_HEADS, 1, SSD_STATE), lambda b: (b, 0, 0, 0)),
            pl.BlockSpec((SSD_HEADS, 1, 1), lambda b: (0, 0, 0)),
        ],
        out_specs=[
            pl.BlockSpec((1, *hp, SSD_STATE), lambda b: (b, 0, 0, 0)),
            pl.BlockSpec((1, *hp, 1), lambda b: (b, 0, 0, 0)),
        ],
        out_shape=[jax.ShapeDtypeStruct((nb, *hp, SSD_STATE), F32), jax.ShapeDtypeStruct((nb, *hp, 1), F32)],
        compiler_params=_params(("arbitrary",), 32),
    )(h0, x, dt, bh, ch, a_log)


def _sgate_kernel(y_ref, xs_ref, z_ref, dskip_ref, g_ref, o_ref):
    y = (y_ref[...] + dskip_ref[...] * xs_ref[...]) * _silu(z_ref[...])
    o_ref[...] = _rms(y) * g_ref[...]


def _gate_sample(y, xs, z, dskip_x, norm_g):
    return pl.pallas_call(
        _sgate_kernel,
        out_shape=jax.ShapeDtypeStruct(y.shape, F32),
    )(y, xs, z, dskip_x, norm_g)


def _sattn_kernel(qkv_ref, c0_ref, c1_ref, c2_ref, bbuf_ref, bnew_ref, att_ref, knew_ref, vnew_ref):
    scale = 1.0 / math.sqrt(ATT_HEAD_DIM)
    caches = (c0_ref, c1_ref, c2_ref)
    outs, lses = [], []
    for g in range(len(WINDOWS)):
        hs = g * ATT_HPP
        q = qkv_ref[0, hs:hs + ATT_HPP, :]
        k = qkv_ref[0, ATT_HEADS + hs:ATT_HEADS + hs + ATT_HPP, :]
        v = qkv_ref[0, 2 * ATT_HEADS + hs:2 * ATT_HEADS + hs + ATT_HPP, :]
        knew_ref[0, g] = k
        vnew_ref[0, g] = v
        kbuf = caches[g][:, 0]
        vbuf = caches[g][:, 1]
        lb = jnp.sum(kbuf * q[None], axis=-1, keepdims=True) * scale + bbuf_ref[g]
        ln = jnp.sum(k * q, axis=-1, keepdims=True) * scale + bnew_ref[g]
        m = jnp.maximum(jnp.max(lb, axis=0), ln)
        pb = jnp.exp(lb - m[None])
        pn = jnp.exp(ln - m)
        s = jnp.sum(pb, axis=0) + pn
        outs.append((jnp.sum(pb * vbuf, axis=0) + pn * v) / s)
        lses.append(m + jnp.log(s))
    mx = jnp.maximum(jnp.maximum(lses[0], lses[1]), lses[2])
    es = [jnp.exp(l - mx) for l in lses]
    att_ref[0] = (es[0] * outs[0] + es[1] * outs[1] + es[2] * outs[2]) / (es[0] + es[1] + es[2])


def _attn_sample(qkv, caches, bias_buf, bias_new):
    nb = qkv.shape[0]
    hd = ATT_HEAD_DIM
    in_specs = [pl.BlockSpec((1, 3 * ATT_HEADS, hd), lambda b: (b, 0, 0))]
    for _ in caches:
        in_specs.append(pl.BlockSpec((None, ATT_BLOCK, None, 2, ATT_HPP, hd), lambda b: (b, 0, 0, 0, 0, 0)))
    in_specs += [
        pl.BlockSpec((3, ATT_BLOCK, ATT_HPP, 1), lambda b: (0, 0, 0, 0)),
        pl.BlockSpec((3, ATT_HPP, 1), lambda b: (0, 0, 0)),
    ]
    return pl.pallas_call(
        _sattn_kernel,
        grid=(nb,),
        in_specs=in_specs,
        out_specs=[
            pl.BlockSpec((1, ATT_HPP, hd), lambda b: (b, 0, 0)),
            pl.BlockSpec((1, 3, ATT_HPP, hd), lambda b: (b, 0, 0, 0)),
            pl.BlockSpec((1, 3, ATT_HPP, hd), lambda b: (b, 0, 0, 0)),
        ],
        out_shape=[
            jax.ShapeDtypeStruct((nb, ATT_HPP, hd), F32),
            jax.ShapeDtypeStruct((nb, 3, ATT_HPP, hd), F32),
            jax.ShapeDtypeStruct((nb, 3, ATT_HPP, hd), F32),
        ],
        compiler_params=_params(("arbitrary",), 32),
    )(qkv, *caches, bias_buf, bias_new)


def _kvshift_kernel(x_ref, new_ref, o_ref, carry_ref):
    tw = x_ref.shape[0]

    @pl.when(pl.program_id(1) == 0)
    def _():
        carry_ref[...] = new_ref[0]

    o_ref[0:tw - 1] = x_ref[1:tw]
    o_ref[tw - 1] = carry_ref[...]
    carry_ref[...] = x_ref[0]


def _kv_shift(cache, new_row):
    _, nb, w, _, hpp, hd = cache.shape
    tw = min(w, 512)
    nblk = w // tw
    row = (2, hpp, hd)
    return pl.pallas_call(
        _kvshift_kernel,
        name="kv_shift",
        grid=(nb, nblk),
        in_specs=[
            pl.BlockSpec((None, None, tw, *row), lambda b, s: (0, b, nblk - 1 - s, 0, 0, 0)),
            pl.BlockSpec((None, 1, *row), lambda b, s: (b, 0, 0, 0, 0)),
        ],
        out_specs=pl.BlockSpec((None, None, tw, *row), lambda b, s: (0, b, nblk - 1 - s, 0, 0, 0)),
        out_shape=jax.ShapeDtypeStruct(cache.shape, cache.dtype),
        scratch_shapes=[pltpu.VMEM(row, F32)],
        compiler_params=_params(("arbitrary", "arbitrary"), 40),
    )(cache, new_row)


def _spool_kernel(x_ref, g_ref, sc_ref, sh_ref, g1_ref, prev_ref, w_ref, scale_ref, o_ref, u_ref):
    x = x_ref[...]
    u = _normmod(x, g_ref[...], sc_ref[...], sh_ref[...])
    u_ref[...] = u
    for gi, w in enumerate(POOL_WINDOWS):
        ch = slice(gi * POOL_CH, (gi + 1) * POOL_CH)
        s = u[:, ch]
        for k in range(1, w):
            s = s + prev_ref[POOL_MAX - 1 - k, :, ch]
        pooled = s / float(min(w, PAST_LEN + 1)) - u[:, ch]
        mix = _bdot(pooled, w_ref[gi]) * scale_ref[:, ch]
        o_ref[:, ch] = x[:, ch] + g1_ref[:, ch] * mix


def _pool_sample(x, g, sc, sh, g1, prev_t, pool_w, pool_scale):
    return pl.pallas_call(
        _spool_kernel,
        out_shape=[jax.ShapeDtypeStruct(x.shape, F32), jax.ShapeDtypeStruct(x.shape, F32)],
        compiler_params=pltpu.CompilerParams(vmem_limit_bytes=32 * MIB),
    )(x, g, sc, sh, g1, prev_t, pool_w, pool_scale)


def _rel_bucket(dist):
    max_exact = REL_BUCKETS // 2
    n = jnp.maximum(dist, 1).astype(F32)
    large = max_exact + (jnp.log(n / max_exact) / math.log(REL_MAX_DIST / max_exact)
                         * (REL_BUCKETS - max_exact)).astype(jnp.int32)
    large = jnp.minimum(large, REL_BUCKETS - 1)
    return jnp.where(dist < max_exact, dist, large)


def _prompt_bias_table(rel_bias):
    blk = ATT_BLOCK
    j = blk + jnp.arange(blk)[:, None] - jnp.arange(2 * blk)[None, :]
    tables = []
    for gi, dil in enumerate(DILATIONS):
        bucket = _rel_bucket(jnp.clip(j, 0, blk) * dil)[None]
        rb = rel_bias[:, gi * ATT_HPP:(gi + 1) * ATT_HPP].astype(F32)
        tbl = jnp.zeros((ATT_HPP, blk, 2 * blk), F32)
        for k in range(REL_BUCKETS):
            tbl = jnp.where(bucket == k, rb[k][:, None, None], tbl)
        tables.append(tbl)
    return jnp.concatenate(tables, axis=0)


def _sample_bias_table(rel_bias):
    blk = ATT_BLOCK
    bufs, news = [], []
    for gi, dil in enumerate(DILATIONS):
        tbl = rel_bias[:, gi * ATT_HPP:(gi + 1) * ATT_HPP][_rel_bucket(dil * jnp.arange(blk + 1))]
        bufs.append(tbl[blk:0:-1])
        news.append(tbl[0])
    return jnp.stack(bufs)[..., None].astype(F32), jnp.stack(news)[..., None].astype(F32)


def _pad_rows(a, rows, axis=0):
    pad = [(0, 0)] * a.ndim
    pad[axis] = (0, rows - a.shape[axis])
    return jnp.pad(a, pad)


def _mlp_tiles(t):
    return (1024, 512) if t >= 1024 else (t, 512)


def kernel(x_prompt, x_sample, state_ssm, state_conv, cache_kv_w128, cache_kv_w512, cache_kv_w2048, state_pool, c_prompt, c_sample, rel_bias, ada_w, ada_b, norm_mix_g, norm_mlp_g, in_proj_w, conv_w, conv_b, dt_bias, a_log, d_skip, ssd_norm_g, q_norm_g, k_norm_g, out_proj_w, pool_w, pool_scale, mlp_w1, mlp_w2):
    d = D_MODEL
    c_all = _pad_rows(jnp.concatenate([c_prompt, c_sample], axis=0), SAMPLE_ROWS)
    mod = _ada(c_all, ada_w, ada_b)

    def mods(layer):
        parts = [mod[layer, :, i * d:(i + 1) * d] for i in range(6)]
        prompt = [p[:N_PROMPT, None, :] for p in parts]
        sample = [_pad_rows(p[N_PROMPT:N_PROMPT + N_SAMPLE], SAMPLE_ROWS)[None] for p in parts]
        return prompt, sample

    w_in_t = jnp.transpose(in_proj_w[0])
    b_dt = jnp.pad(dt_bias[0], (0, DT_PAD - SSD_HEADS)).reshape(1, DT_PAD)
    a_log_pad = jnp.pad(a_log[0], (0, DT_PAD - SSD_HEADS)).reshape(1, DT_PAD)
    dskip_x = jnp.repeat(d_skip[0], SSD_HEAD_DIM).reshape(1, SSD_INNER)
    ssd_g = ssd_norm_g.reshape(1, SSD_INNER)
    qg, kg = q_norm_g.reshape(1, ATT_HEAD_DIM), k_norm_g.reshape(1, ATT_HEAD_DIM)
    cw, cb = conv_w[0], conv_b.reshape(1, CONV_DIM)
    g_mix = [norm_mix_g[l].reshape(1, d) for l in range(2)]
    g_mlp = [norm_mlp_g[l].reshape(1, d) for l in range(2)]
    pscale = pool_scale.reshape(1, d)
    pool_wb = pool_w[0].astype(BF16)

    (sh1, sc1, g1, sh2, sc2, g2), (sh1s, sc1s, g1s, sh2s, sc2s, g2s) = mods(0)
    (sh1b, sc1b, g1b, sh2b, sc2b, g2b), (sh1sb, sc1sb, g1sb, sh2sb, sc2sb, g2sb) = mods(1)

    u0 = _norm_mod(x_prompt, g_mix[0], sc1, sh1, 512)
    zx, dt = _inproj(u0, w_in_t, 0, ZX_COLS, SEQ, 512, dt_b=b_dt)
    (qkv,) = _inproj(u0, w_in_t, QKV_COL0, 3 * ATT_QKV, SEQ, 512, qk_gains=(qg, kg))
    p_conv = zx[:, SEQ - (SSD_CONV - 1):, SSD_INNER:][None]
    y_ssd, h_fin = _ssd_prompt(zx, dt, cw, cb, a_log_pad, dskip_x, ssd_g)
    p_ssm = h_fin.reshape(1, N_PROMPT, SSD_HEADS, SSD_HEAD_DIM, SSD_STATE)
    att, kv128, kv512, kv2048 = _attn_prompt(qkv, _prompt_bias_table(rel_bias))
    p_kv = [kv.reshape(1, N_PROMPT, w, 2, ATT_HPP, ATT_HEAD_DIM) for kv, w in zip((kv128, kv512, kv2048), WINDOWS)]
    x1 = _outproj(y_ssd, att, out_proj_w, x_prompt, g1, SEQ, 512)
    x2 = _mlp(x1, g_mlp[0], sc2, sh2, g2, mlp_w1, mlp_w2, 0, *_mlp_tiles(SEQ))
    x3, u_last = _pool_prompt(x2, g_mix[1], sc1b, sh1b, g1b, pool_wb, pscale)
    p_pool = u_last[:, 1:][None]
    y_prompt = _mlp(x3, g_mlp[1], sc2b, sh2b, g2b, mlp_w1, mlp_w2, 1, *_mlp_tiles(SEQ))

    rows = SAMPLE_ROWS
    xs0 = _pad_rows(x_sample.reshape(N_SAMPLE, d), rows)[None]
    u0s = _norm_mod(xs0, g_mix[0], sc1s, sh1s, rows)
    zx_s, dt_s = _inproj(u0s, w_in_t, 0, ZX_COLS, rows, 512, dt_b=b_dt)
    (qkv_s,) = _inproj(u0s, w_in_t, QKV_COL0, 3 * ATT_QKV, rows, 512, qk_gains=(qg, kg))
    zx_s, dt_s = zx_s[0], dt_s[0]
    qkv_s = jnp.transpose(qkv_s[0, :, :N_SAMPLE], (1, 0, 2))
    xbc_new = zx_s[:N_SAMPLE, SSD_INNER:]
    s_conv = jnp.concatenate([state_conv[0][:, 1:], xbc_new[:, None, :]], axis=1)[None]
    conv_prev_t = _pad_rows(jnp.transpose(state_conv[0], (1, 0, 2)), rows, axis=1)
    xbc_s = _conv_sample(zx_s, conv_prev_t, cw, cb)
    xs_s = xbc_s[:N_SAMPLE, :SSD_INNER]
    e = SSD_HEADS_PER_GROUP
    b_s = jnp.repeat(xbc_s[:N_SAMPLE, SSD_INNER:SSD_INNER + SSD_GROUPS * SSD_STATE]
                     .reshape(N_SAMPLE, SSD_GROUPS, 1, SSD_STATE), e, axis=1)
    c_s = jnp.repeat(xbc_s[:N_SAMPLE, SSD_INNER + SSD_GROUPS * SSD_STATE:]
                     .reshape(N_SAMPLE, SSD_GROUPS, 1, SSD_STATE), e, axis=1)
    h_new, y_s = _ssd_step(state_ssm[0], xs_s.reshape(N_SAMPLE, SSD_HEADS, SSD_HEAD_DIM, 1),
                           dt_s[:N_SAMPLE, :SSD_HEADS].reshape(N_SAMPLE, SSD_HEADS, 1, 1), b_s, c_s,
                           a_log[0].reshape(SSD_HEADS, 1, 1))
    s_ssm = h_new[None]
    y_s = _gate_sample(_pad_rows(y_s.reshape(N_SAMPLE, SSD_INNER), rows), xbc_s[:, :SSD_INNER],
                       zx_s[:, :SSD_INNER], dskip_x, ssd_g)
    caches = (cache_kv_w128, cache_kv_w512, cache_kv_w2048)
    strided = [c.reshape(N_SAMPLE, ATT_BLOCK, dil, 2, ATT_HPP, ATT_HEAD_DIM) for c, dil in zip(caches, DILATIONS)]
    bias_buf, bias_new = _sample_bias_table(rel_bias)
    att_s, k_new, v_new = _attn_sample(qkv_s, strided, bias_buf, bias_new)
    new_rows = jnp.stack([k_new, v_new], axis=2)
    s_kv = [_kv_shift(c, new_rows[:, gi:gi + 1]) for gi, c in enumerate(caches)]
    att_s = _pad_rows(att_s.reshape(N_SAMPLE, ATT_OUT), rows)
    x1s = _outproj(y_s[None], att_s[None], out_proj_w, xs0, g1s, rows, 512)
    x2s = _mlp(x1s, g_mlp[0], sc2s, sh2s, g2s, mlp_w1, mlp_w2, 0, *_mlp_tiles(rows))
    pool_prev_t = _pad_rows(jnp.transpose(state_pool[0], (1, 0, 2)), rows, axis=1)
    x3s, u_s = _pool_sample(x2s[0], g_mix[1], sc1sb[0], sh1sb[0], g1sb[0], pool_prev_t, pool_wb, pscale)
    s_pool = jnp.concatenate([state_pool[0][:, 1:], u_s[:N_SAMPLE, None, :]], axis=1)[None]
    y_s2 = _mlp(x3s[None], g_mlp[1], sc2sb, sh2sb, g2sb, mlp_w1, mlp_w2, 1, *_mlp_tiles(rows))
    y_sample = y_s2[0, :N_SAMPLE].reshape(N_SAMPLE, 1, d)

    return (y_prompt, y_sample, p_ssm, p_conv, p_kv[0], p_kv[1], p_kv[2], p_pool,
            s_ssm, s_conv, s_kv[0], s_kv[1], s_kv[2], s_pool)
```

```python
import functools
import math

import jax
import jax.numpy as jnp
from jax import lax
from jax.experimental import pallas as pl
from jax.experimental.pallas import tpu as pltpu

F32 = jnp.float32
BF16 = jnp.bfloat16

D_MODEL = 2048
SEQ = 2048
N_PROMPT = 4
N_SAMPLE = 8
SAMPLE_ROWS = 16
PAST_LEN = 16384
SSD_HEADS = 32
SSD_HEAD_DIM = 64
SSD_INNER = SSD_HEADS * SSD_HEAD_DIM
SSD_GROUPS = 8
SSD_HEADS_PER_GROUP = SSD_HEADS // SSD_GROUPS
SSD_STATE = 128
SSD_CONV = 4
SSD_CHUNK = 128
CONV_DIM = SSD_INNER + 2 * SSD_GROUPS * SSD_STATE
ATT_HEAD_DIM = 128
WINDOWS = (128, 512, 2048)
DILATIONS = (1, 4, 16)
ATT_BLOCK = 128
ATT_UNIT_ROWS = 128
ATT_MAX_ROW_STRIDE = 4
ATT_HPP = 8
ATT_HEADS = ATT_HPP * len(WINDOWS)
ATT_QKV = ATT_HEADS * ATT_HEAD_DIM
ATT_OUT = ATT_HPP * ATT_HEAD_DIM
REL_BUCKETS = 32
REL_MAX_DIST = 2048
ZX_COLS = SSD_INNER + CONV_DIM
DT_COL0 = ZX_COLS
QKV_COL0 = ZX_COLS + SSD_HEADS
QKV_LANE_SHIFT = QKV_COL0 % 128
DT_PAD = 128
MIX_OUT = SSD_INNER + ATT_OUT
POOL_WINDOWS = (2, 4, 8, 16)
POOL_CH = D_MODEL // len(POOL_WINDOWS)
POOL_MAX = 16
D_FF = 4 * D_MODEL
EPS = 1e-6
NEG = -1e30
MIB = 1024 * 1024


def _params(semantics, vmem_mib):
    return pltpu.CompilerParams(dimension_semantics=semantics, vmem_limit_bytes=vmem_mib * MIB)


def _sigmoid(x):
    return 0.5 * jnp.tanh(0.5 * x) + 0.5


def _silu(x):
    return x * _sigmoid(x)


def _softplus(x):
    return jnp.maximum(x, 0.0) + jnp.log1p(jnp.exp(-jnp.abs(x)))


def _rms(x):
    return x * lax.rsqrt(jnp.mean(x * x, axis=-1, keepdims=True) + EPS)


def _normmod(x, g, sc, sh):
    return (_rms(x) * g) * (1.0 + sc) + sh


def _normmod_rows(u_ref, x_ref, g_ref, sc_ref, sh_ref):
    tm = x_ref.shape[1]
    rc = min(tm, 256)
    per_row = sc_ref.shape[1] != 1

    def body(i, carry):
        rows = pl.ds(pl.multiple_of(i * rc, rc), rc)
        sc = sc_ref[0, rows, :] if per_row else sc_ref[0]
        sh = sh_ref[0, rows, :] if per_row else sh_ref[0]
        u_ref[rows, :] = _normmod(x_ref[0, rows, :], g_ref[...], sc, sh).astype(BF16)
        return carry

    lax.fori_loop(0, tm // rc, body, 0)


def _bdot(a, b):
    return jnp.dot(a.astype(BF16), b.astype(BF16), preferred_element_type=F32)


def _split3(v):
    hi = v.astype(BF16)
    r = v - hi.astype(F32)
    mid = r.astype(BF16)
    lo = (r - mid.astype(F32)).astype(BF16)
    return hi, mid, lo


def _select_right(v, sel, pieces=3):
    parts = _split3(v)[:pieces]
    out = jnp.dot(parts[0], sel, preferred_element_type=F32)
    for p in parts[1:]:
        out = out + jnp.dot(p, sel, preferred_element_type=F32)
    return out


def _select_left(sel, v):
    hi, mid, lo = _split3(v)
    d = lambda a: jnp.dot(sel, a, preferred_element_type=F32)
    return d(hi) + d(mid) + d(lo)


def _ada_kernel(c_ref, w_ref, b_ref, o_ref):
    o_ref[0] = _bdot(_silu(c_ref[...]), w_ref[0]) + b_ref[0]


def _ada(c_all, ada_w, ada_b):
    depth, _, n = ada_w.shape
    tn = 1024
    return pl.pallas_call(
        _ada_kernel,
        name="ada_mod",
        grid=(depth, n // tn),
        in_specs=[
            pl.BlockSpec((SAMPLE_ROWS, D_MODEL), lambda l, j: (0, 0)),
            pl.BlockSpec((1, D_MODEL, tn), lambda l, j: (l, 0, j)),
            pl.BlockSpec((1, 1, tn), lambda l, j: (l, 0, j)),
        ],
        out_specs=pl.BlockSpec((1, SAMPLE_ROWS, tn), lambda l, j: (l, 0, j)),
        out_shape=jax.ShapeDtypeStruct((depth, SAMPLE_ROWS, n), F32),
        compiler_params=_params(("arbitrary", "arbitrary"), 40),
    )(c_all, ada_w, ada_b.reshape(depth, 1, n))


def _bdot_t(a, bt):
    return lax.dot_general(a.astype(BF16), bt.astype(BF16), (((1,), (1,)), ((), ())), preferred_element_type=F32)


def _norm_kernel(x_ref, g_ref, sc_ref, sh_ref, o_ref):
    o_ref[0] = _normmod(x_ref[0], g_ref[...], sc_ref[0], sh_ref[0]).astype(BF16)


def _norm_mod(x, g, sc, sh, tt):
    nb, t, d = x.shape
    r = sc.shape[1]
    mod_spec = pl.BlockSpec((1, r if r == 1 else tt, d), lambda b, i: (b, 0 if r == 1 else i, 0))
    return pl.pallas_call(
        _norm_kernel,
        name="norm_mod",
        grid=(nb, t // tt),
        in_specs=[pl.BlockSpec((1, tt, d), lambda b, i: (b, i, 0)), pl.BlockSpec((1, d), lambda b, i: (0, 0)),
                  mod_spec, mod_spec],
        out_specs=pl.BlockSpec((1, tt, d), lambda b, i: (b, i, 0)),
        out_shape=jax.ShapeDtypeStruct((nb, t, d), BF16),
        compiler_params=_params(("arbitrary", "arbitrary"), 40),
    )(x, g, sc, sh)


def _inproj_kernel(*refs, mode):
    if mode == "dt":
        u_ref, us_ref, w_ref, wdt_ref, dtb_ref, o_ref, dt_ref, os_ref, dts_ref, ue_ref = refs
    else:
        u_ref, us_ref, w_ref, qg_ref, kg_ref, o_ref, os_ref, ue_ref = refs
    j = pl.program_id(2)
    tm = u_ref.shape[1]
    first_seq = pl.program_id(0) == 0

    @pl.when(j == 0)
    def _():
        rc = 256

        def copy_rows(i, carry):
            rows = pl.ds(pl.multiple_of(i * rc, rc), rc)
            ue_ref[rows, :] = u_ref[0, rows, :]
            return carry

        lax.fori_loop(0, tm // rc, copy_rows, 0)

        @pl.when(first_seq)
        def _():
            ue_ref[tm:, :] = us_ref[0]

        if mode == "dt":
            dt_all = _softplus(_bdot_t(ue_ref[...], wdt_ref[...]) + dtb_ref[...])
            dt_ref[0] = dt_all[0:tm]

            @pl.when(first_seq)
            def _():
                dts_ref[0] = dt_all[tm:]

    acc = _bdot_t(ue_ref[...], w_ref[...])
    if mode == "dt":
        o_ref[0] = acc[0:tm]

        @pl.when(first_seq)
        def _():
            os_ref[j] = acc[tm:]
    else:
        heads = o_ref.shape[1]
        q_tiles = ATT_HEADS // heads

        def emit(head_fn):
            for hc in range(heads):
                val = head_fn(acc[:, hc * ATT_HEAD_DIM:(hc + 1) * ATT_HEAD_DIM])
                o_ref[0, hc] = val[0:tm]

                @pl.when(first_seq)
                def _():
                    os_ref[0, j * heads + hc] = val[tm:]

        @pl.when(j < 2 * q_tiles)
        def _():
            gain = jnp.where(j < q_tiles, qg_ref[...], kg_ref[...])
            emit(lambda a: _rms(a) * gain)

        @pl.when(j >= 2 * q_tiles)
        def _():
            emit(lambda a: a)


def _inproj(u, us, wt, row0, n_cols, tn, dt_b=None, qk_gains=None):
    nb, t, d = u.shape
    tm = t
    rows = us.shape[1]
    has_dt = dt_b is not None
    if row0 % tn == 0:
        w_spec = pl.BlockSpec((tn, d), lambda b, i, j: (row0 // tn + j, 0))
    else:
        assert row0 % 8 == 0
        w_spec = pl.BlockSpec((pl.Element(tn), pl.Element(d)),
                              lambda b, i, j: (pl.multiple_of(row0 + j * tn, 8), 0))
    in_specs = [pl.BlockSpec((1, tm, d), lambda b, i, j: (b, i, 0), pipeline_mode=pl.Buffered(1)),
                pl.BlockSpec((1, rows, d), lambda b, i, j: (0, 0, 0)), w_spec]
    args = [u, us, wt]
    n_tiles = n_cols // tn
    if has_dt:
        in_specs += [pl.BlockSpec((DT_PAD, d), lambda b, i, j: (DT_COL0 // DT_PAD, 0)),
                     pl.BlockSpec((1, DT_PAD), lambda b, i, j: (0, 0))]
        args += [wt, dt_b]
        out_specs = [pl.BlockSpec((1, tm, tn), lambda b, i, j: (b, i, j)),
                     pl.BlockSpec((1, tm, DT_PAD), lambda b, i, j: (b, i, 0)),
                     pl.BlockSpec((n_tiles, rows, tn), lambda b, i, j: (0, 0, 0)),
                     pl.BlockSpec((1, rows, DT_PAD), lambda b, i, j: (0, 0, 0))]
        out_shape = [jax.ShapeDtypeStruct((nb, t, n_cols), F32), jax.ShapeDtypeStruct((nb, t, DT_PAD), F32),
                     jax.ShapeDtypeStruct((n_tiles, rows, tn), F32), jax.ShapeDtypeStruct((1, rows, DT_PAD), F32)]
    else:
        hd = ATT_HEAD_DIM
        in_specs += [pl.BlockSpec((1, hd), lambda b, i, j: (0, 0))] * 2
        args += list(qk_gains)
        out_specs = [pl.BlockSpec((1, tn // hd, tm, hd), lambda b, i, j: (b, j, i, 0)),
                     pl.BlockSpec((1, n_cols // hd, rows, hd), lambda b, i, j: (0, 0, 0, 0))]
        out_shape = [jax.ShapeDtypeStruct((nb, n_cols // hd, t, hd), F32),
                     jax.ShapeDtypeStruct((1, n_cols // hd, rows, hd), F32)]
    return pl.pallas_call(
        functools.partial(_inproj_kernel, mode="dt" if has_dt else "qkv"),
        name="in_proj",
        grid=(nb, t // tm, n_tiles),
        in_specs=in_specs,
        out_specs=out_specs,
        out_shape=out_shape,
        scratch_shapes=[pltpu.VMEM((tm + rows, d), BF16)],
        compiler_params=_params(("arbitrary", "arbitrary", "arbitrary"), 52),
    )(*args)


def _ssd_kernel(z_ref, xraw_ref, bcraw_ref, dt_ref, cw_ref, cb_ref, alog_ref, dskip_ref, g_ref,
                y_ref, hout_ref, h_ref, yacc_ref, act_ref, carry_ref):
    q = SSD_CHUNK
    gw = SSD_HEADS_PER_GROUP * SSD_HEAD_DIM
    slab = 512
    chunk = pl.program_id(1)

    @pl.when(chunk == 0)
    def _():
        h_ref[...] = jnp.zeros(h_ref.shape, F32)
        carry_ref[...] = jnp.zeros(carry_ref.shape, F32)

    row8 = lax.broadcasted_iota(jnp.int32, (8, slab), 0)
    for s in range(CONV_DIM // slab):
        cs = slice(s * slab, (s + 1) * slab)
        src, off = (xraw_ref, s * slab) if s * slab < SSD_INNER else (bcraw_ref, s * slab - SSD_INNER)
        x = src[0, :, off:off + slab]
        w = cw_ref[:, cs]
        bias = cb_ref[:, cs]
        taps = [w[SSD_CONV - 1 - k:SSD_CONV - k] for k in range(SSD_CONV)]
        acc = bias + taps[0] * x
        for k in range(1, SSD_CONV):
            acc = acc + taps[k] * pltpu.roll(x, k, axis=0)
        act_ref[:, cs] = _silu(acc)
        x8 = x[0:8]
        carry = carry_ref[:, cs]
        acc8 = bias + taps[0] * x8
        for k in range(1, SSD_CONV):
            acc8 = acc8 + taps[k] * jnp.where(row8 < k, pltpu.roll(carry, k, axis=0), pltpu.roll(x8, k, axis=0))
        act_ref[0:8, cs] = _silu(acc8)
        carry_ref[:, cs] = x[q - 8:q]

    row = lax.broadcasted_iota(jnp.int32, (q, q), 0)
    col = lax.broadcasted_iota(jnp.int32, (q, q), 1)
    causal = row >= col
    tri = jnp.where(causal, 1.0, 0.0).astype(BF16)

    dt = dt_ref[0]
    a = dt * (-jnp.exp(alog_ref[...]))
    acum = _select_left(tri, a)
    acum_t = acum.T
    dt_t = dt.T
    last = acum[q - 1:q, :]
    decay_end = jnp.exp(last - acum) * dt
    eacum = jnp.exp(acum)
    chunk_decay = jnp.broadcast_to(jnp.exp(last), (8, DT_PAD))

    first_head_lanes = lax.broadcasted_iota(jnp.int32, (q, 2 * SSD_HEAD_DIM), 1) < SSD_HEAD_DIM
    lane_head = lax.broadcasted_iota(jnp.int32, (DT_PAD, gw), 1) // SSD_HEAD_DIM
    row_head = lax.broadcasted_iota(jnp.int32, (DT_PAD, gw), 0)
    b0 = SSD_INNER
    c0 = SSD_INNER + SSD_GROUPS * SSD_STATE
    ssq = jnp.zeros((q, 1), F32)

    for g in range(SSD_GROUPS):
        cs = slice(g * gw, (g + 1) * gw)
        expand = jnp.where(lane_head + g * SSD_HEADS_PER_GROUP == row_head, 1.0, 0.0).astype(BF16)
        decay_end_x = _select_right(decay_end, expand, pieces=2)
        eacum_x = _select_right(eacum, expand, pieces=2)
        chunk_decay_x = _select_right(chunk_decay, expand)[0:1]

        xs = act_ref[:, cs]
        bg_t = act_ref[:, b0 + g * SSD_STATE:b0 + (g + 1) * SSD_STATE].T.astype(BF16)
        cg = act_ref[:, c0 + g * SSD_STATE:c0 + (g + 1) * SSD_STATE].astype(BF16)
        cb = jnp.dot(cg, bg_t, preferred_element_type=F32)
        h_prev = h_ref[:, cs]
        y_g = jnp.dot(cg, h_prev.astype(BF16), preferred_element_type=F32) * eacum_x
        h_ref[:, cs] = h_prev * chunk_decay_x + jnp.dot(bg_t, (xs * decay_end_x).astype(BF16),
                                                        preferred_element_type=F32)
        xg = xs.astype(BF16)

        def y_diag(e):
            hd = g * SSD_HEADS_PER_GROUP + e
            seg = acum[:, hd:hd + 1] - acum_t[hd:hd + 1, :]
            lw = jnp.where(causal, jnp.exp(seg), 0.0) * dt_t[hd:hd + 1, :]
            pair_lanes = slice((e // 2) * 2 * SSD_HEAD_DIM, (e // 2 + 1) * 2 * SSD_HEAD_DIM)
            return jnp.dot((cb * lw).astype(BF16), xg[:, pair_lanes], preferred_element_type=F32)

        y_pairs = [jnp.where(first_head_lanes, y_diag(2 * p), y_diag(2 * p + 1))
                   for p in range(SSD_HEADS_PER_GROUP // 2)]
        y_g = y_g + jnp.concatenate(y_pairs, axis=1)
        y_g = (y_g + dskip_ref[:, cs] * xs) * _silu(z_ref[0, :, cs])
        yacc_ref[:, cs] = y_g
        ssq = ssq + jnp.sum(y_g * y_g, axis=-1, keepdims=True)

    inv_rms = lax.rsqrt(ssq * (1.0 / SSD_INNER) + EPS)
    y_ref[0] = (yacc_ref[...] * inv_rms * g_ref[...]).astype(BF16)

    @pl.when(chunk == pl.num_programs(1) - 1)
    def _():
        hout_ref[0] = h_ref[...].T


def _ssd_prompt(zx, dt, conv_w, conv_b, a_log_pad, dskip_x, norm_g):
    nb, t, _ = zx.shape
    q = SSD_CHUNK
    wide = SSD_INNER
    assert ZX_COLS == 3 * wide
    return pl.pallas_call(
        _ssd_kernel,
        name="ssd_chunk",
        grid=(nb, t // q),
        in_specs=[
            pl.BlockSpec((1, q, wide), lambda b, c: (b, c, 0)),
            pl.BlockSpec((1, q, wide), lambda b, c: (b, c, 1)),
            pl.BlockSpec((1, q, wide), lambda b, c: (b, c, 2)),
            pl.BlockSpec((1, q, DT_PAD), lambda b, c: (b, c, 0)),
            pl.BlockSpec((SSD_CONV, CONV_DIM), lambda b, c: (0, 0)),
            pl.BlockSpec((1, CONV_DIM), lambda b, c: (0, 0)),
            pl.BlockSpec((1, DT_PAD), lambda b, c: (0, 0)),
            pl.BlockSpec((1, SSD_INNER), lambda b, c: (0, 0)),
            pl.BlockSpec((1, SSD_INNER), lambda b, c: (0, 0)),
        ],
        out_specs=[
            pl.BlockSpec((1, q, SSD_INNER), lambda b, c: (b, c, 0)),
            pl.BlockSpec((1, SSD_INNER, SSD_STATE), lambda b, c: (b, 0, 0)),
        ],
        out_shape=[
            jax.ShapeDtypeStruct((nb, t, SSD_INNER), BF16),
            jax.ShapeDtypeStruct((nb, SSD_INNER, SSD_STATE), F32),
        ],
        scratch_shapes=[pltpu.VMEM((SSD_STATE, SSD_INNER), F32), pltpu.VMEM((q, SSD_INNER), F32),
                        pltpu.VMEM((q, CONV_DIM), F32), pltpu.VMEM((8, CONV_DIM), F32)],
        compiler_params=_params(("arbitrary", "arbitrary"), 40),
    )(zx, zx, zx, dt, conv_w, conv_b, a_log_pad, dskip_x, norm_g)


def _attn_kernel(q0, q1, q2, k0, k1, k2, v0, v1, v2, bias0, bias1, bias2,
                 att_ref, kv0_ref, kv1_ref, kv2_ref, qb_s, kb_s, vb_s, ok_s, os_, ls, tmp_s, ltmp_s, kv_sem):
    blk = ATT_BLOCK
    ur = ATT_UNIT_ROWS
    hd = ATT_HEAD_DIM
    scale = 1.0 / math.sqrt(hd)
    batch, slot = pl.program_id(0), pl.program_id(1)
    qrefs, krefs, vrefs, brefs = (q0, q1, q2), (k0, k1, k2), (v0, v1, v2), (bias0, bias1, bias2)
    kvrefs = (kv0_ref, kv1_ref, kv2_ref)
    nt = (((1,), (1,)), ((), ()))

    def cache_copies():
        copies = []
        for g, w in enumerate(WINDOWS):
            for part, src in enumerate((krefs[g], vrefs[g])):
                cols = pl.ds(pl.multiple_of((part * ATT_HPP + slot) * hd, hd), hd)
                copies.append(pltpu.make_async_copy(
                    src.at[0, 0, pl.ds(SEQ - w, w), :], kvrefs[g].at[batch, :, cols], kv_sem.at[2 * g + part]))
        return copies

    for cp in cache_copies():
        cp.start()

    def compute():
        row = lax.broadcasted_iota(jnp.int32, (blk, 2 * blk), 0)
        col = lax.broadcasted_iota(jnp.int32, (blk, 2 * blk), 1)
        ok_s[...] = jnp.where(jnp.where(col < blk, col - row, row - (col - blk)) >= 0, 1.0, 0.0)

        f = ATT_MAX_ROW_STRIDE
        seg = SEQ // f

        def strided(start, size, stride):
            return pl.ds(start, size) if stride == 1 else pl.ds(start, size, stride=stride)

        def regroup(src_ref, dil, emit):
            if dil <= f:
                sub = SEQ // dil
                ch = min(sub, 256)
                for r in range(dil):
                    for c in range(sub // ch):
                        emit(pl.ds(r * sub + c * ch, ch), src_ref[0, 0, strided(r + c * ch * dil, ch, dil), :])
            else:
                assert dil == f * f
                for r1 in range(f):
                    for c in range(seg // 256):
                        tmp_s[pl.ds(r1 * seg + c * 256, 256), :] = src_ref[0, 0, strided(r1 + c * 256 * f, 256, f), :]
                sub = seg // f
                for r1 in range(f):
                    for r2 in range(f):
                        emit(pl.ds((f * r2 + r1) * sub, sub), tmp_s[strided(r1 * seg + r2, sub, f), :])

        for g, dil in enumerate(DILATIONS):
            vb_s[g, :, hd:2 * hd] = jnp.ones((SEQ, hd), BF16)

            def put_q(rows, val, g=g):
                qb_s[g, rows, :] = val.astype(BF16)

            def put_k(rows, val, g=g):
                kb_s[g, rows, :] = val.astype(BF16)

            def put_v(rows, val, g=g):
                vb_s[g, rows, 0:hd] = val.astype(BF16)

            regroup(qrefs[g], dil, put_q)
            regroup(krefs[g], dil, put_k)
            regroup(vrefs[g], dil, put_v)

        for g, dil in enumerate(DILATIONS):
            nblk = SEQ // (blk * dil)
            for r in range(dil):
                for n in range(nblk):
                    base = (r * nblk + n) * blk
                    key0, col0 = (base - blk, 0) if n > 0 else (base, blk)
                    kcat = kb_s[g, key0:base + blk, :]
                    vcat = vb_s[g, key0:base + blk, :]
                    for half in range(blk // ur):
                        rs = slice(half * ur, (half + 1) * ur)
                        qb = qb_s[g, base + half * ur:base + (half + 1) * ur, :]
                        l = lax.dot_general(qb, kcat, nt, preferred_element_type=F32)
                        l = jnp.where(ok_s[rs, col0:2 * blk] > 0.5, l * scale + brefs[g][0, rs, col0:2 * blk], NEG)
                        m = jnp.max(l, axis=-1, keepdims=True)
                        p = jnp.exp(l - m).astype(BF16)
                        oe = jnp.dot(p, vcat, preferred_element_type=F32)
                        den = oe[:, hd:2 * hd]
                        s0 = n * blk + half * ur
                        if dil <= f:
                            dst = strided(r + s0 * dil, ur, dil)
                            os_[g, dst, :] = oe[:, 0:hd] / den
                            ls[g, dst, :] = m + jnp.log(den)
                        else:
                            dst = strided((r % f) * seg + f * s0 + r // f, ur, f)
                            tmp_s[dst, :] = oe[:, 0:hd] / den
                            ltmp_s[dst, :] = m + jnp.log(den)
            if dil > f:
                for r1 in range(f):
                    for c in range(seg // 256):
                        dst = strided(r1 + c * 256 * f, 256, f)
                        os_[g, dst, :] = tmp_s[pl.ds(r1 * seg + c * 256, 256), :]
                        ls[g, dst, :] = ltmp_s[pl.ds(r1 * seg + c * 256, 256), :]

        def mix_body(i, carry):
            rows = pl.ds(pl.multiple_of(i * blk, blk), blk)
            l0, l1, l2 = ls[0, rows, :], ls[1, rows, :], ls[2, rows, :]
            mx = jnp.maximum(jnp.maximum(l0, l1), l2)
            e0, e1, e2 = jnp.exp(l0 - mx), jnp.exp(l1 - mx), jnp.exp(l2 - mx)
            num = e0 * os_[0, rows, :] + e1 * os_[1, rows, :] + e2 * os_[2, rows, :]
            att_ref[0, rows, :] = (num / (e0 + e1 + e2)).astype(BF16)
            return carry
        lax.fori_loop(0, SEQ // blk, mix_body, 0)

    compute()
    for cp in cache_copies():
        cp.wait()


def _attn_prompt(qkv, bias):
    nb = qkv.shape[0]
    hd = ATT_HEAD_DIM

    def head_spec(part, g):
        return pl.BlockSpec((1, 1, SEQ, hd), lambda b, h: (b, part * ATT_HEADS + g * ATT_HPP + h, 0, 0))

    in_specs = [head_spec(part, g) for part in range(3) for g in range(3)]
    in_specs += [pl.BlockSpec((1, ATT_BLOCK, 2 * ATT_BLOCK), functools.partial(lambda b, h, g: (g * ATT_HPP + h, 0, 0), g=g))
                 for g in range(3)]
    out_specs = [pl.BlockSpec((1, SEQ, hd), lambda b, h: (b, 0, h))]
    out_shape = [jax.ShapeDtypeStruct((nb, SEQ, ATT_OUT), BF16)]
    for w in WINDOWS:
        out_specs.append(pl.BlockSpec(memory_space=pl.ANY))
        out_shape.append(jax.ShapeDtypeStruct((nb, w, 2 * ATT_OUT), F32))
    return pl.pallas_call(
        _attn_kernel,
        name="attn_prompt",
        grid=(nb, ATT_HPP),
        in_specs=in_specs,
        out_specs=out_specs,
        out_shape=out_shape,
        scratch_shapes=[
            pltpu.VMEM((3, SEQ, hd), BF16), pltpu.VMEM((3, SEQ, hd), BF16), pltpu.VMEM((3, SEQ, 2 * hd), BF16),
            pltpu.VMEM((ATT_BLOCK, 2 * ATT_BLOCK), F32),
            pltpu.VMEM((3, SEQ, hd), F32), pltpu.VMEM((3, SEQ, hd), F32),
            pltpu.VMEM((SEQ, hd), F32), pltpu.VMEM((SEQ, hd), F32),
            pltpu.SemaphoreType.DMA((2 * len(WINDOWS),)),
        ],
        compiler_params=_params(("arbitrary", "arbitrary"), 48),
    )(*([qkv] * 9), bias, bias, bias)


def _outproj_kernel(y_ref, a_ref, wy_ref, wa_ref, x_ref, g1_ref, o_ref):
    mix = _bdot(y_ref[0], wy_ref[0]) + _bdot(a_ref[0], wa_ref[0])
    o_ref[0] = x_ref[0] + g1_ref[0] * mix


def _outproj(y, att, w, x, g1, tm, tn):
    nb, t, d = x.shape
    r = g1.shape[1]
    return pl.pallas_call(
        _outproj_kernel,
        name="out_proj",
        grid=(nb, t // tm, d // tn),
        in_specs=[
            pl.BlockSpec((1, tm, SSD_INNER), lambda b, i, j: (b, i, 0), pipeline_mode=pl.Buffered(1)),
            pl.BlockSpec((1, tm, ATT_OUT), lambda b, i, j: (b, i, 0), pipeline_mode=pl.Buffered(1)),
            pl.BlockSpec((1, SSD_INNER, tn), lambda b, i, j: (0, 0, j)),
            pl.BlockSpec((1, ATT_OUT, tn), lambda b, i, j: (0, SSD_INNER // ATT_OUT, j)),
            pl.BlockSpec((1, tm, tn), lambda b, i, j: (b, i, j)),
            pl.BlockSpec((1, r if r == 1 else tm, tn), lambda b, i, j: (b, 0 if r == 1 else i, j)),
        ],
        out_specs=pl.BlockSpec((1, tm, tn), lambda b, i, j: (b, i, j)),
        out_shape=jax.ShapeDtypeStruct((nb, t, d), F32),
        compiler_params=_params(("arbitrary", "arbitrary", "arbitrary"), 48),
    )(y, att, w, w, x, g1)


def _mlp_kernel(x_ref, g_ref, sc_ref, sh_ref, g2_ref, xs_ref, scs_ref, shs_ref, g2s_ref, w1_ref, w2_ref,
                o_ref, os_ref, u_ref):
    f = pl.program_id(2)
    last = pl.num_programs(2) - 1
    tm = x_ref.shape[1]
    first_tile = jnp.logical_and(pl.program_id(0) == 0, pl.program_id(1) == 0)

    def partial_out():
        h = jnp.maximum(_bdot(u_ref[...], w1_ref[0]), 0.0)
        return _bdot(h * h, w2_ref[0])

    @pl.when(f == 0)
    def _():
        _normmod_rows(u_ref, x_ref, g_ref, sc_ref, sh_ref)

        @pl.when(first_tile)
        def _():
            u_ref[tm:, :] = _normmod(xs_ref[0], g_ref[...], scs_ref[0], shs_ref[0]).astype(BF16)

        part = partial_out()
        o_ref[0] = part[0:tm]

        @pl.when(first_tile)
        def _():
            os_ref[0] = part[tm:]

    @pl.when(jnp.logical_and(f > 0, f < last))
    def _():
        part = partial_out()
        o_ref[0] += part[0:tm]

        @pl.when(first_tile)
        def _():
            os_ref[0] += part[tm:]

    @pl.when(f == last)
    def _():
        part = partial_out()
        o_ref[0] = x_ref[0] + g2_ref[0] * (o_ref[0] + part[0:tm])

        @pl.when(first_tile)
        def _():
            os_ref[0] = xs_ref[0] + g2s_ref[0] * (os_ref[0] + part[tm:])


def _mlp(x, sc, sh, g2, xs, scs, shs, g2s, g, w1, w2, layer, tm, tf):
    nb, t, d = x.shape
    rows = xs.shape[1]
    mod_spec = pl.BlockSpec((1, 1, d), lambda b, i, f: (b, 0, 0))
    row_spec = pl.BlockSpec((1, rows, d), lambda b, i, f: (0, 0, 0))
    return pl.pallas_call(
        _mlp_kernel,
        name="mlp",
        grid=(nb, t // tm, D_FF // tf),
        in_specs=[
            pl.BlockSpec((1, tm, d), lambda b, i, f: (b, i, 0), pipeline_mode=pl.Buffered(1)),
            pl.BlockSpec((1, d), lambda b, i, f: (0, 0)),
            mod_spec, mod_spec, mod_spec,
            row_spec, row_spec, row_spec, row_spec,
            pl.BlockSpec((1, d, tf), lambda b, i, f: (layer, 0, f)),
            pl.BlockSpec((1, tf, d), lambda b, i, f: (layer, f, 0)),
        ],
        out_specs=[pl.BlockSpec((1, tm, d), lambda b, i, f: (b, i, 0), pipeline_mode=pl.Buffered(1)), row_spec],
        out_shape=[jax.ShapeDtypeStruct((nb, t, d), F32), jax.ShapeDtypeStruct((1, rows, d), F32)],
        scratch_shapes=[pltpu.VMEM((tm + rows, d), BF16)],
        compiler_params=_params(("arbitrary", "arbitrary", "arbitrary"), 56),
    )(x, g, sc, sh, g2, xs, scs, shs, g2s, w1, w2)


def _pool_kernel(x_ref, g_ref, sc_ref, sh_ref, g1_ref, w_ref, scale_ref, o_ref, ulast_ref, ext_ref, *, tt):
    halo = POOL_MAX
    ti = pl.program_id(1)

    @pl.when(ti == 0)
    def _():
        ext_ref[0:halo, :] = jnp.zeros((halo, D_MODEL), F32)

    x = x_ref[0]
    u = _normmod(x, g_ref[...], sc_ref[0], sh_ref[0])
    ext_ref[halo:halo + tt, :] = u
    pos = ti * tt + lax.broadcasted_iota(jnp.int32, (tt, 1), 0)
    for gi, w in enumerate(POOL_WINDOWS):
        ch = slice(gi * POOL_CH, (gi + 1) * POOL_CH)
        acc = ext_ref[:, ch]
        span = 1
        while span < w:
            acc = acc + pltpu.roll(acc, span, axis=0)
            span *= 2
        s = acc[halo:halo + tt]
        inv_cnt = 1.0 / jnp.minimum(w, pos + 1).astype(F32)
        pooled = s * inv_cnt - u[:, ch]
        mix = _bdot(pooled, w_ref[gi]) * scale_ref[:, ch]
        o_ref[0, :, ch] = x[:, ch] + g1_ref[0, :, ch] * mix
    ulast_ref[0] = u[tt - halo:tt]
    ext_ref[0:halo, :] = u[tt - halo:tt]


def _pool_prompt(x, g, sc, sh, g1, pool_w, pool_scale):
    nb, t, d = x.shape
    tt = 512
    mod_spec = pl.BlockSpec((1, 1, d), lambda b, i: (b, 0, 0))
    return pl.pallas_call(
        functools.partial(_pool_kernel, tt=tt),
        name="pool_mix",
        grid=(nb, t // tt),
        in_specs=[
            pl.BlockSpec((1, tt, d), lambda b, i: (b, i, 0)),
            pl.BlockSpec((1, d), lambda b, i: (0, 0)),
            mod_spec, mod_spec, mod_spec,
            pl.BlockSpec((len(POOL_WINDOWS), POOL_CH, POOL_CH), lambda b, i: (0, 0, 0)),
            pl.BlockSpec((1, d), lambda b, i: (0, 0)),
        ],
        out_specs=[
            pl.BlockSpec((1, tt, d), lambda b, i: (b, i, 0)),
            pl.BlockSpec((1, POOL_MAX, d), lambda b, i: (b, 0, 0)),
        ],
        out_shape=[jax.ShapeDtypeStruct((nb, t, d), F32), jax.ShapeDtypeStruct((nb, POOL_MAX, d), F32)],
        scratch_shapes=[pltpu.VMEM((tt + POOL_MAX, d), F32)],
        compiler_params=_params(("arbitrary", "arbitrary"), 48),
    )(x, g, sc, sh, g1, pool_w, pool_scale)


def _sconv_kernel(zx_ref, prev_ref, w_ref, b_ref, o_ref):
    w = w_ref[...]
    acc = b_ref[...] + w[SSD_CONV - 1:SSD_CONV] * zx_ref[:, SSD_INNER:ZX_COLS]
    for k in range(SSD_CONV - 1):
        acc = acc + w[k:k + 1] * prev_ref[k]
    o_ref[...] = _silu(acc)


def _conv_sample(zx, prev_t, conv_w, conv_b):
    return pl.pallas_call(
        _sconv_kernel,
        out_shape=jax.ShapeDtypeStruct((SAMPLE_ROWS, CONV_DIM), F32),
        compiler_params=pltpu.CompilerParams(vmem_limit_bytes=32 * MIB),
    )(zx, prev_t, conv_w, conv_b)


def _sstep_kernel(h0_ref, x_ref, dt_ref, b_ref, c_ref, alog_ref, h_ref, y_ref):
    dt = dt_ref[0]
    h = jnp.exp(dt * (-jnp.exp(alog_ref[...]))) * h0_ref[0] + (dt * x_ref[0]) * b_ref[0]
    h_ref[0] = h
    y_ref[0] = jnp.sum(h * c_ref[0], axis=-1, keepdims=True)


def _ssd_step(h0, x, dt, bh, ch, a_log):
    nb = h0.shape[0]
    hp = (SSD_HEADS, SSD_HEAD_DIM)
    return pl.pallas_call(
        _sstep_kernel,
        grid=(nb,),
        in_specs=[
            pl.BlockSpec((1, *hp, SSD_STATE), lambda b: (b, 0, 0, 0)),
            pl.BlockSpec((1, *hp, 1), lambda b: (b, 0, 0, 0)),
            pl.BlockSpec((1, SSD_HEADS, 1, 1), lambda b: (b, 0, 0, 0)),
            pl.BlockSpec((1, SSD_HEADS, 1, SSD_STATE), lambda b: (b, 0, 0, 0)),
            pl.BlockSpec((1, SSD_HEADS, 1, SSD_STATE), lambda b: (b, 0, 0, 0)),
            pl.BlockSpec((SSD_HEADS, 1, 1), lambda b: (0, 0, 0)),
        ],
        out_specs=[
            pl.BlockSpec((1, *hp, SSD_STATE), lambda b: (b, 0, 0, 0)),
            pl.BlockSpec((1, *hp, 1), lambda b: (b, 0, 0, 0)),
        ],
        out_shape=[jax.ShapeDtypeStruct((nb, *hp, SSD_STATE), F32), jax.ShapeDtypeStruct((nb, *hp, 1), F32)],
        compiler_params=_params(("arbitrary",), 32),
    )(h0, x, dt, bh, ch, a_log)


def _sgate_kernel(y_ref, xs_ref, z_ref, dskip_ref, g_ref, o_ref):
    y = (y_ref[...] + dskip_ref[...] * xs_ref[...]) * _silu(z_ref[...])
    o_ref[...] = _rms(y) * g_ref[...]


def _gate_sample(y, xs, z, dskip_x, norm_g):
    return pl.pallas_call(
        _sgate_kernel,
        out_shape=jax.ShapeDtypeStruct(y.shape, F32),
    )(y, xs, z, dskip_x, norm_g)


def _sattn_kernel(qkv_ref, c0_ref, c1_ref, c2_ref, bbuf_ref, bnew_ref, att_ref, knew_ref, vnew_ref):
    scale = 1.0 / math.sqrt(ATT_HEAD_DIM)
    caches = (c0_ref, c1_ref, c2_ref)
    outs, lses = [], []
    for g in range(len(WINDOWS)):
        hs = g * ATT_HPP
        q = qkv_ref[0, hs:hs + ATT_HPP, :]
        k = qkv_ref[0, ATT_HEADS + hs:ATT_HEADS + hs + ATT_HPP, :]
        v = qkv_ref[0, 2 * ATT_HEADS + hs:2 * ATT_HEADS + hs + ATT_HPP, :]
        knew_ref[0, g] = k
        vnew_ref[0, g] = v
        kbuf = caches[g][:, 0]
        vbuf = caches[g][:, 1]
        lb = jnp.sum(kbuf * q[None], axis=-1, keepdims=True) * scale + bbuf_ref[g]
        ln = jnp.sum(k * q, axis=-1, keepdims=True) * scale + bnew_ref[g]
        m = jnp.maximum(jnp.max(lb, axis=0), ln)
        pb = jnp.exp(lb - m[None])
        pn = jnp.exp(ln - m)
        s = jnp.sum(pb, axis=0) + pn
        outs.append((jnp.sum(pb * vbuf, axis=0) + pn * v) / s)
        lses.append(m + jnp.log(s))
    mx = jnp.maximum(jnp.maximum(lses[0], lses[1]), lses[2])
    es = [jnp.exp(l - mx) for l in lses]
    att_ref[0] = (es[0] * outs[0] + es[1] * outs[1] + es[2] * outs[2]) / (es[0] + es[1] + es[2])


def _attn_sample(qkv, caches, bias_buf, bias_new):
    nb = qkv.shape[0]
    hd = ATT_HEAD_DIM
    in_specs = [pl.BlockSpec((1, 3 * ATT_HEADS, hd), lambda b: (b, 0, 0))]
    for _ in caches:
        in_specs.append(pl.BlockSpec((None, ATT_BLOCK, None, 2, ATT_HPP, hd), lambda b: (b, 0, 0, 0, 0, 0)))
    in_specs += [
        pl.BlockSpec((3, ATT_BLOCK, ATT_HPP, 1), lambda b: (0, 0, 0, 0)),
        pl.BlockSpec((3, ATT_HPP, 1), lambda b: (0, 0, 0)),
    ]
    return pl.pallas_call(
        _sattn_kernel,
        grid=(nb,),
        in_specs=in_specs,
        out_specs=[
            pl.BlockSpec((1, ATT_HPP, hd), lambda b: (b, 0, 0)),
            pl.BlockSpec((1, 3, ATT_HPP, hd), lambda b: (b, 0, 0, 0)),
            pl.BlockSpec((1, 3, ATT_HPP, hd), lambda b: (b, 0, 0, 0)),
        ],
        out_shape=[
            jax.ShapeDtypeStruct((nb, ATT_HPP, hd), F32),
            jax.ShapeDtypeStruct((nb, 3, ATT_HPP, hd), F32),
            jax.ShapeDtypeStruct((nb, 3, ATT_HPP, hd), F32),
        ],
        compiler_params=_params(("arbitrary",), 32),
    )(qkv, *caches, bias_buf, bias_new)


def _kvshift_kernel(x_ref, new_ref, o_ref, carry_ref):
    tw = x_ref.shape[0]

    @pl.when(pl.program_id(1) == 0)
    def _():
        carry_ref[...] = new_ref[0]

    o_ref[0:tw - 1] = x_ref[1:tw]
    o_ref[tw - 1] = carry_ref[...]
    carry_ref[...] = x_ref[0]


def _kv_shift(cache, new_row):
    _, nb, w, _, hpp, hd = cache.shape
    tw = min(w, 512)
    nblk = w // tw
    row = (2, hpp, hd)
    return pl.pallas_call(
        _kvshift_kernel,
        name="kv_shift",
        grid=(nb, nblk),
        in_specs=[
            pl.BlockSpec((None, None, tw, *row), lambda b, s: (0, b, nblk - 1 - s, 0, 0, 0)),
            pl.BlockSpec((None, 1, *row), lambda b, s: (b, 0, 0, 0, 0)),
        ],
        out_specs=pl.BlockSpec((None, None, tw, *row), lambda b, s: (0, b, nblk - 1 - s, 0, 0, 0)),
        out_shape=jax.ShapeDtypeStruct(cache.shape, cache.dtype),
        scratch_shapes=[pltpu.VMEM(row, F32)],
        compiler_params=_params(("arbitrary", "arbitrary"), 40),
    )(cache, new_row)


def _spool_kernel(x_ref, g_ref, sc_ref, sh_ref, g1_ref, prev_ref, w_ref, scale_ref, o_ref, u_ref):
    x = x_ref[...]
    u = _normmod(x, g_ref[...], sc_ref[...], sh_ref[...])
    u_ref[...] = u
    for gi, w in enumerate(POOL_WINDOWS):
        ch = slice(gi * POOL_CH, (gi + 1) * POOL_CH)
        s = u[:, ch]
        for k in range(1, w):
            s = s + prev_ref[POOL_MAX - 1 - k, :, ch]
        pooled = s / float(min(w, PAST_LEN + 1)) - u[:, ch]
        mix = _bdot(pooled, w_ref[gi]) * scale_ref[:, ch]
        o_ref[:, ch] = x[:, ch] + g1_ref[:, ch] * mix


def _pool_sample(x, g, sc, sh, g1, prev_t, pool_w, pool_scale):
    return pl.pallas_call(
        _spool_kernel,
        out_shape=[jax.ShapeDtypeStruct(x.shape, F32), jax.ShapeDtypeStruct(x.shape, F32)],
        compiler_params=pltpu.CompilerParams(vmem_limit_bytes=32 * MIB),
    )(x, g, sc, sh, g1, prev_t, pool_w, pool_scale)


def _rel_bucket(dist):
    max_exact = REL_BUCKETS // 2
    n = jnp.maximum(dist, 1).astype(F32)
    large = max_exact + (jnp.log(n / max_exact) / math.log(REL_MAX_DIST / max_exact)
                         * (REL_BUCKETS - max_exact)).astype(jnp.int32)
    large = jnp.minimum(large, REL_BUCKETS - 1)
    return jnp.where(dist < max_exact, dist, large)


def _prompt_bias_table(rel_bias):
    blk = ATT_BLOCK
    j = blk + jnp.arange(blk)[:, None] - jnp.arange(2 * blk)[None, :]
    tables = []
    for gi, dil in enumerate(DILATIONS):
        bucket = _rel_bucket(jnp.clip(j, 0, blk) * dil)[None]
        rb = rel_bias[:, gi * ATT_HPP:(gi + 1) * ATT_HPP].astype(F32)
        tbl = jnp.zeros((ATT_HPP, blk, 2 * blk), F32)
        for k in range(REL_BUCKETS):
            tbl = jnp.where(bucket == k, rb[k][:, None, None], tbl)
        tables.append(tbl)
    return jnp.concatenate(tables, axis=0)


def _sample_bias_table(rel_bias):
    blk = ATT_BLOCK
    bufs, news = [], []
    for gi, dil in enumerate(DILATIONS):
        tbl = rel_bias[:, gi * ATT_HPP:(gi + 1) * ATT_HPP][_rel_bucket(dil * jnp.arange(blk + 1))]
        bufs.append(tbl[blk:0:-1])
        news.append(tbl[0])
    return jnp.stack(bufs)[..., None].astype(F32), jnp.stack(news)[..., None].astype(F32)


def _pad_rows(a, rows, axis=0):
    pad = [(0, 0)] * a.ndim
    pad[axis] = (0, rows - a.shape[axis])
    return jnp.pad(a, pad)


MLP_TILES = (1024, 512)


def kernel(x_prompt, x_sample, state_ssm, state_conv, cache_kv_w128, cache_kv_w512, cache_kv_w2048, state_pool, c_prompt, c_sample, rel_bias, ada_w, ada_b, norm_mix_g, norm_mlp_g, in_proj_w, conv_w, conv_b, dt_bias, a_log, d_skip, ssd_norm_g, q_norm_g, k_norm_g, out_proj_w, pool_w, pool_scale, mlp_w1, mlp_w2):
    d = D_MODEL
    c_all = _pad_rows(jnp.concatenate([c_prompt, c_sample], axis=0), SAMPLE_ROWS)
    mod = _ada(c_all, ada_w, ada_b)

    def mods(layer):
        parts = [mod[layer, :, i * d:(i + 1) * d] for i in range(6)]
        prompt = [p[:N_PROMPT, None, :] for p in parts]
        sample = [_pad_rows(p[N_PROMPT:N_PROMPT + N_SAMPLE], SAMPLE_ROWS)[None] for p in parts]
        return prompt, sample

    w_in_t = jnp.transpose(in_proj_w[0])
    b_dt = jnp.pad(dt_bias[0], (0, DT_PAD - SSD_HEADS)).reshape(1, DT_PAD)
    a_log_pad = jnp.pad(a_log[0], (0, DT_PAD - SSD_HEADS)).reshape(1, DT_PAD)
    dskip_x = jnp.repeat(d_skip[0], SSD_HEAD_DIM).reshape(1, SSD_INNER)
    ssd_g = ssd_norm_g.reshape(1, SSD_INNER)
    qg, kg = q_norm_g.reshape(1, ATT_HEAD_DIM), k_norm_g.reshape(1, ATT_HEAD_DIM)
    cw, cb = conv_w[0], conv_b.reshape(1, CONV_DIM)
    g_mix = [norm_mix_g[l].reshape(1, d) for l in range(2)]
    g_mlp = [norm_mlp_g[l].reshape(1, d) for l in range(2)]
    pscale = pool_scale.reshape(1, d)
    pool_wb = pool_w[0].astype(BF16)

    (sh1, sc1, g1, sh2, sc2, g2), (sh1s, sc1s, g1s, sh2s, sc2s, g2s) = mods(0)
    (sh1b, sc1b, g1b, sh2b, sc2b, g2b), (sh1sb, sc1sb, g1sb, sh2sb, sc2sb, g2sb) = mods(1)

    rows = SAMPLE_ROWS
    xs0 = _pad_rows(x_sample.reshape(N_SAMPLE, d), rows)[None]
    u0 = _norm_mod(x_prompt, g_mix[0], sc1, sh1, 512)
    u0s = _norm_mod(xs0, g_mix[0], sc1s, sh1s, rows)
    zx, dt, zx_s, dt_s = _inproj(u0, u0s, w_in_t, 0, ZX_COLS, 512, dt_b=b_dt)
    qkv, qkv_s = _inproj(u0, u0s, w_in_t, QKV_COL0, 3 * ATT_QKV, 512, qk_gains=(qg, kg))
    p_conv = zx[:, SEQ - (SSD_CONV - 1):, SSD_INNER:][None]
    y_ssd, h_fin = _ssd_prompt(zx, dt, cw, cb, a_log_pad, dskip_x, ssd_g)
    p_ssm = h_fin.reshape(1, N_PROMPT, SSD_HEADS, SSD_HEAD_DIM, SSD_STATE)
    att, kv128, kv512, kv2048 = _attn_prompt(qkv, _prompt_bias_table(rel_bias))
    p_kv = [kv.reshape(1, N_PROMPT, w, 2, ATT_HPP, ATT_HEAD_DIM) for kv, w in zip((kv128, kv512, kv2048), WINDOWS)]
    x1 = _outproj(y_ssd, att, out_proj_w, x_prompt, g1, SEQ, 512)

    zx_s = jnp.transpose(zx_s, (1, 0, 2)).reshape(rows, ZX_COLS)
    dt_s = dt_s[0]
    qkv_s = jnp.transpose(qkv_s[0, :, :N_SAMPLE], (1, 0, 2))
    xbc_new = zx_s[:N_SAMPLE, SSD_INNER:]
    s_conv = jnp.concatenate([state_conv[0][:, 1:], xbc_new[:, None, :]], axis=1)[None]
    conv_prev_t = _pad_rows(jnp.transpose(state_conv[0], (1, 0, 2)), rows, axis=1)
    xbc_s = _conv_sample(zx_s, conv_prev_t, cw, cb)
    xs_s = xbc_s[:N_SAMPLE, :SSD_INNER]
    e = SSD_HEADS_PER_GROUP
    b_s = jnp.repeat(xbc_s[:N_SAMPLE, SSD_INNER:SSD_INNER + SSD_GROUPS * SSD_STATE]
                     .reshape(N_SAMPLE, SSD_GROUPS, 1, SSD_STATE), e, axis=1)
    c_s = jnp.repeat(xbc_s[:N_SAMPLE, SSD_INNER + SSD_GROUPS * SSD_STATE:]
                     .reshape(N_SAMPLE, SSD_GROUPS, 1, SSD_STATE), e, axis=1)
    h_new, y_s = _ssd_step(state_ssm[0], xs_s.reshape(N_SAMPLE, SSD_HEADS, SSD_HEAD_DIM, 1),
                           dt_s[:N_SAMPLE, :SSD_HEADS].reshape(N_SAMPLE, SSD_HEADS, 1, 1), b_s, c_s,
                           a_log[0].reshape(SSD_HEADS, 1, 1))
    s_ssm = h_new[None]
    y_s = _gate_sample(_pad_rows(y_s.reshape(N_SAMPLE, SSD_INNER), rows), xbc_s[:, :SSD_INNER],
                       zx_s[:, :SSD_INNER], dskip_x, ssd_g)
    caches = (cache_kv_w128, cache_kv_w512, cache_kv_w2048)
    strided = [c.reshape(N_SAMPLE, ATT_BLOCK, dil, 2, ATT_HPP, ATT_HEAD_DIM) for c, dil in zip(caches, DILATIONS)]
    bias_buf, bias_new = _sample_bias_table(rel_bias)
    att_s, k_new, v_new = _attn_sample(qkv_s, strided, bias_buf, bias_new)
    new_rows = jnp.stack([k_new, v_new], axis=2)
    s_kv = [_kv_shift(c, new_rows[:, gi:gi + 1]) for gi, c in enumerate(caches)]
    att_s = _pad_rows(att_s.reshape(N_SAMPLE, ATT_OUT), rows)
    x1s = _outproj(y_s[None], att_s[None], out_proj_w, xs0, g1s, rows, 512)

    x2, x2s = _mlp(x1, sc2, sh2, g2, x1s, sc2s, sh2s, g2s, g_mlp[0], mlp_w1, mlp_w2, 0, *MLP_TILES)
    x3, u_last = _pool_prompt(x2, g_mix[1], sc1b, sh1b, g1b, pool_wb, pscale)
    p_pool = u_last[:, 1:][None]
    pool_prev_t = _pad_rows(jnp.transpose(state_pool[0], (1, 0, 2)), rows, axis=1)
    x3s, u_s = _pool_sample(x2s[0], g_mix[1], sc1sb[0], sh1sb[0], g1sb[0], pool_prev_t, pool_wb, pscale)
    s_pool = jnp.concatenate([state_pool[0][:, 1:], u_s[:N_SAMPLE, None, :]], axis=1)[None]
    y_prompt, y_s2 = _mlp(x3, sc2b, sh2b, g2b, x3s[None], sc2sb, sh2sb, g2sb, g_mlp[1], mlp_w1, mlp_w2, 1, *MLP_TILES)
    y_sample = y_s2[0, :N_SAMPLE].reshape(N_SAMPLE, 1, d)

    return (y_prompt, y_sample, p_ssm, p_conv, p_kv[0], p_kv[1], p_kv[2], p_pool,
            s_ssm, s_conv, s_kv[0], s_kv[1], s_kv[2], s_pool)
```

```python
import functools
import math

import jax
import jax.numpy as jnp
from jax import lax
from jax.experimental import pallas as pl
from jax.experimental.pallas import tpu as pltpu

F32 = jnp.float32
BF16 = jnp.bfloat16

D_MODEL = 2048
SEQ = 2048
N_PROMPT = 4
N_SAMPLE = 8
SAMPLE_ROWS = 16
PAST_LEN = 16384
SSD_HEADS = 32
SSD_HEAD_DIM = 64
SSD_INNER = SSD_HEADS * SSD_HEAD_DIM
SSD_GROUPS = 8
SSD_HEADS_PER_GROUP = SSD_HEADS // SSD_GROUPS
SSD_STATE = 128
SSD_CONV = 4
SSD_CHUNK = 128
CONV_DIM = SSD_INNER + 2 * SSD_GROUPS * SSD_STATE
ATT_HEAD_DIM = 128
WINDOWS = (128, 512, 2048)
DILATIONS = (1, 4, 16)
ATT_BLOCK = 128
ATT_UNIT_ROWS = 128
ATT_MAX_ROW_STRIDE = 4
ATT_HPP = 8
ATT_HEADS = ATT_HPP * len(WINDOWS)
ATT_QKV = ATT_HEADS * ATT_HEAD_DIM
ATT_OUT = ATT_HPP * ATT_HEAD_DIM
REL_BUCKETS = 32
REL_MAX_DIST = 2048
ZX_COLS = SSD_INNER + CONV_DIM
DT_COL0 = ZX_COLS
QKV_COL0 = ZX_COLS + SSD_HEADS
QKV_LANE_SHIFT = QKV_COL0 % 128
DT_PAD = 128
MIX_OUT = SSD_INNER + ATT_OUT
POOL_WINDOWS = (2, 4, 8, 16)
POOL_CH = D_MODEL // len(POOL_WINDOWS)
POOL_MAX = 16
D_FF = 4 * D_MODEL
EPS = 1e-6
NEG = -1e30
MIB = 1024 * 1024


def _params(semantics, vmem_mib):
    return pltpu.CompilerParams(dimension_semantics=semantics, vmem_limit_bytes=vmem_mib * MIB)


def _sigmoid(x):
    return 0.5 * jnp.tanh(0.5 * x) + 0.5


def _silu(x):
    return x * _sigmoid(x)


def _softplus(x):
    return jnp.maximum(x, 0.0) + jnp.log1p(jnp.exp(-jnp.abs(x)))


def _rms(x):
    return x * lax.rsqrt(jnp.mean(x * x, axis=-1, keepdims=True) + EPS)


def _normmod(x, g, sc, sh):
    return (_rms(x) * g) * (1.0 + sc) + sh


def _normmod_rows(u_ref, x_ref, g_ref, sc_ref, sh_ref):
    tm = x_ref.shape[1]
    rc = min(tm, 256)
    per_row = sc_ref.shape[1] != 1

    def body(i, carry):
        rows = pl.ds(pl.multiple_of(i * rc, rc), rc)
        sc = sc_ref[0, rows, :] if per_row else sc_ref[0]
        sh = sh_ref[0, rows, :] if per_row else sh_ref[0]
        u_ref[rows, :] = _normmod(x_ref[0, rows, :], g_ref[...], sc, sh).astype(BF16)
        return carry

    lax.fori_loop(0, tm // rc, body, 0)


def _bdot(a, b):
    return jnp.dot(a.astype(BF16), b.astype(BF16), preferred_element_type=F32)


def _split3(v):
    hi = v.astype(BF16)
    r = v - hi.astype(F32)
    mid = r.astype(BF16)
    lo = (r - mid.astype(F32)).astype(BF16)
    return hi, mid, lo


def _select_right(v, sel, pieces=3):
    parts = _split3(v)[:pieces]
    out = jnp.dot(parts[0], sel, preferred_element_type=F32)
    for p in parts[1:]:
        out = out + jnp.dot(p, sel, preferred_element_type=F32)
    return out


def _select_left(sel, v):
    hi, mid, lo = _split3(v)
    d = lambda a: jnp.dot(sel, a, preferred_element_type=F32)
    return d(hi) + d(mid) + d(lo)


def _ada_kernel(c_ref, w_ref, b_ref, o_ref):
    o_ref[0] = _bdot(_silu(c_ref[...]), w_ref[0]) + b_ref[0]


def _ada(c_all, ada_w, ada_b):
    depth, _, n = ada_w.shape
    tn = 1024
    return pl.pallas_call(
        _ada_kernel,
        name="ada_mod",
        grid=(depth, n // tn),
        in_specs=[
            pl.BlockSpec((SAMPLE_ROWS, D_MODEL), lambda l, j: (0, 0)),
            pl.BlockSpec((1, D_MODEL, tn), lambda l, j: (l, 0, j)),
            pl.BlockSpec((1, 1, tn), lambda l, j: (l, 0, j)),
        ],
        out_specs=pl.BlockSpec((1, SAMPLE_ROWS, tn), lambda l, j: (l, 0, j)),
        out_shape=jax.ShapeDtypeStruct((depth, SAMPLE_ROWS, n), F32),
        compiler_params=_params(("arbitrary", "arbitrary"), 40),
    )(c_all, ada_w, ada_b.reshape(depth, 1, n))


def _bdot_t(a, bt):
    return lax.dot_general(a.astype(BF16), bt.astype(BF16), (((1,), (1,)), ((), ())), preferred_element_type=F32)


def _norm_kernel(x_ref, g_ref, sc_ref, sh_ref, o_ref):
    o_ref[0] = _normmod(x_ref[0], g_ref[...], sc_ref[0], sh_ref[0]).astype(BF16)


def _norm_mod(x, g, sc, sh, tt):
    nb, t, d = x.shape
    r = sc.shape[1]
    mod_spec = pl.BlockSpec((1, r if r == 1 else tt, d), lambda b, i: (b, 0 if r == 1 else i, 0))
    return pl.pallas_call(
        _norm_kernel,
        name="norm_mod",
        grid=(nb, t // tt),
        in_specs=[pl.BlockSpec((1, tt, d), lambda b, i: (b, i, 0)), pl.BlockSpec((1, d), lambda b, i: (0, 0)),
                  mod_spec, mod_spec],
        out_specs=pl.BlockSpec((1, tt, d), lambda b, i: (b, i, 0)),
        out_shape=jax.ShapeDtypeStruct((nb, t, d), BF16),
        compiler_params=_params(("arbitrary", "arbitrary"), 40),
    )(x, g, sc, sh)


def _inproj_kernel(*refs, mode):
    if mode == "dt":
        u_ref, us_ref, w_ref, wdt_ref, dtb_ref, o_ref, dt_ref, os_ref, dts_ref, ue_ref = refs
    else:
        u_ref, us_ref, w_ref, qg_ref, kg_ref, o_ref, os_ref, ue_ref = refs
    j = pl.program_id(2)
    tm = u_ref.shape[1]
    first_seq = pl.program_id(0) == 0

    @pl.when(j == 0)
    def _():
        rc = 256

        def copy_rows(i, carry):
            rows = pl.ds(pl.multiple_of(i * rc, rc), rc)
            ue_ref[rows, :] = u_ref[0, rows, :]
            return carry

        lax.fori_loop(0, tm // rc, copy_rows, 0)

        @pl.when(first_seq)
        def _():
            ue_ref[tm:, :] = us_ref[0]

        if mode == "dt":
            dt_all = _softplus(_bdot_t(ue_ref[...], wdt_ref[...]) + dtb_ref[...])
            dt_ref[0] = dt_all[0:tm]

            @pl.when(first_seq)
            def _():
                dts_ref[0] = dt_all[tm:]

    acc = _bdot_t(ue_ref[...], w_ref[...])
    if mode == "dt":
        o_ref[0] = acc[0:tm]

        @pl.when(first_seq)
        def _():
            os_ref[j] = acc[tm:]
    else:
        heads = o_ref.shape[1]
        q_tiles = ATT_HEADS // heads

        def emit(head_fn):
            for hc in range(heads):
                val = head_fn(acc[:, hc * ATT_HEAD_DIM:(hc + 1) * ATT_HEAD_DIM])
                o_ref[0, hc] = val[0:tm]

                @pl.when(first_seq)
                def _():
                    os_ref[0, j * heads + hc] = val[tm:]

        @pl.when(j < 2 * q_tiles)
        def _():
            gain = jnp.where(j < q_tiles, qg_ref[...], kg_ref[...])
            emit(lambda a: _rms(a) * gain)

        @pl.when(j >= 2 * q_tiles)
        def _():
            emit(lambda a: a)


def _inproj(u, us, wt, row0, n_cols, tn, dt_b=None, qk_gains=None):
    nb, t, d = u.shape
    tm = t
    rows = us.shape[1]
    has_dt = dt_b is not None
    if row0 % tn == 0:
        w_spec = pl.BlockSpec((tn, d), lambda b, i, j: (row0 // tn + j, 0))
    else:
        assert row0 % 8 == 0
        w_spec = pl.BlockSpec((pl.Element(tn), pl.Element(d)),
                              lambda b, i, j: (pl.multiple_of(row0 + j * tn, 8), 0))
    in_specs = [pl.BlockSpec((1, tm, d), lambda b, i, j: (b, i, 0), pipeline_mode=pl.Buffered(1)),
                pl.BlockSpec((1, rows, d), lambda b, i, j: (0, 0, 0)), w_spec]
    args = [u, us, wt]
    n_tiles = n_cols // tn
    if has_dt:
        in_specs += [pl.BlockSpec((DT_PAD, d), lambda b, i, j: (DT_COL0 // DT_PAD, 0)),
                     pl.BlockSpec((1, DT_PAD), lambda b, i, j: (0, 0))]
        args += [wt, dt_b]
        out_specs = [pl.BlockSpec((1, tm, tn), lambda b, i, j: (b, i, j)),
                     pl.BlockSpec((1, tm, DT_PAD), lambda b, i, j: (b, i, 0)),
                     pl.BlockSpec((n_tiles, rows, tn), lambda b, i, j: (0, 0, 0)),
                     pl.BlockSpec((1, rows, DT_PAD), lambda b, i, j: (0, 0, 0))]
        out_shape = [jax.ShapeDtypeStruct((nb, t, n_cols), F32), jax.ShapeDtypeStruct((nb, t, DT_PAD), F32),
                     jax.ShapeDtypeStruct((n_tiles, rows, tn), F32), jax.ShapeDtypeStruct((1, rows, DT_PAD), F32)]
    else:
        hd = ATT_HEAD_DIM
        in_specs += [pl.BlockSpec((1, hd), lambda b, i, j: (0, 0))] * 2
        args += list(qk_gains)
        out_specs = [pl.BlockSpec((1, tn // hd, tm, hd), lambda b, i, j: (b, j, i, 0)),
                     pl.BlockSpec((1, n_cols // hd, rows, hd), lambda b, i, j: (0, 0, 0, 0))]
        out_shape = [jax.ShapeDtypeStruct((nb, n_cols // hd, t, hd), F32),
                     jax.ShapeDtypeStruct((1, n_cols // hd, rows, hd), F32)]
    return pl.pallas_call(
        functools.partial(_inproj_kernel, mode="dt" if has_dt else "qkv"),
        name="in_proj",
        grid=(nb, t // tm, n_tiles),
        in_specs=in_specs,
        out_specs=out_specs,
        out_shape=out_shape,
        scratch_shapes=[pltpu.VMEM((tm + rows, d), BF16)],
        compiler_params=_params(("arbitrary", "arbitrary", "arbitrary"), 52),
    )(*args)


def _ssd_kernel(z_ref, xraw_ref, bcraw_ref, dt_ref, cw_ref, cb_ref, alog_ref, dskip_ref, g_ref,
                c0_ref, c1_ref, c2_ref, n0_ref, n1_ref, n2_ref,
                y_ref, hout_ref, o0_ref, o1_ref, o2_ref,
                h_ref, yacc_ref, act_ref, carry_ref, k0_ref, k1_ref, k2_ref, *, shift_blocks):
    q = SSD_CHUNK
    gw = SSD_HEADS_PER_GROUP * SSD_HEAD_DIM
    slab = 512
    chunk = pl.program_id(1)

    step = pl.program_id(0) * pl.num_programs(1) + chunk
    for x_ref, new_ref, o_ref, k_ref, per in zip((c0_ref, c1_ref, c2_ref), (n0_ref, n1_ref, n2_ref),
                                                 (o0_ref, o1_ref, o2_ref), (k0_ref, k1_ref, k2_ref), shift_blocks):
        def shift_block(x_ref=x_ref, new_ref=new_ref, o_ref=o_ref, k_ref=k_ref, per=per):
            tw = x_ref.shape[0]

            @pl.when(step % per == 0)
            def _():
                k_ref[...] = new_ref[step // per]

            o_ref[0:tw - 1] = x_ref[1:tw]
            o_ref[tw - 1] = k_ref[...]
            k_ref[...] = x_ref[0]

        if N_SAMPLE * per == N_PROMPT * (SEQ // SSD_CHUNK):
            shift_block()
        else:
            pl.when(step < N_SAMPLE * per)(shift_block)

    @pl.when(chunk == 0)
    def _():
        h_ref[...] = jnp.zeros(h_ref.shape, F32)
        carry_ref[...] = jnp.zeros(carry_ref.shape, F32)

    row8 = lax.broadcasted_iota(jnp.int32, (8, slab), 0)
    for s in range(CONV_DIM // slab):
        cs = slice(s * slab, (s + 1) * slab)
        src, off = (xraw_ref, s * slab) if s * slab < SSD_INNER else (bcraw_ref, s * slab - SSD_INNER)
        x = src[0, :, off:off + slab]
        w = cw_ref[:, cs]
        bias = cb_ref[:, cs]
        taps = [w[SSD_CONV - 1 - k:SSD_CONV - k] for k in range(SSD_CONV)]
        acc = bias + taps[0] * x
        for k in range(1, SSD_CONV):
            acc = acc + taps[k] * pltpu.roll(x, k, axis=0)
        act_ref[:, cs] = _silu(acc)
        x8 = x[0:8]
        carry = carry_ref[:, cs]
        acc8 = bias + taps[0] * x8
        for k in range(1, SSD_CONV):
            acc8 = acc8 + taps[k] * jnp.where(row8 < k, pltpu.roll(carry, k, axis=0), pltpu.roll(x8, k, axis=0))
        act_ref[0:8, cs] = _silu(acc8)
        carry_ref[:, cs] = x[q - 8:q]

    row = lax.broadcasted_iota(jnp.int32, (q, q), 0)
    col = lax.broadcasted_iota(jnp.int32, (q, q), 1)
    causal = row >= col
    tri = jnp.where(causal, 1.0, 0.0).astype(BF16)

    dt = dt_ref[0]
    a = dt * (-jnp.exp(alog_ref[...]))
    acum = _select_left(tri, a)
    acum_t = acum.T
    dt_t = dt.T
    last = acum[q - 1:q, :]
    decay_end = jnp.exp(last - acum) * dt
    eacum = jnp.exp(acum)
    chunk_decay = jnp.broadcast_to(jnp.exp(last), (8, DT_PAD))

    first_head_lanes = lax.broadcasted_iota(jnp.int32, (q, 2 * SSD_HEAD_DIM), 1) < SSD_HEAD_DIM
    lane_head = lax.broadcasted_iota(jnp.int32, (DT_PAD, gw), 1) // SSD_HEAD_DIM
    row_head = lax.broadcasted_iota(jnp.int32, (DT_PAD, gw), 0)
    b0 = SSD_INNER
    c0 = SSD_INNER + SSD_GROUPS * SSD_STATE
    ssq = jnp.zeros((q, 1), F32)

    for g in range(SSD_GROUPS):
        cs = slice(g * gw, (g + 1) * gw)
        expand = jnp.where(lane_head + g * SSD_HEADS_PER_GROUP == row_head, 1.0, 0.0).astype(BF16)
        decay_end_x = _select_right(decay_end, expand, pieces=2)
        eacum_x = _select_right(eacum, expand, pieces=2)
        chunk_decay_x = _select_right(chunk_decay, expand)[0:1]

        xs = act_ref[:, cs]
        bg_t = act_ref[:, b0 + g * SSD_STATE:b0 + (g + 1) * SSD_STATE].T.astype(BF16)
        cg = act_ref[:, c0 + g * SSD_STATE:c0 + (g + 1) * SSD_STATE].astype(BF16)
        cb = jnp.dot(cg, bg_t, preferred_element_type=F32)
        h_prev = h_ref[:, cs]
        y_g = jnp.dot(cg, h_prev.astype(BF16), preferred_element_type=F32) * eacum_x
        h_ref[:, cs] = h_prev * chunk_decay_x + jnp.dot(bg_t, (xs * decay_end_x).astype(BF16),
                                                        preferred_element_type=F32)
        xg = xs.astype(BF16)

        def y_diag(e):
            hd = g * SSD_HEADS_PER_GROUP + e
            seg = acum[:, hd:hd + 1] - acum_t[hd:hd + 1, :]
            lw = jnp.where(causal, jnp.exp(seg), 0.0) * dt_t[hd:hd + 1, :]
            pair_lanes = slice((e // 2) * 2 * SSD_HEAD_DIM, (e // 2 + 1) * 2 * SSD_HEAD_DIM)
            return jnp.dot((cb * lw).astype(BF16), xg[:, pair_lanes], preferred_element_type=F32)

        y_pairs = [jnp.where(first_head_lanes, y_diag(2 * p), y_diag(2 * p + 1))
                   for p in range(SSD_HEADS_PER_GROUP // 2)]
        y_g = y_g + jnp.concatenate(y_pairs, axis=1)
        y_g = (y_g + dskip_ref[:, cs] * xs) * _silu(z_ref[0, :, cs])
        yacc_ref[:, cs] = y_g
        ssq = ssq + jnp.sum(y_g * y_g, axis=-1, keepdims=True)

    inv_rms = lax.rsqrt(ssq * (1.0 / SSD_INNER) + EPS)
    y_ref[0] = (yacc_ref[...] * inv_rms * g_ref[...]).astype(BF16)

    @pl.when(chunk == pl.num_programs(1) - 1)
    def _():
        hout_ref[0] = h_ref[...].T


def _ssd_prompt(zx, dt, conv_w, conv_b, a_log_pad, dskip_x, norm_g, caches, new_rows):
    nb, t, _ = zx.shape
    q = SSD_CHUNK
    n_chunks = t // q
    wide = SSD_INNER
    assert ZX_COLS == 3 * wide
    row = caches[0].shape[3:]
    cache_specs, shift_blocks = [], []
    for cache in caches:
        w = cache.shape[2]
        tw = min(w, 256)
        per = w // tw
        assert N_SAMPLE * per <= nb * n_chunks
        shift_blocks.append(per)

        def block_index(b, c, per=per):
            s = jnp.minimum(b * n_chunks + c, N_SAMPLE * per - 1)
            return (0, s // per, per - 1 - s % per, 0, 0, 0)

        cache_specs.append(pl.BlockSpec((None, None, tw, *row), block_index))
    new_specs = [pl.BlockSpec(n.shape, lambda b, c: (0, 0, 0, 0)) for n in new_rows]
    return pl.pallas_call(
        functools.partial(_ssd_kernel, shift_blocks=tuple(shift_blocks)),
        name="ssd_chunk",
        grid=(nb, n_chunks),
        in_specs=[
            pl.BlockSpec((1, q, wide), lambda b, c: (b, c, 0)),
            pl.BlockSpec((1, q, wide), lambda b, c: (b, c, 1)),
            pl.BlockSpec((1, q, wide), lambda b, c: (b, c, 2)),
            pl.BlockSpec((1, q, DT_PAD), lambda b, c: (b, c, 0)),
            pl.BlockSpec((SSD_CONV, CONV_DIM), lambda b, c: (0, 0)),
            pl.BlockSpec((1, CONV_DIM), lambda b, c: (0, 0)),
            pl.BlockSpec((1, DT_PAD), lambda b, c: (0, 0)),
            pl.BlockSpec((1, SSD_INNER), lambda b, c: (0, 0)),
            pl.BlockSpec((1, SSD_INNER), lambda b, c: (0, 0)),
            *cache_specs, *new_specs,
        ],
        out_specs=[
            pl.BlockSpec((1, q, SSD_INNER), lambda b, c: (b, c, 0)),
            pl.BlockSpec((1, SSD_INNER, SSD_STATE), lambda b, c: (b, 0, 0)),
            *cache_specs,
        ],
        out_shape=[
            jax.ShapeDtypeStruct((nb, t, SSD_INNER), BF16),
            jax.ShapeDtypeStruct((nb, SSD_INNER, SSD_STATE), F32),
            *[jax.ShapeDtypeStruct(c.shape, c.dtype) for c in caches],
        ],
        scratch_shapes=[pltpu.VMEM((SSD_STATE, SSD_INNER), F32), pltpu.VMEM((q, SSD_INNER), F32),
                        pltpu.VMEM((q, CONV_DIM), F32), pltpu.VMEM((8, CONV_DIM), F32),
                        *[pltpu.VMEM(row, F32) for _ in caches]],
        compiler_params=_params(("arbitrary", "arbitrary"), 48),
    )(zx, zx, zx, dt, conv_w, conv_b, a_log_pad, dskip_x, norm_g, *caches, *new_rows)


def _attn_kernel(q0, q1, q2, k0, k1, k2, v0, v1, v2, bias0, bias1, bias2,
                 att_ref, kv0_ref, kv1_ref, kv2_ref, qb_s, kb_s, vb_s, ok_s, os_, ls, tmp_s, ltmp_s, kv_sem):
    blk = ATT_BLOCK
    ur = ATT_UNIT_ROWS
    hd = ATT_HEAD_DIM
    scale = 1.0 / math.sqrt(hd)
    batch, slot = pl.program_id(0), pl.program_id(1)
    qrefs, krefs, vrefs, brefs = (q0, q1, q2), (k0, k1, k2), (v0, v1, v2), (bias0, bias1, bias2)
    kvrefs = (kv0_ref, kv1_ref, kv2_ref)
    nt = (((1,), (1,)), ((), ()))

    def cache_copies():
        copies = []
        for g, w in enumerate(WINDOWS):
            for part, src in enumerate((krefs[g], vrefs[g])):
                cols = pl.ds(pl.multiple_of((part * ATT_HPP + slot) * hd, hd), hd)
                copies.append(pltpu.make_async_copy(
                    src.at[0, 0, pl.ds(SEQ - w, w), :], kvrefs[g].at[batch, :, cols], kv_sem.at[2 * g + part]))
        return copies

    for cp in cache_copies():
        cp.start()

    def compute():
        row = lax.broadcasted_iota(jnp.int32, (blk, 2 * blk), 0)
        col = lax.broadcasted_iota(jnp.int32, (blk, 2 * blk), 1)
        ok_s[...] = jnp.where(jnp.where(col < blk, col - row, row - (col - blk)) >= 0, 1.0, 0.0)

        f = ATT_MAX_ROW_STRIDE
        seg = SEQ // f

        def strided(start, size, stride):
            return pl.ds(start, size) if stride == 1 else pl.ds(start, size, stride=stride)

        def regroup(src_ref, dil, emit):
            if dil <= f:
                sub = SEQ // dil
                ch = min(sub, 256)
                for r in range(dil):
                    for c in range(sub // ch):
                        emit(pl.ds(r * sub + c * ch, ch), src_ref[0, 0, strided(r + c * ch * dil, ch, dil), :])
            else:
                assert dil == f * f
                for r1 in range(f):
                    for c in range(seg // 256):
                        tmp_s[pl.ds(r1 * seg + c * 256, 256), :] = src_ref[0, 0, strided(r1 + c * 256 * f, 256, f), :]
                sub = seg // f
                for r1 in range(f):
                    for r2 in range(f):
                        emit(pl.ds((f * r2 + r1) * sub, sub), tmp_s[strided(r1 * seg + r2, sub, f), :])

        for g, dil in enumerate(DILATIONS):
            vb_s[g, :, hd:2 * hd] = jnp.ones((SEQ, hd), BF16)

            def put_q(rows, val, g=g):
                qb_s[g, rows, :] = val.astype(BF16)

            def put_k(rows, val, g=g):
                kb_s[g, rows, :] = val.astype(BF16)

            def put_v(rows, val, g=g):
                vb_s[g, rows, 0:hd] = val.astype(BF16)

            regroup(qrefs[g], dil, put_q)
            regroup(krefs[g], dil, put_k)
            regroup(vrefs[g], dil, put_v)

        for g, dil in enumerate(DILATIONS):
            nblk = SEQ // (blk * dil)
            for r in range(dil):
                for n in range(nblk):
                    base = (r * nblk + n) * blk
                    key0, col0 = (base - blk, 0) if n > 0 else (base, blk)
                    kcat = kb_s[g, key0:base + blk, :]
                    vcat = vb_s[g, key0:base + blk, :]
                    for half in range(blk // ur):
                        rs = slice(half * ur, (half + 1) * ur)
                        qb = qb_s[g, base + half * ur:base + (half + 1) * ur, :]
                        l = lax.dot_general(qb, kcat, nt, preferred_element_type=F32)
                        l = jnp.where(ok_s[rs, col0:2 * blk] > 0.5, l * scale + brefs[g][0, rs, col0:2 * blk], NEG)
                        m = jnp.max(l, axis=-1, keepdims=True)
                        p = jnp.exp(l - m).astype(BF16)
                        oe = jnp.dot(p, vcat, preferred_element_type=F32)
                        den = oe[:, hd:2 * hd]
                        s0 = n * blk + half * ur
                        if dil <= f:
                            dst = strided(r + s0 * dil, ur, dil)
                            os_[g, dst, :] = oe[:, 0:hd] / den
                            ls[g, dst, :] = m + jnp.log(den)
                        else:
                            dst = strided((r % f) * seg + f * s0 + r // f, ur, f)
                            tmp_s[dst, :] = oe[:, 0:hd] / den
                            ltmp_s[dst, :] = m + jnp.log(den)
            if dil > f:
                for r1 in range(f):
                    for c in range(seg // 256):
                        dst = strided(r1 + c * 256 * f, 256, f)
                        os_[g, dst, :] = tmp_s[pl.ds(r1 * seg + c * 256, 256), :]
                        ls[g, dst, :] = ltmp_s[pl.ds(r1 * seg + c * 256, 256), :]

        def mix_body(i, carry):
            rows = pl.ds(pl.multiple_of(i * blk, blk), blk)
            l0, l1, l2 = ls[0, rows, :], ls[1, rows, :], ls[2, rows, :]
            mx = jnp.maximum(jnp.maximum(l0, l1), l2)
            e0, e1, e2 = jnp.exp(l0 - mx), jnp.exp(l1 - mx), jnp.exp(l2 - mx)
            num = e0 * os_[0, rows, :] + e1 * os_[1, rows, :] + e2 * os_[2, rows, :]
            att_ref[0, rows, :] = (num / (e0 + e1 + e2)).astype(BF16)
            return carry
        lax.fori_loop(0, SEQ // blk, mix_body, 0)

    compute()
    for cp in cache_copies():
        cp.wait()


def _attn_prompt(qkv, bias):
    nb = qkv.shape[0]
    hd = ATT_HEAD_DIM

    def head_spec(part, g):
        return pl.BlockSpec((1, 1, SEQ, hd), lambda b, h: (b, part * ATT_HEADS + g * ATT_HPP + h, 0, 0))

    in_specs = [head_spec(part, g) for part in range(3) for g in range(3)]
    in_specs += [pl.BlockSpec((1, ATT_BLOCK, 2 * ATT_BLOCK), functools.partial(lambda b, h, g: (g * ATT_HPP + h, 0, 0), g=g))
                 for g in range(3)]
    out_specs = [pl.BlockSpec((1, SEQ, hd), lambda b, h: (b, 0, h))]
    out_shape = [jax.ShapeDtypeStruct((nb, SEQ, ATT_OUT), BF16)]
    for w in WINDOWS:
        out_specs.append(pl.BlockSpec(memory_space=pl.ANY))
        out_shape.append(jax.ShapeDtypeStruct((nb, w, 2 * ATT_OUT), F32))
    return pl.pallas_call(
        _attn_kernel,
        name="attn_prompt",
        grid=(nb, ATT_HPP),
        in_specs=in_specs,
        out_specs=out_specs,
        out_shape=out_shape,
        scratch_shapes=[
            pltpu.VMEM((3, SEQ, hd), BF16), pltpu.VMEM((3, SEQ, hd), BF16), pltpu.VMEM((3, SEQ, 2 * hd), BF16),
            pltpu.VMEM((ATT_BLOCK, 2 * ATT_BLOCK), F32),
            pltpu.VMEM((3, SEQ, hd), F32), pltpu.VMEM((3, SEQ, hd), F32),
            pltpu.VMEM((SEQ, hd), F32), pltpu.VMEM((SEQ, hd), F32),
            pltpu.SemaphoreType.DMA((2 * len(WINDOWS),)),
        ],
        compiler_params=_params(("arbitrary", "arbitrary"), 48),
    )(*([qkv] * 9), bias, bias, bias)


def _outproj_kernel(y_ref, a_ref, wy_ref, wa_ref, x_ref, g1_ref, o_ref):
    mix = _bdot(y_ref[0], wy_ref[0]) + _bdot(a_ref[0], wa_ref[0])
    o_ref[0] = x_ref[0] + g1_ref[0] * mix


def _outproj(y, att, w, x, g1, tm, tn):
    nb, t, d = x.shape
    r = g1.shape[1]
    return pl.pallas_call(
        _outproj_kernel,
        name="out_proj",
        grid=(nb, t // tm, d // tn),
        in_specs=[
            pl.BlockSpec((1, tm, SSD_INNER), lambda b, i, j: (b, i, 0), pipeline_mode=pl.Buffered(1)),
            pl.BlockSpec((1, tm, ATT_OUT), lambda b, i, j: (b, i, 0), pipeline_mode=pl.Buffered(1)),
            pl.BlockSpec((1, SSD_INNER, tn), lambda b, i, j: (0, 0, j)),
            pl.BlockSpec((1, ATT_OUT, tn), lambda b, i, j: (0, SSD_INNER // ATT_OUT, j)),
            pl.BlockSpec((1, tm, tn), lambda b, i, j: (b, i, j)),
            pl.BlockSpec((1, r if r == 1 else tm, tn), lambda b, i, j: (b, 0 if r == 1 else i, j)),
        ],
        out_specs=pl.BlockSpec((1, tm, tn), lambda b, i, j: (b, i, j)),
        out_shape=jax.ShapeDtypeStruct((nb, t, d), F32),
        compiler_params=_params(("arbitrary", "arbitrary", "arbitrary"), 48),
    )(y, att, w, w, x, g1)


def _mlp_kernel(x_ref, g_ref, sc_ref, sh_ref, g2_ref, xs_ref, scs_ref, shs_ref, g2s_ref, w1_ref, w2_ref,
                o_ref, os_ref, u_ref):
    f = pl.program_id(2)
    last = pl.num_programs(2) - 1
    tm = x_ref.shape[1]
    first_tile = jnp.logical_and(pl.program_id(0) == 0, pl.program_id(1) == 0)

    def partial_out():
        h = jnp.maximum(_bdot(u_ref[...], w1_ref[0]), 0.0)
        return _bdot(h * h, w2_ref[0])

    @pl.when(f == 0)
    def _():
        _normmod_rows(u_ref, x_ref, g_ref, sc_ref, sh_ref)

        @pl.when(first_tile)
        def _():
            u_ref[tm:, :] = _normmod(xs_ref[0], g_ref[...], scs_ref[0], shs_ref[0]).astype(BF16)

        part = partial_out()
        o_ref[0] = part[0:tm]

        @pl.when(first_tile)
        def _():
            os_ref[0] = part[tm:]

    @pl.when(jnp.logical_and(f > 0, f < last))
    def _():
        part = partial_out()
        o_ref[0] += part[0:tm]

        @pl.when(first_tile)
        def _():
            os_ref[0] += part[tm:]

    @pl.when(f == last)
    def _():
        part = partial_out()
        o_ref[0] = x_ref[0] + g2_ref[0] * (o_ref[0] + part[0:tm])

        @pl.when(first_tile)
        def _():
            os_ref[0] = xs_ref[0] + g2s_ref[0] * (os_ref[0] + part[tm:])


def _mlp(x, sc, sh, g2, xs, scs, shs, g2s, g, w1, w2, layer, tm, tf):
    nb, t, d = x.shape
    rows = xs.shape[1]
    mod_spec = pl.BlockSpec((1, 1, d), lambda b, i, f: (b, 0, 0))
    row_spec = pl.BlockSpec((1, rows, d), lambda b, i, f: (0, 0, 0))
    return pl.pallas_call(
        _mlp_kernel,
        name="mlp",
        grid=(nb, t // tm, D_FF // tf),
        in_specs=[
            pl.BlockSpec((1, tm, d), lambda b, i, f: (b, i, 0), pipeline_mode=pl.Buffered(1)),
            pl.BlockSpec((1, d), lambda b, i, f: (0, 0)),
            mod_spec, mod_spec, mod_spec,
            row_spec, row_spec, row_spec, row_spec,
            pl.BlockSpec((1, d, tf), lambda b, i, f: (layer, 0, f)),
            pl.BlockSpec((1, tf, d), lambda b, i, f: (layer, f, 0)),
        ],
        out_specs=[pl.BlockSpec((1, tm, d), lambda b, i, f: (b, i, 0), pipeline_mode=pl.Buffered(1)), row_spec],
        out_shape=[jax.ShapeDtypeStruct((nb, t, d), F32), jax.ShapeDtypeStruct((1, rows, d), F32)],
        scratch_shapes=[pltpu.VMEM((tm + rows, d), BF16)],
        compiler_params=_params(("arbitrary", "arbitrary", "arbitrary"), 56),
    )(x, g, sc, sh, g2, xs, scs, shs, g2s, w1, w2)


def _pool_kernel(x_ref, g_ref, sc_ref, sh_ref, g1_ref, w_ref, scale_ref, o_ref, ulast_ref, ext_ref, *, tt):
    halo = POOL_MAX
    ti = pl.program_id(1)

    @pl.when(ti == 0)
    def _():
        ext_ref[0:halo, :] = jnp.zeros((halo, D_MODEL), F32)

    x = x_ref[0]
    u = _normmod(x, g_ref[...], sc_ref[0], sh_ref[0])
    ext_ref[halo:halo + tt, :] = u
    pos = ti * tt + lax.broadcasted_iota(jnp.int32, (tt, 1), 0)
    for gi, w in enumerate(POOL_WINDOWS):
        ch = slice(gi * POOL_CH, (gi + 1) * POOL_CH)
        acc = ext_ref[:, ch]
        span = 1
        while span < w:
            acc = acc + pltpu.roll(acc, span, axis=0)
            span *= 2
        s = acc[halo:halo + tt]
        inv_cnt = 1.0 / jnp.minimum(w, pos + 1).astype(F32)
        pooled = s * inv_cnt - u[:, ch]
        mix = _bdot(pooled, w_ref[gi]) * scale_ref[:, ch]
        o_ref[0, :, ch] = x[:, ch] + g1_ref[0, :, ch] * mix
    ulast_ref[0] = u[tt - halo:tt]
    ext_ref[0:halo, :] = u[tt - halo:tt]


def _pool_prompt(x, g, sc, sh, g1, pool_w, pool_scale):
    nb, t, d = x.shape
    tt = 512
    mod_spec = pl.BlockSpec((1, 1, d), lambda b, i: (b, 0, 0))
    return pl.pallas_call(
        functools.partial(_pool_kernel, tt=tt),
        name="pool_mix",
        grid=(nb, t // tt),
        in_specs=[
            pl.BlockSpec((1, tt, d), lambda b, i: (b, i, 0)),
            pl.BlockSpec((1, d), lambda b, i: (0, 0)),
            mod_spec, mod_spec, mod_spec,
            pl.BlockSpec((len(POOL_WINDOWS), POOL_CH, POOL_CH), lambda b, i: (0, 0, 0)),
            pl.BlockSpec((1, d), lambda b, i: (0, 0)),
        ],
        out_specs=[
            pl.BlockSpec((1, tt, d), lambda b, i: (b, i, 0)),
            pl.BlockSpec((1, POOL_MAX, d), lambda b, i: (b, 0, 0)),
        ],
        out_shape=[jax.ShapeDtypeStruct((nb, t, d), F32), jax.ShapeDtypeStruct((nb, POOL_MAX, d), F32)],
        scratch_shapes=[pltpu.VMEM((tt + POOL_MAX, d), F32)],
        compiler_params=_params(("arbitrary", "arbitrary"), 48),
    )(x, g, sc, sh, g1, pool_w, pool_scale)


def _sconv_kernel(zx_ref, prev_ref, w_ref, b_ref, o_ref):
    w = w_ref[...]
    acc = b_ref[...] + w[SSD_CONV - 1:SSD_CONV] * zx_ref[:, SSD_INNER:ZX_COLS]
    for k in range(SSD_CONV - 1):
        acc = acc + w[k:k + 1] * prev_ref[k]
    o_ref[...] = _silu(acc)


def _conv_sample(zx, prev_t, conv_w, conv_b):
    return pl.pallas_call(
        _sconv_kernel,
        out_shape=jax.ShapeDtypeStruct((SAMPLE_ROWS, CONV_DIM), F32),
        compiler_params=pltpu.CompilerParams(vmem_limit_bytes=32 * MIB),
    )(zx, prev_t, conv_w, conv_b)


def _sstep_kernel(h0_ref, x_ref, dt_ref, b_ref, c_ref, alog_ref, h_ref, y_ref):
    dt = dt_ref[0]
    h = jnp.exp(dt * (-jnp.exp(alog_ref[...]))) * h0_ref[0] + (dt * x_ref[0]) * b_ref[0]
    h_ref[0] = h
    y_ref[0] = jnp.sum(h * c_ref[0], axis=-1, keepdims=True)


def _ssd_step(h0, x, dt, bh, ch, a_log):
    nb = h0.shape[0]
    hp = (SSD_HEADS, SSD_HEAD_DIM)
    return pl.pallas_call(
        _sstep_kernel,
        grid=(nb,),
        in_specs=[
            pl.BlockSpec((1, *hp, SSD_STATE), lambda b: (b, 0, 0, 0)),
            pl.BlockSpec((1, *hp, 1), lambda b: (b, 0, 0, 0)),
            pl.BlockSpec((1, SSD_HEADS, 1, 1), lambda b: (b, 0, 0, 0)),
            pl.BlockSpec((1, SSD_HEADS, 1, SSD_STATE), lambda b: (b, 0, 0, 0)),
            pl.BlockSpec((1, SSD_HEADS, 1, SSD_STATE), lambda b: (b, 0, 0, 0)),
            pl.BlockSpec((SSD_HEADS, 1, 1), lambda b: (0, 0, 0)),
        ],
        out_specs=[
            pl.BlockSpec((1, *hp, SSD_STATE), lambda b: (b, 0, 0, 0)),
            pl.BlockSpec((1, *hp, 1), lambda b: (b, 0, 0, 0)),
        ],
        out_shape=[jax.ShapeDtypeStruct((nb, *hp, SSD_STATE), F32), jax.ShapeDtypeStruct((nb, *hp, 1), F32)],
        compiler_params=_params(("arbitrary",), 32),
    )(h0, x, dt, bh, ch, a_log)


def _sgate_kernel(y_ref, xs_ref, z_ref, dskip_ref, g_ref, o_ref):
    y = (y_ref[...] + dskip_ref[...] * xs_ref[...]) * _silu(z_ref[...])
    o_ref[...] = _rms(y) * g_ref[...]


def _gate_sample(y, xs, z, dskip_x, norm_g):
    return pl.pallas_call(
        _sgate_kernel,
        out_shape=jax.ShapeDtypeStruct(y.shape, F32),
    )(y, xs, z, dskip_x, norm_g)


def _sattn_kernel(qkv_ref, c0_ref, c1_ref, c2_ref, bbuf_ref, bnew_ref, att_ref, knew_ref, vnew_ref):
    scale = 1.0 / math.sqrt(ATT_HEAD_DIM)
    caches = (c0_ref, c1_ref, c2_ref)
    outs, lses = [], []
    for g in range(len(WINDOWS)):
        hs = g * ATT_HPP
        q = qkv_ref[0, hs:hs + ATT_HPP, :]
        k = qkv_ref[0, ATT_HEADS + hs:ATT_HEADS + hs + ATT_HPP, :]
        v = qkv_ref[0, 2 * ATT_HEADS + hs:2 * ATT_HEADS + hs + ATT_HPP, :]
        knew_ref[0, g] = k
        vnew_ref[0, g] = v
        kbuf = caches[g][:, 0]
        vbuf = caches[g][:, 1]
        lb = jnp.sum(kbuf * q[None], axis=-1, keepdims=True) * scale + bbuf_ref[g]
        ln = jnp.sum(k * q, axis=-1, keepdims=True) * scale + bnew_ref[g]
        m = jnp.maximum(jnp.max(lb, axis=0), ln)
        pb = jnp.exp(lb - m[None])
        pn = jnp.exp(ln - m)
        s = jnp.sum(pb, axis=0) + pn
        outs.append((jnp.sum(pb * vbuf, axis=0) + pn * v) / s)
        lses.append(m + jnp.log(s))
    mx = jnp.maximum(jnp.maximum(lses[0], lses[1]), lses[2])
    es = [jnp.exp(l - mx) for l in lses]
    att_ref[0] = (es[0] * outs[0] + es[1] * outs[1] + es[2] * outs[2]) / (es[0] + es[1] + es[2])


def _attn_sample(qkv, caches, bias_buf, bias_new):
    nb = qkv.shape[0]
    hd = ATT_HEAD_DIM
    in_specs = [pl.BlockSpec((1, 3 * ATT_HEADS, hd), lambda b: (b, 0, 0))]
    for _ in caches:
        in_specs.append(pl.BlockSpec((None, ATT_BLOCK, None, 2, ATT_HPP, hd), lambda b: (b, 0, 0, 0, 0, 0)))
    in_specs += [
        pl.BlockSpec((3, ATT_BLOCK, ATT_HPP, 1), lambda b: (0, 0, 0, 0)),
        pl.BlockSpec((3, ATT_HPP, 1), lambda b: (0, 0, 0)),
    ]
    return pl.pallas_call(
        _sattn_kernel,
        grid=(nb,),
        in_specs=in_specs,
        out_specs=[
            pl.BlockSpec((1, ATT_HPP, hd), lambda b: (b, 0, 0)),
            pl.BlockSpec((1, 3, ATT_HPP, hd), lambda b: (b, 0, 0, 0)),
            pl.BlockSpec((1, 3, ATT_HPP, hd), lambda b: (b, 0, 0, 0)),
        ],
        out_shape=[
            jax.ShapeDtypeStruct((nb, ATT_HPP, hd), F32),
            jax.ShapeDtypeStruct((nb, 3, ATT_HPP, hd), F32),
            jax.ShapeDtypeStruct((nb, 3, ATT_HPP, hd), F32),
        ],
        compiler_params=_params(("arbitrary",), 32),
    )(qkv, *caches, bias_buf, bias_new)


def _spool_kernel(x_ref, g_ref, sc_ref, sh_ref, g1_ref, prev_ref, w_ref, scale_ref, o_ref, u_ref):
    x = x_ref[...]
    u = _normmod(x, g_ref[...], sc_ref[...], sh_ref[...])
    u_ref[...] = u
    for gi, w in enumerate(POOL_WINDOWS):
        ch = slice(gi * POOL_CH, (gi + 1) * POOL_CH)
        s = u[:, ch]
        for k in range(1, w):
            s = s + prev_ref[POOL_MAX - 1 - k, :, ch]
        pooled = s / float(min(w, PAST_LEN + 1)) - u[:, ch]
        mix = _bdot(pooled, w_ref[gi]) * scale_ref[:, ch]
        o_ref[:, ch] = x[:, ch] + g1_ref[:, ch] * mix


def _pool_sample(x, g, sc, sh, g1, prev_t, pool_w, pool_scale):
    return pl.pallas_call(
        _spool_kernel,
        out_shape=[jax.ShapeDtypeStruct(x.shape, F32), jax.ShapeDtypeStruct(x.shape, F32)],
        compiler_params=pltpu.CompilerParams(vmem_limit_bytes=32 * MIB),
    )(x, g, sc, sh, g1, prev_t, pool_w, pool_scale)


def _rel_bucket(dist):
    max_exact = REL_BUCKETS // 2
    n = jnp.maximum(dist, 1).astype(F32)
    large = max_exact + (jnp.log(n / max_exact) / math.log(REL_MAX_DIST / max_exact)
                         * (REL_BUCKETS - max_exact)).astype(jnp.int32)
    large = jnp.minimum(large, REL_BUCKETS - 1)
    return jnp.where(dist < max_exact, dist, large)


def _prompt_bias_table(rel_bias):
    blk = ATT_BLOCK
    j = blk + jnp.arange(blk)[:, None] - jnp.arange(2 * blk)[None, :]
    tables = []
    for gi, dil in enumerate(DILATIONS):
        bucket = _rel_bucket(jnp.clip(j, 0, blk) * dil)[None]
        rb = rel_bias[:, gi * ATT_HPP:(gi + 1) * ATT_HPP].astype(F32)
        tbl = jnp.zeros((ATT_HPP, blk, 2 * blk), F32)
        for k in range(REL_BUCKETS):
            tbl = jnp.where(bucket == k, rb[k][:, None, None], tbl)
        tables.append(tbl)
    return jnp.concatenate(tables, axis=0)


def _sample_bias_table(rel_bias):
    blk = ATT_BLOCK
    bufs, news = [], []
    for gi, dil in enumerate(DILATIONS):
        tbl = rel_bias[:, gi * ATT_HPP:(gi + 1) * ATT_HPP][_rel_bucket(dil * jnp.arange(blk + 1))]
        bufs.append(tbl[blk:0:-1])
        news.append(tbl[0])
    return jnp.stack(bufs)[..., None].astype(F32), jnp.stack(news)[..., None].astype(F32)


def _pad_rows(a, rows, axis=0):
    pad = [(0, 0)] * a.ndim
    pad[axis] = (0, rows - a.shape[axis])
    return jnp.pad(a, pad)


MLP_TILES = (1024, 512)


def kernel(x_prompt, x_sample, state_ssm, state_conv, cache_kv_w128, cache_kv_w512, cache_kv_w2048, state_pool, c_prompt, c_sample, rel_bias, ada_w, ada_b, norm_mix_g, norm_mlp_g, in_proj_w, conv_w, conv_b, dt_bias, a_log, d_skip, ssd_norm_g, q_norm_g, k_norm_g, out_proj_w, pool_w, pool_scale, mlp_w1, mlp_w2):
    d = D_MODEL
    c_all = _pad_rows(jnp.concatenate([c_prompt, c_sample], axis=0), SAMPLE_ROWS)
    mod = _ada(c_all, ada_w, ada_b)

    def mods(layer):
        parts = [mod[layer, :, i * d:(i + 1) * d] for i in range(6)]
        prompt = [p[:N_PROMPT, None, :] for p in parts]
        sample = [_pad_rows(p[N_PROMPT:N_PROMPT + N_SAMPLE], SAMPLE_ROWS)[None] for p in parts]
        return prompt, sample

    w_in_t = jnp.transpose(in_proj_w[0])
    b_dt = jnp.pad(dt_bias[0], (0, DT_PAD - SSD_HEADS)).reshape(1, DT_PAD)
    a_log_pad = jnp.pad(a_log[0], (0, DT_PAD - SSD_HEADS)).reshape(1, DT_PAD)
    dskip_x = jnp.repeat(d_skip[0], SSD_HEAD_DIM).reshape(1, SSD_INNER)
    ssd_g = ssd_norm_g.reshape(1, SSD_INNER)
    qg, kg = q_norm_g.reshape(1, ATT_HEAD_DIM), k_norm_g.reshape(1, ATT_HEAD_DIM)
    cw, cb = conv_w[0], conv_b.reshape(1, CONV_DIM)
    g_mix = [norm_mix_g[l].reshape(1, d) for l in range(2)]
    g_mlp = [norm_mlp_g[l].reshape(1, d) for l in range(2)]
    pscale = pool_scale.reshape(1, d)
    pool_wb = pool_w[0].astype(BF16)

    (sh1, sc1, g1, sh2, sc2, g2), (sh1s, sc1s, g1s, sh2s, sc2s, g2s) = mods(0)
    (sh1b, sc1b, g1b, sh2b, sc2b, g2b), (sh1sb, sc1sb, g1sb, sh2sb, sc2sb, g2sb) = mods(1)

    rows = SAMPLE_ROWS
    xs0 = _pad_rows(x_sample.reshape(N_SAMPLE, d), rows)[None]
    u0 = _norm_mod(x_prompt, g_mix[0], sc1, sh1, 512)
    u0s = _norm_mod(xs0, g_mix[0], sc1s, sh1s, rows)
    zx, dt, zx_s, dt_s = _inproj(u0, u0s, w_in_t, 0, ZX_COLS, 512, dt_b=b_dt)
    qkv, qkv_s = _inproj(u0, u0s, w_in_t, QKV_COL0, 3 * ATT_QKV, 512, qk_gains=(qg, kg))
    p_conv = zx[:, SEQ - (SSD_CONV - 1):, SSD_INNER:][None]

    zx_s = jnp.transpose(zx_s, (1, 0, 2)).reshape(rows, ZX_COLS)
    dt_s = dt_s[0]
    qkv_s = jnp.transpose(qkv_s[0, :, :N_SAMPLE], (1, 0, 2))
    xbc_new = zx_s[:N_SAMPLE, SSD_INNER:]
    s_conv = jnp.concatenate([state_conv[0][:, 1:], xbc_new[:, None, :]], axis=1)[None]
    conv_prev_t = _pad_rows(jnp.transpose(state_conv[0], (1, 0, 2)), rows, axis=1)
    xbc_s = _conv_sample(zx_s, conv_prev_t, cw, cb)
    xs_s = xbc_s[:N_SAMPLE, :SSD_INNER]
    e = SSD_HEADS_PER_GROUP
    b_s = jnp.repeat(xbc_s[:N_SAMPLE, SSD_INNER:SSD_INNER + SSD_GROUPS * SSD_STATE]
                     .reshape(N_SAMPLE, SSD_GROUPS, 1, SSD_STATE), e, axis=1)
    c_s = jnp.repeat(xbc_s[:N_SAMPLE, SSD_INNER + SSD_GROUPS * SSD_STATE:]
                     .reshape(N_SAMPLE, SSD_GROUPS, 1, SSD_STATE), e, axis=1)
    h_new, y_s = _ssd_step(state_ssm[0], xs_s.reshape(N_SAMPLE, SSD_HEADS, SSD_HEAD_DIM, 1),
                           dt_s[:N_SAMPLE, :SSD_HEADS].reshape(N_SAMPLE, SSD_HEADS, 1, 1), b_s, c_s,
                           a_log[0].reshape(SSD_HEADS, 1, 1))
    s_ssm = h_new[None]
    y_s = _gate_sample(_pad_rows(y_s.reshape(N_SAMPLE, SSD_INNER), rows), xbc_s[:, :SSD_INNER],
                       zx_s[:, :SSD_INNER], dskip_x, ssd_g)
    caches = (cache_kv_w128, cache_kv_w512, cache_kv_w2048)
    strided = [c.reshape(N_SAMPLE, ATT_BLOCK, dil, 2, ATT_HPP, ATT_HEAD_DIM) for c, dil in zip(caches, DILATIONS)]
    bias_buf, bias_new = _sample_bias_table(rel_bias)
    att_s, k_new, v_new = _attn_sample(qkv_s, strided, bias_buf, bias_new)
    new_rows = jnp.stack([k_new, v_new], axis=2)
    att_s = _pad_rows(att_s.reshape(N_SAMPLE, ATT_OUT), rows)
    x1s = _outproj(y_s[None], att_s[None], out_proj_w, xs0, g1s, rows, 512)

    y_ssd, h_fin, *s_kv = _ssd_prompt(zx, dt, cw, cb, a_log_pad, dskip_x, ssd_g, caches,
                                      [new_rows[:, gi] for gi in range(len(caches))])
    p_ssm = h_fin.reshape(1, N_PROMPT, SSD_HEADS, SSD_HEAD_DIM, SSD_STATE)
    att, kv128, kv512, kv2048 = _attn_prompt(qkv, _prompt_bias_table(rel_bias))
    p_kv = [kv.reshape(1, N_PROMPT, w, 2, ATT_HPP, ATT_HEAD_DIM) for kv, w in zip((kv128, kv512, kv2048), WINDOWS)]
    x1 = _outproj(y_ssd, att, out_proj_w, x_prompt, g1, SEQ, 512)

    x2, x2s = _mlp(x1, sc2, sh2, g2, x1s, sc2s, sh2s, g2s, g_mlp[0], mlp_w1, mlp_w2, 0, *MLP_TILES)
    x3, u_last = _pool_prompt(x2, g_mix[1], sc1b, sh1b, g1b, pool_wb, pscale)
    p_pool = u_last[:, 1:][None]
    pool_prev_t = _pad_rows(jnp.transpose(state_pool[0], (1, 0, 2)), rows, axis=1)
    x3s, u_s = _pool_sample(x2s[0], g_mix[1], sc1sb[0], sh1sb[0], g1sb[0], pool_prev_t, pool_wb, pscale)
    s_pool = jnp.concatenate([state_pool[0][:, 1:], u_s[:N_SAMPLE, None, :]], axis=1)[None]
    y_prompt, y_s2 = _mlp(x3, sc2b, sh2b, g2b, x3s[None], sc2sb, sh2sb, g2sb, g_mlp[1], mlp_w1, mlp_w2, 1, *MLP_TILES)
    y_sample = y_s2[0, :N_SAMPLE].reshape(N_SAMPLE, 1, d)

    return (y_prompt, y_sample, p_ssm, p_conv, p_kv[0], p_kv[1], p_kv[2], p_pool,
            s_ssm, s_conv, s_kv[0], s_kv[1], s_kv[2], s_pool)
```

```python
import functools
import math

import jax
import jax.numpy as jnp
from jax import lax
from jax.experimental import pallas as pl
from jax.experimental.pallas import tpu as pltpu

F32 = jnp.float32
BF16 = jnp.bfloat16

D_MODEL = 2048
SEQ = 2048
N_PROMPT = 4
N_SAMPLE = 8
SAMPLE_ROWS = 16
PAST_LEN = 16384
SSD_HEADS = 32
SSD_HEAD_DIM = 64
SSD_INNER = SSD_HEADS * SSD_HEAD_DIM
SSD_GROUPS = 8
SSD_HEADS_PER_GROUP = SSD_HEADS // SSD_GROUPS
SSD_STATE = 128
SSD_CONV = 4
SSD_CHUNK = 128
CONV_DIM = SSD_INNER + 2 * SSD_GROUPS * SSD_STATE
ATT_HEAD_DIM = 128
WINDOWS = (128, 512, 2048)
DILATIONS = (1, 4, 16)
ATT_BLOCK = 128
ATT_UNIT_ROWS = 128
ATT_MAX_ROW_STRIDE = 4
ATT_HPP = 8
ATT_HEADS = ATT_HPP * len(WINDOWS)
ATT_QKV = ATT_HEADS * ATT_HEAD_DIM
ATT_OUT = ATT_HPP * ATT_HEAD_DIM
REL_BUCKETS = 32
REL_MAX_DIST = 2048
ZX_COLS = SSD_INNER + CONV_DIM
DT_COL0 = ZX_COLS
QKV_COL0 = ZX_COLS + SSD_HEADS
QKV_LANE_SHIFT = QKV_COL0 % 128
DT_PAD = 128
MIX_OUT = SSD_INNER + ATT_OUT
POOL_WINDOWS = (2, 4, 8, 16)
POOL_CH = D_MODEL // len(POOL_WINDOWS)
POOL_MAX = 16
D_FF = 4 * D_MODEL
EPS = 1e-6
NEG = -1e30
MIB = 1024 * 1024


def _params(semantics, vmem_mib):
    return pltpu.CompilerParams(dimension_semantics=semantics, vmem_limit_bytes=vmem_mib * MIB)


def _sigmoid(x):
    return 0.5 * jnp.tanh(0.5 * x) + 0.5


def _silu(x):
    return x * _sigmoid(x)


def _softplus(x):
    return jnp.maximum(x, 0.0) + jnp.log1p(jnp.exp(-jnp.abs(x)))


def _rms(x):
    return x * lax.rsqrt(jnp.mean(x * x, axis=-1, keepdims=True) + EPS)


def _normmod(x, g, sc, sh):
    return (_rms(x) * g) * (1.0 + sc) + sh


def _normmod_rows(u_ref, x_ref, g_ref, sc_ref, sh_ref):
    tm = x_ref.shape[1]
    rc = min(tm, 256)
    per_row = sc_ref.shape[1] != 1

    def body(i, carry):
        rows = pl.ds(pl.multiple_of(i * rc, rc), rc)
        sc = sc_ref[0, rows, :] if per_row else sc_ref[0]
        sh = sh_ref[0, rows, :] if per_row else sh_ref[0]
        u_ref[rows, :] = _normmod(x_ref[0, rows, :], g_ref[...], sc, sh).astype(BF16)
        return carry

    lax.fori_loop(0, tm // rc, body, 0)


def _bdot(a, b):
    return jnp.dot(a.astype(BF16), b.astype(BF16), preferred_element_type=F32)


def _split3(v):
    hi = v.astype(BF16)
    r = v - hi.astype(F32)
    mid = r.astype(BF16)
    lo = (r - mid.astype(F32)).astype(BF16)
    return hi, mid, lo


def _select_right(v, sel, pieces=3):
    parts = _split3(v)[:pieces]
    out = jnp.dot(parts[0], sel, preferred_element_type=F32)
    for p in parts[1:]:
        out = out + jnp.dot(p, sel, preferred_element_type=F32)
    return out


def _select_left(sel, v):
    hi, mid, lo = _split3(v)
    d = lambda a: jnp.dot(sel, a, preferred_element_type=F32)
    return d(hi) + d(mid) + d(lo)


def _ada_kernel(c_ref, w_ref, b_ref, o_ref):
    o_ref[0] = _bdot(_silu(c_ref[...]), w_ref[0]) + b_ref[0]


def _ada(c_all, ada_w, ada_b):
    depth, _, n = ada_w.shape
    tn = 1024
    return pl.pallas_call(
        _ada_kernel,
        name="ada_mod",
        grid=(depth, n // tn),
        in_specs=[
            pl.BlockSpec((SAMPLE_ROWS, D_MODEL), lambda l, j: (0, 0)),
            pl.BlockSpec((1, D_MODEL, tn), lambda l, j: (l, 0, j)),
            pl.BlockSpec((1, 1, tn), lambda l, j: (l, 0, j)),
        ],
        out_specs=pl.BlockSpec((1, SAMPLE_ROWS, tn), lambda l, j: (l, 0, j)),
        out_shape=jax.ShapeDtypeStruct((depth, SAMPLE_ROWS, n), F32),
        compiler_params=_params(("arbitrary", "arbitrary"), 40),
    )(c_all, ada_w, ada_b.reshape(depth, 1, n))


def _bdot_t(a, bt):
    return lax.dot_general(a.astype(BF16), bt.astype(BF16), (((1,), (1,)), ((), ())), preferred_element_type=F32)


def _norm_kernel(x_ref, g_ref, sc_ref, sh_ref, o_ref):
    o_ref[0] = _normmod(x_ref[0], g_ref[...], sc_ref[0], sh_ref[0]).astype(BF16)


def _norm_mod(x, g, sc, sh, tt):
    nb, t, d = x.shape
    r = sc.shape[1]
    mod_spec = pl.BlockSpec((1, r if r == 1 else tt, d), lambda b, i: (b, 0 if r == 1 else i, 0))
    return pl.pallas_call(
        _norm_kernel,
        name="norm_mod",
        grid=(nb, t // tt),
        in_specs=[pl.BlockSpec((1, tt, d), lambda b, i: (b, i, 0)), pl.BlockSpec((1, d), lambda b, i: (0, 0)),
                  mod_spec, mod_spec],
        out_specs=pl.BlockSpec((1, tt, d), lambda b, i: (b, i, 0)),
        out_shape=jax.ShapeDtypeStruct((nb, t, d), BF16),
        compiler_params=_params(("arbitrary", "arbitrary"), 40),
    )(x, g, sc, sh)


def _inproj_kernel(*refs, mode):
    if mode == "dt":
        u_ref, us_ref, w_ref, wdt_ref, dtb_ref, o_ref, dt_ref, os_ref, dts_ref, ue_ref = refs
    else:
        u_ref, us_ref, w_ref, qg_ref, kg_ref, o_ref, os_ref, ue_ref = refs
    j = pl.program_id(2)
    tm = u_ref.shape[1]
    first_seq = pl.program_id(0) == 0

    @pl.when(j == 0)
    def _():
        rc = 256

        def copy_rows(i, carry):
            rows = pl.ds(pl.multiple_of(i * rc, rc), rc)
            ue_ref[rows, :] = u_ref[0, rows, :]
            return carry

        lax.fori_loop(0, tm // rc, copy_rows, 0)

        @pl.when(first_seq)
        def _():
            ue_ref[tm:, :] = us_ref[0]

        if mode == "dt":
            dt_all = _softplus(_bdot_t(ue_ref[...], wdt_ref[...]) + dtb_ref[...])
            dt_ref[0] = dt_all[0:tm]

            @pl.when(first_seq)
            def _():
                dts_ref[0] = dt_all[tm:]

    acc = _bdot_t(ue_ref[...], w_ref[...])
    if mode == "dt":
        o_ref[0] = acc[0:tm]

        @pl.when(first_seq)
        def _():
            os_ref[j] = acc[tm:]
    else:
        heads = o_ref.shape[1]
        q_tiles = ATT_HEADS // heads

        def emit(head_fn):
            for hc in range(heads):
                val = head_fn(acc[:, hc * ATT_HEAD_DIM:(hc + 1) * ATT_HEAD_DIM])
                o_ref[0, hc] = val[0:tm]

                @pl.when(first_seq)
                def _():
                    os_ref[0, j * heads + hc] = val[tm:]

        @pl.when(j < 2 * q_tiles)
        def _():
            gain = jnp.where(j < q_tiles, qg_ref[...], kg_ref[...])
            emit(lambda a: _rms(a) * gain)

        @pl.when(j >= 2 * q_tiles)
        def _():
            emit(lambda a: a)


def _inproj(u, us, wt, row0, n_cols, tn, dt_b=None, qk_gains=None):
    nb, t, d = u.shape
    tm = t
    rows = us.shape[1]
    has_dt = dt_b is not None
    if row0 % tn == 0:
        w_spec = pl.BlockSpec((tn, d), lambda b, i, j: (row0 // tn + j, 0))
    else:
        assert row0 % 8 == 0
        w_spec = pl.BlockSpec((pl.Element(tn), pl.Element(d)),
                              lambda b, i, j: (pl.multiple_of(row0 + j * tn, 8), 0))
    in_specs = [pl.BlockSpec((1, tm, d), lambda b, i, j: (b, i, 0)),
                pl.BlockSpec((1, rows, d), lambda b, i, j: (0, 0, 0)), w_spec]
    args = [u, us, wt]
    n_tiles = n_cols // tn
    if has_dt:
        in_specs += [pl.BlockSpec((DT_PAD, d), lambda b, i, j: (DT_COL0 // DT_PAD, 0)),
                     pl.BlockSpec((1, DT_PAD), lambda b, i, j: (0, 0))]
        args += [wt, dt_b]
        out_specs = [pl.BlockSpec((1, tm, tn), lambda b, i, j: (b, i, j)),
                     pl.BlockSpec((1, tm, DT_PAD), lambda b, i, j: (b, i, 0)),
                     pl.BlockSpec((n_tiles, rows, tn), lambda b, i, j: (0, 0, 0)),
                     pl.BlockSpec((1, rows, DT_PAD), lambda b, i, j: (0, 0, 0))]
        out_shape = [jax.ShapeDtypeStruct((nb, t, n_cols), F32), jax.ShapeDtypeStruct((nb, t, DT_PAD), F32),
                     jax.ShapeDtypeStruct((n_tiles, rows, tn), F32), jax.ShapeDtypeStruct((1, rows, DT_PAD), F32)]
    else:
        hd = ATT_HEAD_DIM
        in_specs += [pl.BlockSpec((1, hd), lambda b, i, j: (0, 0))] * 2
        args += list(qk_gains)
        out_specs = [pl.BlockSpec((1, tn // hd, tm, hd), lambda b, i, j: (b, j, i, 0)),
                     pl.BlockSpec((1, n_cols // hd, rows, hd), lambda b, i, j: (0, 0, 0, 0))]
        out_shape = [jax.ShapeDtypeStruct((nb, n_cols // hd, t, hd), F32),
                     jax.ShapeDtypeStruct((1, n_cols // hd, rows, hd), F32)]
    return pl.pallas_call(
        functools.partial(_inproj_kernel, mode="dt" if has_dt else "qkv"),
        name="in_proj",
        grid=(nb, t // tm, n_tiles),
        in_specs=in_specs,
        out_specs=out_specs,
        out_shape=out_shape,
        scratch_shapes=[pltpu.VMEM((tm + rows, d), BF16)],
        compiler_params=_params(("arbitrary", "arbitrary", "arbitrary"), 60),
    )(*args)


def _ssd_kernel(z_ref, xraw_ref, bcraw_ref, dt_ref, cw_ref, cb_ref, alog_ref, dskip_ref, g_ref,
                c0_ref, c1_ref, c2_ref, n0_ref, n1_ref, n2_ref,
                y_ref, hout_ref, o0_ref, o1_ref, o2_ref,
                h_ref, yacc_ref, act_ref, carry_ref, k0_ref, k1_ref, k2_ref, *, shift_blocks):
    q = SSD_CHUNK
    gw = SSD_HEADS_PER_GROUP * SSD_HEAD_DIM
    slab = 512
    chunk = pl.program_id(1)

    step = pl.program_id(0) * pl.num_programs(1) + chunk
    for x_ref, new_ref, o_ref, k_ref, per in zip((c0_ref, c1_ref, c2_ref), (n0_ref, n1_ref, n2_ref),
                                                 (o0_ref, o1_ref, o2_ref), (k0_ref, k1_ref, k2_ref), shift_blocks):
        def shift_block(x_ref=x_ref, new_ref=new_ref, o_ref=o_ref, k_ref=k_ref, per=per):
            tw = x_ref.shape[0]

            @pl.when(step % per == 0)
            def _():
                k_ref[...] = new_ref[step // per]

            o_ref[0:tw - 1] = x_ref[1:tw]
            o_ref[tw - 1] = k_ref[...]
            k_ref[...] = x_ref[0]

        if N_SAMPLE * per == N_PROMPT * (SEQ // SSD_CHUNK):
            shift_block()
        else:
            pl.when(step < N_SAMPLE * per)(shift_block)

    @pl.when(chunk == 0)
    def _():
        h_ref[...] = jnp.zeros(h_ref.shape, F32)
        carry_ref[...] = jnp.zeros(carry_ref.shape, F32)

    row8 = lax.broadcasted_iota(jnp.int32, (8, slab), 0)
    for s in range(CONV_DIM // slab):
        cs = slice(s * slab, (s + 1) * slab)
        src, off = (xraw_ref, s * slab) if s * slab < SSD_INNER else (bcraw_ref, s * slab - SSD_INNER)
        x = src[0, :, off:off + slab]
        w = cw_ref[:, cs]
        bias = cb_ref[:, cs]
        taps = [w[SSD_CONV - 1 - k:SSD_CONV - k] for k in range(SSD_CONV)]
        acc = bias + taps[0] * x
        for k in range(1, SSD_CONV):
            acc = acc + taps[k] * pltpu.roll(x, k, axis=0)
        act_ref[:, cs] = _silu(acc)
        x8 = x[0:8]
        carry = carry_ref[:, cs]
        acc8 = bias + taps[0] * x8
        for k in range(1, SSD_CONV):
            acc8 = acc8 + taps[k] * jnp.where(row8 < k, pltpu.roll(carry, k, axis=0), pltpu.roll(x8, k, axis=0))
        act_ref[0:8, cs] = _silu(acc8)
        carry_ref[:, cs] = x[q - 8:q]

    row = lax.broadcasted_iota(jnp.int32, (q, q), 0)
    col = lax.broadcasted_iota(jnp.int32, (q, q), 1)
    causal = row >= col
    tri = jnp.where(causal, 1.0, 0.0).astype(BF16)

    dt = dt_ref[0]
    a = dt * (-jnp.exp(alog_ref[...]))
    acum = _select_left(tri, a)
    acum_t = acum.T
    dt_t = dt.T
    last = acum[q - 1:q, :]
    decay_end = jnp.exp(last - acum) * dt
    eacum = jnp.exp(acum)
    chunk_decay = jnp.broadcast_to(jnp.exp(last), (8, DT_PAD))

    first_head_lanes = lax.broadcasted_iota(jnp.int32, (q, 2 * SSD_HEAD_DIM), 1) < SSD_HEAD_DIM
    lane_head = lax.broadcasted_iota(jnp.int32, (DT_PAD, gw), 1) // SSD_HEAD_DIM
    row_head = lax.broadcasted_iota(jnp.int32, (DT_PAD, gw), 0)
    b0 = SSD_INNER
    c0 = SSD_INNER + SSD_GROUPS * SSD_STATE
    ssq = jnp.zeros((q, 1), F32)

    for g in range(SSD_GROUPS):
        cs = slice(g * gw, (g + 1) * gw)
        expand = jnp.where(lane_head + g * SSD_HEADS_PER_GROUP == row_head, 1.0, 0.0).astype(BF16)
        decay_end_x = _select_right(decay_end, expand, pieces=2)
        eacum_x = _select_right(eacum, expand, pieces=2)
        chunk_decay_x = _select_right(chunk_decay, expand)[0:1]

        xs = act_ref[:, cs]
        bg_t = act_ref[:, b0 + g * SSD_STATE:b0 + (g + 1) * SSD_STATE].T.astype(BF16)
        cg = act_ref[:, c0 + g * SSD_STATE:c0 + (g + 1) * SSD_STATE].astype(BF16)
        cb = jnp.dot(cg, bg_t, preferred_element_type=F32)
        h_prev = h_ref[:, cs]
        y_g = jnp.dot(cg, h_prev.astype(BF16), preferred_element_type=F32) * eacum_x
        h_ref[:, cs] = h_prev * chunk_decay_x + jnp.dot(bg_t, (xs * decay_end_x).astype(BF16),
                                                        preferred_element_type=F32)
        xg = xs.astype(BF16)

        def y_diag(e):
            hd = g * SSD_HEADS_PER_GROUP + e
            seg = acum[:, hd:hd + 1] - acum_t[hd:hd + 1, :]
            lw = jnp.where(causal, jnp.exp(seg), 0.0) * dt_t[hd:hd + 1, :]
            pair_lanes = slice((e // 2) * 2 * SSD_HEAD_DIM, (e // 2 + 1) * 2 * SSD_HEAD_DIM)
            return jnp.dot((cb * lw).astype(BF16), xg[:, pair_lanes], preferred_element_type=F32)

        y_pairs = [jnp.where(first_head_lanes, y_diag(2 * p), y_diag(2 * p + 1))
                   for p in range(SSD_HEADS_PER_GROUP // 2)]
        y_g = y_g + jnp.concatenate(y_pairs, axis=1)
        y_g = (y_g + dskip_ref[:, cs] * xs) * _silu(z_ref[0, :, cs])
        yacc_ref[:, cs] = y_g
        ssq = ssq + jnp.sum(y_g * y_g, axis=-1, keepdims=True)

    inv_rms = lax.rsqrt(ssq * (1.0 / SSD_INNER) + EPS)
    y_ref[0] = (yacc_ref[...] * inv_rms * g_ref[...]).astype(BF16)

    @pl.when(chunk == pl.num_programs(1) - 1)
    def _():
        hout_ref[0] = h_ref[...].T


def _ssd_prompt(zx, dt, conv_w, conv_b, a_log_pad, dskip_x, norm_g, caches, new_rows):
    nb, t, _ = zx.shape
    q = SSD_CHUNK
    n_chunks = t // q
    wide = SSD_INNER
    assert ZX_COLS == 3 * wide
    row = caches[0].shape[3:]
    cache_specs, shift_blocks = [], []
    for cache in caches:
        w = cache.shape[2]
        tw = min(w, 256)
        per = w // tw
        assert N_SAMPLE * per <= nb * n_chunks
        shift_blocks.append(per)

        def block_index(b, c, per=per):
            s = jnp.minimum(b * n_chunks + c, N_SAMPLE * per - 1)
            return (0, s // per, per - 1 - s % per, 0, 0, 0)

        cache_specs.append(pl.BlockSpec((None, None, tw, *row), block_index))
    new_specs = [pl.BlockSpec(n.shape, lambda b, c: (0, 0, 0, 0)) for n in new_rows]
    return pl.pallas_call(
        functools.partial(_ssd_kernel, shift_blocks=tuple(shift_blocks)),
        name="ssd_chunk",
        grid=(nb, n_chunks),
        in_specs=[
            pl.BlockSpec((1, q, wide), lambda b, c: (b, c, 0)),
            pl.BlockSpec((1, q, wide), lambda b, c: (b, c, 1)),
            pl.BlockSpec((1, q, wide), lambda b, c: (b, c, 2)),
            pl.BlockSpec((1, q, DT_PAD), lambda b, c: (b, c, 0)),
            pl.BlockSpec((SSD_CONV, CONV_DIM), lambda b, c: (0, 0)),
            pl.BlockSpec((1, CONV_DIM), lambda b, c: (0, 0)),
            pl.BlockSpec((1, DT_PAD), lambda b, c: (0, 0)),
            pl.BlockSpec((1, SSD_INNER), lambda b, c: (0, 0)),
            pl.BlockSpec((1, SSD_INNER), lambda b, c: (0, 0)),
            *cache_specs, *new_specs,
        ],
        out_specs=[
            pl.BlockSpec((1, q, SSD_INNER), lambda b, c: (b, c, 0)),
            pl.BlockSpec((1, SSD_INNER, SSD_STATE), lambda b, c: (b, 0, 0)),
            *cache_specs,
        ],
        out_shape=[
            jax.ShapeDtypeStruct((nb, t, SSD_INNER), BF16),
            jax.ShapeDtypeStruct((nb, SSD_INNER, SSD_STATE), F32),
            *[jax.ShapeDtypeStruct(c.shape, c.dtype) for c in caches],
        ],
        scratch_shapes=[pltpu.VMEM((SSD_STATE, SSD_INNER), F32), pltpu.VMEM((q, SSD_INNER), F32),
                        pltpu.VMEM((q, CONV_DIM), F32), pltpu.VMEM((8, CONV_DIM), F32),
                        *[pltpu.VMEM(row, F32) for _ in caches]],
        compiler_params=_params(("arbitrary", "arbitrary"), 48),
    )(zx, zx, zx, dt, conv_w, conv_b, a_log_pad, dskip_x, norm_g, *caches, *new_rows)


def _attn_kernel(q0, q1, q2, k0, k1, k2, v0, v1, v2, bias0, bias1, bias2,
                 att_ref, kv0_ref, kv1_ref, kv2_ref, qb_s, kb_s, vb_s, ok_s, os_, ls, tmp_s, ltmp_s, kv_sem):
    blk = ATT_BLOCK
    ur = ATT_UNIT_ROWS
    hd = ATT_HEAD_DIM
    scale = 1.0 / math.sqrt(hd)
    batch, slot = pl.program_id(0), pl.program_id(1)
    qrefs, krefs, vrefs, brefs = (q0, q1, q2), (k0, k1, k2), (v0, v1, v2), (bias0, bias1, bias2)
    kvrefs = (kv0_ref, kv1_ref, kv2_ref)
    nt = (((1,), (1,)), ((), ()))

    def cache_copies():
        copies = []
        for g, w in enumerate(WINDOWS):
            for part, src in enumerate((krefs[g], vrefs[g])):
                cols = pl.ds(pl.multiple_of((part * ATT_HPP + slot) * hd, hd), hd)
                copies.append(pltpu.make_async_copy(
                    src.at[0, 0, pl.ds(SEQ - w, w), :], kvrefs[g].at[batch, :, cols], kv_sem.at[2 * g + part]))
        return copies

    for cp in cache_copies():
        cp.start()

    def compute():
        row = lax.broadcasted_iota(jnp.int32, (blk, 2 * blk), 0)
        col = lax.broadcasted_iota(jnp.int32, (blk, 2 * blk), 1)
        ok_s[...] = jnp.where(jnp.where(col < blk, col - row, row - (col - blk)) >= 0, 1.0, 0.0)

        f = ATT_MAX_ROW_STRIDE
        seg = SEQ // f

        def strided(start, size, stride):
            return pl.ds(start, size) if stride == 1 else pl.ds(start, size, stride=stride)

        def regroup(src_ref, dil, emit):
            if dil <= f:
                sub = SEQ // dil
                ch = min(sub, 256)
                for r in range(dil):
                    for c in range(sub // ch):
                        emit(pl.ds(r * sub + c * ch, ch), src_ref[0, 0, strided(r + c * ch * dil, ch, dil), :])
            else:
                assert dil == f * f
                for r1 in range(f):
                    for c in range(seg // 256):
                        tmp_s[pl.ds(r1 * seg + c * 256, 256), :] = src_ref[0, 0, strided(r1 + c * 256 * f, 256, f), :]
                sub = seg // f
                for r1 in range(f):
                    for r2 in range(f):
                        emit(pl.ds((f * r2 + r1) * sub, sub), tmp_s[strided(r1 * seg + r2, sub, f), :])

        for g, dil in enumerate(DILATIONS):
            vb_s[g, :, hd:2 * hd] = jnp.ones((SEQ, hd), BF16)

            def put_q(rows, val, g=g):
                qb_s[g, rows, :] = val.astype(BF16)

            def put_k(rows, val, g=g):
                kb_s[g, rows, :] = val.astype(BF16)

            def put_v(rows, val, g=g):
                vb_s[g, rows, 0:hd] = val.astype(BF16)

            regroup(qrefs[g], dil, put_q)
            regroup(krefs[g], dil, put_k)
            regroup(vrefs[g], dil, put_v)

        for g, dil in enumerate(DILATIONS):
            nblk = SEQ // (blk * dil)
            for r in range(dil):
                for n in range(nblk):
                    base = (r * nblk + n) * blk
                    key0, col0 = (base - blk, 0) if n > 0 else (base, blk)
                    kcat = kb_s[g, key0:base + blk, :]
                    vcat = vb_s[g, key0:base + blk, :]
                    for half in range(blk // ur):
                        rs = slice(half * ur, (half + 1) * ur)
                        qb = qb_s[g, base + half * ur:base + (half + 1) * ur, :]
                        l = lax.dot_general(qb, kcat, nt, preferred_element_type=F32)
                        l = jnp.where(ok_s[rs, col0:2 * blk] > 0.5, l * scale + brefs[g][0, rs, col0:2 * blk], NEG)
                        m = jnp.max(l, axis=-1, keepdims=True)
                        p = jnp.exp(l - m).astype(BF16)
                        oe = jnp.dot(p, vcat, preferred_element_type=F32)
                        den = oe[:, hd:2 * hd]
                        s0 = n * blk + half * ur
                        if dil <= f:
                            dst = strided(r + s0 * dil, ur, dil)
                            os_[g, dst, :] = oe[:, 0:hd] / den
                            ls[g, dst, :] = m + jnp.log(den)
                        else:
                            dst = strided((r % f) * seg + f * s0 + r // f, ur, f)
                            tmp_s[dst, :] = oe[:, 0:hd] / den
                            ltmp_s[dst, :] = m + jnp.log(den)
            if dil > f:
                for r1 in range(f):
                    for c in range(seg // 256):
                        dst = strided(r1 + c * 256 * f, 256, f)
                        os_[g, dst, :] = tmp_s[pl.ds(r1 * seg + c * 256, 256), :]
                        ls[g, dst, :] = ltmp_s[pl.ds(r1 * seg + c * 256, 256), :]

        def mix_body(i, carry):
            rows = pl.ds(pl.multiple_of(i * blk, blk), blk)
            l0, l1, l2 = ls[0, rows, :], ls[1, rows, :], ls[2, rows, :]
            mx = jnp.maximum(jnp.maximum(l0, l1), l2)
            e0, e1, e2 = jnp.exp(l0 - mx), jnp.exp(l1 - mx), jnp.exp(l2 - mx)
            num = e0 * os_[0, rows, :] + e1 * os_[1, rows, :] + e2 * os_[2, rows, :]
            att_ref[0, rows, :] = (num / (e0 + e1 + e2)).astype(BF16)
            return carry
        lax.fori_loop(0, SEQ // blk, mix_body, 0)

    compute()
    for cp in cache_copies():
        cp.wait()


def _attn_prompt(qkv, bias):
    nb = qkv.shape[0]
    hd = ATT_HEAD_DIM

    def head_spec(part, g):
        return pl.BlockSpec((1, 1, SEQ, hd), lambda b, h: (b, part * ATT_HEADS + g * ATT_HPP + h, 0, 0))

    in_specs = [head_spec(part, g) for part in range(3) for g in range(3)]
    in_specs += [pl.BlockSpec((1, ATT_BLOCK, 2 * ATT_BLOCK), functools.partial(lambda b, h, g: (g * ATT_HPP + h, 0, 0), g=g))
                 for g in range(3)]
    out_specs = [pl.BlockSpec((1, SEQ, hd), lambda b, h: (b, 0, h))]
    out_shape = [jax.ShapeDtypeStruct((nb, SEQ, ATT_OUT), BF16)]
    for w in WINDOWS:
        out_specs.append(pl.BlockSpec(memory_space=pl.ANY))
        out_shape.append(jax.ShapeDtypeStruct((nb, w, 2 * ATT_OUT), F32))
    return pl.pallas_call(
        _attn_kernel,
        name="attn_prompt",
        grid=(nb, ATT_HPP),
        in_specs=in_specs,
        out_specs=out_specs,
        out_shape=out_shape,
        scratch_shapes=[
            pltpu.VMEM((3, SEQ, hd), BF16), pltpu.VMEM((3, SEQ, hd), BF16), pltpu.VMEM((3, SEQ, 2 * hd), BF16),
            pltpu.VMEM((ATT_BLOCK, 2 * ATT_BLOCK), F32),
            pltpu.VMEM((3, SEQ, hd), F32), pltpu.VMEM((3, SEQ, hd), F32),
            pltpu.VMEM((SEQ, hd), F32), pltpu.VMEM((SEQ, hd), F32),
            pltpu.SemaphoreType.DMA((2 * len(WINDOWS),)),
        ],
        compiler_params=_params(("arbitrary", "arbitrary"), 48),
    )(*([qkv] * 9), bias, bias, bias)


def _outproj_kernel(y_ref, a_ref, wy_ref, wa_ref, x_ref, g1_ref, o_ref):
    mix = _bdot(y_ref[0], wy_ref[0]) + _bdot(a_ref[0], wa_ref[0])
    o_ref[0] = x_ref[0] + g1_ref[0] * mix


def _outproj(y, att, w, x, g1, tm, tn):
    nb, t, d = x.shape
    r = g1.shape[1]
    return pl.pallas_call(
        _outproj_kernel,
        name="out_proj",
        grid=(nb, t // tm, d // tn),
        in_specs=[
            pl.BlockSpec((1, tm, SSD_INNER), lambda b, i, j: (b, i, 0)),
            pl.BlockSpec((1, tm, ATT_OUT), lambda b, i, j: (b, i, 0)),
            pl.BlockSpec((1, SSD_INNER, tn), lambda b, i, j: (0, 0, j)),
            pl.BlockSpec((1, ATT_OUT, tn), lambda b, i, j: (0, SSD_INNER // ATT_OUT, j)),
            pl.BlockSpec((1, tm, tn), lambda b, i, j: (b, i, j)),
            pl.BlockSpec((1, r if r == 1 else tm, tn), lambda b, i, j: (b, 0 if r == 1 else i, j)),
        ],
        out_specs=pl.BlockSpec((1, tm, tn), lambda b, i, j: (b, i, j)),
        out_shape=jax.ShapeDtypeStruct((nb, t, d), F32),
        compiler_params=_params(("arbitrary", "arbitrary", "arbitrary"), 62),
    )(y, att, w, w, x, g1)


def _mlp_kernel(x_ref, g_ref, sc_ref, sh_ref, g2_ref, xs_ref, scs_ref, shs_ref, g2s_ref, w1_ref, w2_ref,
                o_ref, os_ref, u_ref):
    f = pl.program_id(2)
    last = pl.num_programs(2) - 1
    tm = x_ref.shape[1]
    first_tile = jnp.logical_and(pl.program_id(0) == 0, pl.program_id(1) == 0)

    def partial_out():
        h = jnp.maximum(_bdot(u_ref[...], w1_ref[0]), 0.0)
        return _bdot(h * h, w2_ref[0])

    @pl.when(f == 0)
    def _():
        _normmod_rows(u_ref, x_ref, g_ref, sc_ref, sh_ref)

        @pl.when(first_tile)
        def _():
            u_ref[tm:, :] = _normmod(xs_ref[0], g_ref[...], scs_ref[0], shs_ref[0]).astype(BF16)

        part = partial_out()
        o_ref[0] = part[0:tm]

        @pl.when(first_tile)
        def _():
            os_ref[0] = part[tm:]

    @pl.when(jnp.logical_and(f > 0, f < last))
    def _():
        part = partial_out()
        o_ref[0] += part[0:tm]

        @pl.when(first_tile)
        def _():
            os_ref[0] += part[tm:]

    @pl.when(f == last)
    def _():
        part = partial_out()
        o_ref[0] = x_ref[0] + g2_ref[0] * (o_ref[0] + part[0:tm])

        @pl.when(first_tile)
        def _():
            os_ref[0] = xs_ref[0] + g2s_ref[0] * (os_ref[0] + part[tm:])


def _mlp(x, sc, sh, g2, xs, scs, shs, g2s, g, w1, w2, layer, tm, tf):
    nb, t, d = x.shape
    rows = xs.shape[1]
    mod_spec = pl.BlockSpec((1, 1, d), lambda b, i, f: (b, 0, 0))
    row_spec = pl.BlockSpec((1, rows, d), lambda b, i, f: (0, 0, 0))
    return pl.pallas_call(
        _mlp_kernel,
        name="mlp",
        grid=(nb, t // tm, D_FF // tf),
        in_specs=[
            pl.BlockSpec((1, tm, d), lambda b, i, f: (b, i, 0)),
            pl.BlockSpec((1, d), lambda b, i, f: (0, 0)),
            mod_spec, mod_spec, mod_spec,
            row_spec, row_spec, row_spec, row_spec,
            pl.BlockSpec((1, d, tf), lambda b, i, f: (layer, 0, f)),
            pl.BlockSpec((1, tf, d), lambda b, i, f: (layer, f, 0)),
        ],
        out_specs=[pl.BlockSpec((1, tm, d), lambda b, i, f: (b, i, 0)), row_spec],
        out_shape=[jax.ShapeDtypeStruct((nb, t, d), F32), jax.ShapeDtypeStruct((1, rows, d), F32)],
        scratch_shapes=[pltpu.VMEM((tm + rows, d), BF16)],
        compiler_params=_params(("arbitrary", "arbitrary", "arbitrary"), 62),
    )(x, g, sc, sh, g2, xs, scs, shs, g2s, w1, w2)


def _pool_kernel(x_ref, g_ref, sc_ref, sh_ref, g1_ref, w_ref, scale_ref, o_ref, ulast_ref, ext_ref, *, tt):
    halo = POOL_MAX
    ti = pl.program_id(1)

    @pl.when(ti == 0)
    def _():
        ext_ref[0:halo, :] = jnp.zeros((halo, D_MODEL), F32)

    x = x_ref[0]
    u = _normmod(x, g_ref[...], sc_ref[0], sh_ref[0])
    ext_ref[halo:halo + tt, :] = u
    pos = ti * tt + lax.broadcasted_iota(jnp.int32, (tt, 1), 0)
    for gi, w in enumerate(POOL_WINDOWS):
        ch = slice(gi * POOL_CH, (gi + 1) * POOL_CH)
        acc = ext_ref[:, ch]
        span = 1
        while span < w:
            acc = acc + pltpu.roll(acc, span, axis=0)
            span *= 2
        s = acc[halo:halo + tt]
        inv_cnt = 1.0 / jnp.minimum(w, pos + 1).astype(F32)
        pooled = s * inv_cnt - u[:, ch]
        mix = _bdot(pooled, w_ref[gi]) * scale_ref[:, ch]
        o_ref[0, :, ch] = x[:, ch] + g1_ref[0, :, ch] * mix
    ulast_ref[0] = u[tt - halo:tt]
    ext_ref[0:halo, :] = u[tt - halo:tt]


def _pool_prompt(x, g, sc, sh, g1, pool_w, pool_scale):
    nb, t, d = x.shape
    tt = 512
    mod_spec = pl.BlockSpec((1, 1, d), lambda b, i: (b, 0, 0))
    return pl.pallas_call(
        functools.partial(_pool_kernel, tt=tt),
        name="pool_mix",
        grid=(nb, t // tt),
        in_specs=[
            pl.BlockSpec((1, tt, d), lambda b, i: (b, i, 0)),
            pl.BlockSpec((1, d), lambda b, i: (0, 0)),
            mod_spec, mod_spec, mod_spec,
            pl.BlockSpec((len(POOL_WINDOWS), POOL_CH, POOL_CH), lambda b, i: (0, 0, 0)),
            pl.BlockSpec((1, d), lambda b, i: (0, 0)),
        ],
        out_specs=[
            pl.BlockSpec((1, tt, d), lambda b, i: (b, i, 0)),
            pl.BlockSpec((1, POOL_MAX, d), lambda b, i: (b, 0, 0)),
        ],
        out_shape=[jax.ShapeDtypeStruct((nb, t, d), F32), jax.ShapeDtypeStruct((nb, POOL_MAX, d), F32)],
        scratch_shapes=[pltpu.VMEM((tt + POOL_MAX, d), F32)],
        compiler_params=_params(("arbitrary", "arbitrary"), 48),
    )(x, g, sc, sh, g1, pool_w, pool_scale)


def _sconv_kernel(zx_ref, prev_ref, w_ref, b_ref, o_ref):
    w = w_ref[...]
    acc = b_ref[...] + w[SSD_CONV - 1:SSD_CONV] * zx_ref[:, SSD_INNER:ZX_COLS]
    for k in range(SSD_CONV - 1):
        acc = acc + w[k:k + 1] * prev_ref[k]
    o_ref[...] = _silu(acc)


def _conv_sample(zx, prev_t, conv_w, conv_b):
    return pl.pallas_call(
        _sconv_kernel,
        out_shape=jax.ShapeDtypeStruct((SAMPLE_ROWS, CONV_DIM), F32),
        compiler_params=pltpu.CompilerParams(vmem_limit_bytes=32 * MIB),
    )(zx, prev_t, conv_w, conv_b)


def _sstep_kernel(h0_ref, x_ref, dt_ref, b_ref, c_ref, alog_ref, h_ref, y_ref):
    dt = dt_ref[0]
    h = jnp.exp(dt * (-jnp.exp(alog_ref[...]))) * h0_ref[0] + (dt * x_ref[0]) * b_ref[0]
    h_ref[0] = h
    y_ref[0] = jnp.sum(h * c_ref[0], axis=-1, keepdims=True)


def _ssd_step(h0, x, dt, bh, ch, a_log):
    nb = h0.shape[0]
    hp = (SSD_HEADS, SSD_HEAD_DIM)
    return pl.pallas_call(
        _sstep_kernel,
        grid=(nb,),
        in_specs=[
            pl.BlockSpec((1, *hp, SSD_STATE), lambda b: (b, 0, 0, 0)),
            pl.BlockSpec((1, *hp, 1), lambda b: (b, 0, 0, 0)),
            pl.BlockSpec((1, SSD_HEADS, 1, 1), lambda b: (b, 0, 0, 0)),
            pl.BlockSpec((1, SSD_HEADS, 1, SSD_STATE), lambda b: (b, 0, 0, 0)),
            pl.BlockSpec((1, SSD_HEADS, 1, SSD_STATE), lambda b: (b, 0, 0, 0)),
            pl.BlockSpec((SSD_HEADS, 1, 1), lambda b: (0, 0, 0)),
        ],
        out_specs=[
            pl.BlockSpec((1, *hp, SSD_STATE), lambda b: (b, 0, 0, 0)),
            pl.BlockSpec((1, *hp, 1), lambda b: (b, 0, 0, 0)),
        ],
        out_shape=[jax.ShapeDtypeStruct((nb, *hp, SSD_STATE), F32), jax.ShapeDtypeStruct((nb, *hp, 1), F32)],
        compiler_params=_params(("arbitrary",), 32),
    )(h0, x, dt, bh, ch, a_log)


def _sgate_kernel(y_ref, xs_ref, z_ref, dskip_ref, g_ref, o_ref):
    y = (y_ref[...] + dskip_ref[...] * xs_ref[...]) * _silu(z_ref[...])
    o_ref[...] = _rms(y) * g_ref[...]


def _gate_sample(y, xs, z, dskip_x, norm_g):
    return pl.pallas_call(
        _sgate_kernel,
        out_shape=jax.ShapeDtypeStruct(y.shape, F32),
    )(y, xs, z, dskip_x, norm_g)


def _sattn_kernel(qkv_ref, c0_ref, c1_ref, c2_ref, bbuf_ref, bnew_ref, att_ref, knew_ref, vnew_ref):
    scale = 1.0 / math.sqrt(ATT_HEAD_DIM)
    caches = (c0_ref, c1_ref, c2_ref)
    outs, lses = [], []
    for g in range(len(WINDOWS)):
        hs = g * ATT_HPP
        q = qkv_ref[0, hs:hs + ATT_HPP, :]
        k = qkv_ref[0, ATT_HEADS + hs:ATT_HEADS + hs + ATT_HPP, :]
        v = qkv_ref[0, 2 * ATT_HEADS + hs:2 * ATT_HEADS + hs + ATT_HPP, :]
        knew_ref[0, g] = k
        vnew_ref[0, g] = v
        kbuf = caches[g][:, 0]
        vbuf = caches[g][:, 1]
        lb = jnp.sum(kbuf * q[None], axis=-1, keepdims=True) * scale + bbuf_ref[g]
        ln = jnp.sum(k * q, axis=-1, keepdims=True) * scale + bnew_ref[g]
        m = jnp.maximum(jnp.max(lb, axis=0), ln)
        pb = jnp.exp(lb - m[None])
        pn = jnp.exp(ln - m)
        s = jnp.sum(pb, axis=0) + pn
        outs.append((jnp.sum(pb * vbuf, axis=0) + pn * v) / s)
        lses.append(m + jnp.log(s))
    mx = jnp.maximum(jnp.maximum(lses[0], lses[1]), lses[2])
    es = [jnp.exp(l - mx) for l in lses]
    att_ref[0] = (es[0] * outs[0] + es[1] * outs[1] + es[2] * outs[2]) / (es[0] + es[1] + es[2])


def _attn_sample(qkv, caches, bias_buf, bias_new):
    nb = qkv.shape[0]
    hd = ATT_HEAD_DIM
    in_specs = [pl.BlockSpec((1, 3 * ATT_HEADS, hd), lambda b: (b, 0, 0))]
    for _ in caches:
        in_specs.append(pl.BlockSpec((None, ATT_BLOCK, None, 2, ATT_HPP, hd), lambda b: (b, 0, 0, 0, 0, 0)))
    in_specs += [
        pl.BlockSpec((3, ATT_BLOCK, ATT_HPP, 1), lambda b: (0, 0, 0, 0)),
        pl.BlockSpec((3, ATT_HPP, 1), lambda b: (0, 0, 0)),
    ]
    return pl.pallas_call(
        _sattn_kernel,
        grid=(nb,),
        in_specs=in_specs,
        out_specs=[
            pl.BlockSpec((1, ATT_HPP, hd), lambda b: (b, 0, 0)),
            pl.BlockSpec((1, 3, ATT_HPP, hd), lambda b: (b, 0, 0, 0)),
            pl.BlockSpec((1, 3, ATT_HPP, hd), lambda b: (b, 0, 0, 0)),
        ],
        out_shape=[
            jax.ShapeDtypeStruct((nb, ATT_HPP, hd), F32),
            jax.ShapeDtypeStruct((nb, 3, ATT_HPP, hd), F32),
            jax.ShapeDtypeStruct((nb, 3, ATT_HPP, hd), F32),
        ],
        compiler_params=_params(("arbitrary",), 32),
    )(qkv, *caches, bias_buf, bias_new)


def _spool_kernel(x_ref, g_ref, sc_ref, sh_ref, g1_ref, prev_ref, w_ref, scale_ref, o_ref, u_ref):
    x = x_ref[...]
    u = _normmod(x, g_ref[...], sc_ref[...], sh_ref[...])
    u_ref[...] = u
    for gi, w in enumerate(POOL_WINDOWS):
        ch = slice(gi * POOL_CH, (gi + 1) * POOL_CH)
        s = u[:, ch]
        for k in range(1, w):
            s = s + prev_ref[POOL_MAX - 1 - k, :, ch]
        pooled = s / float(min(w, PAST_LEN + 1)) - u[:, ch]
        mix = _bdot(pooled, w_ref[gi]) * scale_ref[:, ch]
        o_ref[:, ch] = x[:, ch] + g1_ref[:, ch] * mix


def _pool_sample(x, g, sc, sh, g1, prev_t, pool_w, pool_scale):
    return pl.pallas_call(
        _spool_kernel,
        out_shape=[jax.ShapeDtypeStruct(x.shape, F32), jax.ShapeDtypeStruct(x.shape, F32)],
        compiler_params=pltpu.CompilerParams(vmem_limit_bytes=32 * MIB),
    )(x, g, sc, sh, g1, prev_t, pool_w, pool_scale)


def _rel_bucket(dist):
    max_exact = REL_BUCKETS // 2
    n = jnp.maximum(dist, 1).astype(F32)
    large = max_exact + (jnp.log(n / max_exact) / math.log(REL_MAX_DIST / max_exact)
                         * (REL_BUCKETS - max_exact)).astype(jnp.int32)
    large = jnp.minimum(large, REL_BUCKETS - 1)
    return jnp.where(dist < max_exact, dist, large)


def _prompt_bias_table(rel_bias):
    blk = ATT_BLOCK
    j = blk + jnp.arange(blk)[:, None] - jnp.arange(2 * blk)[None, :]
    tables = []
    for gi, dil in enumerate(DILATIONS):
        bucket = _rel_bucket(jnp.clip(j, 0, blk) * dil)[None]
        rb = rel_bias[:, gi * ATT_HPP:(gi + 1) * ATT_HPP].astype(F32)
        tbl = jnp.zeros((ATT_HPP, blk, 2 * blk), F32)
        for k in range(REL_BUCKETS):
            tbl = jnp.where(bucket == k, rb[k][:, None, None], tbl)
        tables.append(tbl)
    return jnp.concatenate(tables, axis=0)


def _sample_bias_table(rel_bias):
    blk = ATT_BLOCK
    bufs, news = [], []
    for gi, dil in enumerate(DILATIONS):
        tbl = rel_bias[:, gi * ATT_HPP:(gi + 1) * ATT_HPP][_rel_bucket(dil * jnp.arange(blk + 1))]
        bufs.append(tbl[blk:0:-1])
        news.append(tbl[0])
    return jnp.stack(bufs)[..., None].astype(F32), jnp.stack(news)[..., None].astype(F32)


def _pad_rows(a, rows, axis=0):
    pad = [(0, 0)] * a.ndim
    pad[axis] = (0, rows - a.shape[axis])
    return jnp.pad(a, pad)


MLP_TILES = (1024, 512)


def kernel(x_prompt, x_sample, state_ssm, state_conv, cache_kv_w128, cache_kv_w512, cache_kv_w2048, state_pool, c_prompt, c_sample, rel_bias, ada_w, ada_b, norm_mix_g, norm_mlp_g, in_proj_w, conv_w, conv_b, dt_bias, a_log, d_skip, ssd_norm_g, q_norm_g, k_norm_g, out_proj_w, pool_w, pool_scale, mlp_w1, mlp_w2):
    d = D_MODEL
    c_all = _pad_rows(jnp.concatenate([c_prompt, c_sample], axis=0), SAMPLE_ROWS)
    mod = _ada(c_all, ada_w, ada_b)

    def mods(layer):
        parts = [mod[layer, :, i * d:(i + 1) * d] for i in range(6)]
        prompt = [p[:N_PROMPT, None, :] for p in parts]
        sample = [_pad_rows(p[N_PROMPT:N_PROMPT + N_SAMPLE], SAMPLE_ROWS)[None] for p in parts]
        return prompt, sample

    w_in_t = jnp.transpose(in_proj_w[0])
    b_dt = jnp.pad(dt_bias[0], (0, DT_PAD - SSD_HEADS)).reshape(1, DT_PAD)
    a_log_pad = jnp.pad(a_log[0], (0, DT_PAD - SSD_HEADS)).reshape(1, DT_PAD)
    dskip_x = jnp.repeat(d_skip[0], SSD_HEAD_DIM).reshape(1, SSD_INNER)
    ssd_g = ssd_norm_g.reshape(1, SSD_INNER)
    qg, kg = q_norm_g.reshape(1, ATT_HEAD_DIM), k_norm_g.reshape(1, ATT_HEAD_DIM)
    cw, cb = conv_w[0], conv_b.reshape(1, CONV_DIM)
    g_mix = [norm_mix_g[l].reshape(1, d) for l in range(2)]
    g_mlp = [norm_mlp_g[l].reshape(1, d) for l in range(2)]
    pscale = pool_scale.reshape(1, d)
    pool_wb = pool_w[0].astype(BF16)

    (sh1, sc1, g1, sh2, sc2, g2), (sh1s, sc1s, g1s, sh2s, sc2s, g2s) = mods(0)
    (sh1b, sc1b, g1b, sh2b, sc2b, g2b), (sh1sb, sc1sb, g1sb, sh2sb, sc2sb, g2sb) = mods(1)

    rows = SAMPLE_ROWS
    xs0 = _pad_rows(x_sample.reshape(N_SAMPLE, d), rows)[None]
    u0 = _norm_mod(x_prompt, g_mix[0], sc1, sh1, 512)
    u0s = _norm_mod(xs0, g_mix[0], sc1s, sh1s, rows)
    zx, dt, zx_s, dt_s = _inproj(u0, u0s, w_in_t, 0, ZX_COLS, 512, dt_b=b_dt)
    qkv, qkv_s = _inproj(u0, u0s, w_in_t, QKV_COL0, 3 * ATT_QKV, 512, qk_gains=(qg, kg))
    p_conv = zx[:, SEQ - (SSD_CONV - 1):, SSD_INNER:][None]

    zx_s = jnp.transpose(zx_s, (1, 0, 2)).reshape(rows, ZX_COLS)
    dt_s = dt_s[0]
    qkv_s = jnp.transpose(qkv_s[0, :, :N_SAMPLE], (1, 0, 2))
    xbc_new = zx_s[:N_SAMPLE, SSD_INNER:]
    s_conv = jnp.concatenate([state_conv[0][:, 1:], xbc_new[:, None, :]], axis=1)[None]
    conv_prev_t = _pad_rows(jnp.transpose(state_conv[0], (1, 0, 2)), rows, axis=1)
    xbc_s = _conv_sample(zx_s, conv_prev_t, cw, cb)
    xs_s = xbc_s[:N_SAMPLE, :SSD_INNER]
    e = SSD_HEADS_PER_GROUP
    b_s = jnp.repeat(xbc_s[:N_SAMPLE, SSD_INNER:SSD_INNER + SSD_GROUPS * SSD_STATE]
                     .reshape(N_SAMPLE, SSD_GROUPS, 1, SSD_STATE), e, axis=1)
    c_s = jnp.repeat(xbc_s[:N_SAMPLE, SSD_INNER + SSD_GROUPS * SSD_STATE:]
                     .reshape(N_SAMPLE, SSD_GROUPS, 1, SSD_STATE), e, axis=1)
    h_new, y_s = _ssd_step(state_ssm[0], xs_s.reshape(N_SAMPLE, SSD_HEADS, SSD_HEAD_DIM, 1),
                           dt_s[:N_SAMPLE, :SSD_HEADS].reshape(N_SAMPLE, SSD_HEADS, 1, 1), b_s, c_s,
                           a_log[0].reshape(SSD_HEADS, 1, 1))
    s_ssm = h_new[None]
    y_s = _gate_sample(_pad_rows(y_s.reshape(N_SAMPLE, SSD_INNER), rows), xbc_s[:, :SSD_INNER],
                       zx_s[:, :SSD_INNER], dskip_x, ssd_g)
    caches = (cache_kv_w128, cache_kv_w512, cache_kv_w2048)
    strided = [c.reshape(N_SAMPLE, ATT_BLOCK, dil, 2, ATT_HPP, ATT_HEAD_DIM) for c, dil in zip(caches, DILATIONS)]
    bias_buf, bias_new = _sample_bias_table(rel_bias)
    att_s, k_new, v_new = _attn_sample(qkv_s, strided, bias_buf, bias_new)
    new_rows = jnp.stack([k_new, v_new], axis=2)
    att_s = _pad_rows(att_s.reshape(N_SAMPLE, ATT_OUT), rows)
    x1s = _outproj(y_s[None], att_s[None], out_proj_w, xs0, g1s, rows, 512)

    y_ssd, h_fin, *s_kv = _ssd_prompt(zx, dt, cw, cb, a_log_pad, dskip_x, ssd_g, caches,
                                      [new_rows[:, gi] for gi in range(len(caches))])
    p_ssm = h_fin.reshape(1, N_PROMPT, SSD_HEADS, SSD_HEAD_DIM, SSD_STATE)
    att, kv128, kv512, kv2048 = _attn_prompt(qkv, _prompt_bias_table(rel_bias))
    p_kv = [kv.reshape(1, N_PROMPT, w, 2, ATT_HPP, ATT_HEAD_DIM) for kv, w in zip((kv128, kv512, kv2048), WINDOWS)]
    x1 = _outproj(y_ssd, att, out_proj_w, x_prompt, g1, SEQ, 512)

    x2, x2s = _mlp(x1, sc2, sh2, g2, x1s, sc2s, sh2s, g2s, g_mlp[0], mlp_w1, mlp_w2, 0, *MLP_TILES)
    x3, u_last = _pool_prompt(x2, g_mix[1], sc1b, sh1b, g1b, pool_wb, pscale)
    p_pool = u_last[:, 1:][None]
    pool_prev_t = _pad_rows(jnp.transpose(state_pool[0], (1, 0, 2)), rows, axis=1)
    x3s, u_s = _pool_sample(x2s[0], g_mix[1], sc1sb[0], sh1sb[0], g1sb[0], pool_prev_t, pool_wb, pscale)
    s_pool = jnp.concatenate([state_pool[0][:, 1:], u_s[:N_SAMPLE, None, :]], axis=1)[None]
    y_prompt, y_s2 = _mlp(x3, sc2b, sh2b, g2b, x3s[None], sc2sb, sh2sb, g2sb, g_mlp[1], mlp_w1, mlp_w2, 1, *MLP_TILES)
    y_sample = y_s2[0, :N_SAMPLE].reshape(N_SAMPLE, 1, d)

    return (y_prompt, y_sample, p_ssm, p_conv, p_kv[0], p_kv[1], p_kv[2], p_pool,
            s_ssm, s_conv, s_kv[0], s_kv[1], s_kv[2], s_pool)
```

```python
import functools
import math

import jax
import jax.numpy as jnp
from jax import lax
from jax.experimental import pallas as pl
from jax.experimental.pallas import tpu as pltpu

F32 = jnp.float32
BF16 = jnp.bfloat16

D_MODEL = 2048
SEQ = 2048
N_PROMPT = 4
N_SAMPLE = 8
SAMPLE_ROWS = 16
PAST_LEN = 16384
SSD_HEADS = 32
SSD_HEAD_DIM = 64
SSD_INNER = SSD_HEADS * SSD_HEAD_DIM
SSD_GROUPS = 8
SSD_HEADS_PER_GROUP = SSD_HEADS // SSD_GROUPS
SSD_STATE = 128
SSD_CONV = 4
SSD_CHUNK = 128
CONV_DIM = SSD_INNER + 2 * SSD_GROUPS * SSD_STATE
ATT_HEAD_DIM = 128
WINDOWS = (128, 512, 2048)
DILATIONS = (1, 4, 16)
ATT_BLOCK = 128
ATT_UNIT_ROWS = 128
ATT_MAX_ROW_STRIDE = 4
ATT_HPP = 8
ATT_HEADS = ATT_HPP * len(WINDOWS)
ATT_QKV = ATT_HEADS * ATT_HEAD_DIM
ATT_OUT = ATT_HPP * ATT_HEAD_DIM
REL_BUCKETS = 32
REL_MAX_DIST = 2048
ZX_COLS = SSD_INNER + CONV_DIM
DT_COL0 = ZX_COLS
QKV_COL0 = ZX_COLS + SSD_HEADS
DT_PAD = 128
POOL_WINDOWS = (2, 4, 8, 16)
POOL_CH = D_MODEL // len(POOL_WINDOWS)
POOL_MAX = 16
D_FF = 4 * D_MODEL
EPS = 1e-6
NEG = -1e30
MIB = 1024 * 1024


def _params(semantics, vmem_mib):
    return pltpu.CompilerParams(dimension_semantics=semantics, vmem_limit_bytes=vmem_mib * MIB)


def _silu(x):
    h = 0.5 * x
    return h * (1.0 + jnp.tanh(h))


def _softplus(x):
    return jnp.maximum(x, 0.0) + jnp.log1p(jnp.exp(-jnp.abs(x)))


def _rms(x):
    return x * lax.rsqrt(jnp.mean(x * x, axis=-1, keepdims=True) + EPS)


def _normmod(x, g, sc, sh):
    return (_rms(x) * g) * (1.0 + sc) + sh


def _normmod_rows(u_ref, x_ref, g_ref, sc_ref, sh_ref):
    tm = x_ref.shape[1]
    rc = min(tm, 256)
    per_row = sc_ref.shape[1] != 1

    def body(i, carry):
        rows = pl.ds(pl.multiple_of(i * rc, rc), rc)
        sc = sc_ref[0, rows, :] if per_row else sc_ref[0]
        sh = sh_ref[0, rows, :] if per_row else sh_ref[0]
        u_ref[rows, :] = _normmod(x_ref[0, rows, :], g_ref[...], sc, sh).astype(BF16)
        return carry

    lax.fori_loop(0, tm // rc, body, 0)


def _bdot(a, b):
    return jnp.dot(a.astype(BF16), b.astype(BF16), preferred_element_type=F32)


def _split3(v):
    hi = v.astype(BF16)
    r = v - hi.astype(F32)
    mid = r.astype(BF16)
    lo = (r - mid.astype(F32)).astype(BF16)
    return hi, mid, lo


def _select_right(v, sel, pieces=3):
    parts = _split3(v)[:pieces]
    out = jnp.dot(parts[0], sel, preferred_element_type=F32)
    for p in parts[1:]:
        out = out + jnp.dot(p, sel, preferred_element_type=F32)
    return out


def _select_left(sel, v):
    hi, mid, lo = _split3(v)
    d = lambda a: jnp.dot(sel, a, preferred_element_type=F32)
    return d(hi) + d(mid) + d(lo)


def _ada_kernel(c_ref, w_ref, b_ref, o_ref):
    o_ref[0] = _bdot(_silu(c_ref[...]), w_ref[0]) + b_ref[0]


def _ada(c_all, ada_w, ada_b):
    depth, _, n = ada_w.shape
    tn = 1024
    return pl.pallas_call(
        _ada_kernel,
        name="ada_mod",
        grid=(depth, n // tn),
        in_specs=[
            pl.BlockSpec((SAMPLE_ROWS, D_MODEL), lambda l, j: (0, 0)),
            pl.BlockSpec((1, D_MODEL, tn), lambda l, j: (l, 0, j)),
            pl.BlockSpec((1, 1, tn), lambda l, j: (l, 0, j)),
        ],
        out_specs=pl.BlockSpec((1, SAMPLE_ROWS, tn), lambda l, j: (l, 0, j)),
        out_shape=jax.ShapeDtypeStruct((depth, SAMPLE_ROWS, n), F32),
        compiler_params=_params(("arbitrary", "arbitrary"), 40),
    )(c_all, ada_w, ada_b.reshape(depth, 1, n))


def _bdot_t(a, bt):
    return lax.dot_general(a.astype(BF16), bt.astype(BF16), (((1,), (1,)), ((), ())), preferred_element_type=F32)


def _norm_kernel(x_ref, g_ref, sc_ref, sh_ref, o_ref):
    o_ref[0] = _normmod(x_ref[0], g_ref[...], sc_ref[0], sh_ref[0]).astype(BF16)


def _norm_mod(x, g, sc, sh, tt):
    nb, t, d = x.shape
    r = sc.shape[1]
    mod_spec = pl.BlockSpec((1, r if r == 1 else tt, d), lambda b, i: (b, 0 if r == 1 else i, 0))
    return pl.pallas_call(
        _norm_kernel,
        name="norm_mod",
        grid=(nb, t // tt),
        in_specs=[pl.BlockSpec((1, tt, d), lambda b, i: (b, i, 0)), pl.BlockSpec((1, d), lambda b, i: (0, 0)),
                  mod_spec, mod_spec],
        out_specs=pl.BlockSpec((1, tt, d), lambda b, i: (b, i, 0)),
        out_shape=jax.ShapeDtypeStruct((nb, t, d), BF16),
        compiler_params=_params(("arbitrary", "arbitrary"), 40),
    )(x, g, sc, sh)


def _inproj_kernel(*refs, mode):
    if mode == "dt":
        u_ref, us_ref, w_ref, wdt_ref, dtb_ref, o_ref, dt_ref, os_ref, dts_ref, ue_ref = refs
    else:
        u_ref, us_ref, w_ref, qg_ref, kg_ref, o_ref, os_ref, ue_ref = refs
    j = pl.program_id(2)
    tm = u_ref.shape[1]
    first_seq = pl.program_id(0) == 0

    @pl.when(j == 0)
    def _():
        rc = 256

        def copy_rows(i, carry):
            rows = pl.ds(pl.multiple_of(i * rc, rc), rc)
            ue_ref[rows, :] = u_ref[0, rows, :]
            return carry

        lax.fori_loop(0, tm // rc, copy_rows, 0)

        @pl.when(first_seq)
        def _():
            ue_ref[tm:, :] = us_ref[0]

        if mode == "dt":
            dt_all = _softplus(_bdot_t(ue_ref[...], wdt_ref[...]) + dtb_ref[...])
            dt_ref[0] = dt_all[0:tm]

            @pl.when(first_seq)
            def _():
                dts_ref[0] = dt_all[tm:]

    acc = _bdot_t(ue_ref[...], w_ref[...])
    if mode == "dt":
        o_ref[0] = acc[0:tm]

        @pl.when(first_seq)
        def _():
            os_ref[j] = acc[tm:]
    else:
        heads = o_ref.shape[1]
        q_tiles = ATT_HEADS // heads

        def emit(head_fn):
            for hc in range(heads):
                val = head_fn(acc[:, hc * ATT_HEAD_DIM:(hc + 1) * ATT_HEAD_DIM])
                o_ref[0, hc] = val[0:tm]

                @pl.when(first_seq)
                def _():
                    os_ref[0, j * heads + hc] = val[tm:]

        @pl.when(j < 2 * q_tiles)
        def _():
            gain = jnp.where(j < q_tiles, qg_ref[...], kg_ref[...])
            emit(lambda a: _rms(a) * gain)

        @pl.when(j >= 2 * q_tiles)
        def _():
            emit(lambda a: a)


def _inproj(u, us, wt, row0, n_cols, tn, dt_b=None, qk_gains=None):
    nb, t, d = u.shape
    tm = t
    rows = us.shape[1]
    has_dt = dt_b is not None
    if row0 % tn == 0:
        w_spec = pl.BlockSpec((tn, d), lambda b, i, j: (row0 // tn + j, 0))
    else:
        assert row0 % 8 == 0
        w_spec = pl.BlockSpec((pl.Element(tn), pl.Element(d)),
                              lambda b, i, j: (pl.multiple_of(row0 + j * tn, 8), 0))
    in_specs = [pl.BlockSpec((1, tm, d), lambda b, i, j: (b, i, 0)),
                pl.BlockSpec((1, rows, d), lambda b, i, j: (0, 0, 0)), w_spec]
    args = [u, us, wt]
    n_tiles = n_cols // tn
    if has_dt:
        in_specs += [pl.BlockSpec((DT_PAD, d), lambda b, i, j: (DT_COL0 // DT_PAD, 0)),
                     pl.BlockSpec((1, DT_PAD), lambda b, i, j: (0, 0))]
        args += [wt, dt_b]
        out_specs = [pl.BlockSpec((1, tm, tn), lambda b, i, j: (b, i, j)),
                     pl.BlockSpec((1, tm, DT_PAD), lambda b, i, j: (b, i, 0)),
                     pl.BlockSpec((n_tiles, rows, tn), lambda b, i, j: (0, 0, 0)),
                     pl.BlockSpec((1, rows, DT_PAD), lambda b, i, j: (0, 0, 0))]
        out_shape = [jax.ShapeDtypeStruct((nb, t, n_cols), F32), jax.ShapeDtypeStruct((nb, t, DT_PAD), F32),
                     jax.ShapeDtypeStruct((n_tiles, rows, tn), F32), jax.ShapeDtypeStruct((1, rows, DT_PAD), F32)]
    else:
        hd = ATT_HEAD_DIM
        in_specs += [pl.BlockSpec((1, hd), lambda b, i, j: (0, 0))] * 2
        args += list(qk_gains)
        out_specs = [pl.BlockSpec((1, tn // hd, tm, hd), lambda b, i, j: (b, j, i, 0)),
                     pl.BlockSpec((1, n_cols // hd, rows, hd), lambda b, i, j: (0, 0, 0, 0))]
        out_shape = [jax.ShapeDtypeStruct((nb, n_cols // hd, t, hd), F32),
                     jax.ShapeDtypeStruct((1, n_cols // hd, rows, hd), F32)]
    return pl.pallas_call(
        functools.partial(_inproj_kernel, mode="dt" if has_dt else "qkv"),
        name="in_proj",
        grid=(nb, t // tm, n_tiles),
        in_specs=in_specs,
        out_specs=out_specs,
        out_shape=out_shape,
        scratch_shapes=[pltpu.VMEM((tm + rows, d), BF16)],
        compiler_params=_params(("arbitrary", "arbitrary", "arbitrary"), 60),
    )(*args)


def _ssd_kernel(z_ref, xraw_ref, bcraw_ref, dt_ref, cw_ref, cb_ref, alog_ref, dskip_ref, g_ref,
                c0_ref, c1_ref, c2_ref, n0_ref, n1_ref, n2_ref,
                y_ref, hout_ref, o0_ref, o1_ref, o2_ref,
                h_ref, yacc_ref, act_ref, carry_ref, k0_ref, k1_ref, k2_ref, *, shift_blocks):
    q = SSD_CHUNK
    gw = SSD_HEADS_PER_GROUP * SSD_HEAD_DIM
    slab = 512
    chunk = pl.program_id(1)

    step = pl.program_id(0) * pl.num_programs(1) + chunk
    for x_ref, new_ref, o_ref, k_ref, per in zip((c0_ref, c1_ref, c2_ref), (n0_ref, n1_ref, n2_ref),
                                                 (o0_ref, o1_ref, o2_ref), (k0_ref, k1_ref, k2_ref), shift_blocks):
        def shift_block(x_ref=x_ref, new_ref=new_ref, o_ref=o_ref, k_ref=k_ref, per=per):
            tw = x_ref.shape[0]

            @pl.when(step % per == 0)
            def _():
                k_ref[...] = new_ref[step // per]

            o_ref[0:tw - 1] = x_ref[1:tw]
            o_ref[tw - 1] = k_ref[...]
            k_ref[...] = x_ref[0]

        if N_SAMPLE * per == N_PROMPT * (SEQ // SSD_CHUNK):
            shift_block()
        else:
            pl.when(step < N_SAMPLE * per)(shift_block)

    @pl.when(chunk == 0)
    def _():
        h_ref[...] = jnp.zeros(h_ref.shape, F32)
        carry_ref[...] = jnp.zeros(carry_ref.shape, F32)

    sub = lax.broadcasted_iota(jnp.int32, (q // 8, 8, slab), 1)
    for s in range(CONV_DIM // slab):
        cs = slice(s * slab, (s + 1) * slab)
        src, off = (xraw_ref, s * slab) if s * slab < SSD_INNER else (bcraw_ref, s * slab - SSD_INNER)
        x = src[0, :, off:off + slab]
        x3 = x.reshape(q // 8, 8, slab)
        carry = carry_ref[:, cs]
        w = cw_ref[:, cs]
        taps = [w[SSD_CONV - 1 - k:SSD_CONV - k] for k in range(SSD_CONV)]
        acc = cb_ref[:, cs] + taps[0] * x3
        for k in range(1, SSD_CONV):
            rot = pltpu.roll(x3, k, axis=1)
            rot_prev = jnp.concatenate([pltpu.roll(carry, k, axis=0)[None], rot[:-1]], axis=0)
            acc = acc + taps[k] * jnp.where(sub < k, rot_prev, rot)
        act_ref[:, cs] = _silu(acc).reshape(q, slab)
        carry_ref[:, cs] = x[q - 8:q]

    row = lax.broadcasted_iota(jnp.int32, (q, q), 0)
    col = lax.broadcasted_iota(jnp.int32, (q, q), 1)
    causal = row >= col
    tri = jnp.where(causal, 1.0, 0.0).astype(BF16)

    dt = dt_ref[0]
    a = dt * (-jnp.exp(alog_ref[...]))
    acum = _select_left(tri, a)
    acum_t = acum.T
    dt_t = dt.T
    last = acum[q - 1:q, :]
    decay_end = jnp.exp(last - acum) * dt
    eacum = jnp.exp(acum)
    chunk_decay = jnp.broadcast_to(jnp.exp(last), (8, DT_PAD))

    first_head_lanes = lax.broadcasted_iota(jnp.int32, (q, 2 * SSD_HEAD_DIM), 1) < SSD_HEAD_DIM
    lane_head = lax.broadcasted_iota(jnp.int32, (DT_PAD, gw), 1) // SSD_HEAD_DIM
    row_head = lax.broadcasted_iota(jnp.int32, (DT_PAD, gw), 0)
    b0 = SSD_INNER
    c0 = SSD_INNER + SSD_GROUPS * SSD_STATE
    ssq = jnp.zeros((q, 1), F32)

    for g in range(SSD_GROUPS):
        cs = slice(g * gw, (g + 1) * gw)
        expand = jnp.where(lane_head + g * SSD_HEADS_PER_GROUP == row_head, 1.0, 0.0).astype(BF16)
        decay_end_x = _select_right(decay_end, expand, pieces=2)
        eacum_x = _select_right(eacum, expand, pieces=2)
        chunk_decay_x = _select_right(chunk_decay, expand)[0:1]

        xs = act_ref[:, cs]
        bg_t = act_ref[:, b0 + g * SSD_STATE:b0 + (g + 1) * SSD_STATE].T.astype(BF16)
        cg = act_ref[:, c0 + g * SSD_STATE:c0 + (g + 1) * SSD_STATE].astype(BF16)
        cb = jnp.dot(cg, bg_t, preferred_element_type=F32)
        h_prev = h_ref[:, cs]
        y_g = jnp.dot(cg, h_prev.astype(BF16), preferred_element_type=F32) * eacum_x
        h_ref[:, cs] = h_prev * chunk_decay_x + jnp.dot(bg_t, (xs * decay_end_x).astype(BF16),
                                                        preferred_element_type=F32)
        xg = xs.astype(BF16)

        def y_diag(e):
            hd = g * SSD_HEADS_PER_GROUP + e
            seg = acum[:, hd:hd + 1] - acum_t[hd:hd + 1, :]
            lw = jnp.where(causal, jnp.exp(seg), 0.0) * dt_t[hd:hd + 1, :]
            pair_lanes = slice((e // 2) * 2 * SSD_HEAD_DIM, (e // 2 + 1) * 2 * SSD_HEAD_DIM)
            return jnp.dot((cb * lw).astype(BF16), xg[:, pair_lanes], preferred_element_type=F32)

        y_pairs = [jnp.where(first_head_lanes, y_diag(2 * p), y_diag(2 * p + 1))
                   for p in range(SSD_HEADS_PER_GROUP // 2)]
        y_g = y_g + jnp.concatenate(y_pairs, axis=1)
        y_g = (y_g + dskip_ref[:, cs] * xs) * _silu(z_ref[0, :, cs])
        yacc_ref[:, cs] = y_g
        ssq = ssq + jnp.sum(y_g * y_g, axis=-1, keepdims=True)

    inv_rms = lax.rsqrt(ssq * (1.0 / SSD_INNER) + EPS)
    y_ref[0] = (yacc_ref[...] * inv_rms * g_ref[...]).astype(BF16)

    @pl.when(chunk == pl.num_programs(1) - 1)
    def _():
        hout_ref[0] = h_ref[...].T


def _ssd_prompt(zx, dt, conv_w, conv_b, a_log_pad, dskip_x, norm_g, caches, new_rows):
    nb, t, _ = zx.shape
    q = SSD_CHUNK
    n_chunks = t // q
    wide = SSD_INNER
    assert ZX_COLS == 3 * wide
    row = caches[0].shape[3:]
    cache_specs, shift_blocks = [], []
    for cache in caches:
        w = cache.shape[2]
        tw = min(w, 256)
        per = w // tw
        assert N_SAMPLE * per <= nb * n_chunks
        shift_blocks.append(per)

        def block_index(b, c, per=per):
            s = jnp.minimum(b * n_chunks + c, N_SAMPLE * per - 1)
            return (0, s // per, per - 1 - s % per, 0, 0, 0)

        cache_specs.append(pl.BlockSpec((None, None, tw, *row), block_index))
    new_specs = [pl.BlockSpec(n.shape, lambda b, c: (0, 0, 0, 0)) for n in new_rows]
    return pl.pallas_call(
        functools.partial(_ssd_kernel, shift_blocks=tuple(shift_blocks)),
        name="ssd_chunk",
        grid=(nb, n_chunks),
        in_specs=[
            pl.BlockSpec((1, q, wide), lambda b, c: (b, c, 0)),
            pl.BlockSpec((1, q, wide), lambda b, c: (b, c, 1)),
            pl.BlockSpec((1, q, wide), lambda b, c: (b, c, 2)),
            pl.BlockSpec((1, q, DT_PAD), lambda b, c: (b, c, 0)),
            pl.BlockSpec((SSD_CONV, CONV_DIM), lambda b, c: (0, 0)),
            pl.BlockSpec((1, CONV_DIM), lambda b, c: (0, 0)),
            pl.BlockSpec((1, DT_PAD), lambda b, c: (0, 0)),
            pl.BlockSpec((1, SSD_INNER), lambda b, c: (0, 0)),
            pl.BlockSpec((1, SSD_INNER), lambda b, c: (0, 0)),
            *cache_specs, *new_specs,
        ],
        out_specs=[
            pl.BlockSpec((1, q, SSD_INNER), lambda b, c: (b, c, 0)),
            pl.BlockSpec((1, SSD_INNER, SSD_STATE), lambda b, c: (b, 0, 0)),
            *cache_specs,
        ],
        out_shape=[
            jax.ShapeDtypeStruct((nb, t, SSD_INNER), BF16),
            jax.ShapeDtypeStruct((nb, SSD_INNER, SSD_STATE), F32),
            *[jax.ShapeDtypeStruct(c.shape, c.dtype) for c in caches],
        ],
        scratch_shapes=[pltpu.VMEM((SSD_STATE, SSD_INNER), F32), pltpu.VMEM((q, SSD_INNER), F32),
                        pltpu.VMEM((q, CONV_DIM), F32), pltpu.VMEM((8, CONV_DIM), F32),
                        *[pltpu.VMEM(row, F32) for _ in caches]],
        compiler_params=_params(("arbitrary", "arbitrary"), 48),
    )(zx, zx, zx, dt, conv_w, conv_b, a_log_pad, dskip_x, norm_g, *caches, *new_rows)


def _attn_kernel(q0, q1, q2, k0, k1, k2, v0, v1, v2, bias0, bias1, bias2,
                 att_ref, kv0_ref, kv1_ref, kv2_ref, qb_s, kb_s, vb_s, ok_s, os_, ls, tmp_s, ltmp_s, kv_sem):
    blk = ATT_BLOCK
    ur = ATT_UNIT_ROWS
    hd = ATT_HEAD_DIM
    scale = 1.0 / math.sqrt(hd)
    batch, slot = pl.program_id(0), pl.program_id(1)
    qrefs, krefs, vrefs, brefs = (q0, q1, q2), (k0, k1, k2), (v0, v1, v2), (bias0, bias1, bias2)
    kvrefs = (kv0_ref, kv1_ref, kv2_ref)
    nt = (((1,), (1,)), ((), ()))

    def cache_copies():
        copies = []
        for g, w in enumerate(WINDOWS):
            for part, src in enumerate((krefs[g], vrefs[g])):
                cols = pl.ds(pl.multiple_of((part * ATT_HPP + slot) * hd, hd), hd)
                copies.append(pltpu.make_async_copy(
                    src.at[0, 0, pl.ds(SEQ - w, w), :], kvrefs[g].at[batch, :, cols], kv_sem.at[2 * g + part]))
        return copies

    for cp in cache_copies():
        cp.start()

    def compute():
        row = lax.broadcasted_iota(jnp.int32, (blk, 2 * blk), 0)
        col = lax.broadcasted_iota(jnp.int32, (blk, 2 * blk), 1)
        ok_s[...] = jnp.where(jnp.where(col < blk, col - row, row - (col - blk)) >= 0, 1.0, 0.0)

        f = ATT_MAX_ROW_STRIDE
        seg = SEQ // f

        def strided(start, size, stride):
            return pl.ds(start, size) if stride == 1 else pl.ds(start, size, stride=stride)

        def regroup(src_ref, dil, emit):
            if dil <= f:
                sub = SEQ // dil
                ch = min(sub, 256)
                for r in range(dil):
                    for c in range(sub // ch):
                        emit(pl.ds(r * sub + c * ch, ch), src_ref[0, 0, strided(r + c * ch * dil, ch, dil), :])
            else:
                assert dil == f * f
                for r1 in range(f):
                    for c in range(seg // 256):
                        tmp_s[pl.ds(r1 * seg + c * 256, 256), :] = src_ref[0, 0, strided(r1 + c * 256 * f, 256, f), :]
                sub = seg // f
                for r1 in range(f):
                    for r2 in range(f):
                        emit(pl.ds((f * r2 + r1) * sub, sub), tmp_s[strided(r1 * seg + r2, sub, f), :])

        for g, dil in enumerate(DILATIONS):
            vb_s[g, :, hd:2 * hd] = jnp.ones((SEQ, hd), BF16)

            def put_q(rows, val, g=g):
                qb_s[g, rows, :] = val.astype(BF16)

            def put_k(rows, val, g=g):
                kb_s[g, rows, :] = val.astype(BF16)

            def put_v(rows, val, g=g):
                vb_s[g, rows, 0:hd] = val.astype(BF16)

            regroup(qrefs[g], dil, put_q)
            regroup(krefs[g], dil, put_k)
            regroup(vrefs[g], dil, put_v)

        for g, dil in enumerate(DILATIONS):
            nblk = SEQ // (blk * dil)
            for r in range(dil):
                for n in range(nblk):
                    base = (r * nblk + n) * blk
                    key0, col0 = (base - blk, 0) if n > 0 else (base, blk)
                    kcat = kb_s[g, key0:base + blk, :]
                    vcat = vb_s[g, key0:base + blk, :]
                    for half in range(blk // ur):
                        rs = slice(half * ur, (half + 1) * ur)
                        qb = qb_s[g, base + half * ur:base + (half + 1) * ur, :]
                        l = lax.dot_general(qb, kcat, nt, preferred_element_type=F32)
                        l = jnp.where(ok_s[rs, col0:2 * blk] > 0.5, l * scale + brefs[g][0, rs, col0:2 * blk], NEG)
                        m = jnp.max(l, axis=-1, keepdims=True)
                        p = jnp.exp(l - m).astype(BF16)
                        oe = jnp.dot(p, vcat, preferred_element_type=F32)
                        den = oe[:, hd:2 * hd]
                        s0 = n * blk + half * ur
                        if dil <= f:
                            dst = strided(r + s0 * dil, ur, dil)
                            os_[g, dst, :] = oe[:, 0:hd] / den
                            ls[g, dst, :] = m + jnp.log(den)
                        else:
                            dst = strided((r % f) * seg + f * s0 + r // f, ur, f)
                            tmp_s[dst, :] = oe[:, 0:hd] / den
                            ltmp_s[dst, :] = m + jnp.log(den)
            if dil > f:
                for r1 in range(f):
                    for c in range(seg // 256):
                        dst = strided(r1 + c * 256 * f, 256, f)
                        os_[g, dst, :] = tmp_s[pl.ds(r1 * seg + c * 256, 256), :]
                        ls[g, dst, :] = ltmp_s[pl.ds(r1 * seg + c * 256, 256), :]

        def mix_body(i, carry):
            rows = pl.ds(pl.multiple_of(i * blk, blk), blk)
            l0, l1, l2 = ls[0, rows, :], ls[1, rows, :], ls[2, rows, :]
            mx = jnp.maximum(jnp.maximum(l0, l1), l2)
            e0, e1, e2 = jnp.exp(l0 - mx), jnp.exp(l1 - mx), jnp.exp(l2 - mx)
            num = e0 * os_[0, rows, :] + e1 * os_[1, rows, :] + e2 * os_[2, rows, :]
            att_ref[0, rows, :] = (num / (e0 + e1 + e2)).astype(BF16)
            return carry
        lax.fori_loop(0, SEQ // blk, mix_body, 0)

    compute()
    for cp in cache_copies():
        cp.wait()


def _attn_prompt(qkv, bias):
    nb = qkv.shape[0]
    hd = ATT_HEAD_DIM

    def head_spec(part, g):
        return pl.BlockSpec((1, 1, SEQ, hd), lambda b, h: (b, part * ATT_HEADS + g * ATT_HPP + h, 0, 0))

    in_specs = [head_spec(part, g) for part in range(3) for g in range(3)]
    in_specs += [pl.BlockSpec((1, ATT_BLOCK, 2 * ATT_BLOCK), functools.partial(lambda b, h, g: (g * ATT_HPP + h, 0, 0), g=g))
                 for g in range(3)]
    out_specs = [pl.BlockSpec((1, SEQ, hd), lambda b, h: (b, 0, h))]
    out_shape = [jax.ShapeDtypeStruct((nb, SEQ, ATT_OUT), BF16)]
    for w in WINDOWS:
        out_specs.append(pl.BlockSpec(memory_space=pl.ANY))
        out_shape.append(jax.ShapeDtypeStruct((nb, w, 2 * ATT_OUT), F32))
    return pl.pallas_call(
        _attn_kernel,
        name="attn_prompt",
        grid=(nb, ATT_HPP),
        in_specs=in_specs,
        out_specs=out_specs,
        out_shape=out_shape,
        scratch_shapes=[
            pltpu.VMEM((3, SEQ, hd), BF16), pltpu.VMEM((3, SEQ, hd), BF16), pltpu.VMEM((3, SEQ, 2 * hd), BF16),
            pltpu.VMEM((ATT_BLOCK, 2 * ATT_BLOCK), F32),
            pltpu.VMEM((3, SEQ, hd), F32), pltpu.VMEM((3, SEQ, hd), F32),
            pltpu.VMEM((SEQ, hd), F32), pltpu.VMEM((SEQ, hd), F32),
            pltpu.SemaphoreType.DMA((2 * len(WINDOWS),)),
        ],
        compiler_params=_params(("arbitrary", "arbitrary"), 48),
    )(*([qkv] * 9), bias, bias, bias)


def _outproj_kernel(y_ref, a_ref, wy_ref, wa_ref, x_ref, g1_ref, o_ref):
    mix = _bdot(y_ref[0], wy_ref[0]) + _bdot(a_ref[0], wa_ref[0])
    o_ref[0] = x_ref[0] + g1_ref[0] * mix


def _outproj(y, att, w, x, g1, tm, tn):
    nb, t, d = x.shape
    r = g1.shape[1]
    return pl.pallas_call(
        _outproj_kernel,
        name="out_proj",
        grid=(nb, t // tm, d // tn),
        in_specs=[
            pl.BlockSpec((1, tm, SSD_INNER), lambda b, i, j: (b, i, 0)),
            pl.BlockSpec((1, tm, ATT_OUT), lambda b, i, j: (b, i, 0)),
            pl.BlockSpec((1, SSD_INNER, tn), lambda b, i, j: (0, 0, j)),
            pl.BlockSpec((1, ATT_OUT, tn), lambda b, i, j: (0, SSD_INNER // ATT_OUT, j)),
            pl.BlockSpec((1, tm, tn), lambda b, i, j: (b, i, j)),
            pl.BlockSpec((1, r if r == 1 else tm, tn), lambda b, i, j: (b, 0 if r == 1 else i, j)),
        ],
        out_specs=pl.BlockSpec((1, tm, tn), lambda b, i, j: (b, i, j)),
        out_shape=jax.ShapeDtypeStruct((nb, t, d), F32),
        compiler_params=_params(("arbitrary", "arbitrary", "arbitrary"), 62),
    )(y, att, w, w, x, g1)


def _mlp_kernel(x_ref, g_ref, sc_ref, sh_ref, g2_ref, xs_ref, scs_ref, shs_ref, g2s_ref, w1_ref, w2_ref,
                o_ref, os_ref, u_ref):
    f = pl.program_id(2)
    last = pl.num_programs(2) - 1
    tm = x_ref.shape[1]
    first_tile = jnp.logical_and(pl.program_id(0) == 0, pl.program_id(1) == 0)

    def partial_out():
        h = jnp.maximum(_bdot(u_ref[...], w1_ref[0]), 0.0)
        return _bdot(h * h, w2_ref[0])

    @pl.when(f == 0)
    def _():
        _normmod_rows(u_ref, x_ref, g_ref, sc_ref, sh_ref)

        @pl.when(first_tile)
        def _():
            u_ref[tm:, :] = _normmod(xs_ref[0], g_ref[...], scs_ref[0], shs_ref[0]).astype(BF16)

        part = partial_out()
        o_ref[0] = part[0:tm]

        @pl.when(first_tile)
        def _():
            os_ref[0] = part[tm:]

    @pl.when(jnp.logical_and(f > 0, f < last))
    def _():
        part = partial_out()
        o_ref[0] += part[0:tm]

        @pl.when(first_tile)
        def _():
            os_ref[0] += part[tm:]

    @pl.when(f == last)
    def _():
        part = partial_out()
        o_ref[0] = x_ref[0] + g2_ref[0] * (o_ref[0] + part[0:tm])

        @pl.when(first_tile)
        def _():
            os_ref[0] = xs_ref[0] + g2s_ref[0] * (os_ref[0] + part[tm:])


def _mlp(x, sc, sh, g2, xs, scs, shs, g2s, g, w1, w2, layer, tm, tf):
    nb, t, d = x.shape
    rows = xs.shape[1]
    mod_spec = pl.BlockSpec((1, 1, d), lambda b, i, f: (b, 0, 0))
    row_spec = pl.BlockSpec((1, rows, d), lambda b, i, f: (0, 0, 0))
    return pl.pallas_call(
        _mlp_kernel,
        name="mlp",
        grid=(nb, t // tm, D_FF // tf),
        in_specs=[
            pl.BlockSpec((1, tm, d), lambda b, i, f: (b, i, 0)),
            pl.BlockSpec((1, d), lambda b, i, f: (0, 0)),
            mod_spec, mod_spec, mod_spec,
            row_spec, row_spec, row_spec, row_spec,
            pl.BlockSpec((1, d, tf), lambda b, i, f: (layer, 0, f)),
            pl.BlockSpec((1, tf, d), lambda b, i, f: (layer, f, 0)),
        ],
        out_specs=[pl.BlockSpec((1, tm, d), lambda b, i, f: (b, i, 0)), row_spec],
        out_shape=[jax.ShapeDtypeStruct((nb, t, d), F32), jax.ShapeDtypeStruct((1, rows, d), F32)],
        scratch_shapes=[pltpu.VMEM((tm + rows, d), BF16)],
        compiler_params=_params(("arbitrary", "arbitrary", "arbitrary"), 62),
    )(x, g, sc, sh, g2, xs, scs, shs, g2s, w1, w2)


def _pool_kernel(x_ref, g_ref, sc_ref, sh_ref, g1_ref, w_ref, scale_ref, o_ref, ulast_ref, ext_ref, *, tt):
    halo = POOL_MAX
    ti = pl.program_id(1)

    @pl.when(ti == 0)
    def _():
        ext_ref[0:halo, :] = jnp.zeros((halo, D_MODEL), F32)

    x = x_ref[0]
    u = _normmod(x, g_ref[...], sc_ref[0], sh_ref[0])
    ext_ref[halo:halo + tt, :] = u
    pos = ti * tt + lax.broadcasted_iota(jnp.int32, (tt, 1), 0)
    for gi, w in enumerate(POOL_WINDOWS):
        ch = slice(gi * POOL_CH, (gi + 1) * POOL_CH)
        acc = ext_ref[:, ch]
        span = 1
        while span < w:
            acc = acc + pltpu.roll(acc, span, axis=0)
            span *= 2
        s = acc[halo:halo + tt]
        inv_cnt = 1.0 / jnp.minimum(w, pos + 1).astype(F32)
        pooled = s * inv_cnt - u[:, ch]
        mix = _bdot(pooled, w_ref[gi]) * scale_ref[:, ch]
        o_ref[0, :, ch] = x[:, ch] + g1_ref[0, :, ch] * mix
    ulast_ref[0] = u[tt - halo:tt]
    ext_ref[0:halo, :] = u[tt - halo:tt]


def _pool_prompt(x, g, sc, sh, g1, pool_w, pool_scale):
    nb, t, d = x.shape
    tt = 512
    mod_spec = pl.BlockSpec((1, 1, d), lambda b, i: (b, 0, 0))
    return pl.pallas_call(
        functools.partial(_pool_kernel, tt=tt),
        name="pool_mix",
        grid=(nb, t // tt),
        in_specs=[
            pl.BlockSpec((1, tt, d), lambda b, i: (b, i, 0)),
            pl.BlockSpec((1, d), lambda b, i: (0, 0)),
            mod_spec, mod_spec, mod_spec,
            pl.BlockSpec((len(POOL_WINDOWS), POOL_CH, POOL_CH), lambda b, i: (0, 0, 0)),
            pl.BlockSpec((1, d), lambda b, i: (0, 0)),
        ],
        out_specs=[
            pl.BlockSpec((1, tt, d), lambda b, i: (b, i, 0)),
            pl.BlockSpec((1, POOL_MAX, d), lambda b, i: (b, 0, 0)),
        ],
        out_shape=[jax.ShapeDtypeStruct((nb, t, d), F32), jax.ShapeDtypeStruct((nb, POOL_MAX, d), F32)],
        scratch_shapes=[pltpu.VMEM((tt + POOL_MAX, d), F32)],
        compiler_params=_params(("arbitrary", "arbitrary"), 48),
    )(x, g, sc, sh, g1, pool_w, pool_scale)


def _sconv_kernel(zx_ref, prev_ref, w_ref, b_ref, o_ref):
    w = w_ref[...]
    acc = b_ref[...] + w[SSD_CONV - 1:SSD_CONV] * zx_ref[:, SSD_INNER:ZX_COLS]
    for k in range(SSD_CONV - 1):
        acc = acc + w[k:k + 1] * prev_ref[k]
    o_ref[...] = _silu(acc)


def _conv_sample(zx, prev_t, conv_w, conv_b):
    return pl.pallas_call(
        _sconv_kernel,
        out_shape=jax.ShapeDtypeStruct((SAMPLE_ROWS, CONV_DIM), F32),
        compiler_params=pltpu.CompilerParams(vmem_limit_bytes=32 * MIB),
    )(zx, prev_t, conv_w, conv_b)


def _sstep_kernel(h0_ref, x_ref, dt_ref, b_ref, c_ref, alog_ref, h_ref, y_ref):
    dt = dt_ref[0]
    h = jnp.exp(dt * (-jnp.exp(alog_ref[...]))) * h0_ref[0] + (dt * x_ref[0]) * b_ref[0]
    h_ref[0] = h
    y_ref[0] = jnp.sum(h * c_ref[0], axis=-1, keepdims=True)


def _ssd_step(h0, x, dt, bh, ch, a_log):
    nb = h0.shape[0]
    hp = (SSD_HEADS, SSD_HEAD_DIM)
    return pl.pallas_call(
        _sstep_kernel,
        grid=(nb,),
        in_specs=[
            pl.BlockSpec((1, *hp, SSD_STATE), lambda b: (b, 0, 0, 0)),
            pl.BlockSpec((1, *hp, 1), lambda b: (b, 0, 0, 0)),
            pl.BlockSpec((1, SSD_HEADS, 1, 1), lambda b: (b, 0, 0, 0)),
            pl.BlockSpec((1, SSD_HEADS, 1, SSD_STATE), lambda b: (b, 0, 0, 0)),
            pl.BlockSpec((1, SSD_HEADS, 1, SSD_STATE), lambda b: (b, 0, 0, 0)),
            pl.BlockSpec((SSD_HEADS, 1, 1), lambda b: (0, 0, 0)),
        ],
        out_specs=[
            pl.BlockSpec((1, *hp, SSD_STATE), lambda b: (b, 0, 0, 0)),
            pl.BlockSpec((1, *hp, 1), lambda b: (b, 0, 0, 0)),
        ],
        out_shape=[jax.ShapeDtypeStruct((nb, *hp, SSD_STATE), F32), jax.ShapeDtypeStruct((nb, *hp, 1), F32)],
        compiler_params=_params(("arbitrary",), 32),
    )(h0, x, dt, bh, ch, a_log)


def _sgate_kernel(y_ref, xs_ref, z_ref, dskip_ref, g_ref, o_ref):
    y = (y_ref[...] + dskip_ref[...] * xs_ref[...]) * _silu(z_ref[...])
    o_ref[...] = _rms(y) * g_ref[...]


def _gate_sample(y, xs, z, dskip_x, norm_g):
    return pl.pallas_call(
        _sgate_kernel,
        out_shape=jax.ShapeDtypeStruct(y.shape, F32),
    )(y, xs, z, dskip_x, norm_g)


def _sattn_kernel(qkv_ref, c0_ref, c1_ref, c2_ref, bbuf_ref, bnew_ref, att_ref, knew_ref, vnew_ref):
    scale = 1.0 / math.sqrt(ATT_HEAD_DIM)
    caches = (c0_ref, c1_ref, c2_ref)
    outs, lses = [], []
    for g in range(len(WINDOWS)):
        hs = g * ATT_HPP
        q = qkv_ref[0, hs:hs + ATT_HPP, :]
        k = qkv_ref[0, ATT_HEADS + hs:ATT_HEADS + hs + ATT_HPP, :]
        v = qkv_ref[0, 2 * ATT_HEADS + hs:2 * ATT_HEADS + hs + ATT_HPP, :]
        knew_ref[0, g] = k
        vnew_ref[0, g] = v
        kbuf = caches[g][:, 0]
        vbuf = caches[g][:, 1]
        lb = jnp.sum(kbuf * q[None], axis=-1, keepdims=True) * scale + bbuf_ref[g]
        ln = jnp.sum(k * q, axis=-1, keepdims=True) * scale + bnew_ref[g]
        m = jnp.maximum(jnp.max(lb, axis=0), ln)
        pb = jnp.exp(lb - m[None])
        pn = jnp.exp(ln - m)
        s = jnp.sum(pb, axis=0) + pn
        outs.append((jnp.sum(pb * vbuf, axis=0) + pn * v) / s)
        lses.append(m + jnp.log(s))
    mx = jnp.maximum(jnp.maximum(lses[0], lses[1]), lses[2])
    es = [jnp.exp(l - mx) for l in lses]
    att_ref[0] = (es[0] * outs[0] + es[1] * outs[1] + es[2] * outs[2]) / (es[0] + es[1] + es[2])


def _attn_sample(qkv, caches, bias_buf, bias_new):
    nb = qkv.shape[0]
    hd = ATT_HEAD_DIM
    in_specs = [pl.BlockSpec((1, 3 * ATT_HEADS, hd), lambda b: (b, 0, 0))]
    for _ in caches:
        in_specs.append(pl.BlockSpec((None, ATT_BLOCK, None, 2, ATT_HPP, hd), lambda b: (b, 0, 0, 0, 0, 0)))
    in_specs += [
        pl.BlockSpec((3, ATT_BLOCK, ATT_HPP, 1), lambda b: (0, 0, 0, 0)),
        pl.BlockSpec((3, ATT_HPP, 1), lambda b: (0, 0, 0)),
    ]
    return pl.pallas_call(
        _sattn_kernel,
        grid=(nb,),
        in_specs=in_specs,
        out_specs=[
            pl.BlockSpec((1, ATT_HPP, hd), lambda b: (b, 0, 0)),
            pl.BlockSpec((1, 3, ATT_HPP, hd), lambda b: (b, 0, 0, 0)),
            pl.BlockSpec((1, 3, ATT_HPP, hd), lambda b: (b, 0, 0, 0)),
        ],
        out_shape=[
            jax.ShapeDtypeStruct((nb, ATT_HPP, hd), F32),
            jax.ShapeDtypeStruct((nb, 3, ATT_HPP, hd), F32),
            jax.ShapeDtypeStruct((nb, 3, ATT_HPP, hd), F32),
        ],
        compiler_params=_params(("arbitrary",), 32),
    )(qkv, *caches, bias_buf, bias_new)


def _spool_kernel(x_ref, g_ref, sc_ref, sh_ref, g1_ref, prev_ref, w_ref, scale_ref, o_ref, u_ref):
    x = x_ref[...]
    u = _normmod(x, g_ref[...], sc_ref[...], sh_ref[...])
    u_ref[...] = u
    for gi, w in enumerate(POOL_WINDOWS):
        ch = slice(gi * POOL_CH, (gi + 1) * POOL_CH)
        s = u[:, ch]
        for k in range(1, w):
            s = s + prev_ref[POOL_MAX - 1 - k, :, ch]
        pooled = s / float(min(w, PAST_LEN + 1)) - u[:, ch]
        mix = _bdot(pooled, w_ref[gi]) * scale_ref[:, ch]
        o_ref[:, ch] = x[:, ch] + g1_ref[:, ch] * mix


def _pool_sample(x, g, sc, sh, g1, prev_t, pool_w, pool_scale):
    return pl.pallas_call(
        _spool_kernel,
        out_shape=[jax.ShapeDtypeStruct(x.shape, F32), jax.ShapeDtypeStruct(x.shape, F32)],
        compiler_params=pltpu.CompilerParams(vmem_limit_bytes=32 * MIB),
    )(x, g, sc, sh, g1, prev_t, pool_w, pool_scale)


def _rel_bucket(dist):
    max_exact = REL_BUCKETS // 2
    n = jnp.maximum(dist, 1).astype(F32)
    large = max_exact + (jnp.log(n / max_exact) / math.log(REL_MAX_DIST / max_exact)
                         * (REL_BUCKETS - max_exact)).astype(jnp.int32)
    large = jnp.minimum(large, REL_BUCKETS - 1)
    return jnp.where(dist < max_exact, dist, large)


def _prompt_bias_table(rel_bias):
    blk = ATT_BLOCK
    j = blk + jnp.arange(blk)[:, None] - jnp.arange(2 * blk)[None, :]
    tables = []
    for gi, dil in enumerate(DILATIONS):
        bucket = _rel_bucket(jnp.clip(j, 0, blk) * dil)[None]
        rb = rel_bias[:, gi * ATT_HPP:(gi + 1) * ATT_HPP].astype(F32)
        tbl = jnp.zeros((ATT_HPP, blk, 2 * blk), F32)
        for k in range(REL_BUCKETS):
            tbl = jnp.where(bucket == k, rb[k][:, None, None], tbl)
        tables.append(tbl)
    return jnp.concatenate(tables, axis=0)


def _sample_bias_table(rel_bias):
    blk = ATT_BLOCK
    bufs, news = [], []
    for gi, dil in enumerate(DILATIONS):
        tbl = rel_bias[:, gi * ATT_HPP:(gi + 1) * ATT_HPP][_rel_bucket(dil * jnp.arange(blk + 1))]
        bufs.append(tbl[blk:0:-1])
        news.append(tbl[0])
    return jnp.stack(bufs)[..., None].astype(F32), jnp.stack(news)[..., None].astype(F32)


def _pad_rows(a, rows, axis=0):
    pad = [(0, 0)] * a.ndim
    pad[axis] = (0, rows - a.shape[axis])
    return jnp.pad(a, pad)


MLP_TILES = (1024, 512)


def kernel(x_prompt, x_sample, state_ssm, state_conv, cache_kv_w128, cache_kv_w512, cache_kv_w2048, state_pool, c_prompt, c_sample, rel_bias, ada_w, ada_b, norm_mix_g, norm_mlp_g, in_proj_w, conv_w, conv_b, dt_bias, a_log, d_skip, ssd_norm_g, q_norm_g, k_norm_g, out_proj_w, pool_w, pool_scale, mlp_w1, mlp_w2):
    d = D_MODEL
    c_all = _pad_rows(jnp.concatenate([c_prompt, c_sample], axis=0), SAMPLE_ROWS)
    mod = _ada(c_all, ada_w, ada_b)

    def mods(layer):
        parts = [mod[layer, :, i * d:(i + 1) * d] for i in range(6)]
        prompt = [p[:N_PROMPT, None, :] for p in parts]
        sample = [_pad_rows(p[N_PROMPT:N_PROMPT + N_SAMPLE], SAMPLE_ROWS)[None] for p in parts]
        return prompt, sample

    w_in_t = jnp.transpose(in_proj_w[0])
    b_dt = jnp.pad(dt_bias[0], (0, DT_PAD - SSD_HEADS)).reshape(1, DT_PAD)
    a_log_pad = jnp.pad(a_log[0], (0, DT_PAD - SSD_HEADS)).reshape(1, DT_PAD)
    dskip_x = jnp.repeat(d_skip[0], SSD_HEAD_DIM).reshape(1, SSD_INNER)
    ssd_g = ssd_norm_g.reshape(1, SSD_INNER)
    qg, kg = q_norm_g.reshape(1, ATT_HEAD_DIM), k_norm_g.reshape(1, ATT_HEAD_DIM)
    cw, cb = conv_w[0], conv_b.reshape(1, CONV_DIM)
    g_mix = [norm_mix_g[l].reshape(1, d) for l in range(2)]
    g_mlp = [norm_mlp_g[l].reshape(1, d) for l in range(2)]
    pscale = pool_scale.reshape(1, d)
    pool_wb = pool_w[0].astype(BF16)

    (sh1, sc1, g1, sh2, sc2, g2), (sh1s, sc1s, g1s, sh2s, sc2s, g2s) = mods(0)
    (sh1b, sc1b, g1b, sh2b, sc2b, g2b), (sh1sb, sc1sb, g1sb, sh2sb, sc2sb, g2sb) = mods(1)

    rows = SAMPLE_ROWS
    xs0 = _pad_rows(x_sample.reshape(N_SAMPLE, d), rows)[None]
    u0 = _norm_mod(x_prompt, g_mix[0], sc1, sh1, 1024)
    u0s = _norm_mod(xs0, g_mix[0], sc1s, sh1s, rows)
    zx, dt, zx_s, dt_s = _inproj(u0, u0s, w_in_t, 0, ZX_COLS, 512, dt_b=b_dt)
    qkv, qkv_s = _inproj(u0, u0s, w_in_t, QKV_COL0, 3 * ATT_QKV, 512, qk_gains=(qg, kg))
    p_conv = zx[:, SEQ - (SSD_CONV - 1):, SSD_INNER:][None]

    zx_s = jnp.transpose(zx_s, (1, 0, 2)).reshape(rows, ZX_COLS)
    dt_s = dt_s[0]
    qkv_s = jnp.transpose(qkv_s[0, :, :N_SAMPLE], (1, 0, 2))
    xbc_new = zx_s[:N_SAMPLE, SSD_INNER:]
    s_conv = jnp.concatenate([state_conv[0][:, 1:], xbc_new[:, None, :]], axis=1)[None]
    conv_prev_t = _pad_rows(jnp.transpose(state_conv[0], (1, 0, 2)), rows, axis=1)
    xbc_s = _conv_sample(zx_s, conv_prev_t, cw, cb)
    xs_s = xbc_s[:N_SAMPLE, :SSD_INNER]
    e = SSD_HEADS_PER_GROUP
    b_s = jnp.repeat(xbc_s[:N_SAMPLE, SSD_INNER:SSD_INNER + SSD_GROUPS * SSD_STATE]
                     .reshape(N_SAMPLE, SSD_GROUPS, 1, SSD_STATE), e, axis=1)
    c_s = jnp.repeat(xbc_s[:N_SAMPLE, SSD_INNER + SSD_GROUPS * SSD_STATE:]
                     .reshape(N_SAMPLE, SSD_GROUPS, 1, SSD_STATE), e, axis=1)
    h_new, y_s = _ssd_step(state_ssm[0], xs_s.reshape(N_SAMPLE, SSD_HEADS, SSD_HEAD_DIM, 1),
                           dt_s[:N_SAMPLE, :SSD_HEADS].reshape(N_SAMPLE, SSD_HEADS, 1, 1), b_s, c_s,
                           a_log[0].reshape(SSD_HEADS, 1, 1))
    s_ssm = h_new[None]
    y_s = _gate_sample(_pad_rows(y_s.reshape(N_SAMPLE, SSD_INNER), rows), xbc_s[:, :SSD_INNER],
                       zx_s[:, :SSD_INNER], dskip_x, ssd_g)
    caches = (cache_kv_w128, cache_kv_w512, cache_kv_w2048)
    strided = [c.reshape(N_SAMPLE, ATT_BLOCK, dil, 2, ATT_HPP, ATT_HEAD_DIM) for c, dil in zip(caches, DILATIONS)]
    bias_buf, bias_new = _sample_bias_table(rel_bias)
    att_s, k_new, v_new = _attn_sample(qkv_s, strided, bias_buf, bias_new)
    new_rows = jnp.stack([k_new, v_new], axis=2)
    att_s = _pad_rows(att_s.reshape(N_SAMPLE, ATT_OUT), rows)
    x1s = _outproj(y_s[None], att_s[None], out_proj_w, xs0, g1s, rows, 512)

    y_ssd, h_fin, *s_kv = _ssd_prompt(zx, dt, cw, cb, a_log_pad, dskip_x, ssd_g, caches,
                                      [new_rows[:, gi] for gi in range(len(caches))])
    p_ssm = h_fin.reshape(1, N_PROMPT, SSD_HEADS, SSD_HEAD_DIM, SSD_STATE)
    att, kv128, kv512, kv2048 = _attn_prompt(qkv, _prompt_bias_table(rel_bias))
    p_kv = [kv.reshape(1, N_PROMPT, w, 2, ATT_HPP, ATT_HEAD_DIM) for kv, w in zip((kv128, kv512, kv2048), WINDOWS)]
    x1 = _outproj(y_ssd, att, out_proj_w, x_prompt, g1, SEQ, 512)

    x2, x2s = _mlp(x1, sc2, sh2, g2, x1s, sc2s, sh2s, g2s, g_mlp[0], mlp_w1, mlp_w2, 0, *MLP_TILES)
    x3, u_last = _pool_prompt(x2, g_mix[1], sc1b, sh1b, g1b, pool_wb, pscale)
    p_pool = u_last[:, 1:][None]
    pool_prev_t = _pad_rows(jnp.transpose(state_pool[0], (1, 0, 2)), rows, axis=1)
    x3s, u_s = _pool_sample(x2s[0], g_mix[1], sc1sb[0], sh1sb[0], g1sb[0], pool_prev_t, pool_wb, pscale)
    s_pool = jnp.concatenate([state_pool[0][:, 1:], u_s[:N_SAMPLE, None, :]], axis=1)[None]
    y_prompt, y_s2 = _mlp(x3, sc2b, sh2b, g2b, x3s[None], sc2sb, sh2sb, g2sb, g_mlp[1], mlp_w1, mlp_w2, 1, *MLP_TILES)
    y_sample = y_s2[0, :N_SAMPLE].reshape(N_SAMPLE, 1, d)

    return (y_prompt, y_sample, p_ssm, p_conv, p_kv[0], p_kv[1], p_kv[2], p_pool,
            s_ssm, s_conv, s_kv[0], s_kv[1], s_kv[2], s_pool)
```

```python
import functools
import math

import jax
import jax.numpy as jnp
from jax import lax
from jax.experimental import pallas as pl
from jax.experimental.pallas import tpu as pltpu

F32 = jnp.float32
BF16 = jnp.bfloat16

D_MODEL = 2048
SEQ = 2048
N_PROMPT = 4
N_SAMPLE = 8
SAMPLE_ROWS = 16
PAST_LEN = 16384
SSD_HEADS = 32
SSD_HEAD_DIM = 64
SSD_INNER = SSD_HEADS * SSD_HEAD_DIM
SSD_GROUPS = 8
SSD_HEADS_PER_GROUP = SSD_HEADS // SSD_GROUPS
SSD_STATE = 128
SSD_CONV = 4
SSD_CHUNK = 128
CONV_DIM = SSD_INNER + 2 * SSD_GROUPS * SSD_STATE
ATT_HEAD_DIM = 128
WINDOWS = (128, 512, 2048)
DILATIONS = (1, 4, 16)
ATT_BLOCK = 128
ATT_UNIT_ROWS = 128
ATT_MAX_ROW_STRIDE = 4
ATT_HPP = 8
ATT_HEADS = ATT_HPP * len(WINDOWS)
ATT_QKV = ATT_HEADS * ATT_HEAD_DIM
ATT_OUT = ATT_HPP * ATT_HEAD_DIM
REL_BUCKETS = 32
REL_MAX_DIST = 2048
ZX_COLS = SSD_INNER + CONV_DIM
DT_COL0 = ZX_COLS
QKV_COL0 = ZX_COLS + SSD_HEADS
DT_PAD = 128
POOL_WINDOWS = (2, 4, 8, 16)
POOL_CH = D_MODEL // len(POOL_WINDOWS)
POOL_MAX = 16
D_FF = 4 * D_MODEL
EPS = 1e-6
NEG = -1e30
MIB = 1024 * 1024


def _params(semantics, vmem_mib):
    return pltpu.CompilerParams(dimension_semantics=semantics, vmem_limit_bytes=vmem_mib * MIB)


def _silu(x):
    h = 0.5 * x
    return h * (1.0 + jnp.tanh(h))


def _softplus(x):
    return jnp.maximum(x, 0.0) + jnp.log1p(jnp.exp(-jnp.abs(x)))


def _rms(x):
    return x * lax.rsqrt(jnp.mean(x * x, axis=-1, keepdims=True) + EPS)


def _normmod(x, g, sc, sh):
    return (_rms(x) * g) * (1.0 + sc) + sh


def _normmod_rows(u_ref, x_ref, g_ref, sc_ref, sh_ref):
    tm = x_ref.shape[1]
    rc = min(tm, 256)
    per_row = sc_ref.shape[1] != 1

    def body(i, carry):
        rows = pl.ds(pl.multiple_of(i * rc, rc), rc)
        sc = sc_ref[0, rows, :] if per_row else sc_ref[0]
        sh = sh_ref[0, rows, :] if per_row else sh_ref[0]
        u_ref[rows, :] = _normmod(x_ref[0, rows, :], g_ref[...], sc, sh).astype(BF16)
        return carry

    lax.fori_loop(0, tm // rc, body, 0)


def _bdot(a, b):
    return jnp.dot(a.astype(BF16), b.astype(BF16), preferred_element_type=F32)


def _split3(v):
    hi = v.astype(BF16)
    r = v - hi.astype(F32)
    mid = r.astype(BF16)
    lo = (r - mid.astype(F32)).astype(BF16)
    return hi, mid, lo


def _select_right(v, sel, pieces=3):
    parts = _split3(v)[:pieces]
    out = jnp.dot(parts[0], sel, preferred_element_type=F32)
    for p in parts[1:]:
        out = out + jnp.dot(p, sel, preferred_element_type=F32)
    return out


def _select_left(sel, v):
    hi, mid, lo = _split3(v)
    d = lambda a: jnp.dot(sel, a, preferred_element_type=F32)
    return d(hi) + d(mid) + d(lo)


def _ada_kernel(c_ref, w_ref, b_ref, o_ref):
    o_ref[0] = _bdot(_silu(c_ref[...]), w_ref[0]) + b_ref[0]


def _ada(c_all, ada_w, ada_b):
    depth, _, n = ada_w.shape
    tn = 1024
    return pl.pallas_call(
        _ada_kernel,
        name="ada_mod",
        grid=(depth, n // tn),
        in_specs=[
            pl.BlockSpec((SAMPLE_ROWS, D_MODEL), lambda l, j: (0, 0)),
            pl.BlockSpec((1, D_MODEL, tn), lambda l, j: (l, 0, j)),
            pl.BlockSpec((1, 1, tn), lambda l, j: (l, 0, j)),
        ],
        out_specs=pl.BlockSpec((1, SAMPLE_ROWS, tn), lambda l, j: (l, 0, j)),
        out_shape=jax.ShapeDtypeStruct((depth, SAMPLE_ROWS, n), F32),
        compiler_params=_params(("arbitrary", "arbitrary"), 40),
    )(c_all, ada_w, ada_b.reshape(depth, 1, n))


def _bdot_t(a, bt):
    return lax.dot_general(a.astype(BF16), bt.astype(BF16), (((1,), (1,)), ((), ())), preferred_element_type=F32)


def _norm_kernel(x_ref, g_ref, sc_ref, sh_ref, o_ref):
    o_ref[0] = _normmod(x_ref[0], g_ref[...], sc_ref[0], sh_ref[0]).astype(BF16)


def _norm_mod(x, g, sc, sh, tt):
    nb, t, d = x.shape
    r = sc.shape[1]
    mod_spec = pl.BlockSpec((1, r if r == 1 else tt, d), lambda b, i: (b, 0 if r == 1 else i, 0))
    return pl.pallas_call(
        _norm_kernel,
        name="norm_mod",
        grid=(nb, t // tt),
        in_specs=[pl.BlockSpec((1, tt, d), lambda b, i: (b, i, 0)), pl.BlockSpec((1, d), lambda b, i: (0, 0)),
                  mod_spec, mod_spec],
        out_specs=pl.BlockSpec((1, tt, d), lambda b, i: (b, i, 0)),
        out_shape=jax.ShapeDtypeStruct((nb, t, d), BF16),
        compiler_params=_params(("arbitrary", "arbitrary"), 40),
    )(x, g, sc, sh)


def _inproj_kernel(*refs, mode):
    if mode == "dt":
        u_ref, us_ref, w_ref, wdt_ref, dtb_ref, o_ref, dt_ref, os_ref, dts_ref, ue_ref = refs
    else:
        u_ref, us_ref, w_ref, qg_ref, kg_ref, o_ref, os_ref, ue_ref = refs
    j = pl.program_id(2)
    tm = u_ref.shape[1]
    first_seq = pl.program_id(0) == 0

    @pl.when(j == 0)
    def _():
        rc = 256

        def copy_rows(i, carry):
            rows = pl.ds(pl.multiple_of(i * rc, rc), rc)
            ue_ref[rows, :] = u_ref[0, rows, :]
            return carry

        lax.fori_loop(0, tm // rc, copy_rows, 0)

        @pl.when(first_seq)
        def _():
            ue_ref[tm:, :] = us_ref[0]

        if mode == "dt":
            dt_all = _softplus(_bdot_t(ue_ref[...], wdt_ref[...]) + dtb_ref[...])
            dt_ref[0] = dt_all[0:tm]

            @pl.when(first_seq)
            def _():
                dts_ref[0] = dt_all[tm:]

    acc = _bdot_t(ue_ref[...], w_ref[...])
    if mode == "dt":
        o_ref[0] = acc[0:tm]

        @pl.when(first_seq)
        def _():
            os_ref[j] = acc[tm:]
    else:
        heads = o_ref.shape[1]
        q_tiles = ATT_HEADS // heads

        def emit(head_fn):
            for hc in range(heads):
                val = head_fn(acc[:, hc * ATT_HEAD_DIM:(hc + 1) * ATT_HEAD_DIM])
                o_ref[0, hc] = val[0:tm]

                @pl.when(first_seq)
                def _():
                    os_ref[0, j * heads + hc] = val[tm:]

        @pl.when(j < 2 * q_tiles)
        def _():
            gain = jnp.where(j < q_tiles, qg_ref[...], kg_ref[...])
            emit(lambda a: _rms(a) * gain)

        @pl.when(j >= 2 * q_tiles)
        def _():
            emit(lambda a: a)


def _inproj(u, us, wt, row0, n_cols, tn, dt_b=None, qk_gains=None):
    nb, t, d = u.shape
    tm = t
    rows = us.shape[1]
    has_dt = dt_b is not None
    if row0 % tn == 0:
        w_spec = pl.BlockSpec((tn, d), lambda b, i, j: (row0 // tn + j, 0))
    else:
        assert row0 % 8 == 0
        w_spec = pl.BlockSpec((pl.Element(tn), pl.Element(d)),
                              lambda b, i, j: (pl.multiple_of(row0 + j * tn, 8), 0))
    in_specs = [pl.BlockSpec((1, tm, d), lambda b, i, j: (b, i, 0)),
                pl.BlockSpec((1, rows, d), lambda b, i, j: (0, 0, 0)), w_spec]
    args = [u, us, wt]
    n_tiles = n_cols // tn
    if has_dt:
        in_specs += [pl.BlockSpec((DT_PAD, d), lambda b, i, j: (DT_COL0 // DT_PAD, 0)),
                     pl.BlockSpec((1, DT_PAD), lambda b, i, j: (0, 0))]
        args += [wt, dt_b]
        out_specs = [pl.BlockSpec((1, tm, tn), lambda b, i, j: (b, i, j)),
                     pl.BlockSpec((1, tm, DT_PAD), lambda b, i, j: (b, i, 0)),
                     pl.BlockSpec((n_tiles, rows, tn), lambda b, i, j: (0, 0, 0)),
                     pl.BlockSpec((1, rows, DT_PAD), lambda b, i, j: (0, 0, 0))]
        out_shape = [jax.ShapeDtypeStruct((nb, t, n_cols), F32), jax.ShapeDtypeStruct((nb, t, DT_PAD), F32),
                     jax.ShapeDtypeStruct((n_tiles, rows, tn), F32), jax.ShapeDtypeStruct((1, rows, DT_PAD), F32)]
    else:
        hd = ATT_HEAD_DIM
        in_specs += [pl.BlockSpec((1, hd), lambda b, i, j: (0, 0))] * 2
        args += list(qk_gains)
        out_specs = [pl.BlockSpec((1, tn // hd, tm, hd), lambda b, i, j: (b, j, i, 0)),
                     pl.BlockSpec((1, n_cols // hd, rows, hd), lambda b, i, j: (0, 0, 0, 0))]
        out_shape = [jax.ShapeDtypeStruct((nb, n_cols // hd, t, hd), F32),
                     jax.ShapeDtypeStruct((1, n_cols // hd, rows, hd), F32)]
    return pl.pallas_call(
        functools.partial(_inproj_kernel, mode="dt" if has_dt else "qkv"),
        name="in_proj",
        grid=(nb, t // tm, n_tiles),
        in_specs=in_specs,
        out_specs=out_specs,
        out_shape=out_shape,
        scratch_shapes=[pltpu.VMEM((tm + rows, d), BF16)],
        compiler_params=_params(("arbitrary", "arbitrary", "arbitrary"), 60),
    )(*args)


def _ssd_kernel(z_ref, xraw_ref, bcraw_ref, dt_ref, cw_ref, cb_ref, alog_ref, dskip_ref, g_ref,
                c0_ref, c1_ref, c2_ref, n0_ref, n1_ref, n2_ref,
                y_ref, hout_ref, o0_ref, o1_ref, o2_ref,
                h_ref, yacc_ref, act_ref, carry_ref, k0_ref, k1_ref, k2_ref, *, shift_blocks):
    q = SSD_CHUNK
    gw = SSD_HEADS_PER_GROUP * SSD_HEAD_DIM
    slab = 512
    chunk = pl.program_id(1)

    step = pl.program_id(0) * pl.num_programs(1) + chunk
    for x_ref, new_ref, o_ref, k_ref, per in zip((c0_ref, c1_ref, c2_ref), (n0_ref, n1_ref, n2_ref),
                                                 (o0_ref, o1_ref, o2_ref), (k0_ref, k1_ref, k2_ref), shift_blocks):
        def shift_block(x_ref=x_ref, new_ref=new_ref, o_ref=o_ref, k_ref=k_ref, per=per):
            tw = x_ref.shape[0]

            @pl.when(step % per == 0)
            def _():
                k_ref[...] = new_ref[step // per]

            o_ref[0:tw - 1] = x_ref[1:tw]
            o_ref[tw - 1] = k_ref[...]
            k_ref[...] = x_ref[0]

        if N_SAMPLE * per == N_PROMPT * (SEQ // SSD_CHUNK):
            shift_block()
        else:
            pl.when(step < N_SAMPLE * per)(shift_block)

    @pl.when(chunk == 0)
    def _():
        h_ref[...] = jnp.zeros(h_ref.shape, F32)
        carry_ref[...] = jnp.zeros(carry_ref.shape, F32)

    sub = lax.broadcasted_iota(jnp.int32, (q // 8, 8, slab), 1)
    for s in range(CONV_DIM // slab):
        cs = slice(s * slab, (s + 1) * slab)
        src, off = (xraw_ref, s * slab) if s * slab < SSD_INNER else (bcraw_ref, s * slab - SSD_INNER)
        x = src[0, :, off:off + slab]
        x3 = x.reshape(q // 8, 8, slab)
        carry = carry_ref[:, cs]
        w = cw_ref[:, cs]
        taps = [w[SSD_CONV - 1 - k:SSD_CONV - k] for k in range(SSD_CONV)]
        acc = cb_ref[:, cs] + taps[0] * x3
        for k in range(1, SSD_CONV):
            rot = pltpu.roll(x3, k, axis=1)
            rot_prev = jnp.concatenate([pltpu.roll(carry, k, axis=0)[None], rot[:-1]], axis=0)
            acc = acc + taps[k] * jnp.where(sub < k, rot_prev, rot)
        act_ref[:, cs] = _silu(acc).reshape(q, slab)
        carry_ref[:, cs] = x[q - 8:q]

    row = lax.broadcasted_iota(jnp.int32, (q, q), 0)
    col = lax.broadcasted_iota(jnp.int32, (q, q), 1)
    causal = row >= col
    tri = jnp.where(causal, 1.0, 0.0).astype(BF16)

    dt = dt_ref[0]
    a = dt * (-jnp.exp(alog_ref[...]))
    acum = _select_left(tri, a)
    acum_t = acum.T
    dt_t = dt.T
    last = acum[q - 1:q, :]
    decay_end = jnp.exp(last - acum) * dt
    eacum = jnp.exp(acum)
    chunk_decay = jnp.broadcast_to(jnp.exp(last), (8, DT_PAD))

    first_head_lanes = lax.broadcasted_iota(jnp.int32, (q, 2 * SSD_HEAD_DIM), 1) < SSD_HEAD_DIM
    lane_head = lax.broadcasted_iota(jnp.int32, (DT_PAD, gw), 1) // SSD_HEAD_DIM
    row_head = lax.broadcasted_iota(jnp.int32, (DT_PAD, gw), 0)
    b0 = SSD_INNER
    c0 = SSD_INNER + SSD_GROUPS * SSD_STATE
    ssq = jnp.zeros((q, 1), F32)

    for g in range(SSD_GROUPS):
        cs = slice(g * gw, (g + 1) * gw)
        expand = jnp.where(lane_head + g * SSD_HEADS_PER_GROUP == row_head, 1.0, 0.0).astype(BF16)
        decay_end_x = _select_right(decay_end, expand, pieces=2)
        eacum_x = _select_right(eacum, expand, pieces=2)
        chunk_decay_x = _select_right(chunk_decay, expand)[0:1]

        xs = act_ref[:, cs]
        bg_t = act_ref[:, b0 + g * SSD_STATE:b0 + (g + 1) * SSD_STATE].T.astype(BF16)
        cg = act_ref[:, c0 + g * SSD_STATE:c0 + (g + 1) * SSD_STATE].astype(BF16)
        cb = jnp.dot(cg, bg_t, preferred_element_type=F32)
        h_prev = h_ref[:, cs]
        y_g = jnp.dot(cg, h_prev.astype(BF16), preferred_element_type=F32) * eacum_x
        h_ref[:, cs] = h_prev * chunk_decay_x + jnp.dot(bg_t, (xs * decay_end_x).astype(BF16),
                                                        preferred_element_type=F32)
        xg = xs.astype(BF16)

        def y_diag(e):
            hd = g * SSD_HEADS_PER_GROUP + e
            seg = acum[:, hd:hd + 1] - acum_t[hd:hd + 1, :]
            lw = jnp.where(causal, jnp.exp(seg), 0.0) * dt_t[hd:hd + 1, :]
            pair_lanes = slice((e // 2) * 2 * SSD_HEAD_DIM, (e // 2 + 1) * 2 * SSD_HEAD_DIM)
            return jnp.dot((cb * lw).astype(BF16), xg[:, pair_lanes], preferred_element_type=F32)

        y_pairs = [jnp.where(first_head_lanes, y_diag(2 * p), y_diag(2 * p + 1))
                   for p in range(SSD_HEADS_PER_GROUP // 2)]
        y_g = y_g + jnp.concatenate(y_pairs, axis=1)
        y_g = (y_g + dskip_ref[:, cs] * xs) * _silu(z_ref[0, :, cs])
        yacc_ref[:, cs] = y_g
        ssq = ssq + jnp.sum(y_g * y_g, axis=-1, keepdims=True)

    inv_rms = lax.rsqrt(ssq * (1.0 / SSD_INNER) + EPS)
    y_ref[0] = (yacc_ref[...] * inv_rms * g_ref[...]).astype(BF16)

    @pl.when(chunk == pl.num_programs(1) - 1)
    def _():
        hout_ref[0] = h_ref[...].T


def _ssd_prompt(zx, dt, conv_w, conv_b, a_log_pad, dskip_x, norm_g, caches, new_rows):
    nb, t, _ = zx.shape
    q = SSD_CHUNK
    n_chunks = t // q
    wide = SSD_INNER
    assert ZX_COLS == 3 * wide
    row = caches[0].shape[3:]
    cache_specs, shift_blocks = [], []
    for cache in caches:
        w = cache.shape[2]
        tw = min(w, 256)
        per = w // tw
        assert N_SAMPLE * per <= nb * n_chunks
        shift_blocks.append(per)

        def block_index(b, c, per=per):
            s = jnp.minimum(b * n_chunks + c, N_SAMPLE * per - 1)
            return (0, s // per, per - 1 - s % per, 0, 0, 0)

        cache_specs.append(pl.BlockSpec((None, None, tw, *row), block_index))
    new_specs = [pl.BlockSpec(n.shape, lambda b, c: (0, 0, 0, 0)) for n in new_rows]
    return pl.pallas_call(
        functools.partial(_ssd_kernel, shift_blocks=tuple(shift_blocks)),
        name="ssd_chunk",
        grid=(nb, n_chunks),
        in_specs=[
            pl.BlockSpec((1, q, wide), lambda b, c: (b, c, 0)),
            pl.BlockSpec((1, q, wide), lambda b, c: (b, c, 1)),
            pl.BlockSpec((1, q, wide), lambda b, c: (b, c, 2)),
            pl.BlockSpec((1, q, DT_PAD), lambda b, c: (b, c, 0)),
            pl.BlockSpec((SSD_CONV, CONV_DIM), lambda b, c: (0, 0)),
            pl.BlockSpec((1, CONV_DIM), lambda b, c: (0, 0)),
            pl.BlockSpec((1, DT_PAD), lambda b, c: (0, 0)),
            pl.BlockSpec((1, SSD_INNER), lambda b, c: (0, 0)),
            pl.BlockSpec((1, SSD_INNER), lambda b, c: (0, 0)),
            *cache_specs, *new_specs,
        ],
        out_specs=[
            pl.BlockSpec((1, q, SSD_INNER), lambda b, c: (b, c, 0)),
            pl.BlockSpec((1, SSD_INNER, SSD_STATE), lambda b, c: (b, 0, 0)),
            *cache_specs,
        ],
        out_shape=[
            jax.ShapeDtypeStruct((nb, t, SSD_INNER), BF16),
            jax.ShapeDtypeStruct((nb, SSD_INNER, SSD_STATE), F32),
            *[jax.ShapeDtypeStruct(c.shape, c.dtype) for c in caches],
        ],
        scratch_shapes=[pltpu.VMEM((SSD_STATE, SSD_INNER), F32), pltpu.VMEM((q, SSD_INNER), F32),
                        pltpu.VMEM((q, CONV_DIM), F32), pltpu.VMEM((8, CONV_DIM), F32),
                        *[pltpu.VMEM(row, F32) for _ in caches]],
        compiler_params=_params(("arbitrary", "arbitrary"), 48),
    )(zx, zx, zx, dt, conv_w, conv_b, a_log_pad, dskip_x, norm_g, *caches, *new_rows)


def _attn_kernel(q0, q1, q2, k0, k1, k2, v0, v1, v2, bias0, bias1, bias2,
                 att_ref, kv0_ref, kv1_ref, kv2_ref, qb_s, kb_s, vb_s, ok_s, os_, ls, tmp_s, ltmp_s, kv_sem):
    blk = ATT_BLOCK
    ur = ATT_UNIT_ROWS
    hd = ATT_HEAD_DIM
    scale = 1.0 / math.sqrt(hd)
    batch, slot = pl.program_id(0), pl.program_id(1)
    qrefs, krefs, vrefs, brefs = (q0, q1, q2), (k0, k1, k2), (v0, v1, v2), (bias0, bias1, bias2)
    kvrefs = (kv0_ref, kv1_ref, kv2_ref)
    nt = (((1,), (1,)), ((), ()))

    def cache_copies():
        copies = []
        for g, w in enumerate(WINDOWS):
            for part, src in enumerate((krefs[g], vrefs[g])):
                cols = pl.ds(pl.multiple_of((part * ATT_HPP + slot) * hd, hd), hd)
                copies.append(pltpu.make_async_copy(
                    src.at[0, 0, pl.ds(SEQ - w, w), :], kvrefs[g].at[batch, :, cols], kv_sem.at[2 * g + part]))
        return copies

    for cp in cache_copies():
        cp.start()

    def compute():
        row = lax.broadcasted_iota(jnp.int32, (blk, 2 * blk), 0)
        col = lax.broadcasted_iota(jnp.int32, (blk, 2 * blk), 1)
        ok_s[...] = jnp.where(jnp.where(col < blk, col - row, row - (col - blk)) >= 0, 1.0, 0.0)

        f = ATT_MAX_ROW_STRIDE
        seg = SEQ // f

        def strided(start, size, stride):
            return pl.ds(start, size) if stride == 1 else pl.ds(start, size, stride=stride)

        def regroup(src_ref, dil, emit):
            if dil <= f:
                sub = SEQ // dil
                ch = min(sub, 256)
                for r in range(dil):
                    for c in range(sub // ch):
                        emit(pl.ds(r * sub + c * ch, ch), src_ref[0, 0, strided(r + c * ch * dil, ch, dil), :])
            else:
                assert dil == f * f
                for r1 in range(f):
                    for c in range(seg // 256):
                        tmp_s[pl.ds(r1 * seg + c * 256, 256), :] = src_ref[0, 0, strided(r1 + c * 256 * f, 256, f), :]
                sub = seg // f
                for r1 in range(f):
                    for r2 in range(f):
                        emit(pl.ds((f * r2 + r1) * sub, sub), tmp_s[strided(r1 * seg + r2, sub, f), :])

        for g, dil in enumerate(DILATIONS):
            vb_s[g, :, hd:2 * hd] = jnp.ones((SEQ, hd), BF16)

            def put_q(rows, val, g=g):
                qb_s[g, rows, :] = val.astype(BF16)

            def put_k(rows, val, g=g):
                kb_s[g, rows, :] = val.astype(BF16)

            def put_v(rows, val, g=g):
                vb_s[g, rows, 0:hd] = val.astype(BF16)

            regroup(qrefs[g], dil, put_q)
            regroup(krefs[g], dil, put_k)
            regroup(vrefs[g], dil, put_v)

        for g, dil in enumerate(DILATIONS):
            nblk = SEQ // (blk * dil)
            for r in range(dil):
                for n in range(nblk):
                    base = (r * nblk + n) * blk
                    key0, col0 = (base - blk, 0) if n > 0 else (base, blk)
                    kcat = kb_s[g, key0:base + blk, :]
                    vcat = vb_s[g, key0:base + blk, :]
                    for half in range(blk // ur):
                        rs = slice(half * ur, (half + 1) * ur)
                        qb = qb_s[g, base + half * ur:base + (half + 1) * ur, :]
                        l = lax.dot_general(qb, kcat, nt, preferred_element_type=F32)
                        l = jnp.where(ok_s[rs, col0:2 * blk] > 0.5, l * scale + brefs[g][0, rs, col0:2 * blk], NEG)
                        m = jnp.max(l, axis=-1, keepdims=True)
                        p = jnp.exp(l - m).astype(BF16)
                        oe = jnp.dot(p, vcat, preferred_element_type=F32)
                        den = oe[:, hd:2 * hd]
                        s0 = n * blk + half * ur
                        if dil <= f:
                            dst = strided(r + s0 * dil, ur, dil)
                            os_[g, dst, :] = oe[:, 0:hd] / den
                            ls[g, dst, :] = m + jnp.log(den)
                        else:
                            dst = strided((r % f) * seg + f * s0 + r // f, ur, f)
                            tmp_s[dst, :] = oe[:, 0:hd] / den
                            ltmp_s[dst, :] = m + jnp.log(den)
            if dil > f:
                for r1 in range(f):
                    for c in range(seg // 256):
                        dst = strided(r1 + c * 256 * f, 256, f)
                        os_[g, dst, :] = tmp_s[pl.ds(r1 * seg + c * 256, 256), :]
                        ls[g, dst, :] = ltmp_s[pl.ds(r1 * seg + c * 256, 256), :]

        def mix_body(i, carry):
            rows = pl.ds(pl.multiple_of(i * blk, blk), blk)
            l0, l1, l2 = ls[0, rows, :], ls[1, rows, :], ls[2, rows, :]
            mx = jnp.maximum(jnp.maximum(l0, l1), l2)
            e0, e1, e2 = jnp.exp(l0 - mx), jnp.exp(l1 - mx), jnp.exp(l2 - mx)
            num = e0 * os_[0, rows, :] + e1 * os_[1, rows, :] + e2 * os_[2, rows, :]
            att_ref[0, rows, :] = (num / (e0 + e1 + e2)).astype(BF16)
            return carry
        lax.fori_loop(0, SEQ // blk, mix_body, 0)

    compute()
    for cp in cache_copies():
        cp.wait()


def _attn_prompt(qkv, bias):
    nb = qkv.shape[0]
    hd = ATT_HEAD_DIM

    def head_spec(part, g):
        return pl.BlockSpec((1, 1, SEQ, hd), lambda b, h: (b, part * ATT_HEADS + g * ATT_HPP + h, 0, 0))

    in_specs = [head_spec(part, g) for part in range(3) for g in range(3)]
    in_specs += [pl.BlockSpec((1, ATT_BLOCK, 2 * ATT_BLOCK), functools.partial(lambda b, h, g: (g * ATT_HPP + h, 0, 0), g=g))
                 for g in range(3)]
    out_specs = [pl.BlockSpec((1, SEQ, hd), lambda b, h: (b, 0, h))]
    out_shape = [jax.ShapeDtypeStruct((nb, SEQ, ATT_OUT), BF16)]
    for w in WINDOWS:
        out_specs.append(pl.BlockSpec(memory_space=pl.ANY))
        out_shape.append(jax.ShapeDtypeStruct((nb, w, 2 * ATT_OUT), F32))
    return pl.pallas_call(
        _attn_kernel,
        name="attn_prompt",
        grid=(nb, ATT_HPP),
        in_specs=in_specs,
        out_specs=out_specs,
        out_shape=out_shape,
        scratch_shapes=[
            pltpu.VMEM((3, SEQ, hd), BF16), pltpu.VMEM((3, SEQ, hd), BF16), pltpu.VMEM((3, SEQ, 2 * hd), BF16),
            pltpu.VMEM((ATT_BLOCK, 2 * ATT_BLOCK), F32),
            pltpu.VMEM((3, SEQ, hd), F32), pltpu.VMEM((3, SEQ, hd), F32),
            pltpu.VMEM((SEQ, hd), F32), pltpu.VMEM((SEQ, hd), F32),
            pltpu.SemaphoreType.DMA((2 * len(WINDOWS),)),
        ],
        compiler_params=_params(("arbitrary", "arbitrary"), 48),
    )(*([qkv] * 9), bias, bias, bias)


def _outproj_kernel(y_ref, a_ref, wy_ref, wa_ref, x_ref, g1_ref, o_ref):
    mix = _bdot(y_ref[0], wy_ref[0]) + _bdot(a_ref[0], wa_ref[0])
    o_ref[0] = x_ref[0] + g1_ref[0] * mix


def _outproj(y, att, w, x, g1, tm, tn):
    nb, t, d = x.shape
    r = g1.shape[1]
    return pl.pallas_call(
        _outproj_kernel,
        name="out_proj",
        grid=(nb, t // tm, d // tn),
        in_specs=[
            pl.BlockSpec((1, tm, SSD_INNER), lambda b, i, j: (b, i, 0)),
            pl.BlockSpec((1, tm, ATT_OUT), lambda b, i, j: (b, i, 0)),
            pl.BlockSpec((1, SSD_INNER, tn), lambda b, i, j: (0, 0, j)),
            pl.BlockSpec((1, ATT_OUT, tn), lambda b, i, j: (0, SSD_INNER // ATT_OUT, j)),
            pl.BlockSpec((1, tm, tn), lambda b, i, j: (b, i, j)),
            pl.BlockSpec((1, r if r == 1 else tm, tn), lambda b, i, j: (b, 0 if r == 1 else i, j)),
        ],
        out_specs=pl.BlockSpec((1, tm, tn), lambda b, i, j: (b, i, j)),
        out_shape=jax.ShapeDtypeStruct((nb, t, d), F32),
        compiler_params=_params(("arbitrary", "arbitrary", "arbitrary"), 62),
    )(y, att, w, w, x, g1)


def _mlp_kernel(x_ref, g_ref, sc_ref, sh_ref, g2_ref, xs_ref, scs_ref, shs_ref, g2s_ref, w1_ref, w2_ref,
                o_ref, os_ref, u_ref):
    f = pl.program_id(2)
    last = pl.num_programs(2) - 1
    tm = x_ref.shape[1]
    first_tile = jnp.logical_and(pl.program_id(0) == 0, pl.program_id(1) == 0)

    def partial_out():
        h = jnp.maximum(_bdot(u_ref[...], w1_ref[0]), 0.0)
        return _bdot(h * h, w2_ref[0])

    @pl.when(f == 0)
    def _():
        _normmod_rows(u_ref, x_ref, g_ref, sc_ref, sh_ref)

        @pl.when(first_tile)
        def _():
            u_ref[tm:, :] = _normmod(xs_ref[0], g_ref[...], scs_ref[0], shs_ref[0]).astype(BF16)

        part = partial_out()
        o_ref[0] = part[0:tm]

        @pl.when(first_tile)
        def _():
            os_ref[0] = part[tm:]

    @pl.when(jnp.logical_and(f > 0, f < last))
    def _():
        part = partial_out()
        o_ref[0] += part[0:tm]

        @pl.when(first_tile)
        def _():
            os_ref[0] += part[tm:]

    @pl.when(f == last)
    def _():
        part = partial_out()
        o_ref[0] = x_ref[0] + g2_ref[0] * (o_ref[0] + part[0:tm])

        @pl.when(first_tile)
        def _():
            os_ref[0] = xs_ref[0] + g2s_ref[0] * (os_ref[0] + part[tm:])


def _mlp(x, sc, sh, g2, xs, scs, shs, g2s, g, w1, w2, layer, tm, tf):
    nb, t, d = x.shape
    rows = xs.shape[1]
    mod_spec = pl.BlockSpec((1, 1, d), lambda b, i, f: (b, 0, 0))
    row_spec = pl.BlockSpec((1, rows, d), lambda b, i, f: (0, 0, 0))
    return pl.pallas_call(
        _mlp_kernel,
        name="mlp",
        grid=(nb, t // tm, D_FF // tf),
        in_specs=[
            pl.BlockSpec((1, tm, d), lambda b, i, f: (b, i, 0)),
            pl.BlockSpec((1, d), lambda b, i, f: (0, 0)),
            mod_spec, mod_spec, mod_spec,
            row_spec, row_spec, row_spec, row_spec,
            pl.BlockSpec((1, d, tf), lambda b, i, f: (layer, 0, f)),
            pl.BlockSpec((1, tf, d), lambda b, i, f: (layer, f, 0)),
        ],
        out_specs=[pl.BlockSpec((1, tm, d), lambda b, i, f: (b, i, 0)), row_spec],
        out_shape=[jax.ShapeDtypeStruct((nb, t, d), F32), jax.ShapeDtypeStruct((1, rows, d), F32)],
        scratch_shapes=[pltpu.VMEM((tm + rows, d), BF16)],
        compiler_params=_params(("arbitrary", "arbitrary", "arbitrary"), 62),
    )(x, g, sc, sh, g2, xs, scs, shs, g2s, w1, w2)


def _pool_kernel(x_ref, g_ref, sc_ref, sh_ref, g1_ref, w_ref, scale_ref, o_ref, ulast_ref, ext_ref, *, tt):
    halo = POOL_MAX
    ti = pl.program_id(1)

    @pl.when(ti == 0)
    def _():
        ext_ref[0:halo, :] = jnp.zeros((halo, D_MODEL), F32)

    x = x_ref[0]
    u = _normmod(x, g_ref[...], sc_ref[0], sh_ref[0])
    ext_ref[halo:halo + tt, :] = u
    pos = ti * tt + lax.broadcasted_iota(jnp.int32, (tt, 1), 0)
    for gi, w in enumerate(POOL_WINDOWS):
        ch = slice(gi * POOL_CH, (gi + 1) * POOL_CH)
        acc = ext_ref[:, ch]
        span = 1
        while span < w:
            acc = acc + pltpu.roll(acc, span, axis=0)
            span *= 2
        s = acc[halo:halo + tt]
        inv_cnt = 1.0 / jnp.minimum(w, pos + 1).astype(F32)
        pooled = s * inv_cnt - u[:, ch]
        mix = _bdot(pooled, w_ref[gi]) * scale_ref[:, ch]
        o_ref[0, :, ch] = x[:, ch] + g1_ref[0, :, ch] * mix
    ulast_ref[0] = u[tt - halo:tt]
    ext_ref[0:halo, :] = u[tt - halo:tt]


def _pool_prompt(x, g, sc, sh, g1, pool_w, pool_scale):
    nb, t, d = x.shape
    tt = 512
    mod_spec = pl.BlockSpec((1, 1, d), lambda b, i: (b, 0, 0))
    return pl.pallas_call(
        functools.partial(_pool_kernel, tt=tt),
        name="pool_mix",
        grid=(nb, t // tt),
        in_specs=[
            pl.BlockSpec((1, tt, d), lambda b, i: (b, i, 0)),
            pl.BlockSpec((1, d), lambda b, i: (0, 0)),
            mod_spec, mod_spec, mod_spec,
            pl.BlockSpec((len(POOL_WINDOWS), POOL_CH, POOL_CH), lambda b, i: (0, 0, 0)),
            pl.BlockSpec((1, d), lambda b, i: (0, 0)),
        ],
        out_specs=[
            pl.BlockSpec((1, tt, d), lambda b, i: (b, i, 0)),
            pl.BlockSpec((1, POOL_MAX, d), lambda b, i: (b, 0, 0)),
        ],
        out_shape=[jax.ShapeDtypeStruct((nb, t, d), F32), jax.ShapeDtypeStruct((nb, POOL_MAX, d), F32)],
        scratch_shapes=[pltpu.VMEM((tt + POOL_MAX, d), F32)],
        compiler_params=_params(("arbitrary", "arbitrary"), 48),
    )(x, g, sc, sh, g1, pool_w, pool_scale)


def _sconv_kernel(zx_ref, prev_ref, w_ref, b_ref, o_ref):
    w = w_ref[...]
    acc = b_ref[...] + w[SSD_CONV - 1:SSD_CONV] * zx_ref[:, SSD_INNER:ZX_COLS]
    for k in range(SSD_CONV - 1):
        acc = acc + w[k:k + 1] * prev_ref[k]
    o_ref[...] = _silu(acc)


def _conv_sample(zx, prev_t, conv_w, conv_b):
    return pl.pallas_call(
        _sconv_kernel,
        out_shape=jax.ShapeDtypeStruct((SAMPLE_ROWS, CONV_DIM), F32),
        compiler_params=pltpu.CompilerParams(vmem_limit_bytes=32 * MIB),
    )(zx, prev_t, conv_w, conv_b)


def _sstep_kernel(h0_ref, x_ref, dt_ref, b_ref, c_ref, alog_ref, h_ref, y_ref):
    p = SSD_HEAD_DIM
    pad = jnp.zeros((128 - x_ref.shape[1], 128), F32)
    xt = jnp.concatenate([x_ref[0], pad], axis=0).T
    dt = dt_ref[0]
    decay = jnp.exp(dt * (-jnp.exp(alog_ref[...])))
    lane = lax.broadcasted_iota(jnp.int32, (p, 128), 1)
    yt = [jnp.zeros((p, 128), F32), jnp.zeros((p, 128), F32)]
    for hd in range(SSD_HEADS):
        r, e = divmod(hd, 2)
        g = hd // SSD_HEADS_PER_GROUP
        xcol = xt[e * p:(e + 1) * p, r:r + 1]
        h = decay[:, hd:hd + 1] * h0_ref[0, hd] + (dt[:, hd:hd + 1] * xcol) * b_ref[0, g:g + 1, :]
        h_ref[0, hd] = h
        ycol = jnp.sum(h * c_ref[0, g:g + 1, :], axis=-1, keepdims=True)
        yt[e] = jnp.where(lane == r, ycol, yt[e])
    y_ref[0] = jnp.concatenate(yt, axis=0).T[0:x_ref.shape[1], :]


def _ssd_step(h0, x, dt, b, c, a_log_pad):
    nb = h0.shape[0]
    hp = (SSD_HEADS, SSD_HEAD_DIM)
    rows = SSD_INNER // 128
    row_spec = pl.BlockSpec((1, rows, 128), lambda i: (i, 0, 0))
    gn_spec = pl.BlockSpec((1, SSD_GROUPS, SSD_STATE), lambda i: (i, 0, 0))
    h_new, y = pl.pallas_call(
        _sstep_kernel,
        grid=(nb,),
        in_specs=[
            pl.BlockSpec((1, *hp, SSD_STATE), lambda i: (i, 0, 0, 0)),
            row_spec,
            pl.BlockSpec((1, 1, DT_PAD), lambda i: (i, 0, 0)),
            gn_spec, gn_spec,
            pl.BlockSpec((1, DT_PAD), lambda i: (0, 0)),
        ],
        out_specs=[pl.BlockSpec((1, *hp, SSD_STATE), lambda i: (i, 0, 0, 0)), row_spec],
        out_shape=[jax.ShapeDtypeStruct((nb, *hp, SSD_STATE), F32), jax.ShapeDtypeStruct((nb, rows, 128), F32)],
        compiler_params=_params(("arbitrary",), 32),
    )(h0, x.reshape(nb, rows, 128), dt.reshape(nb, 1, DT_PAD), b, c, a_log_pad)
    return h_new, y.reshape(nb, SSD_INNER)


def _sgate_kernel(y_ref, xs_ref, z_ref, dskip_ref, g_ref, o_ref):
    y = (y_ref[...] + dskip_ref[...] * xs_ref[...]) * _silu(z_ref[...])
    o_ref[...] = _rms(y) * g_ref[...]


def _gate_sample(y, xs, z, dskip_x, norm_g):
    return pl.pallas_call(
        _sgate_kernel,
        out_shape=jax.ShapeDtypeStruct(y.shape, F32),
    )(y, xs, z, dskip_x, norm_g)


def _sattn_kernel(qkv_ref, c0_ref, c1_ref, c2_ref, bbuf_ref, bnew_ref, att_ref, knew_ref, vnew_ref):
    scale = 1.0 / math.sqrt(ATT_HEAD_DIM)
    caches = (c0_ref, c1_ref, c2_ref)
    outs, lses = [], []
    for g in range(len(WINDOWS)):
        hs = g * ATT_HPP
        q = qkv_ref[0, hs:hs + ATT_HPP, :]
        k = qkv_ref[0, ATT_HEADS + hs:ATT_HEADS + hs + ATT_HPP, :]
        v = qkv_ref[0, 2 * ATT_HEADS + hs:2 * ATT_HEADS + hs + ATT_HPP, :]
        knew_ref[0, g] = k
        vnew_ref[0, g] = v
        kbuf = caches[g][:, 0]
        vbuf = caches[g][:, 1]
        lb = jnp.sum(kbuf * q[None], axis=-1, keepdims=True) * scale + bbuf_ref[g]
        ln = jnp.sum(k * q, axis=-1, keepdims=True) * scale + bnew_ref[g]
        m = jnp.maximum(jnp.max(lb, axis=0), ln)
        pb = jnp.exp(lb - m[None])
        pn = jnp.exp(ln - m)
        s = jnp.sum(pb, axis=0) + pn
        outs.append((jnp.sum(pb * vbuf, axis=0) + pn * v) / s)
        lses.append(m + jnp.log(s))
    mx = jnp.maximum(jnp.maximum(lses[0], lses[1]), lses[2])
    es = [jnp.exp(l - mx) for l in lses]
    att_ref[0] = (es[0] * outs[0] + es[1] * outs[1] + es[2] * outs[2]) / (es[0] + es[1] + es[2])


def _attn_sample(qkv, caches, bias_buf, bias_new):
    nb = qkv.shape[0]
    hd = ATT_HEAD_DIM
    in_specs = [pl.BlockSpec((1, 3 * ATT_HEADS, hd), lambda b: (b, 0, 0))]
    for _ in caches:
        in_specs.append(pl.BlockSpec((None, ATT_BLOCK, None, 2, ATT_HPP, hd), lambda b: (b, 0, 0, 0, 0, 0)))
    in_specs += [
        pl.BlockSpec((3, ATT_BLOCK, ATT_HPP, 1), lambda b: (0, 0, 0, 0)),
        pl.BlockSpec((3, ATT_HPP, 1), lambda b: (0, 0, 0)),
    ]
    return pl.pallas_call(
        _sattn_kernel,
        grid=(nb,),
        in_specs=in_specs,
        out_specs=[
            pl.BlockSpec((1, ATT_HPP, hd), lambda b: (b, 0, 0)),
            pl.BlockSpec((1, 3, ATT_HPP, hd), lambda b: (b, 0, 0, 0)),
            pl.BlockSpec((1, 3, ATT_HPP, hd), lambda b: (b, 0, 0, 0)),
        ],
        out_shape=[
            jax.ShapeDtypeStruct((nb, ATT_HPP, hd), F32),
            jax.ShapeDtypeStruct((nb, 3, ATT_HPP, hd), F32),
            jax.ShapeDtypeStruct((nb, 3, ATT_HPP, hd), F32),
        ],
        compiler_params=_params(("arbitrary",), 32),
    )(qkv, *caches, bias_buf, bias_new)


def _spool_kernel(x_ref, g_ref, sc_ref, sh_ref, g1_ref, prev_ref, w_ref, scale_ref, o_ref, u_ref):
    x = x_ref[...]
    u = _normmod(x, g_ref[...], sc_ref[...], sh_ref[...])
    u_ref[...] = u
    for gi, w in enumerate(POOL_WINDOWS):
        ch = slice(gi * POOL_CH, (gi + 1) * POOL_CH)
        s = u[:, ch]
        for k in range(1, w):
            s = s + prev_ref[POOL_MAX - 1 - k, :, ch]
        pooled = s / float(min(w, PAST_LEN + 1)) - u[:, ch]
        mix = _bdot(pooled, w_ref[gi]) * scale_ref[:, ch]
        o_ref[:, ch] = x[:, ch] + g1_ref[:, ch] * mix


def _pool_sample(x, g, sc, sh, g1, prev_t, pool_w, pool_scale):
    return pl.pallas_call(
        _spool_kernel,
        out_shape=[jax.ShapeDtypeStruct(x.shape, F32), jax.ShapeDtypeStruct(x.shape, F32)],
        compiler_params=pltpu.CompilerParams(vmem_limit_bytes=32 * MIB),
    )(x, g, sc, sh, g1, prev_t, pool_w, pool_scale)


def _rel_bucket(dist):
    max_exact = REL_BUCKETS // 2
    n = jnp.maximum(dist, 1).astype(F32)
    large = max_exact + (jnp.log(n / max_exact) / math.log(REL_MAX_DIST / max_exact)
                         * (REL_BUCKETS - max_exact)).astype(jnp.int32)
    large = jnp.minimum(large, REL_BUCKETS - 1)
    return jnp.where(dist < max_exact, dist, large)


def _prompt_bias_table(rel_bias):
    blk = ATT_BLOCK
    j = blk + jnp.arange(blk)[:, None] - jnp.arange(2 * blk)[None, :]
    tables = []
    for gi, dil in enumerate(DILATIONS):
        bucket = _rel_bucket(jnp.clip(j, 0, blk) * dil)[None]
        rb = rel_bias[:, gi * ATT_HPP:(gi + 1) * ATT_HPP].astype(F32)
        tbl = jnp.zeros((ATT_HPP, blk, 2 * blk), F32)
        for k in range(REL_BUCKETS):
            tbl = jnp.where(bucket == k, rb[k][:, None, None], tbl)
        tables.append(tbl)
    return jnp.concatenate(tables, axis=0)


def _sample_bias_table(rel_bias):
    blk = ATT_BLOCK
    bufs, news = [], []
    for gi, dil in enumerate(DILATIONS):
        tbl = rel_bias[:, gi * ATT_HPP:(gi + 1) * ATT_HPP][_rel_bucket(dil * jnp.arange(blk + 1))]
        bufs.append(tbl[blk:0:-1])
        news.append(tbl[0])
    return jnp.stack(bufs)[..., None].astype(F32), jnp.stack(news)[..., None].astype(F32)


def _pad_rows(a, rows, axis=0):
    pad = [(0, 0)] * a.ndim
    pad[axis] = (0, rows - a.shape[axis])
    return jnp.pad(a, pad)


MLP_TILES = (1024, 512)


def kernel(x_prompt, x_sample, state_ssm, state_conv, cache_kv_w128, cache_kv_w512, cache_kv_w2048, state_pool, c_prompt, c_sample, rel_bias, ada_w, ada_b, norm_mix_g, norm_mlp_g, in_proj_w, conv_w, conv_b, dt_bias, a_log, d_skip, ssd_norm_g, q_norm_g, k_norm_g, out_proj_w, pool_w, pool_scale, mlp_w1, mlp_w2):
    d = D_MODEL
    c_all = _pad_rows(jnp.concatenate([c_prompt, c_sample], axis=0), SAMPLE_ROWS)
    mod = _ada(c_all, ada_w, ada_b)

    def mods(layer):
        parts = [mod[layer, :, i * d:(i + 1) * d] for i in range(6)]
        prompt = [p[:N_PROMPT, None, :] for p in parts]
        sample = [_pad_rows(p[N_PROMPT:N_PROMPT + N_SAMPLE], SAMPLE_ROWS)[None] for p in parts]
        return prompt, sample

    w_in_t = jnp.transpose(in_proj_w[0])
    b_dt = jnp.pad(dt_bias[0], (0, DT_PAD - SSD_HEADS)).reshape(1, DT_PAD)
    a_log_pad = jnp.pad(a_log[0], (0, DT_PAD - SSD_HEADS)).reshape(1, DT_PAD)
    dskip_x = jnp.repeat(d_skip[0], SSD_HEAD_DIM).reshape(1, SSD_INNER)
    ssd_g = ssd_norm_g.reshape(1, SSD_INNER)
    qg, kg = q_norm_g.reshape(1, ATT_HEAD_DIM), k_norm_g.reshape(1, ATT_HEAD_DIM)
    cw, cb = conv_w[0], conv_b.reshape(1, CONV_DIM)
    g_mix = [norm_mix_g[l].reshape(1, d) for l in range(2)]
    g_mlp = [norm_mlp_g[l].reshape(1, d) for l in range(2)]
    pscale = pool_scale.reshape(1, d)
    pool_wb = pool_w[0].astype(BF16)

    (sh1, sc1, g1, sh2, sc2, g2), (sh1s, sc1s, g1s, sh2s, sc2s, g2s) = mods(0)
    (sh1b, sc1b, g1b, sh2b, sc2b, g2b), (sh1sb, sc1sb, g1sb, sh2sb, sc2sb, g2sb) = mods(1)

    rows = SAMPLE_ROWS
    xs0 = _pad_rows(x_sample.reshape(N_SAMPLE, d), rows)[None]
    u0 = _norm_mod(x_prompt, g_mix[0], sc1, sh1, 1024)
    u0s = _norm_mod(xs0, g_mix[0], sc1s, sh1s, rows)
    zx, dt, zx_s, dt_s = _inproj(u0, u0s, w_in_t, 0, ZX_COLS, 512, dt_b=b_dt)
    qkv, qkv_s = _inproj(u0, u0s, w_in_t, QKV_COL0, 3 * ATT_QKV, 512, qk_gains=(qg, kg))
    p_conv = zx[:, SEQ - (SSD_CONV - 1):, SSD_INNER:][None]

    zx_s = jnp.transpose(zx_s, (1, 0, 2)).reshape(rows, ZX_COLS)
    dt_s = dt_s[0]
    qkv_s = jnp.transpose(qkv_s[0, :, :N_SAMPLE], (1, 0, 2))
    xbc_new = zx_s[:N_SAMPLE, SSD_INNER:]
    s_conv = jnp.concatenate([state_conv[0][:, 1:], xbc_new[:, None, :]], axis=1)[None]
    conv_prev_t = _pad_rows(jnp.transpose(state_conv[0], (1, 0, 2)), rows, axis=1)
    xbc_s = _conv_sample(zx_s, conv_prev_t, cw, cb)
    bc_s = xbc_s[:N_SAMPLE, SSD_INNER:].reshape(N_SAMPLE, 2, SSD_GROUPS, SSD_STATE)
    h_new, y_s = _ssd_step(state_ssm[0], xbc_s[:N_SAMPLE, :SSD_INNER], dt_s[:N_SAMPLE], bc_s[:, 0], bc_s[:, 1], a_log_pad)
    s_ssm = h_new[None]
    y_s = _gate_sample(_pad_rows(y_s, rows), xbc_s[:, :SSD_INNER], zx_s[:, :SSD_INNER], dskip_x, ssd_g)
    caches = (cache_kv_w128, cache_kv_w512, cache_kv_w2048)
    strided = [c.reshape(N_SAMPLE, ATT_BLOCK, dil, 2, ATT_HPP, ATT_HEAD_DIM) for c, dil in zip(caches, DILATIONS)]
    bias_buf, bias_new = _sample_bias_table(rel_bias)
    att_s, k_new, v_new = _attn_sample(qkv_s, strided, bias_buf, bias_new)
    new_rows = jnp.stack([k_new, v_new], axis=2)
    att_s = _pad_rows(att_s.reshape(N_SAMPLE, ATT_OUT), rows)
    x1s = _outproj(y_s[None], att_s[None], out_proj_w, xs0, g1s, rows, 512)

    y_ssd, h_fin, *s_kv = _ssd_prompt(zx, dt, cw, cb, a_log_pad, dskip_x, ssd_g, caches,
                                      [new_rows[:, gi] for gi in range(len(caches))])
    p_ssm = h_fin.reshape(1, N_PROMPT, SSD_HEADS, SSD_HEAD_DIM, SSD_STATE)
    att, kv128, kv512, kv2048 = _attn_prompt(qkv, _prompt_bias_table(rel_bias))
    p_kv = [kv.reshape(1, N_PROMPT, w, 2, ATT_HPP, ATT_HEAD_DIM) for kv, w in zip((kv128, kv512, kv2048), WINDOWS)]
    x1 = _outproj(y_ssd, att, out_proj_w, x_prompt, g1, SEQ, 512)

    x2, x2s = _mlp(x1, sc2, sh2, g2, x1s, sc2s, sh2s, g2s, g_mlp[0], mlp_w1, mlp_w2, 0, *MLP_TILES)
    x3, u_last = _pool_prompt(x2, g_mix[1], sc1b, sh1b, g1b, pool_wb, pscale)
    p_pool = u_last[:, 1:][None]
    pool_prev_t = _pad_rows(jnp.transpose(state_pool[0], (1, 0, 2)), rows, axis=1)
    x3s, u_s = _pool_sample(x2s[0], g_mix[1], sc1sb[0], sh1sb[0], g1sb[0], pool_prev_t, pool_wb, pscale)
    s_pool = jnp.concatenate([state_pool[0][:, 1:], u_s[:N_SAMPLE, None, :]], axis=1)[None]
    y_prompt, y_s2 = _mlp(x3, sc2b, sh2b, g2b, x3s[None], sc2sb, sh2sb, g2sb, g_mlp[1], mlp_w1, mlp_w2, 1, *MLP_TILES)
    y_sample = y_s2[0, :N_SAMPLE].reshape(N_SAMPLE, 1, d)

    return (y_prompt, y_sample, p_ssm, p_conv, p_kv[0], p_kv[1], p_kv[2], p_pool,
            s_ssm, s_conv, s_kv[0], s_kv[1], s_kv[2], s_pool)
```

```python
import functools
import math

import jax
import jax.numpy as jnp
from jax import lax
from jax.experimental import pallas as pl
from jax.experimental.pallas import tpu as pltpu

F32 = jnp.float32
BF16 = jnp.bfloat16

D_MODEL = 2048
SEQ = 2048
N_PROMPT = 4
N_SAMPLE = 8
SAMPLE_ROWS = 16
PAST_LEN = 16384
SSD_HEADS = 32
SSD_HEAD_DIM = 64
SSD_INNER = SSD_HEADS * SSD_HEAD_DIM
SSD_GROUPS = 8
SSD_HEADS_PER_GROUP = SSD_HEADS // SSD_GROUPS
SSD_STATE = 128
SSD_CONV = 4
SSD_CHUNK = 128
CONV_DIM = SSD_INNER + 2 * SSD_GROUPS * SSD_STATE
ATT_HEAD_DIM = 128
WINDOWS = (128, 512, 2048)
DILATIONS = (1, 4, 16)
ATT_BLOCK = 128
ATT_UNIT_ROWS = 128
ATT_MAX_ROW_STRIDE = 4
ATT_HPP = 8
ATT_HEADS = ATT_HPP * len(WINDOWS)
ATT_QKV = ATT_HEADS * ATT_HEAD_DIM
ATT_OUT = ATT_HPP * ATT_HEAD_DIM
REL_BUCKETS = 32
REL_MAX_DIST = 2048
ZX_COLS = SSD_INNER + CONV_DIM
DT_COL0 = ZX_COLS
QKV_COL0 = ZX_COLS + SSD_HEADS
DT_PAD = 128
POOL_WINDOWS = (2, 4, 8, 16)
POOL_CH = D_MODEL // len(POOL_WINDOWS)
POOL_MAX = 16
D_FF = 4 * D_MODEL
EPS = 1e-6
NEG = -1e30
LOG2E = math.log2(math.e)
MIB = 1024 * 1024


def _params(semantics, vmem_mib):
    return pltpu.CompilerParams(dimension_semantics=semantics, vmem_limit_bytes=vmem_mib * MIB)


def _silu(x):
    h = 0.5 * x
    return h * (1.0 + jnp.tanh(h))


def _softplus(x):
    return jnp.maximum(x, 0.0) + jnp.log1p(jnp.exp(-jnp.abs(x)))


def _rms(x):
    return x * lax.rsqrt(jnp.mean(x * x, axis=-1, keepdims=True) + EPS)


def _normmod(x, g, sc, sh):
    return (_rms(x) * g) * (1.0 + sc) + sh


def _normmod_rows(u_ref, x_ref, g_ref, sc_ref, sh_ref):
    tm = x_ref.shape[1]
    rc = min(tm, 256)
    per_row = sc_ref.shape[1] != 1

    def body(i, carry):
        rows = pl.ds(pl.multiple_of(i * rc, rc), rc)
        sc = sc_ref[0, rows, :] if per_row else sc_ref[0]
        sh = sh_ref[0, rows, :] if per_row else sh_ref[0]
        u_ref[rows, :] = _normmod(x_ref[0, rows, :], g_ref[...], sc, sh).astype(BF16)
        return carry

    lax.fori_loop(0, tm // rc, body, 0)


def _bdot(a, b):
    return jnp.dot(a.astype(BF16), b.astype(BF16), preferred_element_type=F32)


def _split3(v):
    hi = v.astype(BF16)
    r = v - hi.astype(F32)
    mid = r.astype(BF16)
    lo = (r - mid.astype(F32)).astype(BF16)
    return hi, mid, lo


def _select_right(v, sel, pieces=3):
    parts = _split3(v)[:pieces]
    out = jnp.dot(parts[0], sel, preferred_element_type=F32)
    for p in parts[1:]:
        out = out + jnp.dot(p, sel, preferred_element_type=F32)
    return out


def _select_left(sel, v):
    hi, mid, lo = _split3(v)
    d = lambda a: jnp.dot(sel, a, preferred_element_type=F32)
    return d(hi) + d(mid) + d(lo)


def _ada_kernel(c_ref, w_ref, b_ref, o_ref):
    o_ref[0] = _bdot(_silu(c_ref[...]), w_ref[0]) + b_ref[0]


def _ada(c_all, ada_w, ada_b):
    depth, _, n = ada_w.shape
    tn = 1024
    return pl.pallas_call(
        _ada_kernel,
        name="ada_mod",
        grid=(depth, n // tn),
        in_specs=[
            pl.BlockSpec((SAMPLE_ROWS, D_MODEL), lambda l, j: (0, 0)),
            pl.BlockSpec((1, D_MODEL, tn), lambda l, j: (l, 0, j)),
            pl.BlockSpec((1, 1, tn), lambda l, j: (l, 0, j)),
        ],
        out_specs=pl.BlockSpec((1, SAMPLE_ROWS, tn), lambda l, j: (l, 0, j)),
        out_shape=jax.ShapeDtypeStruct((depth, SAMPLE_ROWS, n), F32),
        compiler_params=_params(("arbitrary", "arbitrary"), 40),
    )(c_all, ada_w, ada_b.reshape(depth, 1, n))


def _bdot_t(a, bt):
    return lax.dot_general(a.astype(BF16), bt.astype(BF16), (((1,), (1,)), ((), ())), preferred_element_type=F32)


def _norm_kernel(x_ref, g_ref, sc_ref, sh_ref, o_ref):
    o_ref[0] = _normmod(x_ref[0], g_ref[...], sc_ref[0], sh_ref[0]).astype(BF16)


def _norm_mod(x, g, sc, sh, tt):
    nb, t, d = x.shape
    r = sc.shape[1]
    mod_spec = pl.BlockSpec((1, r if r == 1 else tt, d), lambda b, i: (b, 0 if r == 1 else i, 0))
    return pl.pallas_call(
        _norm_kernel,
        name="norm_mod",
        grid=(nb, t // tt),
        in_specs=[pl.BlockSpec((1, tt, d), lambda b, i: (b, i, 0)), pl.BlockSpec((1, d), lambda b, i: (0, 0)),
                  mod_spec, mod_spec],
        out_specs=pl.BlockSpec((1, tt, d), lambda b, i: (b, i, 0)),
        out_shape=jax.ShapeDtypeStruct((nb, t, d), BF16),
        compiler_params=_params(("arbitrary", "arbitrary"), 40),
    )(x, g, sc, sh)


def _inproj_kernel(*refs, mode):
    if mode == "dt":
        u_ref, us_ref, w_ref, wdt_ref, dtb_ref, o_ref, dt_ref, os_ref, dts_ref, ue_ref = refs
    else:
        u_ref, us_ref, w_ref, qg_ref, kg_ref, o_ref, os_ref, ue_ref = refs
    j = pl.program_id(2)
    tm = u_ref.shape[1]
    first_seq = pl.program_id(0) == 0

    @pl.when(j == 0)
    def _():
        rc = 256

        def copy_rows(i, carry):
            rows = pl.ds(pl.multiple_of(i * rc, rc), rc)
            ue_ref[rows, :] = u_ref[0, rows, :]
            return carry

        lax.fori_loop(0, tm // rc, copy_rows, 0)

        @pl.when(first_seq)
        def _():
            ue_ref[tm:, :] = us_ref[0]

        if mode == "dt":
            dt_all = _softplus(_bdot_t(ue_ref[...], wdt_ref[...]) + dtb_ref[...])
            dt_ref[0] = dt_all[0:tm]

            @pl.when(first_seq)
            def _():
                dts_ref[0] = dt_all[tm:]

    acc = _bdot_t(ue_ref[...], w_ref[...])
    if mode == "dt":
        o_ref[0] = acc[0:tm]

        @pl.when(first_seq)
        def _():
            os_ref[j] = acc[tm:]
    else:
        heads = o_ref.shape[1]
        q_tiles = ATT_HEADS // heads

        def emit(head_fn):
            for hc in range(heads):
                val = head_fn(acc[:, hc * ATT_HEAD_DIM:(hc + 1) * ATT_HEAD_DIM])
                o_ref[0, hc] = val[0:tm]

                @pl.when(first_seq)
                def _():
                    os_ref[0, j * heads + hc] = val[tm:]

        @pl.when(j < 2 * q_tiles)
        def _():
            gain = jnp.where(j < q_tiles, qg_ref[...], kg_ref[...])
            emit(lambda a: _rms(a) * gain)

        @pl.when(j >= 2 * q_tiles)
        def _():
            emit(lambda a: a)


def _inproj(u, us, wt, row0, n_cols, tn, dt_b=None, qk_gains=None):
    nb, t, d = u.shape
    tm = t
    rows = us.shape[1]
    has_dt = dt_b is not None
    if row0 % tn == 0:
        w_spec = pl.BlockSpec((tn, d), lambda b, i, j: (row0 // tn + j, 0))
    else:
        assert row0 % 8 == 0
        w_spec = pl.BlockSpec((pl.Element(tn), pl.Element(d)),
                              lambda b, i, j: (pl.multiple_of(row0 + j * tn, 8), 0))
    in_specs = [pl.BlockSpec((1, tm, d), lambda b, i, j: (b, i, 0)),
                pl.BlockSpec((1, rows, d), lambda b, i, j: (0, 0, 0)), w_spec]
    args = [u, us, wt]
    n_tiles = n_cols // tn
    if has_dt:
        in_specs += [pl.BlockSpec((DT_PAD, d), lambda b, i, j: (DT_COL0 // DT_PAD, 0)),
                     pl.BlockSpec((1, DT_PAD), lambda b, i, j: (0, 0))]
        args += [wt, dt_b]
        out_specs = [pl.BlockSpec((1, tm, tn), lambda b, i, j: (b, i, j)),
                     pl.BlockSpec((1, tm, DT_PAD), lambda b, i, j: (b, i, 0)),
                     pl.BlockSpec((n_tiles, rows, tn), lambda b, i, j: (0, 0, 0)),
                     pl.BlockSpec((1, rows, DT_PAD), lambda b, i, j: (0, 0, 0))]
        out_shape = [jax.ShapeDtypeStruct((nb, t, n_cols), F32), jax.ShapeDtypeStruct((nb, t, DT_PAD), F32),
                     jax.ShapeDtypeStruct((n_tiles, rows, tn), F32), jax.ShapeDtypeStruct((1, rows, DT_PAD), F32)]
    else:
        hd = ATT_HEAD_DIM
        in_specs += [pl.BlockSpec((1, hd), lambda b, i, j: (0, 0))] * 2
        args += list(qk_gains)
        out_specs = [pl.BlockSpec((1, tn // hd, tm, hd), lambda b, i, j: (b, j, i, 0)),
                     pl.BlockSpec((1, n_cols // hd, rows, hd), lambda b, i, j: (0, 0, 0, 0))]
        out_shape = [jax.ShapeDtypeStruct((nb, n_cols // hd, t, hd), F32),
                     jax.ShapeDtypeStruct((1, n_cols // hd, rows, hd), F32)]
    return pl.pallas_call(
        functools.partial(_inproj_kernel, mode="dt" if has_dt else "qkv"),
        name="in_proj",
        grid=(nb, t // tm, n_tiles),
        in_specs=in_specs,
        out_specs=out_specs,
        out_shape=out_shape,
        scratch_shapes=[pltpu.VMEM((tm + rows, d), BF16)],
        compiler_params=_params(("arbitrary", "arbitrary", "arbitrary"), 60),
    )(*args)


def _ssd_kernel(z_ref, xraw_ref, bcraw_ref, dt_ref, cw_ref, cb_ref, alog_ref, dskip_ref, g_ref,
                c0_ref, c1_ref, c2_ref, n0_ref, n1_ref, n2_ref,
                y_ref, hout_ref, o0_ref, o1_ref, o2_ref,
                h_ref, yacc_ref, act_ref, carry_ref, k0_ref, k1_ref, k2_ref, *, shift_blocks):
    q = SSD_CHUNK
    gw = SSD_HEADS_PER_GROUP * SSD_HEAD_DIM
    slab = 512
    chunk = pl.program_id(1)

    step = pl.program_id(0) * pl.num_programs(1) + chunk
    for x_ref, new_ref, o_ref, k_ref, per in zip((c0_ref, c1_ref, c2_ref), (n0_ref, n1_ref, n2_ref),
                                                 (o0_ref, o1_ref, o2_ref), (k0_ref, k1_ref, k2_ref), shift_blocks):
        def shift_block(x_ref=x_ref, new_ref=new_ref, o_ref=o_ref, k_ref=k_ref, per=per):
            tw = x_ref.shape[0]

            @pl.when(step % per == 0)
            def _():
                k_ref[...] = new_ref[step // per]

            o_ref[0:tw - 1] = x_ref[1:tw]
            o_ref[tw - 1] = k_ref[...]
            k_ref[...] = x_ref[0]

        if N_SAMPLE * per == N_PROMPT * (SEQ // SSD_CHUNK):
            shift_block()
        else:
            pl.when(step < N_SAMPLE * per)(shift_block)

    @pl.when(chunk == 0)
    def _():
        h_ref[...] = jnp.zeros(h_ref.shape, F32)
        carry_ref[...] = jnp.zeros(carry_ref.shape, F32)

    sub = lax.broadcasted_iota(jnp.int32, (q // 8, 8, slab), 1)
    for s in range(CONV_DIM // slab):
        cs = slice(s * slab, (s + 1) * slab)
        src, off = (xraw_ref, s * slab) if s * slab < SSD_INNER else (bcraw_ref, s * slab - SSD_INNER)
        x = src[0, :, off:off + slab]
        x3 = x.reshape(q // 8, 8, slab)
        carry = carry_ref[:, cs]
        w = cw_ref[:, cs]
        taps = [w[SSD_CONV - 1 - k:SSD_CONV - k] for k in range(SSD_CONV)]
        acc = cb_ref[:, cs] + taps[0] * x3
        for k in range(1, SSD_CONV):
            rot = pltpu.roll(x3, k, axis=1)
            rot_prev = jnp.concatenate([pltpu.roll(carry, k, axis=0)[None], rot[:-1]], axis=0)
            acc = acc + taps[k] * jnp.where(sub < k, rot_prev, rot)
        act_ref[:, cs] = _silu(acc).reshape(q, slab)
        carry_ref[:, cs] = x[q - 8:q]

    row = lax.broadcasted_iota(jnp.int32, (q, q), 0)
    col = lax.broadcasted_iota(jnp.int32, (q, q), 1)
    causal = row >= col
    tri = jnp.where(causal, 1.0, 0.0).astype(BF16)

    dt = dt_ref[0]
    a = dt * (-jnp.exp(alog_ref[...]))
    acum = _select_left(tri, a)
    acum_t = acum.T
    dt_t = dt.T
    last = acum[q - 1:q, :]
    decay_end = jnp.exp(last - acum) * dt
    eacum = jnp.exp(acum)
    chunk_decay = jnp.broadcast_to(jnp.exp(last), (8, DT_PAD))

    first_head_lanes = lax.broadcasted_iota(jnp.int32, (q, 2 * SSD_HEAD_DIM), 1) < SSD_HEAD_DIM
    lane_head = lax.broadcasted_iota(jnp.int32, (DT_PAD, gw), 1) // SSD_HEAD_DIM
    row_head = lax.broadcasted_iota(jnp.int32, (DT_PAD, gw), 0)
    b0 = SSD_INNER
    c0 = SSD_INNER + SSD_GROUPS * SSD_STATE
    ssq = jnp.zeros((q, 1), F32)

    for g in range(SSD_GROUPS):
        cs = slice(g * gw, (g + 1) * gw)
        expand = jnp.where(lane_head + g * SSD_HEADS_PER_GROUP == row_head, 1.0, 0.0).astype(BF16)
        decay_end_x = _select_right(decay_end, expand, pieces=2)
        eacum_x = _select_right(eacum, expand, pieces=2)
        chunk_decay_x = _select_right(chunk_decay, expand)[0:1]

        xs = act_ref[:, cs]
        bg_t = act_ref[:, b0 + g * SSD_STATE:b0 + (g + 1) * SSD_STATE].T.astype(BF16)
        cg = act_ref[:, c0 + g * SSD_STATE:c0 + (g + 1) * SSD_STATE].astype(BF16)
        cb = jnp.dot(cg, bg_t, preferred_element_type=F32)
        h_prev = h_ref[:, cs]
        y_g = jnp.dot(cg, h_prev.astype(BF16), preferred_element_type=F32) * eacum_x
        h_ref[:, cs] = h_prev * chunk_decay_x + jnp.dot(bg_t, (xs * decay_end_x).astype(BF16),
                                                        preferred_element_type=F32)
        xg = xs.astype(BF16)

        def y_diag(e):
            hd = g * SSD_HEADS_PER_GROUP + e
            seg = acum[:, hd:hd + 1] - acum_t[hd:hd + 1, :]
            lw = jnp.where(causal, jnp.exp(seg), 0.0) * dt_t[hd:hd + 1, :]
            pair_lanes = slice((e // 2) * 2 * SSD_HEAD_DIM, (e // 2 + 1) * 2 * SSD_HEAD_DIM)
            return jnp.dot((cb * lw).astype(BF16), xg[:, pair_lanes], preferred_element_type=F32)

        y_pairs = [jnp.where(first_head_lanes, y_diag(2 * p), y_diag(2 * p + 1))
                   for p in range(SSD_HEADS_PER_GROUP // 2)]
        y_g = y_g + jnp.concatenate(y_pairs, axis=1)
        y_g = (y_g + dskip_ref[:, cs] * xs) * _silu(z_ref[0, :, cs])
        yacc_ref[:, cs] = y_g
        ssq = ssq + jnp.sum(y_g * y_g, axis=-1, keepdims=True)

    inv_rms = lax.rsqrt(ssq * (1.0 / SSD_INNER) + EPS)
    y_ref[0] = (yacc_ref[...] * inv_rms * g_ref[...]).astype(BF16)

    @pl.when(chunk == pl.num_programs(1) - 1)
    def _():
        hout_ref[0] = h_ref[...].T


def _ssd_prompt(zx, dt, conv_w, conv_b, a_log_pad, dskip_x, norm_g, caches, new_rows):
    nb, t, _ = zx.shape
    q = SSD_CHUNK
    n_chunks = t // q
    wide = SSD_INNER
    assert ZX_COLS == 3 * wide
    row = caches[0].shape[3:]
    cache_specs, shift_blocks = [], []
    for cache in caches:
        w = cache.shape[2]
        tw = min(w, 256)
        per = w // tw
        assert N_SAMPLE * per <= nb * n_chunks
        shift_blocks.append(per)

        def block_index(b, c, per=per):
            s = jnp.minimum(b * n_chunks + c, N_SAMPLE * per - 1)
            return (0, s // per, per - 1 - s % per, 0, 0, 0)

        cache_specs.append(pl.BlockSpec((None, None, tw, *row), block_index))
    new_specs = [pl.BlockSpec(n.shape, lambda b, c: (0, 0, 0, 0)) for n in new_rows]
    return pl.pallas_call(
        functools.partial(_ssd_kernel, shift_blocks=tuple(shift_blocks)),
        name="ssd_chunk",
        grid=(nb, n_chunks),
        in_specs=[
            pl.BlockSpec((1, q, wide), lambda b, c: (b, c, 0)),
            pl.BlockSpec((1, q, wide), lambda b, c: (b, c, 1)),
            pl.BlockSpec((1, q, wide), lambda b, c: (b, c, 2)),
            pl.BlockSpec((1, q, DT_PAD), lambda b, c: (b, c, 0)),
            pl.BlockSpec((SSD_CONV, CONV_DIM), lambda b, c: (0, 0)),
            pl.BlockSpec((1, CONV_DIM), lambda b, c: (0, 0)),
            pl.BlockSpec((1, DT_PAD), lambda b, c: (0, 0)),
            pl.BlockSpec((1, SSD_INNER), lambda b, c: (0, 0)),
            pl.BlockSpec((1, SSD_INNER), lambda b, c: (0, 0)),
            *cache_specs, *new_specs,
        ],
        out_specs=[
            pl.BlockSpec((1, q, SSD_INNER), lambda b, c: (b, c, 0)),
            pl.BlockSpec((1, SSD_INNER, SSD_STATE), lambda b, c: (b, 0, 0)),
            *cache_specs,
        ],
        out_shape=[
            jax.ShapeDtypeStruct((nb, t, SSD_INNER), BF16),
            jax.ShapeDtypeStruct((nb, SSD_INNER, SSD_STATE), F32),
            *[jax.ShapeDtypeStruct(c.shape, c.dtype) for c in caches],
        ],
        scratch_shapes=[pltpu.VMEM((SSD_STATE, SSD_INNER), F32), pltpu.VMEM((q, SSD_INNER), F32),
                        pltpu.VMEM((q, CONV_DIM), F32), pltpu.VMEM((8, CONV_DIM), F32),
                        *[pltpu.VMEM(row, F32) for _ in caches]],
        compiler_params=_params(("arbitrary", "arbitrary"), 48),
    )(zx, zx, zx, dt, conv_w, conv_b, a_log_pad, dskip_x, norm_g, *caches, *new_rows)


def _attn_kernel(q0, q1, q2, k0, k1, k2, v0, v1, v2, bias0, bias1, bias2,
                 att_ref, kv0_ref, kv1_ref, kv2_ref, qb_s, kb_s, vb_s, ok_s, os_, ls, tmp_s, ltmp_s, kv_sem):
    blk = ATT_BLOCK
    ur = ATT_UNIT_ROWS
    hd = ATT_HEAD_DIM
    scale = LOG2E / math.sqrt(hd)
    batch, slot = pl.program_id(0), pl.program_id(1)
    qrefs, krefs, vrefs, brefs = (q0, q1, q2), (k0, k1, k2), (v0, v1, v2), (bias0, bias1, bias2)
    kvrefs = (kv0_ref, kv1_ref, kv2_ref)
    nt = (((1,), (1,)), ((), ()))

    def cache_copies():
        copies = []
        for g, w in enumerate(WINDOWS):
            for part, src in enumerate((krefs[g], vrefs[g])):
                cols = pl.ds(pl.multiple_of((part * ATT_HPP + slot) * hd, hd), hd)
                copies.append(pltpu.make_async_copy(
                    src.at[0, 0, pl.ds(SEQ - w, w), :], kvrefs[g].at[batch, :, cols], kv_sem.at[2 * g + part]))
        return copies

    for cp in cache_copies():
        cp.start()

    def compute():
        row = lax.broadcasted_iota(jnp.int32, (blk, 2 * blk), 0)
        col = lax.broadcasted_iota(jnp.int32, (blk, 2 * blk), 1)
        ok_s[...] = jnp.where(jnp.where(col < blk, col - row, row - (col - blk)) >= 0, 1.0, 0.0)

        f = ATT_MAX_ROW_STRIDE
        seg = SEQ // f

        def strided(start, size, stride):
            return pl.ds(start, size) if stride == 1 else pl.ds(start, size, stride=stride)

        def regroup(src_ref, dil, emit):
            if dil <= f:
                sub = SEQ // dil
                ch = min(sub, 256)
                for r in range(dil):
                    for c in range(sub // ch):
                        emit(pl.ds(r * sub + c * ch, ch), src_ref[0, 0, strided(r + c * ch * dil, ch, dil), :])
            else:
                assert dil == f * f
                for r1 in range(f):
                    for c in range(seg // 256):
                        tmp_s[pl.ds(r1 * seg + c * 256, 256), :] = src_ref[0, 0, strided(r1 + c * 256 * f, 256, f), :]
                sub = seg // f
                for r1 in range(f):
                    for r2 in range(f):
                        emit(pl.ds((f * r2 + r1) * sub, sub), tmp_s[strided(r1 * seg + r2, sub, f), :])

        for g, dil in enumerate(DILATIONS):
            vb_s[g, :, hd:2 * hd] = jnp.ones((SEQ, hd), BF16)

            def put_q(rows, val, g=g):
                qb_s[g, rows, :] = val.astype(BF16)

            def put_k(rows, val, g=g):
                kb_s[g, rows, :] = val.astype(BF16)

            def put_v(rows, val, g=g):
                vb_s[g, rows, 0:hd] = val.astype(BF16)

            regroup(qrefs[g], dil, put_q)
            regroup(krefs[g], dil, put_k)
            regroup(vrefs[g], dil, put_v)

        for g, dil in enumerate(DILATIONS):
            nblk = SEQ // (blk * dil)
            for r in range(dil):
                for n in range(nblk):
                    base = (r * nblk + n) * blk
                    key0, col0 = (base - blk, 0) if n > 0 else (base, blk)
                    kcat = kb_s[g, key0:base + blk, :]
                    vcat = vb_s[g, key0:base + blk, :]
                    for half in range(blk // ur):
                        rs = slice(half * ur, (half + 1) * ur)
                        qb = qb_s[g, base + half * ur:base + (half + 1) * ur, :]
                        l = lax.dot_general(qb, kcat, nt, preferred_element_type=F32)
                        l = jnp.where(ok_s[rs, col0:2 * blk] > 0.5, l * scale + brefs[g][0, rs, col0:2 * blk], NEG)
                        m = jnp.max(l, axis=-1, keepdims=True)
                        p = jnp.exp2(l - m).astype(BF16)
                        oe = jnp.dot(p, vcat, preferred_element_type=F32)
                        den = oe[:, hd:2 * hd]
                        s0 = n * blk + half * ur
                        if dil <= f:
                            dst = strided(r + s0 * dil, ur, dil)
                            os_[g, dst, :] = oe[:, 0:hd] / den
                            ls[g, dst, :] = m + jnp.log2(den)
                        else:
                            dst = strided((r % f) * seg + f * s0 + r // f, ur, f)
                            tmp_s[dst, :] = oe[:, 0:hd] / den
                            ltmp_s[dst, :] = m + jnp.log2(den)
            if dil > f:
                for r1 in range(f):
                    for c in range(seg // 256):
                        dst = strided(r1 + c * 256 * f, 256, f)
                        os_[g, dst, :] = tmp_s[pl.ds(r1 * seg + c * 256, 256), :]
                        ls[g, dst, :] = ltmp_s[pl.ds(r1 * seg + c * 256, 256), :]

        def mix_body(i, carry):
            rows = pl.ds(pl.multiple_of(i * blk, blk), blk)
            l0, l1, l2 = ls[0, rows, :], ls[1, rows, :], ls[2, rows, :]
            mx = jnp.maximum(jnp.maximum(l0, l1), l2)
            e0, e1, e2 = jnp.exp2(l0 - mx), jnp.exp2(l1 - mx), jnp.exp2(l2 - mx)
            num = e0 * os_[0, rows, :] + e1 * os_[1, rows, :] + e2 * os_[2, rows, :]
            att_ref[0, rows, :] = (num / (e0 + e1 + e2)).astype(BF16)
            return carry
        lax.fori_loop(0, SEQ // blk, mix_body, 0, unroll=2)

    compute()
    for cp in cache_copies():
        cp.wait()


def _attn_prompt(qkv, bias):
    nb = qkv.shape[0]
    hd = ATT_HEAD_DIM

    def head_spec(part, g):
        return pl.BlockSpec((1, 1, SEQ, hd), lambda b, h: (b, part * ATT_HEADS + g * ATT_HPP + h, 0, 0))

    in_specs = [head_spec(part, g) for part in range(3) for g in range(3)]
    in_specs += [pl.BlockSpec((1, ATT_BLOCK, 2 * ATT_BLOCK), functools.partial(lambda b, h, g: (g * ATT_HPP + h, 0, 0), g=g))
                 for g in range(3)]
    out_specs = [pl.BlockSpec((1, SEQ, hd), lambda b, h: (b, 0, h))]
    out_shape = [jax.ShapeDtypeStruct((nb, SEQ, ATT_OUT), BF16)]
    for w in WINDOWS:
        out_specs.append(pl.BlockSpec(memory_space=pl.ANY))
        out_shape.append(jax.ShapeDtypeStruct((nb, w, 2 * ATT_OUT), F32))
    return pl.pallas_call(
        _attn_kernel,
        name="attn_prompt",
        grid=(nb, ATT_HPP),
        in_specs=in_specs,
        out_specs=out_specs,
        out_shape=out_shape,
        scratch_shapes=[
            pltpu.VMEM((3, SEQ, hd), BF16), pltpu.VMEM((3, SEQ, hd), BF16), pltpu.VMEM((3, SEQ, 2 * hd), BF16),
            pltpu.VMEM((ATT_BLOCK, 2 * ATT_BLOCK), F32),
            pltpu.VMEM((3, SEQ, hd), F32), pltpu.VMEM((3, SEQ, hd), F32),
            pltpu.VMEM((SEQ, hd), F32), pltpu.VMEM((SEQ, hd), F32),
            pltpu.SemaphoreType.DMA((2 * len(WINDOWS),)),
        ],
        compiler_params=_params(("arbitrary", "arbitrary"), 48),
    )(*([qkv] * 9), bias, bias, bias)


def _outproj_kernel(y_ref, a_ref, wy_ref, wa_ref, x_ref, g1_ref, o_ref):
    mix = _bdot(y_ref[0], wy_ref[0]) + _bdot(a_ref[0], wa_ref[0])
    o_ref[0] = x_ref[0] + g1_ref[0] * mix


def _outproj(y, att, w, x, g1, tm, tn):
    nb, t, d = x.shape
    r = g1.shape[1]
    return pl.pallas_call(
        _outproj_kernel,
        name="out_proj",
        grid=(nb, t // tm, d // tn),
        in_specs=[
            pl.BlockSpec((1, tm, SSD_INNER), lambda b, i, j: (b, i, 0)),
            pl.BlockSpec((1, tm, ATT_OUT), lambda b, i, j: (b, i, 0)),
            pl.BlockSpec((1, SSD_INNER, tn), lambda b, i, j: (0, 0, j)),
            pl.BlockSpec((1, ATT_OUT, tn), lambda b, i, j: (0, SSD_INNER // ATT_OUT, j)),
            pl.BlockSpec((1, tm, tn), lambda b, i, j: (b, i, j)),
            pl.BlockSpec((1, r if r == 1 else tm, tn), lambda b, i, j: (b, 0 if r == 1 else i, j)),
        ],
        out_specs=pl.BlockSpec((1, tm, tn), lambda b, i, j: (b, i, j)),
        out_shape=jax.ShapeDtypeStruct((nb, t, d), F32),
        compiler_params=_params(("arbitrary", "arbitrary", "arbitrary"), 62),
    )(y, att, w, w, x, g1)


def _mlp_kernel(x_ref, g_ref, sc_ref, sh_ref, g2_ref, xs_ref, scs_ref, shs_ref, g2s_ref, w1_ref, w2_ref,
                o_ref, os_ref, u_ref):
    f = pl.program_id(2)
    last = pl.num_programs(2) - 1
    tm = x_ref.shape[1]
    first_tile = jnp.logical_and(pl.program_id(0) == 0, pl.program_id(1) == 0)

    def partial_out():
        h = jnp.maximum(_bdot(u_ref[...], w1_ref[0]), 0.0)
        return _bdot(h * h, w2_ref[0])

    @pl.when(f == 0)
    def _():
        _normmod_rows(u_ref, x_ref, g_ref, sc_ref, sh_ref)

        @pl.when(first_tile)
        def _():
            u_ref[tm:, :] = _normmod(xs_ref[0], g_ref[...], scs_ref[0], shs_ref[0]).astype(BF16)

        part = partial_out()
        o_ref[0] = part[0:tm]

        @pl.when(first_tile)
        def _():
            os_ref[0] = part[tm:]

    @pl.when(jnp.logical_and(f > 0, f < last))
    def _():
        part = partial_out()
        o_ref[0] += part[0:tm]

        @pl.when(first_tile)
        def _():
            os_ref[0] += part[tm:]

    @pl.when(f == last)
    def _():
        part = partial_out()
        o_ref[0] = x_ref[0] + g2_ref[0] * (o_ref[0] + part[0:tm])

        @pl.when(first_tile)
        def _():
            os_ref[0] = xs_ref[0] + g2s_ref[0] * (os_ref[0] + part[tm:])


def _mlp(x, sc, sh, g2, xs, scs, shs, g2s, g, w1, w2, layer, tm, tf):
    nb, t, d = x.shape
    rows = xs.shape[1]
    mod_spec = pl.BlockSpec((1, 1, d), lambda b, i, f: (b, 0, 0))
    row_spec = pl.BlockSpec((1, rows, d), lambda b, i, f: (0, 0, 0))
    return pl.pallas_call(
        _mlp_kernel,
        name="mlp",
        grid=(nb, t // tm, D_FF // tf),
        in_specs=[
            pl.BlockSpec((1, tm, d), lambda b, i, f: (b, i, 0)),
            pl.BlockSpec((1, d), lambda b, i, f: (0, 0)),
            mod_spec, mod_spec, mod_spec,
            row_spec, row_spec, row_spec, row_spec,
            pl.BlockSpec((1, d, tf), lambda b, i, f: (layer, 0, f)),
            pl.BlockSpec((1, tf, d), lambda b, i, f: (layer, f, 0)),
        ],
        out_specs=[pl.BlockSpec((1, tm, d), lambda b, i, f: (b, i, 0)), row_spec],
        out_shape=[jax.ShapeDtypeStruct((nb, t, d), F32), jax.ShapeDtypeStruct((1, rows, d), F32)],
        scratch_shapes=[pltpu.VMEM((tm + rows, d), BF16)],
        compiler_params=_params(("arbitrary", "arbitrary", "arbitrary"), 62),
    )(x, g, sc, sh, g2, xs, scs, shs, g2s, w1, w2)


def _pool_kernel(x_ref, g_ref, sc_ref, sh_ref, g1_ref, w_ref, scale_ref, o_ref, ulast_ref, ext_ref, *, tt):
    halo = POOL_MAX
    ti = pl.program_id(1)

    @pl.when(ti == 0)
    def _():
        ext_ref[0:halo, :] = jnp.zeros((halo, D_MODEL), F32)

    x = x_ref[0]
    u = _normmod(x, g_ref[...], sc_ref[0], sh_ref[0])
    ext_ref[halo:halo + tt, :] = u
    pos = ti * tt + lax.broadcasted_iota(jnp.int32, (tt, 1), 0)
    for gi, w in enumerate(POOL_WINDOWS):
        ch = slice(gi * POOL_CH, (gi + 1) * POOL_CH)
        acc = ext_ref[:, ch]
        span = 1
        while span < w:
            acc = acc + pltpu.roll(acc, span, axis=0)
            span *= 2
        s = acc[halo:halo + tt]
        inv_cnt = 1.0 / jnp.minimum(w, pos + 1).astype(F32)
        pooled = s * inv_cnt - u[:, ch]
        mix = _bdot(pooled, w_ref[gi]) * scale_ref[:, ch]
        o_ref[0, :, ch] = x[:, ch] + g1_ref[0, :, ch] * mix
    ulast_ref[0] = u[tt - halo:tt]
    ext_ref[0:halo, :] = u[tt - halo:tt]


def _pool_prompt(x, g, sc, sh, g1, pool_w, pool_scale):
    nb, t, d = x.shape
    tt = 512
    mod_spec = pl.BlockSpec((1, 1, d), lambda b, i: (b, 0, 0))
    return pl.pallas_call(
        functools.partial(_pool_kernel, tt=tt),
        name="pool_mix",
        grid=(nb, t // tt),
        in_specs=[
            pl.BlockSpec((1, tt, d), lambda b, i: (b, i, 0)),
            pl.BlockSpec((1, d), lambda b, i: (0, 0)),
            mod_spec, mod_spec, mod_spec,
            pl.BlockSpec((len(POOL_WINDOWS), POOL_CH, POOL_CH), lambda b, i: (0, 0, 0)),
            pl.BlockSpec((1, d), lambda b, i: (0, 0)),
        ],
        out_specs=[
            pl.BlockSpec((1, tt, d), lambda b, i: (b, i, 0)),
            pl.BlockSpec((1, POOL_MAX, d), lambda b, i: (b, 0, 0)),
        ],
        out_shape=[jax.ShapeDtypeStruct((nb, t, d), F32), jax.ShapeDtypeStruct((nb, POOL_MAX, d), F32)],
        scratch_shapes=[pltpu.VMEM((tt + POOL_MAX, d), F32)],
        compiler_params=_params(("arbitrary", "arbitrary"), 48),
    )(x, g, sc, sh, g1, pool_w, pool_scale)


def _sconv_kernel(zx_ref, prev_ref, w_ref, b_ref, o_ref):
    w = w_ref[...]
    acc = b_ref[...] + w[SSD_CONV - 1:SSD_CONV] * zx_ref[:, SSD_INNER:ZX_COLS]
    for k in range(SSD_CONV - 1):
        acc = acc + w[k:k + 1] * prev_ref[k]
    o_ref[...] = _silu(acc)


def _conv_sample(zx, prev_t, conv_w, conv_b):
    return pl.pallas_call(
        _sconv_kernel,
        out_shape=jax.ShapeDtypeStruct((SAMPLE_ROWS, CONV_DIM), F32),
        compiler_params=pltpu.CompilerParams(vmem_limit_bytes=32 * MIB),
    )(zx, prev_t, conv_w, conv_b)


def _sstep_kernel(h0_ref, x_ref, dt_ref, b_ref, c_ref, alog_ref, h_ref, y_ref):
    dt = dt_ref[0]
    h = jnp.exp(dt * (-jnp.exp(alog_ref[...]))) * h0_ref[0] + (dt * x_ref[0]) * b_ref[0]
    h_ref[0] = h
    y_ref[0] = jnp.sum(h * c_ref[0], axis=-1, keepdims=True)


def _ssd_step(h0, x, dt, bh, ch, a_log):
    nb = h0.shape[0]
    hp = (SSD_HEADS, SSD_HEAD_DIM)
    return pl.pallas_call(
        _sstep_kernel,
        grid=(nb,),
        in_specs=[
            pl.BlockSpec((1, *hp, SSD_STATE), lambda b: (b, 0, 0, 0)),
            pl.BlockSpec((1, *hp, 1), lambda b: (b, 0, 0, 0)),
            pl.BlockSpec((1, SSD_HEADS, 1, 1), lambda b: (b, 0, 0, 0)),
            pl.BlockSpec((1, SSD_HEADS, 1, SSD_STATE), lambda b: (b, 0, 0, 0)),
            pl.BlockSpec((1, SSD_HEADS, 1, SSD_STATE), lambda b: (b, 0, 0, 0)),
            pl.BlockSpec((SSD_HEADS, 1, 1), lambda b: (0, 0, 0)),
        ],
        out_specs=[
            pl.BlockSpec((1, *hp, SSD_STATE), lambda b: (b, 0, 0, 0)),
            pl.BlockSpec((1, *hp, 1), lambda b: (b, 0, 0, 0)),
        ],
        out_shape=[jax.ShapeDtypeStruct((nb, *hp, SSD_STATE), F32), jax.ShapeDtypeStruct((nb, *hp, 1), F32)],
        compiler_params=_params(("arbitrary",), 32),
    )(h0, x, dt, bh, ch, a_log)


def _sgate_kernel(y_ref, xs_ref, z_ref, dskip_ref, g_ref, o_ref):
    y = (y_ref[...] + dskip_ref[...] * xs_ref[...]) * _silu(z_ref[...])
    o_ref[...] = _rms(y) * g_ref[...]


def _gate_sample(y, xs, z, dskip_x, norm_g):
    return pl.pallas_call(
        _sgate_kernel,
        out_shape=jax.ShapeDtypeStruct(y.shape, F32),
    )(y, xs, z, dskip_x, norm_g)


def _sattn_kernel(qkv_ref, c0_ref, c1_ref, c2_ref, bbuf_ref, bnew_ref, att_ref, knew_ref, vnew_ref):
    scale = 1.0 / math.sqrt(ATT_HEAD_DIM)
    caches = (c0_ref, c1_ref, c2_ref)
    outs, lses = [], []
    for g in range(len(WINDOWS)):
        hs = g * ATT_HPP
        q = qkv_ref[0, hs:hs + ATT_HPP, :]
        k = qkv_ref[0, ATT_HEADS + hs:ATT_HEADS + hs + ATT_HPP, :]
        v = qkv_ref[0, 2 * ATT_HEADS + hs:2 * ATT_HEADS + hs + ATT_HPP, :]
        knew_ref[0, g] = k
        vnew_ref[0, g] = v
        kbuf = caches[g][:, 0]
        vbuf = caches[g][:, 1]
        lb = jnp.sum(kbuf * q[None], axis=-1, keepdims=True) * scale + bbuf_ref[g]
        ln = jnp.sum(k * q, axis=-1, keepdims=True) * scale + bnew_ref[g]
        m = jnp.maximum(jnp.max(lb, axis=0), ln)
        pb = jnp.exp(lb - m[None])
        pn = jnp.exp(ln - m)
        s = jnp.sum(pb, axis=0) + pn
        outs.append((jnp.sum(pb * vbuf, axis=0) + pn * v) / s)
        lses.append(m + jnp.log(s))
    mx = jnp.maximum(jnp.maximum(lses[0], lses[1]), lses[2])
    es = [jnp.exp(l - mx) for l in lses]
    att_ref[0] = (es[0] * outs[0] + es[1] * outs[1] + es[2] * outs[2]) / (es[0] + es[1] + es[2])


def _attn_sample(qkv, caches, bias_buf, bias_new):
    nb = qkv.shape[0]
    hd = ATT_HEAD_DIM
    in_specs = [pl.BlockSpec((1, 3 * ATT_HEADS, hd), lambda b: (b, 0, 0))]
    for _ in caches:
        in_specs.append(pl.BlockSpec((None, ATT_BLOCK, None, 2, ATT_HPP, hd), lambda b: (b, 0, 0, 0, 0, 0)))
    in_specs += [
        pl.BlockSpec((3, ATT_BLOCK, ATT_HPP, 1), lambda b: (0, 0, 0, 0)),
        pl.BlockSpec((3, ATT_HPP, 1), lambda b: (0, 0, 0)),
    ]
    return pl.pallas_call(
        _sattn_kernel,
        grid=(nb,),
        in_specs=in_specs,
        out_specs=[
            pl.BlockSpec((1, ATT_HPP, hd), lambda b: (b, 0, 0)),
            pl.BlockSpec((1, 3, ATT_HPP, hd), lambda b: (b, 0, 0, 0)),
            pl.BlockSpec((1, 3, ATT_HPP, hd), lambda b: (b, 0, 0, 0)),
        ],
        out_shape=[
            jax.ShapeDtypeStruct((nb, ATT_HPP, hd), F32),
            jax.ShapeDtypeStruct((nb, 3, ATT_HPP, hd), F32),
            jax.ShapeDtypeStruct((nb, 3, ATT_HPP, hd), F32),
        ],
        compiler_params=_params(("arbitrary",), 32),
    )(qkv, *caches, bias_buf, bias_new)


def _spool_kernel(x_ref, g_ref, sc_ref, sh_ref, g1_ref, prev_ref, w_ref, scale_ref, o_ref, u_ref):
    x = x_ref[...]
    u = _normmod(x, g_ref[...], sc_ref[...], sh_ref[...])
    u_ref[...] = u
    for gi, w in enumerate(POOL_WINDOWS):
        ch = slice(gi * POOL_CH, (gi + 1) * POOL_CH)
        s = u[:, ch]
        for k in range(1, w):
            s = s + prev_ref[POOL_MAX - 1 - k, :, ch]
        pooled = s / float(min(w, PAST_LEN + 1)) - u[:, ch]
        mix = _bdot(pooled, w_ref[gi]) * scale_ref[:, ch]
        o_ref[:, ch] = x[:, ch] + g1_ref[:, ch] * mix


def _pool_sample(x, g, sc, sh, g1, prev_t, pool_w, pool_scale):
    return pl.pallas_call(
        _spool_kernel,
        out_shape=[jax.ShapeDtypeStruct(x.shape, F32), jax.ShapeDtypeStruct(x.shape, F32)],
        compiler_params=pltpu.CompilerParams(vmem_limit_bytes=32 * MIB),
    )(x, g, sc, sh, g1, prev_t, pool_w, pool_scale)


def _rel_bucket(dist):
    max_exact = REL_BUCKETS // 2
    n = jnp.maximum(dist, 1).astype(F32)
    large = max_exact + (jnp.log(n / max_exact) / math.log(REL_MAX_DIST / max_exact)
                         * (REL_BUCKETS - max_exact)).astype(jnp.int32)
    large = jnp.minimum(large, REL_BUCKETS - 1)
    return jnp.where(dist < max_exact, dist, large)


def _prompt_bias_table(rel_bias):
    blk = ATT_BLOCK
    j = blk + jnp.arange(blk)[:, None] - jnp.arange(2 * blk)[None, :]
    tables = []
    for gi, dil in enumerate(DILATIONS):
        bucket = _rel_bucket(jnp.clip(j, 0, blk) * dil)[None]
        rb = rel_bias[:, gi * ATT_HPP:(gi + 1) * ATT_HPP].astype(F32)
        tbl = jnp.zeros((ATT_HPP, blk, 2 * blk), F32)
        for k in range(REL_BUCKETS):
            tbl = jnp.where(bucket == k, rb[k][:, None, None], tbl)
        tables.append(tbl)
    return jnp.concatenate(tables, axis=0)


def _sample_bias_table(rel_bias):
    blk = ATT_BLOCK
    bufs, news = [], []
    for gi, dil in enumerate(DILATIONS):
        tbl = rel_bias[:, gi * ATT_HPP:(gi + 1) * ATT_HPP][_rel_bucket(dil * jnp.arange(blk + 1))]
        bufs.append(tbl[blk:0:-1])
        news.append(tbl[0])
    return jnp.stack(bufs)[..., None].astype(F32), jnp.stack(news)[..., None].astype(F32)


def _pad_rows(a, rows, axis=0):
    pad = [(0, 0)] * a.ndim
    pad[axis] = (0, rows - a.shape[axis])
    return jnp.pad(a, pad)


MLP_TILES = (1024, 512)


def kernel(x_prompt, x_sample, state_ssm, state_conv, cache_kv_w128, cache_kv_w512, cache_kv_w2048, state_pool, c_prompt, c_sample, rel_bias, ada_w, ada_b, norm_mix_g, norm_mlp_g, in_proj_w, conv_w, conv_b, dt_bias, a_log, d_skip, ssd_norm_g, q_norm_g, k_norm_g, out_proj_w, pool_w, pool_scale, mlp_w1, mlp_w2):
    d = D_MODEL
    c_all = _pad_rows(jnp.concatenate([c_prompt, c_sample], axis=0), SAMPLE_ROWS)
    mod = _ada(c_all, ada_w, ada_b)

    def mods(layer):
        parts = [mod[layer, :, i * d:(i + 1) * d] for i in range(6)]
        prompt = [p[:N_PROMPT, None, :] for p in parts]
        sample = [_pad_rows(p[N_PROMPT:N_PROMPT + N_SAMPLE], SAMPLE_ROWS)[None] for p in parts]
        return prompt, sample

    w_in_t = jnp.transpose(in_proj_w[0])
    b_dt = jnp.pad(dt_bias[0], (0, DT_PAD - SSD_HEADS)).reshape(1, DT_PAD)
    a_log_pad = jnp.pad(a_log[0], (0, DT_PAD - SSD_HEADS)).reshape(1, DT_PAD)
    dskip_x = jnp.repeat(d_skip[0], SSD_HEAD_DIM).reshape(1, SSD_INNER)
    ssd_g = ssd_norm_g.reshape(1, SSD_INNER)
    qg, kg = q_norm_g.reshape(1, ATT_HEAD_DIM), k_norm_g.reshape(1, ATT_HEAD_DIM)
    cw, cb = conv_w[0], conv_b.reshape(1, CONV_DIM)
    g_mix = [norm_mix_g[l].reshape(1, d) for l in range(2)]
    g_mlp = [norm_mlp_g[l].reshape(1, d) for l in range(2)]
    pscale = pool_scale.reshape(1, d)
    pool_wb = pool_w[0].astype(BF16)

    (sh1, sc1, g1, sh2, sc2, g2), (sh1s, sc1s, g1s, sh2s, sc2s, g2s) = mods(0)
    (sh1b, sc1b, g1b, sh2b, sc2b, g2b), (sh1sb, sc1sb, g1sb, sh2sb, sc2sb, g2sb) = mods(1)

    rows = SAMPLE_ROWS
    xs0 = _pad_rows(x_sample.reshape(N_SAMPLE, d), rows)[None]
    u0 = _norm_mod(x_prompt, g_mix[0], sc1, sh1, 1024)
    u0s = _norm_mod(xs0, g_mix[0], sc1s, sh1s, rows)
    zx, dt, zx_s, dt_s = _inproj(u0, u0s, w_in_t, 0, ZX_COLS, 512, dt_b=b_dt)
    qkv, qkv_s = _inproj(u0, u0s, w_in_t, QKV_COL0, 3 * ATT_QKV, 512, qk_gains=(qg, kg))
    p_conv = zx[:, SEQ - (SSD_CONV - 1):, SSD_INNER:][None]

    zx_s = jnp.transpose(zx_s, (1, 0, 2)).reshape(rows, ZX_COLS)
    dt_s = dt_s[0]
    qkv_s = jnp.transpose(qkv_s[0, :, :N_SAMPLE], (1, 0, 2))
    xbc_new = zx_s[:N_SAMPLE, SSD_INNER:]
    s_conv = jnp.concatenate([state_conv[0][:, 1:], xbc_new[:, None, :]], axis=1)[None]
    conv_prev_t = _pad_rows(jnp.transpose(state_conv[0], (1, 0, 2)), rows, axis=1)
    xbc_s = _conv_sample(zx_s, conv_prev_t, cw, cb)
    xs_s = xbc_s[:N_SAMPLE, :SSD_INNER]
    e = SSD_HEADS_PER_GROUP
    b_s = jnp.repeat(xbc_s[:N_SAMPLE, SSD_INNER:SSD_INNER + SSD_GROUPS * SSD_STATE]
                     .reshape(N_SAMPLE, SSD_GROUPS, 1, SSD_STATE), e, axis=1)
    c_s = jnp.repeat(xbc_s[:N_SAMPLE, SSD_INNER + SSD_GROUPS * SSD_STATE:]
                     .reshape(N_SAMPLE, SSD_GROUPS, 1, SSD_STATE), e, axis=1)
    h_new, y_s = _ssd_step(state_ssm[0], xs_s.reshape(N_SAMPLE, SSD_HEADS, SSD_HEAD_DIM, 1),
                           dt_s[:N_SAMPLE, :SSD_HEADS].reshape(N_SAMPLE, SSD_HEADS, 1, 1), b_s, c_s,
                           a_log[0].reshape(SSD_HEADS, 1, 1))
    s_ssm = h_new[None]
    y_s = _gate_sample(_pad_rows(y_s.reshape(N_SAMPLE, SSD_INNER), rows), xbc_s[:, :SSD_INNER],
                       zx_s[:, :SSD_INNER], dskip_x, ssd_g)
    caches = (cache_kv_w128, cache_kv_w512, cache_kv_w2048)
    strided = [c.reshape(N_SAMPLE, ATT_BLOCK, dil, 2, ATT_HPP, ATT_HEAD_DIM) for c, dil in zip(caches, DILATIONS)]
    bias_buf, bias_new = _sample_bias_table(rel_bias)
    att_s, k_new, v_new = _attn_sample(qkv_s, strided, bias_buf, bias_new)
    new_rows = jnp.stack([k_new, v_new], axis=2)
    att_s = _pad_rows(att_s.reshape(N_SAMPLE, ATT_OUT), rows)
    x1s = _outproj(y_s[None], att_s[None], out_proj_w, xs0, g1s, rows, 512)

    y_ssd, h_fin, *s_kv = _ssd_prompt(zx, dt, cw, cb, a_log_pad, dskip_x, ssd_g, caches,
                                      [new_rows[:, gi] for gi in range(len(caches))])
    p_ssm = h_fin.reshape(1, N_PROMPT, SSD_HEADS, SSD_HEAD_DIM, SSD_STATE)
    att, kv128, kv512, kv2048 = _attn_prompt(qkv, _prompt_bias_table(rel_bias) * LOG2E)
    p_kv = [kv.reshape(1, N_PROMPT, w, 2, ATT_HPP, ATT_HEAD_DIM) for kv, w in zip((kv128, kv512, kv2048), WINDOWS)]
    x1 = _outproj(y_ssd, att, out_proj_w, x_prompt, g1, SEQ, 512)

    x2, x2s = _mlp(x1, sc2, sh2, g2, x1s, sc2s, sh2s, g2s, g_mlp[0], mlp_w1, mlp_w2, 0, *MLP_TILES)
    x3, u_last = _pool_prompt(x2, g_mix[1], sc1b, sh1b, g1b, pool_wb, pscale)
    p_pool = u_last[:, 1:][None]
    pool_prev_t = _pad_rows(jnp.transpose(state_pool[0], (1, 0, 2)), rows, axis=1)
    x3s, u_s = _pool_sample(x2s[0], g_mix[1], sc1sb[0], sh1sb[0], g1sb[0], pool_prev_t, pool_wb, pscale)
    s_pool = jnp.concatenate([state_pool[0][:, 1:], u_s[:N_SAMPLE, None, :]], axis=1)[None]
    y_prompt, y_s2 = _mlp(x3, sc2b, sh2b, g2b, x3s[None], sc2sb, sh2sb, g2sb, g_mlp[1], mlp_w1, mlp_w2, 1, *MLP_TILES)
    y_sample = y_s2[0, :N_SAMPLE].reshape(N_SAMPLE, 1, d)

    return (y_prompt, y_sample, p_ssm, p_conv, p_kv[0], p_kv[1], p_kv[2], p_pool,
            s_ssm, s_conv, s_kv[0], s_kv[1], s_kv[2], s_pool)
```
